```python
import jax, jax.numpy as jnp
from jax import lax
import numpy as np

D_MODEL = 1024
BATCH = 16
SEQ = 2048
DEPTH = 2
DEC_BATCH = 128
DEC_SEQ = 1
PAST_LEN = 16384
PAGE_SIZE = 128

N_META = 16
HEAD_DIM = 64
A_WIDTH = D_MODEL // 2
A_HEADS = A_WIDTH // HEAD_DIM
A_KV_HEADS = 2
A_GROUP = A_HEADS // A_KV_HEADS
WINDOW = 128
BLOCK = 128
ROPE_THETA = 10000.0
B_WIDTH = D_MODEL // 4
B_HEADS = 4
B_DV = B_WIDTH // B_HEADS
B_DK = B_DV // 2
B_RANK = 16
GATE_TAU = 16.0
GLA_CHUNK = 64
C_WIDTH = D_MODEL // 4
CONV_W = 31
N_GROUPS = 4
EXP_PER_GROUP = 4
N_EXPERTS = N_GROUPS * EXP_PER_GROUP
TOP_K = 2
D_EXPERT = D_MODEL // 4
ALPHA = (2 * DEPTH) ** 0.25
BETA = (8 * DEPTH) ** -0.25
LN_EPS = 1e-5
PROJ_SIZES = (A_WIDTH, A_KV_HEADS * HEAD_DIM, A_KV_HEADS * HEAD_DIM,
              B_HEADS * B_DK, B_HEADS * B_DK, B_WIDTH, B_WIDTH, B_RANK, 2 * C_WIDTH)
PROJ_SPLITS = tuple(sum(PROJ_SIZES[:i + 1]) for i in range(len(PROJ_SIZES) - 1))
PROJ_WIDTH = sum(PROJ_SIZES)
F32 = jnp.float32

kernel_name = 'hymba_swa_gla_conformer_hmoe_step'


def layer_norm(x, g, b):
    xf = x.astype(F32)
    xc = xf - jnp.mean(xf, -1, keepdims=True)
    var = jnp.mean(xc * xc, -1, keepdims=True)
    return (xc * lax.rsqrt(var + LN_EPS)).astype(x.dtype) * g + b


def rope(x, pos):
    half = HEAD_DIM // 2
    inv = ROPE_THETA ** (-jnp.arange(half, dtype=F32) / half)
    ang = pos.astype(F32)[:, None] * inv[None, :]
    cos, sin = jnp.cos(ang)[:, None, :], jnp.sin(ang)[:, None, :]
    xf = x.astype(F32)
    x1, x2 = xf[..., :half], xf[..., half:]
    return jnp.concatenate([x1 * cos - x2 * sin, x2 * cos + x1 * sin], -1).astype(x.dtype)


def sink_logits(sink, like):
    return jnp.broadcast_to(sink.astype(F32).reshape(A_KV_HEADS, A_GROUP, 1, 1), like.shape[:-1] + (1,))


def banded_sink_attention(q, k, v, sink):
    bsz, t = q.shape[:2]
    pad = BLOCK - N_META
    padt = lambda a: jnp.pad(a, ((0, 0), (pad, 0), (0, 0), (0, 0)))
    nb = (t + pad) // BLOCK
    qb = padt(q).reshape(bsz, nb, BLOCK, A_KV_HEADS, A_GROUP, HEAD_DIM)
    kb = padt(k).reshape(bsz, nb, BLOCK, A_KV_HEADS, HEAD_DIM)
    vb = padt(v).reshape(bsz, nb, BLOCK, A_KV_HEADS, HEAD_DIM)

    def band(a):
        prev = jnp.pad(a, ((0, 0), (1, 0), (0, 0), (0, 0), (0, 0)))[:, :-1]
        return jnp.concatenate([prev, a], axis=2)

    kband, vband = band(kb), band(vb)
    km, vm = k[:, :N_META], v[:, :N_META]
    scale = HEAD_DIM ** -0.5
    s_band = jnp.einsum('bnqkgd,bnjkd->bnkgqj', qb, kband).astype(F32) * scale
    s_meta = jnp.einsum('bnqkgd,bmkd->bnkgqm', qb, km).astype(F32) * scale
    q_idx = jnp.arange(nb)[:, None] * BLOCK + jnp.arange(BLOCK)[None, :]
    k_idx = (jnp.arange(nb)[:, None] - 1) * BLOCK + jnp.arange(2 * BLOCK)[None, :]
    dist = q_idx[:, :, None] - k_idx[:, None, :]
    band_ok = (k_idx[:, None, :] >= BLOCK) & (dist >= 0) & (dist <= WINDOW)
    meta_ok = jnp.arange(N_META)[None, None, :] <= (q_idx - pad)[:, :, None]
    s_band = jnp.where(band_ok[None, :, None, None], s_band, -jnp.inf)
    s_meta = jnp.where(meta_ok[None, :, None, None], s_meta, -jnp.inf)
    p = jax.nn.softmax(jnp.concatenate([s_band, s_meta, sink_logits(sink, s_band)], -1), axis=-1)
    o = (jnp.einsum('bnkgqj,bnjkd->bnqkgd', p[..., :2 * BLOCK].astype(v.dtype), vband)
         + jnp.einsum('bnkgqm,bmkd->bnqkgd', p[..., 2 * BLOCK:2 * BLOCK + N_META].astype(v.dtype), vm))
    return o.reshape(bsz, nb * BLOCK, A_WIDTH)[:, pad:]


def cached_sink_attention(q, k, v, meta_k, meta_v, win_k, win_v, sink, q_pos):
    bsz, s = q.shape[:2]
    n_win = win_k.shape[1]
    keys = jnp.concatenate([meta_k.astype(k.dtype), win_k.astype(k.dtype), k], axis=1)
    vals = jnp.concatenate([meta_v.astype(v.dtype), win_v.astype(v.dtype), v], axis=1)
    k_pos = jnp.concatenate([jnp.arange(N_META), PAST_LEN - n_win + jnp.arange(n_win), PAST_LEN + jnp.arange(s)])
    is_meta = jnp.arange(k_pos.shape[0]) < N_META
    dist = q_pos[:, None] - k_pos[None, :]
    ok = jnp.where(is_meta[None, :], dist >= 0,
                   (k_pos[None, :] >= N_META) & (dist >= 0) & (dist <= WINDOW))
    qg = q.reshape(bsz, s, A_KV_HEADS, A_GROUP, HEAD_DIM)
    sc = jnp.einsum('bqkgd,bjkd->bkgqj', qg, keys).astype(F32) * HEAD_DIM ** -0.5
    sc = jnp.where(ok[None, None, None], sc, -jnp.inf)
    p = jax.nn.softmax(jnp.concatenate([sc, sink_logits(sink, sc)], -1), axis=-1)[..., :-1]
    o = jnp.einsum('bkgqj,bjkd->bqkgd', p.astype(vals.dtype), vals)
    return o.reshape(bsz, s, A_WIDTH)


def gla_inputs(qb, kb, vb, ab, w_alpha, b_alpha):
    lead = qb.shape[:-1]
    q = qb.reshape(*lead, B_HEADS, B_DK) * (B_DK ** -0.5)
    k = kb.reshape(*lead, B_HEADS, B_DK)
    v = vb.reshape(*lead, B_HEADS, B_DV)
    la = jax.nn.log_sigmoid((ab @ w_alpha + b_alpha).astype(F32)) / GATE_TAU
    return q, k, v, la.reshape(*lead, B_HEADS, B_DK)


def gla_chunk(s_state, inp):
    q, k, v, g = inp
    qf, kf, vf, sf = q.astype(F32), k.astype(F32), v.astype(F32), s_state.astype(F32)
    c = q.shape[2]
    causal = jnp.tril(jnp.ones((c, c), bool))
    decay = jnp.exp(jnp.where(causal[:, :, None], g[:, :, :, None, :] - g[:, :, None, :, :], -jnp.inf))
    att = jnp.einsum('bhtd,bhsd,bhtsd->bhts', qf, kf, decay)
    o = jnp.einsum('bhtd,bhde->bhte', qf * jnp.exp(g), sf) + jnp.einsum('bhts,bhse->bhte', att, vf)
    g_end = g[:, :, -1]
    s_new = jnp.exp(g_end)[..., None] * sf + jnp.einsum('bhsd,bhse->bhde', kf * jnp.exp(g_end[:, :, None] - g), vf)
    return s_new.astype(s_state.dtype), o.astype(q.dtype)


def gla_prompt(q, k, v, la):
    pad = GLA_CHUNK - N_META
    padt = lambda a: jnp.pad(a, ((0, 0), (pad, 0), (0, 0), (0, 0)))
    q, k, v, la = padt(q), padt(k), padt(v), padt(la)
    bsz, tp = q.shape[:2]
    nc = tp // GLA_CHUNK

    def to_chunks(a):
        return a.reshape(bsz, nc, GLA_CHUNK, B_HEADS, a.shape[-1]).transpose(1, 0, 3, 2, 4)

    g = jnp.cumsum(to_chunks(la), axis=3)
    s0 = jnp.zeros((bsz, B_HEADS, B_DK, B_DV), q.dtype)
    s_fin, o = lax.scan(gla_chunk, s0, (to_chunks(q), to_chunks(k), to_chunks(v), g))
    o = o.transpose(1, 0, 3, 2, 4).reshape(bsz, tp, B_HEADS, B_DV)[:, pad:]
    return o, s_fin


def gla_output(o, r, g):
    of = o.astype(F32)
    of = of * lax.rsqrt(jnp.mean(of * of, -1, keepdims=True) + LN_EPS)
    o = of.astype(r.dtype) * g
    return o.reshape(*o.shape[:-2], B_WIDTH) * jax.nn.silu(r)


def glu(cg):
    a, b = jnp.split(cg, 2, axis=-1)
    return a * jax.nn.sigmoid(b)


def conformer_conv(u, hist, dw_w, dw_b, ln_g, ln_b, pw_w, pw_b):
    full = jnp.concatenate([hist.astype(u.dtype), u], axis=1)
    y = lax.conv_general_dilated(full, dw_w[:, None, :].astype(u.dtype), window_strides=(1,), padding='VALID',
                                 dimension_numbers=('NWC', 'WIO', 'NWC'), feature_group_count=C_WIDTH) + dw_b
    y = jax.nn.silu(layer_norm(y, ln_g, ln_b))
    return y @ pw_w + pw_b, full[:, -(CONV_W - 1):]


def hier_moe(h, w_rg, b_rg, w_re, b_re, w_gate, w_up, w_down):
    gl = (h @ w_rg + b_rg).astype(F32)
    gp = jax.nn.softmax(gl, axis=-1)
    _, gi = lax.top_k(gl, 1)
    g_onehot = jax.nn.one_hot(gi[..., 0], N_GROUPS, dtype=F32)
    p_grp = jnp.sum(gp * g_onehot, -1, keepdims=True)
    el = (h @ w_re + b_re).astype(F32).reshape(*h.shape[:-1], N_GROUPS, EXP_PER_GROUP)
    el_sel = jnp.sum(el * g_onehot[..., None], axis=-2)
    tv, ti = lax.top_k(el_sel, TOP_K)
    w = jax.nn.softmax(tv, axis=-1) * p_grp
    eid = gi * EXP_PER_GROUP + ti
    dense_w = jnp.sum(jax.nn.one_hot(eid, N_EXPERTS, dtype=F32) * w[..., None], axis=-2).astype(h.dtype)
    out = jnp.zeros_like(h)
    for e in range(N_EXPERTS):
        he = jax.nn.silu(h @ w_gate[e]) * (h @ w_up[e])
        out = out + dense_w[..., e:e + 1] * (he @ w_down[e])
    return out


def prompt_mixer(h, w_in, sink, w_alpha, b_alpha, gla_g, dw_w, dw_b, cln_g, cln_b, pw_w, pw_b, w_out):
    bsz, t = h.shape[:2]
    qa, ka, va, qb, kb, vb, rb, ab, cg = jnp.split(h @ w_in, PROJ_SPLITS, axis=-1)
    pos = jnp.arange(t)
    qa = rope(qa.reshape(bsz, t, A_HEADS, HEAD_DIM), pos)
    ka = rope(ka.reshape(bsz, t, A_KV_HEADS, HEAD_DIM), pos)
    va = va.reshape(bsz, t, A_KV_HEADS, HEAD_DIM)
    oa = banded_sink_attention(qa, ka, va, sink)
    q, k, v, la = gla_inputs(qb, kb, vb, ab, w_alpha, b_alpha)
    o, s_gla = gla_prompt(q, k, v, la)
    ob = gla_output(o, rb, gla_g)
    u = glu(cg)
    hist = jnp.zeros((bsz, CONV_W - 1, C_WIDTH), u.dtype)
    oc, conv_tail = conformer_conv(u, hist, dw_w, dw_b, cln_g, cln_b, pw_w, pw_b)
    mix = jnp.concatenate([oa, ob, oc], axis=-1) @ w_out
    return mix, (ka[:, :N_META], va[:, :N_META], ka[:, -WINDOW:], va[:, -WINDOW:], s_gla, conv_tail)


def sample_mixer(h, meta_k, meta_v, win_k, win_v, s_gla, s_conv,
                 w_in, sink, w_alpha, b_alpha, gla_g, dw_w, dw_b, cln_g, cln_b, pw_w, pw_b, w_out):
    bsz, s = h.shape[:2]
    qa, ka, va, qb, kb, vb, rb, ab, cg = jnp.split(h @ w_in, PROJ_SPLITS, axis=-1)
    pos = PAST_LEN + jnp.arange(s)
    qa = rope(qa.reshape(bsz, s, A_HEADS, HEAD_DIM), pos)
    ka = rope(ka.reshape(bsz, s, A_KV_HEADS, HEAD_DIM), pos)
    va = va.reshape(bsz, s, A_KV_HEADS, HEAD_DIM)
    oa = cached_sink_attention(qa, ka, va, meta_k, meta_v, win_k, win_v, sink, pos)
    n_win = win_k.shape[1]
    new_wk = jnp.concatenate([win_k.astype(ka.dtype), ka], axis=1)[:, -n_win:]
    new_wv = jnp.concatenate([win_v.astype(va.dtype), va], axis=1)[:, -n_win:]
    q, k, v, la = gla_inputs(qb, kb, vb, ab, w_alpha, b_alpha)
    tr = lambda a: a.transpose(0, 2, 1, 3)
    s_new, o = gla_chunk(s_gla, (tr(q), tr(k), tr(v), jnp.cumsum(tr(la), axis=2)))
    ob = gla_output(tr(o), rb, gla_g)
    oc, conv_tail = conformer_conv(glu(cg), s_conv, dw_w, dw_b, cln_g, cln_b, pw_w, pw_b)
    mix = jnp.concatenate([oa, ob, oc], axis=-1) @ w_out
    return mix, (new_wk, new_wv, s_new, conv_tail)


def setup_inputs(seed: int = 0) -> dict:
    key = jax.random.key(seed)
    keys = iter(jax.random.split(key, 48))

    def nrm(shape, scale):
        return jax.random.normal(next(keys), shape, F32) * scale

    d = D_MODEL
    kv = (A_KV_HEADS, HEAD_DIM)
    return {
        'x_prompt': nrm((BATCH, SEQ, d), 1.0),
        'x_sample': nrm((DEC_BATCH, DEC_SEQ, d), 1.0),
        'cache_meta_k': nrm((DEPTH, DEC_BATCH, N_META) + kv, 1.0),
        'cache_meta_v': nrm((DEPTH, DEC_BATCH, N_META) + kv, 1.0),
        'cache_win_k': nrm((DEPTH, DEC_BATCH, WINDOW) + kv, 1.0),
        'cache_win_v': nrm((DEPTH, DEC_BATCH, WINDOW) + kv, 1.0),
        'state_gla': nrm((DEPTH, DEC_BATCH, B_HEADS, B_DK, B_DV), 0.5),
        'state_conv': nrm((DEPTH, DEC_BATCH, CONV_W - 1, C_WIDTH), 0.5),
        'meta_tokens': nrm((N_META, d), 1.0),
        'ln_in_g': 1.0 + nrm((d,), 0.02),
        'ln_in_b': nrm((d,), 0.02),
        'w_in': nrm((DEPTH, d, PROJ_WIDTH), d ** -0.5),
        'attn_sink': nrm((DEPTH, A_HEADS), 0.5),
        'w_alpha': nrm((DEPTH, B_RANK, B_HEADS * B_DK), B_RANK ** -0.5),
        'b_alpha': nrm((DEPTH, B_HEADS * B_DK), 0.1),
        'gla_norm_g': 1.0 + nrm((DEPTH, B_DV), 0.02),
        'conv_dw_w': nrm((DEPTH, CONV_W, C_WIDTH), CONV_W ** -0.5),
        'conv_dw_b': nrm((DEPTH, C_WIDTH), 0.02),
        'conv_ln_g': 1.0 + nrm((DEPTH, C_WIDTH), 0.02),
        'conv_ln_b': nrm((DEPTH, C_WIDTH), 0.02),
        'conv_pw_w': nrm((DEPTH, C_WIDTH, C_WIDTH), C_WIDTH ** -0.5),
        'conv_pw_b': nrm((DEPTH, C_WIDTH), 0.02),
        'w_out': nrm((DEPTH, d, d), BETA * d ** -0.5),
        'ln1_g': 1.0 + nrm((DEPTH, d), 0.02),
        'ln1_b': nrm((DEPTH, d), 0.02),
        'w_router_group': nrm((DEPTH, d, N_GROUPS), d ** -0.5),
        'b_router_group': nrm((DEPTH, N_GROUPS), 0.01),
        'w_router_expert': nrm((DEPTH, d, N_EXPERTS), d ** -0.5),
        'b_router_expert': nrm((DEPTH, N_EXPERTS), 0.01),
        'w_exp_gate': nrm((DEPTH, N_EXPERTS, d, D_EXPERT), d ** -0.5),
        'w_exp_up': nrm((DEPTH, N_EXPERTS, d, D_EXPERT), d ** -0.5),
        'w_exp_down': nrm((DEPTH, N_EXPERTS, D_EXPERT, d), BETA * D_EXPERT ** -0.5),
        'ln2_g': 1.0 + nrm((DEPTH, d), 0.02),
        'ln2_b': nrm((DEPTH, d), 0.02),
    }


def reference(x_prompt, x_sample, cache_meta_k, cache_meta_v, cache_win_k, cache_win_v, state_gla, state_conv,
              meta_tokens, ln_in_g, ln_in_b, w_in, attn_sink, w_alpha, b_alpha, gla_norm_g,
              conv_dw_w, conv_dw_b, conv_ln_g, conv_ln_b, conv_pw_w, conv_pw_b, w_out, ln1_g, ln1_b,
              w_router_group, b_router_group, w_router_expert, b_router_expert,
              w_exp_gate, w_exp_up, w_exp_down, ln2_g, ln2_b):
    meta = jnp.broadcast_to(meta_tokens.astype(x_prompt.dtype)[None], (x_prompt.shape[0], N_META, D_MODEL))
    xp = layer_norm(jnp.concatenate([meta, x_prompt], axis=1), ln_in_g, ln_in_b)
    xs = layer_norm(x_sample, ln_in_g, ln_in_b)
    st_p = [[] for _ in range(6)]
    st_s = [[] for _ in range(4)]
    for l in range(DEPTH):
        mw = (w_in[l], attn_sink[l], w_alpha[l], b_alpha[l], gla_norm_g[l], conv_dw_w[l], conv_dw_b[l],
              conv_ln_g[l], conv_ln_b[l], conv_pw_w[l], conv_pw_b[l], w_out[l])
        fw = (w_router_group[l], b_router_group[l], w_router_expert[l], b_router_expert[l],
              w_exp_gate[l], w_exp_up[l], w_exp_down[l])
        mix_p, sp = prompt_mixer(xp, *mw)
        xp = layer_norm(ALPHA * xp + mix_p, ln1_g[l], ln1_b[l])
        xp = layer_norm(ALPHA * xp + hier_moe(xp, *fw), ln2_g[l], ln2_b[l])
        mix_s, ss = sample_mixer(xs, cache_meta_k[l], cache_meta_v[l], cache_win_k[l], cache_win_v[l],
                                 state_gla[l], state_conv[l], *mw)
        xs = layer_norm(ALPHA * xs + mix_s, ln1_g[l], ln1_b[l])
        xs = layer_norm(ALPHA * xs + hier_moe(xs, *fw), ln2_g[l], ln2_b[l])
        for lst, a in zip(st_p, sp):
            lst.append(a)
        for lst, a in zip(st_s, ss):
            lst.append(a)
    meta_k_p, meta_v_p, win_k_p, win_v_p, gla_p, conv_p = [jnp.stack(a) for a in st_p]
    win_k_s, win_v_s, gla_s, conv_s = [jnp.stack(a) for a in st_s]
    y_prompt = xp[:, N_META:]
    y_sample = xs
    return (y_prompt, y_sample, meta_k_p, meta_v_p, win_k_p, win_v_p, win_k_s, win_v_s, gla_p, gla_s, conv_p, conv_s)
```

```python
import functools

import jax
import jax.numpy as jnp
from jax import lax
from jax.experimental import pallas as pl
from jax.experimental.pallas import tpu as pltpu

F32 = jnp.float32
BF16 = jnp.bfloat16

D_MODEL = 1024
DEPTH = 2
PAST_LEN = 16384
N_META = 16
HEAD_DIM = 64
A_WIDTH = 512
A_HEADS = 8
A_KV_HEADS = 2
A_GROUP = 4
WINDOW = 128
BLOCK = 128
ROPE_THETA = 10000.0
B_WIDTH = 256
B_HEADS = 4
B_DV = 64
B_DK = 32
B_RANK = 16
GATE_TAU = 16.0
GLA_CHUNK = 64
C_WIDTH = 256
CONV_W = 31
N_GROUPS = 4
EXP_PER_GROUP = 4
N_EXPERTS = 16
D_EXPERT = 256
ALPHA = (2 * DEPTH) ** 0.25
LN_EPS = 1e-5

LANES = 128
META_PAD = BLOCK - N_META
KV_WIDTH = A_KV_HEADS * HEAD_DIM
GK_WIDTH = B_HEADS * B_DK
C_QA = 0
C_KA = C_QA + A_WIDTH
C_VA = C_KA + KV_WIDTH
C_QB = C_VA + KV_WIDTH
C_KB = C_QB + GK_WIDTH
C_VB = C_KB + GK_WIDTH
C_RB = C_VB + B_WIDTH
C_AB = C_RB + B_WIDTH
C_CG = C_AB + LANES
PROJ_PAD_WIDTH = C_CG + 2 * C_WIDTH
NEG = -1e30
VMEM_LIMIT = 56 * 1024 * 1024


def _dot(a, b):
    return jnp.dot(a, b, preferred_element_type=F32)


def _dot_nt(a, b):
    return lax.dot_general(a, b, (((1,), (1,)), ((), ())), preferred_element_type=F32)


def _dot_tn(a, b):
    return lax.dot_general(a, b, (((0,), (0,)), ((), ())), preferred_element_type=F32)


def _ln_rows(x, g, b):
    xc = x - jnp.mean(x, -1, keepdims=True)
    var = jnp.mean(xc * xc, -1, keepdims=True)
    return xc * lax.rsqrt(var + LN_EPS) * g + b


def _silu(x):
    return x * jax.nn.sigmoid(x)


def _split3(x):
    hi = x.astype(BF16)
    r1 = x - hi.astype(F32)
    mid = r1.astype(BF16)
    lo = (r1 - mid.astype(F32)).astype(BF16)
    return hi, mid, lo


def _params(*sem):
    return pltpu.CompilerParams(dimension_semantics=sem, vmem_limit_bytes=VMEM_LIMIT)


def _ln_kernel(x_ref, g_ref, b_ref, o_ref):
    o_ref[...] = _ln_rows(x_ref[...], g_ref[...], b_ref[...])


def _layer_norm(x, g, b, tm):
    n = x.shape[0]
    return pl.pallas_call(
        _ln_kernel,
        grid=(n // tm,),
        in_specs=[pl.BlockSpec((tm, D_MODEL), lambda i: (i, 0)),
                  pl.BlockSpec((1, D_MODEL), lambda i: (0, 0)),
                  pl.BlockSpec((1, D_MODEL), lambda i: (0, 0))],
        out_specs=pl.BlockSpec((tm, D_MODEL), lambda i: (i, 0)),
        out_shape=jax.ShapeDtypeStruct((n, D_MODEL), F32),
        compiler_params=_params("parallel"),
        name="ln_in",
    )(x, g, b)


def _proj_in_kernel(x_ref, w_ref, wa_ref, ba_ref, cos_ref, sin_ref,
                    qa_ref, ka_ref, va_ref, qg_ref, kg_ref, la_ref, vg_ref, rg_ref, u_ref, *, n_pad):
    xb = x_ref[...].astype(BF16)
    tm = xb.shape[0]
    cos = cos_ref[...]
    sin = sin_ref[...]
    lane = lax.broadcasted_iota(jnp.int32, (tm, LANES), 1)
    first_half = (lane & (HEAD_DIM // 2)) == 0

    def rope(z):
        rot = jnp.where(first_half, pltpu.roll(z, LANES - HEAD_DIM // 2, 1), pltpu.roll(z, HEAD_DIM // 2, 1))
        return z * cos + rot * sin

    if n_pad:
        valid = (lax.broadcasted_iota(jnp.int32, (tm, 1), 0) >= n_pad).astype(F32)
    else:
        valid = None

    za = _dot(xb, w_ref[:, C_QA:C_QB])
    for c in range(A_WIDTH // LANES):
        zq = za[:, c * LANES:(c + 1) * LANES]
        qa_ref[:, c * LANES:(c + 1) * LANES] = (rope(zq) * (HEAD_DIM ** -0.5)).astype(BF16)
    ka_ref[...] = rope(za[:, C_KA:C_VA])
    va_ref[...] = za[:, C_VA:C_QB]

    zb = _dot(xb, w_ref[:, C_QB:C_CG])
    o = C_QB
    qg_ref[...] = zb[:, C_QB - o:C_KB - o] * (B_DK ** -0.5)
    kg = zb[:, C_KB - o:C_VB - o]
    vg_ref[...] = zb[:, C_VB - o:C_RB - o]
    rg_ref[...] = zb[:, C_RB - o:C_AB - o]
    ab = zb[:, C_AB - o:C_CG - o].astype(BF16)
    xa = _dot(ab, wa_ref[...]) + ba_ref[...]
    la = (jnp.minimum(xa, 0.0) - jnp.log(1.0 + jnp.exp(-jnp.abs(xa)))) * (1.0 / GATE_TAU)

    zc = _dot(xb, w_ref[:, C_CG:PROJ_PAD_WIDTH])
    u = zc[:, :C_WIDTH] * jax.nn.sigmoid(zc[:, C_WIDTH:])
    if valid is not None:
        kg = kg * valid
        la = la * valid
        u = u * valid
    kg_ref[...] = kg
    la_ref[...] = la
    u_ref[...] = u


def _proj_in(h, w, wa, ba, cos, sin, *, tm, n_pad):
    n = h.shape[0]
    tb = cos.shape[0] // tm
    row = lambda width: pl.BlockSpec((tm, width), lambda i: (i, 0))
    const = lambda shape: pl.BlockSpec(shape, lambda i: (0, 0))
    tab = pl.BlockSpec((tm, LANES), lambda i: (i % tb, 0))
    widths = (A_WIDTH, KV_WIDTH, KV_WIDTH, GK_WIDTH, GK_WIDTH, GK_WIDTH, B_WIDTH, B_WIDTH, C_WIDTH)
    dtypes = (BF16,) + (F32,) * 8
    return pl.pallas_call(
        functools.partial(_proj_in_kernel, n_pad=n_pad),
        grid=(n // tm,),
        in_specs=[row(D_MODEL), const((D_MODEL, PROJ_PAD_WIDTH)), const((LANES, GK_WIDTH)), const((1, GK_WIDTH)),
                  tab, tab],
        out_specs=[row(wd) for wd in widths],
        out_shape=[jax.ShapeDtypeStruct((n, wd), dt) for wd, dt in zip(widths, dtypes)],
        compiler_params=_params("parallel"),
        name="proj_in",
    )(h, w, wa, ba, cos, sin)


def _attn_kernel(sink_ref, q_ref, km_ref, vm_ref, *rest, meta_mode):
    if meta_mode:
        (o_ref,) = rest
        k_all = km_ref[...]
        v_all = vm_ref[...]
    else:
        kp_ref, vp_ref, kc_ref, vc_ref, o_ref = rest
        k_all = jnp.concatenate([km_ref[...], kp_ref[...], kc_ref[...]], axis=0)
        v_all = jnp.concatenate([vm_ref[...], vp_ref[...], vc_ref[...]], axis=0)
    nk = k_all.shape[0]
    nq = A_GROUP * BLOCK
    r = lax.broadcasted_iota(jnp.int32, (nq, nk), 0)
    c = lax.broadcasted_iota(jnp.int32, (nq, nk), 1)
    qi = r & (BLOCK - 1)
    ki = c & (BLOCK - 1)
    if meta_mode:
        ok = (ki >= META_PAD) & (ki <= qi)
    else:
        seg = c >> 7
        prev_shift = jnp.where(pl.program_id(1) >= 1, 0, 2 * BLOCK)
        ok = (((seg == 0) & (ki >= META_PAD)) | ((seg == 1) & (ki >= qi + prev_shift))
              | ((seg == 2) & (ki <= qi)))
    bias = jnp.where(ok, 0.0, NEG)
    q = q_ref[...]
    k_all = k_all.astype(BF16)
    v_all = v_all.astype(BF16)
    for kvh in range(A_KV_HEADS):
        heads = [kvh * A_GROUP + g for g in range(A_GROUP)]
        qs = jnp.concatenate([q[:, h * HEAD_DIM:(h + 1) * HEAD_DIM] for h in heads], axis=0)
        kk = k_all[:, kvh * HEAD_DIM:(kvh + 1) * HEAD_DIM]
        vv = v_all[:, kvh * HEAD_DIM:(kvh + 1) * HEAD_DIM]
        s = _dot_nt(qs, kk) + bias
        sink = sink_ref[kvh * nq:(kvh + 1) * nq, :]
        m = jnp.maximum(jnp.max(s, axis=1, keepdims=True), sink)
        p = jnp.exp(s - m)
        denom = jnp.sum(p, axis=1, keepdims=True) + jnp.exp(sink - m)
        o = _dot(p.astype(BF16), vv) / denom
        for g, h in enumerate(heads):
            o_ref[:, h * HEAD_DIM:(h + 1) * HEAD_DIM] = o[g * BLOCK:(g + 1) * BLOCK].astype(BF16)


def _attn_prompt(sink_col, q, k, v, k_small, v_small, nbatch):
    n = q.shape[0]
    nblk = n // BLOCK // nbatch
    kvs = pl.BlockSpec((BLOCK, KV_WIDTH), lambda b, j: (b * nblk + j, 0))
    kvp = pl.BlockSpec((BLOCK, KV_WIDTH), lambda b, j: (b * nblk + jnp.maximum(j - 1, 0), 0))
    kvm = pl.BlockSpec((BLOCK, KV_WIDTH), lambda b, j: (0, 0))
    return pl.pallas_call(
        functools.partial(_attn_kernel, meta_mode=False),
        grid=(nbatch, nblk),
        in_specs=[pl.BlockSpec((A_HEADS * BLOCK, 1), lambda b, j: (0, 0)),
                  pl.BlockSpec((BLOCK, A_WIDTH), lambda b, j: (b * nblk + j, 0)),
                  kvm, kvm, kvp, kvp, kvs, kvs],
        out_specs=pl.BlockSpec((BLOCK, A_WIDTH), lambda b, j: (b * nblk + j, 0)),
        out_shape=jax.ShapeDtypeStruct((n, A_WIDTH), BF16),
        compiler_params=_params("parallel", "parallel"),
        name="attn_prompt",
    )(sink_col, q, k_small, v_small, k, v, k, v)


def _attn_meta(sink_col, q_small, k_small, v_small):
    blk = lambda width: pl.BlockSpec((BLOCK, width), lambda i: (0, 0))
    return pl.pallas_call(
        functools.partial(_attn_kernel, meta_mode=True),
        grid=(1,),
        in_specs=[pl.BlockSpec((A_HEADS * BLOCK, 1), lambda i: (0, 0)), blk(A_WIDTH), blk(KV_WIDTH), blk(KV_WIDTH)],
        out_specs=blk(A_WIDTH),
        out_shape=jax.ShapeDtypeStruct((BLOCK, A_WIDTH), BF16),
        compiler_params=_params("arbitrary"),
        name="attn_meta",
    )(sink_col, q_small, k_small, v_small)


SAMPLE_BLOCK = 8
HEAD_ROWS = 16


def _attn_sample_kernel(sink_ref, q_ref, kn_ref, vn_ref, mk_ref, mv_ref, wk_ref, wv_ref,
                        o_ref, nwk_ref, nwv_ref):
    sb = q_ref.shape[0]
    kn = kn_ref[...]
    vn = vn_ref[...]
    rep = 8
    k_ext = jnp.concatenate([mk_ref[...], wk_ref[...], jnp.broadcast_to(kn, (sb, rep, KV_WIDTH))], axis=1)
    v_ext = jnp.concatenate([mv_ref[...], wv_ref[...], jnp.broadcast_to(vn, (sb, rep, KV_WIDTH))], axis=1)
    nk = k_ext.shape[1]
    n_real = nk - rep + 1
    s = jnp.einsum('bhd,bjd->bhj', q_ref[...], k_ext.astype(BF16), preferred_element_type=F32)
    col = lax.broadcasted_iota(jnp.int32, s.shape, 2)
    s = jnp.where(col < n_real, s, NEG)
    sink = sink_ref[...][None]
    m = jnp.maximum(jnp.max(s, axis=2, keepdims=True), sink)
    p = jnp.exp(s - m)
    denom = jnp.sum(p, axis=2, keepdims=True) + jnp.exp(sink - m)
    o = jnp.einsum('bhj,bjd->bhd', p.astype(BF16), v_ext.astype(BF16), preferred_element_type=F32)
    o_ref[...] = o / denom
    nwin = wk_ref.shape[1]
    nwk_ref[:, 0:nwin - 1, :] = wk_ref[:, 1:nwin, :]
    nwk_ref[:, nwin - 1:nwin, :] = kn
    nwv_ref[:, 0:nwin - 1, :] = wv_ref[:, 1:nwin, :]
    nwv_ref[:, nwin - 1:nwin, :] = vn


def _attn_sample(sink_col, q_bd, k_new, v_new, meta_k, meta_v, win_k, win_v):
    ns = q_bd.shape[0]
    sb = SAMPLE_BLOCK
    nwin = win_k.shape[1]
    blk = lambda rows: pl.BlockSpec((sb, rows, KV_WIDTH), lambda i: (i, 0, 0))
    return pl.pallas_call(
        _attn_sample_kernel,
        grid=(ns // sb,),
        in_specs=[pl.BlockSpec((HEAD_ROWS, 1), lambda i: (0, 0)),
                  blk(HEAD_ROWS), blk(1), blk(1), blk(N_META), blk(N_META), blk(nwin), blk(nwin)],
        out_specs=[blk(HEAD_ROWS), blk(nwin), blk(nwin)],
        out_shape=[jax.ShapeDtypeStruct((ns, HEAD_ROWS, KV_WIDTH), F32),
                   jax.ShapeDtypeStruct((ns, nwin, KV_WIDTH), F32),
                   jax.ShapeDtypeStruct((ns, nwin, KV_WIDTH), F32)],
        compiler_params=_params("parallel"),
        name="attn_sample",
    )(sink_col, q_bd, k_new, v_new, meta_k, meta_v, win_k, win_v)


def _seg_mean_sq(o, mseg):
    sq = o * o
    hi = sq.astype(BF16)
    lo = (sq - hi.astype(F32)).astype(BF16)
    return _dot(hi, mseg) + _dot(lo, mseg)


def _gla_kernel(q_ref, k_ref, la_ref, v_ref, r_ref, s0_ref, gg_ref, ob_ref, st_ref, st_scr, *, n_chunks):
    cs = GLA_CHUNK
    st_scr[...] = s0_ref[0]
    iota = lambda shape, axis: lax.broadcasted_iota(jnp.int32, shape, axis)
    lg_cs, lg_dk, lg_dv = cs.bit_length() - 1, B_DK.bit_length() - 1, B_DV.bit_length() - 1
    tri = jnp.where(iota((cs, cs), 1) <= iota((cs, cs), 0), 1.0, 0.0).astype(BF16)
    kd_mask = (iota((B_HEADS * cs, GK_WIDTH), 0) >> lg_cs) == (iota((B_HEADS * cs, GK_WIDTH), 1) >> lg_dk)
    vd_mask = (iota((B_HEADS * cs, B_WIDTH), 0) >> lg_cs) == (iota((B_HEADS * cs, B_WIDTH), 1) >> lg_dv)
    st_mask = (iota((B_WIDTH, GK_WIDTH), 0) >> lg_dv) == (iota((B_WIDTH, GK_WIDTH), 1) >> lg_dk)
    causal = (iota((cs, B_HEADS * cs), 1) & (cs - 1)) <= iota((cs, B_HEADS * cs), 0)
    mseg = jnp.where((iota((B_WIDTH, B_WIDTH), 0) >> lg_dv) == (iota((B_WIDTH, B_WIDTH), 1) >> lg_dv),
                     1.0 / B_DV, 0.0).astype(BF16)
    gg = gg_ref[...]

    def body(c, carry):
        rows = pl.ds(pl.multiple_of(c * cs, cs), cs)
        q = q_ref[0, rows, :]
        k = k_ref[0, rows, :]
        v = v_ref[0, rows, :]
        la_h, la_m, la_l = _split3(la_ref[0, rows, :])
        g = _dot(tri, la_h) + _dot(tri, la_m) + _dot(tri, la_l)
        g_end = g[cs - 1:cs, :]
        g_mid = g[cs // 2 - 1:cs // 2, :]
        qt = (q * jnp.exp(g - g_mid)).astype(BF16)
        kt = k * jnp.exp(g_mid - g)
        kbd = jnp.where(kd_mask, jnp.concatenate([kt] * B_HEADS, axis=0), 0.0).astype(BF16)
        att = jnp.where(causal, _dot_nt(qt, kbd), 0.0)
        vbd = jnp.where(vd_mask, jnp.concatenate([v] * B_HEADS, axis=0), 0.0).astype(BF16)
        st = st_scr[...]
        o = _dot(att.astype(BF16), vbd) + _dot_nt((q * jnp.exp(g)).astype(BF16), st.astype(BF16))
        kh = (k * jnp.exp(g_end - g)).astype(BF16)
        upd = _dot_tn(v.astype(BF16), kh)
        st_scr[...] = st * jnp.exp(g_end) + jnp.where(st_mask, upd, 0.0)
        on = o * lax.rsqrt(_seg_mean_sq(o, mseg) + LN_EPS)
        ob_ref[0, rows, :] = (on * gg * _silu(r_ref[0, rows, :])).astype(BF16)
        return carry

    lax.fori_loop(0, n_chunks, body, 0)
    st_ref[0] = st_scr[...]


def _gla_prompt(q, k, la, v, r, s0, gg, nbatch):
    t = q.shape[0] // nbatch
    seq = lambda a: a.reshape(nbatch, t, a.shape[-1])
    blk = lambda width: pl.BlockSpec((1, t, width), lambda b: (b, 0, 0))
    return pl.pallas_call(
        functools.partial(_gla_kernel, n_chunks=t // GLA_CHUNK),
        grid=(nbatch,),
        in_specs=[blk(GK_WIDTH), blk(GK_WIDTH), blk(GK_WIDTH), blk(B_WIDTH), blk(B_WIDTH),
                  pl.BlockSpec((1, B_WIDTH, GK_WIDTH), lambda b: (0, 0, 0)),
                  pl.BlockSpec((1, B_WIDTH), lambda b: (0, 0))],
        out_specs=[blk(B_WIDTH), pl.BlockSpec((1, B_WIDTH, GK_WIDTH), lambda b: (b, 0, 0))],
        out_shape=[jax.ShapeDtypeStruct((nbatch, t, B_WIDTH), BF16),
                   jax.ShapeDtypeStruct((nbatch, B_WIDTH, GK_WIDTH), F32)],
        scratch_shapes=[pltpu.VMEM((B_WIDTH, GK_WIDTH), F32)],
        compiler_params=_params("parallel"),
        name="gla_prompt",
    )(seq(q), seq(k), seq(la), seq(v), seq(r), s0, gg)


def _gla_sample_kernel(q_ref, k_ref, la_ref, v_ref, r_ref, s_ref, gg_ref, ob_ref, sn_ref):
    sb = q_ref.shape[0]
    qt = q_ref[...].T
    kt = k_ref[...].T
    at = jnp.exp(la_ref[...]).T
    v = v_ref[...]
    gg = gg_ref[...]
    for j in range(sb):
        vj = jnp.concatenate(
            [jnp.broadcast_to(v[j:j + 1, h * B_DV:(h + 1) * B_DV], (B_DK, B_DV)) for h in range(B_HEADS)], axis=0)
        s_new = at[:, j:j + 1] * s_ref[j] + kt[:, j:j + 1] * vj
        sn_ref[j] = s_new
        o = jnp.sum((qt[:, j:j + 1] * s_new).reshape(B_HEADS, B_DK, B_DV), axis=1)
        on = o * lax.rsqrt(jnp.mean(o * o, axis=-1, keepdims=True) + LN_EPS)
        ob_ref[j] = on * gg * _silu(r_ref[j])


def _gla_sample(q, k, la, v, r, state, gg):
    ns = q.shape[0]
    sb = SAMPLE_BLOCK
    row = lambda width: pl.BlockSpec((sb, width), lambda i: (i, 0))
    return pl.pallas_call(
        _gla_sample_kernel,
        grid=(ns // sb,),
        in_specs=[row(GK_WIDTH), row(GK_WIDTH), row(GK_WIDTH), row(B_WIDTH),
                  pl.BlockSpec((sb, B_HEADS, B_DV), lambda i: (i, 0, 0)),
                  pl.BlockSpec((sb, GK_WIDTH, B_DV), lambda i: (i, 0, 0)),
                  pl.BlockSpec((1, B_DV), lambda i: (0, 0))],
        out_specs=[pl.BlockSpec((sb, B_HEADS, B_DV), lambda i: (i, 0, 0)),
                   pl.BlockSpec((sb, GK_WIDTH, B_DV), lambda i: (i, 0, 0))],
        out_shape=[jax.ShapeDtypeStruct((ns, B_HEADS, B_DV), F32),
                   jax.ShapeDtypeStruct((ns, GK_WIDTH, B_DV), F32)],
        compiler_params=_params("parallel"),
        name="gla_sample",
    )(q, k, la, v, r, state, gg)


CONV_HIST = 32
CONV_CHUNK = 64


def _conv_post(y, dwb, lng, lnb, pww, pwb):
    y = _silu(_ln_rows(y + dwb, lng, lnb))
    return _dot(y.astype(BF16), pww) + pwb


def _conv_kernel(u_ref, hist_ref, dww_ref, dwb_ref, lng_ref, lnb_ref, pww_ref, pwb_ref,
                 oc_ref, tail_ref, ubuf, *, n_chunks):
    t = u_ref.shape[1]
    ubuf[0:CONV_HIST, :] = hist_ref[...]
    ubuf[CONV_HIST:CONV_HIST + t, :] = u_ref[0]
    off = CONV_HIST - (CONV_W - 1)

    def body(c, carry):
        r0 = pl.multiple_of(c * CONV_CHUNK, CONV_CHUNK)
        win = ubuf[pl.ds(r0, CONV_CHUNK + CONV_HIST), :]
        acc = jnp.zeros((CONV_CHUNK, C_WIDTH), F32)
        for j in range(CONV_W):
            acc = acc + win[j + off:j + off + CONV_CHUNK, :] * dww_ref[j:j + 1, :]
        oc = _conv_post(acc, dwb_ref[...], lng_ref[...], lnb_ref[...], pww_ref[...], pwb_ref[...])
        oc_ref[0, pl.ds(r0, CONV_CHUNK), :] = oc.astype(BF16)
        return carry

    lax.fori_loop(0, n_chunks, body, 0)
    tail_ref[0] = ubuf[t:t + CONV_HIST, :]


def _conv_prompt(u, hist, dww, dwb, lng, lnb, pww, pwb, nbatch):
    t = u.shape[0] // nbatch
    const = lambda shape: pl.BlockSpec(shape, lambda b: (0,) * len(shape))
    return pl.pallas_call(
        functools.partial(_conv_kernel, n_chunks=t // CONV_CHUNK),
        grid=(nbatch,),
        in_specs=[pl.BlockSpec((1, t, C_WIDTH), lambda b: (b, 0, 0)), const((CONV_HIST, C_WIDTH)),
                  const((CONV_HIST, C_WIDTH)), const((1, C_WIDTH)), const((1, C_WIDTH)), const((1, C_WIDTH)),
                  const((C_WIDTH, C_WIDTH)), const((1, C_WIDTH))],
        out_specs=[pl.BlockSpec((1, t, C_WIDTH), lambda b: (b, 0, 0)),
                   pl.BlockSpec((1, CONV_HIST, C_WIDTH), lambda b: (b, 0, 0))],
        out_shape=[jax.ShapeDtypeStruct((nbatch, t, C_WIDTH), BF16),
                   jax.ShapeDtypeStruct((nbatch, CONV_HIST, C_WIDTH), F32)],
        scratch_shapes=[pltpu.VMEM((CONV_HIST + t, C_WIDTH), F32)],
        compiler_params=_params("parallel"),
        name="conv_prompt",
    )(u.reshape(nbatch, t, C_WIDTH), hist, dww, dwb, lng, lnb, pww, pwb)


def _conv_sample_kernel(hist_ref, u_ref, dww_ref, dwb_ref, lng_ref, lnb_ref, pww_ref, pwb_ref, oc_ref, tail_ref):
    nh = CONV_W - 1
    hist = hist_ref[...]
    u = u_ref[...]
    y = jnp.sum(hist * dww_ref[0:nh, :][None], axis=1) + u[:, 0, :] * dww_ref[nh:nh + 1, :]
    oc = _conv_post(y, dwb_ref[...], lng_ref[...], lnb_ref[...], pww_ref[...], pwb_ref[...])
    oc_ref[...] = oc.astype(BF16)
    tail_ref[:, 0:nh - 1, :] = hist_ref[:, 1:nh, :]
    tail_ref[:, nh - 1:nh, :] = u


def _conv_sample(hist, u, dww, dwb, lng, lnb, pww, pwb):
    ns = hist.shape[0]
    nh = CONV_W - 1
    const = lambda shape: pl.BlockSpec(shape, lambda i: (0,) * len(shape))
    return pl.pallas_call(
        _conv_sample_kernel,
        grid=(1,),
        in_specs=[const((ns, nh, C_WIDTH)), const((ns, 1, C_WIDTH)),
                  const((CONV_HIST, C_WIDTH)), const((1, C_WIDTH)), const((1, C_WIDTH)), const((1, C_WIDTH)),
                  const((C_WIDTH, C_WIDTH)), const((1, C_WIDTH))],
        out_specs=[const((ns, C_WIDTH)), const((ns, nh, C_WIDTH))],
        out_shape=[jax.ShapeDtypeStruct((ns, C_WIDTH), BF16),
                   jax.ShapeDtypeStruct((ns, nh, C_WIDTH), F32)],
        compiler_params=_params("arbitrary"),
        name="conv_sample",
    )(hist, u, dww, dwb, lng, lnb, pww, pwb)


def _out_proj_kernel(oa_ref, ob_ref, oc_ref, h_ref, w_ref, g_ref, b_ref, o_ref):
    mix = (_dot(oa_ref[...], w_ref[0:A_WIDTH, :])
           + _dot(ob_ref[...], w_ref[A_WIDTH:A_WIDTH + B_WIDTH, :])
           + _dot(oc_ref[...], w_ref[A_WIDTH + B_WIDTH:D_MODEL, :]))
    o_ref[...] = _ln_rows(ALPHA * h_ref[...] + mix, g_ref[...], b_ref[...])


def _out_proj(oa, ob, oc, h, w, g, b, tm):
    n = h.shape[0]
    row = lambda width: pl.BlockSpec((tm, width), lambda i: (i, 0))
    const = lambda shape: pl.BlockSpec(shape, lambda i: (0, 0))
    return pl.pallas_call(
        _out_proj_kernel,
        grid=(n // tm,),
        in_specs=[row(A_WIDTH), row(B_WIDTH), row(C_WIDTH), row(D_MODEL),
                  const((D_MODEL, D_MODEL)), const((1, D_MODEL)), const((1, D_MODEL))],
        out_specs=row(D_MODEL),
        out_shape=jax.ShapeDtypeStruct((n, D_MODEL), F32),
        compiler_params=_params("parallel"),
        name="out_proj",
    )(oa, ob, oc, h, w, g, b)


ROUTE_LANE0 = N_GROUPS


def _route(x, wr_hi, wr_lo, br):
    x_hi = x.astype(BF16)
    x_lo = (x - x_hi.astype(F32)).astype(BF16)
    logits = _dot(x_hi, wr_hi) + _dot(x_lo, wr_hi) + _dot(x_hi, wr_lo) + br
    lane = lax.broadcasted_iota(jnp.int32, logits.shape, 1).astype(F32)
    far = 1e3
    glm = jnp.where(lane < N_GROUPS, logits, NEG)
    gmax = jnp.max(glm, axis=1, keepdims=True)
    gi = jnp.min(jnp.where(glm == gmax, lane, far), axis=1, keepdims=True)
    p_grp = 1.0 / jnp.sum(jnp.exp(glm - gmax), axis=1, keepdims=True)
    lo = ROUTE_LANE0 + EXP_PER_GROUP * gi
    in_sel = (lane >= lo) & (lane < lo + EXP_PER_GROUP)
    elm = jnp.where(in_sel, logits, NEG)
    v1 = jnp.max(elm, axis=1, keepdims=True)
    i1 = jnp.min(jnp.where(elm == v1, lane, far), axis=1, keepdims=True)
    elm2 = jnp.where(lane == i1, NEG, elm)
    v2 = jnp.max(elm2, axis=1, keepdims=True)
    i2 = jnp.min(jnp.where((elm2 == v2) & in_sel & (lane != i1), lane, far), axis=1, keepdims=True)
    t = jnp.exp(v2 - v1)
    w1 = p_grp / (1.0 + t)
    w2 = w1 * t
    return jnp.where(lane == i1, w1, 0.0) + jnp.where(lane == i2, w2, 0.0)


def _moe_dense_kernel(x_ref, wrh_ref, wrl_ref, br_ref, wg_ref, wu_ref, wd_ref, g_ref, b_ref, o_ref,
                      acc_ref, dw_ref):
    e = pl.program_id(1)

    @pl.when(e == 0)
    def _():
        acc_ref[...] = jnp.zeros_like(acc_ref)
        dw_ref[...] = _route(x_ref[...], wrh_ref[...], wrl_ref[...], br_ref[...])

    xb = x_ref[...].astype(BF16)
    he = _silu(_dot(xb, wg_ref[0])) * _dot(xb, wu_ref[0])
    lane = lax.broadcasted_iota(jnp.int32, dw_ref.shape, 1)
    w_e = jnp.sum(jnp.where(lane == e + ROUTE_LANE0, dw_ref[...], 0.0), axis=1, keepdims=True)
    acc_ref[...] += w_e * _dot(he.astype(BF16), wd_ref[0])

    @pl.when(e == N_EXPERTS - 1)
    def _():
        o_ref[...] = _ln_rows(ALPHA * x_ref[...] + acc_ref[...], g_ref[...], b_ref[...])


def _moe(x, wrh, wrl, br, wg, wu, wd, g, b, tm):
    n = x.shape[0]
    const = lambda shape: pl.BlockSpec(shape, lambda i, e: (0, 0))
    return pl.pallas_call(
        _moe_dense_kernel,
        grid=(n // tm, N_EXPERTS),
        in_specs=[pl.BlockSpec((tm, D_MODEL), lambda i, e: (i, 0)),
                  const((D_MODEL, LANES)), const((D_MODEL, LANES)), const((1, LANES)),
                  pl.BlockSpec((1, D_MODEL, D_EXPERT), lambda i, e: (e, 0, 0)),
                  pl.BlockSpec((1, D_MODEL, D_EXPERT), lambda i, e: (e, 0, 0)),
                  pl.BlockSpec((1, D_EXPERT, D_MODEL), lambda i, e: (e, 0, 0)),
                  const((1, D_MODEL)), const((1, D_MODEL))],
        out_specs=pl.BlockSpec((tm, D_MODEL), lambda i, e: (i, 0)),
        out_shape=jax.ShapeDtypeStruct((n, D_MODEL), F32),
        scratch_shapes=[pltpu.VMEM((tm, D_MODEL), F32), pltpu.VMEM((tm, LANES), F32)],
        compiler_params=_params("parallel", "arbitrary"),
        name="moe",
    )(x, wrh, wrl, br, wg, wu, wd, g, b)


def _rope_tables(pos):
    half = HEAD_DIM // 2
    inv = ROPE_THETA ** (-jnp.arange(half, dtype=F32) / half)
    ang = pos.astype(F32)[:, None] * inv[None, :]
    cos, sin = jnp.cos(ang), jnp.sin(ang)
    cos_t = jnp.concatenate([cos, cos] * (LANES // HEAD_DIM), axis=1)
    sin_t = jnp.concatenate([-sin, sin] * (LANES // HEAD_DIM), axis=1)
    return cos_t, sin_t


def _row(v):
    return v.reshape(1, -1)


def kernel(x_prompt, x_sample, cache_meta_k, cache_meta_v, cache_win_k, cache_win_v, state_gla, state_conv,
           meta_tokens, ln_in_g, ln_in_b, w_in, attn_sink, w_alpha, b_alpha, gla_norm_g,
           conv_dw_w, conv_dw_b, conv_ln_g, conv_ln_b, conv_pw_w, conv_pw_b, w_out, ln1_g, ln1_b,
           w_router_group, b_router_group, w_router_expert, b_router_expert,
           w_exp_gate, w_exp_up, w_exp_down, ln2_g, ln2_b):
    nb, seq, d = x_prompt.shape
    ns = x_sample.shape[0]
    nwin = cache_win_k.shape[2]
    n_big = nb * seq
    n_small = BLOCK + ns
    tm_big = min(512, seq)

    small_in = jnp.concatenate([jnp.zeros((META_PAD, d), F32), meta_tokens.astype(F32),
                                x_sample.reshape(ns, d)], axis=0)
    hb = _layer_norm(x_prompt.reshape(n_big, d), _row(ln_in_g), _row(ln_in_b), tm_big)
    hs = _layer_norm(small_in, _row(ln_in_g), _row(ln_in_b), n_small)

    cos_b, sin_b = _rope_tables(N_META + jnp.arange(seq))
    pos_small = jnp.concatenate([jnp.maximum(jnp.arange(BLOCK) - META_PAD, 0),
                                 jnp.full((ns,), PAST_LEN, jnp.int32)])
    cos_s, sin_s = _rope_tables(pos_small)

    outs = [[] for _ in range(12)]
    zeros_hist = jnp.zeros((CONV_HIST, C_WIDTH), F32)
    zeros_state = jnp.zeros((1, B_WIDTH, GK_WIDTH), F32)
    pad_cols = jnp.zeros((d, LANES - B_RANK), F32)

    for l in range(DEPTH):
        wi = w_in[l]
        w_pad = jnp.concatenate([wi[:, :C_AB + B_RANK], pad_cols, wi[:, C_AB + B_RANK:]], axis=1).astype(BF16)
        wa = jnp.concatenate([w_alpha[l], jnp.zeros((LANES - B_RANK, GK_WIDTH), F32)], axis=0).astype(BF16)
        ba = _row(b_alpha[l])
        sink = attn_sink[l].astype(F32)
        sink_rows = jnp.repeat(sink, BLOCK)[:, None]
        sink_col = jnp.concatenate([sink, jnp.zeros((HEAD_ROWS - A_HEADS,), F32)])[:, None]
        gg_t = _row(jnp.tile(gla_norm_g[l], B_HEADS))
        gg_h = _row(gla_norm_g[l])
        dww = jnp.concatenate([conv_dw_w[l], jnp.zeros((CONV_HIST - CONV_W, C_WIDTH), F32)], axis=0)
        dwb, clg, clb = _row(conv_dw_b[l]), _row(conv_ln_g[l]), _row(conv_ln_b[l])
        pww, pwb = conv_pw_w[l].astype(BF16), _row(conv_pw_b[l])
        wo = w_out[l].astype(BF16)
        wr = jnp.concatenate([w_router_group[l], w_router_expert[l],
                              jnp.zeros((d, LANES - N_GROUPS - N_EXPERTS), F32)], axis=1)
        wr_hi = wr.astype(BF16)
        wr_lo = (wr - wr_hi.astype(F32)).astype(BF16)
        br = _row(jnp.concatenate([b_router_group[l], b_router_expert[l],
                                   jnp.zeros((LANES - N_GROUPS - N_EXPERTS,), F32)]))
        wg, wu, wd = w_exp_gate[l].astype(BF16), w_exp_up[l].astype(BF16), w_exp_down[l].astype(BF16)

        qa_b, ka_b, va_b, qg_b, kg_b, la_b, vg_b, rg_b, u_b = _proj_in(
            hb, w_pad, wa, ba, cos_b, sin_b, tm=tm_big, n_pad=0)
        qa_s, ka_s, va_s, qg_s, kg_s, la_s, vg_s, rg_s, u_s = _proj_in(
            hs, w_pad, wa, ba, cos_s, sin_s, tm=n_small, n_pad=META_PAD)

        oa_b = _attn_prompt(sink_rows, qa_b, ka_b, va_b, ka_s, va_s, nb)
        oa_m = _attn_meta(sink_rows, qa_s, ka_s, va_s)
        q_smp = qa_s[BLOCK:].reshape(ns, A_KV_HEADS, A_GROUP, 1, HEAD_DIM)
        eye = jnp.eye(A_KV_HEADS, dtype=BF16)[None, :, None, :, None]
        q_bd = (q_smp * eye).reshape(ns, A_HEADS, KV_WIDTH)
        q_bd = jnp.pad(q_bd, ((0, 0), (0, HEAD_ROWS - A_HEADS), (0, 0)))
        o_bd, nwk, nwv = _attn_sample(
            sink_col, q_bd, ka_s[BLOCK:].reshape(ns, 1, KV_WIDTH), va_s[BLOCK:].reshape(ns, 1, KV_WIDTH),
            cache_meta_k[l].reshape(ns, N_META, KV_WIDTH), cache_meta_v[l].reshape(ns, N_META, KV_WIDTH),
            cache_win_k[l].reshape(ns, nwin, KV_WIDTH), cache_win_v[l].reshape(ns, nwin, KV_WIDTH))
        o_bd = o_bd[:, :A_HEADS].reshape(ns, A_KV_HEADS, A_GROUP, A_KV_HEADS, HEAD_DIM)
        oa_smp = jnp.stack([o_bd[:, c, :, c, :] for c in range(A_KV_HEADS)], axis=1).reshape(ns, A_WIDTH)
        oa_s = jnp.concatenate([oa_m, oa_smp.astype(BF16)], axis=0)

        ob_m, st_m = _gla_prompt(qg_s[:BLOCK], kg_s[:BLOCK], la_s[:BLOCK], vg_s[:BLOCK], rg_s[:BLOCK],
                                 zeros_state, gg_t, 1)
        ob_b, st_b = _gla_prompt(qg_b, kg_b, la_b, vg_b, rg_b, st_m, gg_t, nb)
        ob_smp, s_new = _gla_sample(qg_s[BLOCK:], kg_s[BLOCK:], la_s[BLOCK:], vg_s[BLOCK:],
                                    rg_s[BLOCK:].reshape(ns, B_HEADS, B_DV),
                                    state_gla[l].reshape(ns, GK_WIDTH, B_DV), gg_h)
        ob_s = jnp.concatenate([ob_m.reshape(BLOCK, B_WIDTH), ob_smp.reshape(ns, B_WIDTH).astype(BF16)], axis=0)
        st5 = st_b.reshape(nb, B_HEADS, B_DV, B_HEADS, B_DK)
        gla_p = jnp.stack([st5[:, h, :, h, :] for h in range(B_HEADS)], axis=1).transpose(0, 1, 3, 2)

        oc_m, _ = _conv_prompt(u_s[:BLOCK], zeros_hist, dww, dwb, clg, clb, pww, pwb, 1)
        oc_b, tail_b = _conv_prompt(u_b, u_s[BLOCK - CONV_HIST:BLOCK], dww, dwb, clg, clb, pww, pwb, nb)
        oc_smp, tail_s = _conv_sample(state_conv[l], u_s[BLOCK:].reshape(ns, 1, C_WIDTH),
                                      dww, dwb, clg, clb, pww, pwb)
        oc_s = jnp.concatenate([oc_m.reshape(BLOCK, C_WIDTH), oc_smp], axis=0)

        l1g, l1b, l2g, l2b = _row(ln1_g[l]), _row(ln1_b[l]), _row(ln2_g[l]), _row(ln2_b[l])
        hb1 = _out_proj(oa_b, ob_b.reshape(n_big, B_WIDTH), oc_b.reshape(n_big, C_WIDTH), hb, wo, l1g, l1b, tm_big)
        hs1 = _out_proj(oa_s, ob_s, oc_s, hs, wo, l1g, l1b, n_small)
        hb = _moe(hb1, wr_hi, wr_lo, br, wg, wu, wd, l2g, l2b, tm_big)
        hs = _moe(hs1, wr_hi, wr_lo, br, wg, wu, wd, l2g, l2b, n_small)

        kv4 = lambda a: a.reshape(a.shape[0], a.shape[1], A_KV_HEADS, HEAD_DIM)
        meta_k = jnp.broadcast_to(ka_s[META_PAD:BLOCK][None], (nb, N_META, KV_WIDTH))
        meta_v = jnp.broadcast_to(va_s[META_PAD:BLOCK][None], (nb, N_META, KV_WIDTH))
        win_k = ka_b.reshape(nb, seq, KV_WIDTH)[:, seq - nwin:]
        win_v = va_b.reshape(nb, seq, KV_WIDTH)[:, seq - nwin:]
        layer_out = (None, None, kv4(meta_k), kv4(meta_v), kv4(win_k), kv4(win_v), kv4(nwk), kv4(nwv),
                     gla_p, s_new.reshape(ns, B_HEADS, B_DK, B_DV),
                     tail_b[:, CONV_HIST - (CONV_W - 1):], tail_s)
        for i in range(2, 12):
            outs[i].append(layer_out[i])

    y_prompt = hb.reshape(nb, seq, d)
    y_sample = hs[BLOCK:].reshape(ns, 1, d)
    return (y_prompt, y_sample) + tuple(jnp.stack(o) for o in outs[2:])
```

```python
import functools

import jax
import jax.numpy as jnp
from jax import lax
from jax.experimental import pallas as pl
from jax.experimental.pallas import tpu as pltpu

F32 = jnp.float32
BF16 = jnp.bfloat16

D_MODEL = 1024
DEPTH = 2
PAST_LEN = 16384
N_META = 16
HEAD_DIM = 64
A_WIDTH = 512
A_HEADS = 8
A_KV_HEADS = 2
A_GROUP = 4
WINDOW = 128
BLOCK = 128
ROPE_THETA = 10000.0
B_WIDTH = 256
B_HEADS = 4
B_DV = 64
B_DK = 32
B_RANK = 16
GATE_TAU = 16.0
GLA_CHUNK = 64
C_WIDTH = 256
CONV_W = 31
N_GROUPS = 4
EXP_PER_GROUP = 4
N_EXPERTS = 16
D_EXPERT = 256
ALPHA = (2 * DEPTH) ** 0.25
LN_EPS = 1e-5

LANES = 128
META_PAD = BLOCK - N_META
KV_WIDTH = A_KV_HEADS * HEAD_DIM
GK_WIDTH = B_HEADS * B_DK
C_QA = 0
C_KA = C_QA + A_WIDTH
C_VA = C_KA + KV_WIDTH
C_QB = C_VA + KV_WIDTH
C_KB = C_QB + GK_WIDTH
C_VB = C_KB + GK_WIDTH
C_RB = C_VB + B_WIDTH
C_AB = C_RB + B_WIDTH
C_CG = C_AB + LANES
PROJ_PAD_WIDTH = C_CG + 2 * C_WIDTH
NEG = -1e30
VMEM_LIMIT = 56 * 1024 * 1024


def _dot(a, b):
    return jnp.dot(a, b, preferred_element_type=F32)


def _dot_nt(a, b):
    return lax.dot_general(a, b, (((1,), (1,)), ((), ())), preferred_element_type=F32)


def _dot_tn(a, b):
    return lax.dot_general(a, b, (((0,), (0,)), ((), ())), preferred_element_type=F32)


def _ln_rows(x, g, b):
    xc = x - jnp.mean(x, -1, keepdims=True)
    var = jnp.mean(xc * xc, -1, keepdims=True)
    return xc * lax.rsqrt(var + LN_EPS) * g + b


def _silu(x):
    return x * jax.nn.sigmoid(x)


def _split3(x):
    hi = x.astype(BF16)
    r1 = x - hi.astype(F32)
    mid = r1.astype(BF16)
    lo = (r1 - mid.astype(F32)).astype(BF16)
    return hi, mid, lo


def _params(*sem):
    return pltpu.CompilerParams(dimension_semantics=sem, vmem_limit_bytes=VMEM_LIMIT)


def _ln_kernel(x_ref, g_ref, b_ref, o_ref):
    o_ref[...] = _ln_rows(x_ref[...], g_ref[...], b_ref[...])


def _layer_norm(x, g, b, tm):
    n = x.shape[0]
    return pl.pallas_call(
        _ln_kernel,
        grid=(n // tm,),
        in_specs=[pl.BlockSpec((tm, D_MODEL), lambda i: (i, 0)),
                  pl.BlockSpec((1, D_MODEL), lambda i: (0, 0)),
                  pl.BlockSpec((1, D_MODEL), lambda i: (0, 0))],
        out_specs=pl.BlockSpec((tm, D_MODEL), lambda i: (i, 0)),
        out_shape=jax.ShapeDtypeStruct((n, D_MODEL), F32),
        compiler_params=_params("parallel"),
        name="ln_in",
    )(x, g, b)


def _proj_in_kernel(x_ref, w_ref, wa_ref, ba_ref, cos_ref, sin_ref,
                    qa_ref, ka_ref, va_ref, qg_ref, kg_ref, la_ref, vg_ref, rg_ref, u_ref, *, n_pad):
    xb = x_ref[...].astype(BF16)
    tm = xb.shape[0]
    cos = cos_ref[...]
    sin = sin_ref[...]
    lane = lax.broadcasted_iota(jnp.int32, (tm, LANES), 1)
    first_half = (lane & (HEAD_DIM // 2)) == 0

    def rope(z):
        rot = jnp.where(first_half, pltpu.roll(z, LANES - HEAD_DIM // 2, 1), pltpu.roll(z, HEAD_DIM // 2, 1))
        return z * cos + rot * sin

    if n_pad:
        valid = (lax.broadcasted_iota(jnp.int32, (tm, 1), 0) >= n_pad).astype(F32)
    else:
        valid = None

    za = _dot(xb, w_ref[:, C_QA:C_QB])
    for c in range(A_WIDTH // LANES):
        zq = za[:, c * LANES:(c + 1) * LANES]
        qa_ref[:, c * LANES:(c + 1) * LANES] = (rope(zq) * (HEAD_DIM ** -0.5)).astype(BF16)
    ka_ref[...] = rope(za[:, C_KA:C_VA])
    va_ref[...] = za[:, C_VA:C_QB]

    zb = _dot(xb, w_ref[:, C_QB:C_CG])
    o = C_QB
    qg_ref[...] = zb[:, C_QB - o:C_KB - o] * (B_DK ** -0.5)
    kg = zb[:, C_KB - o:C_VB - o]
    vg_ref[...] = zb[:, C_VB - o:C_RB - o]
    rg_ref[...] = zb[:, C_RB - o:C_AB - o]
    ab = zb[:, C_AB - o:C_CG - o].astype(BF16)
    xa = _dot(ab, wa_ref[...]) + ba_ref[...]
    la = (jnp.minimum(xa, 0.0) - jnp.log(1.0 + jnp.exp(-jnp.abs(xa)))) * (1.0 / GATE_TAU)

    zc = _dot(xb, w_ref[:, C_CG:PROJ_PAD_WIDTH])
    u = zc[:, :C_WIDTH] * jax.nn.sigmoid(zc[:, C_WIDTH:])
    if valid is not None:
        kg = kg * valid
        la = la * valid
        u = u * valid
    kg_ref[...] = kg
    la_ref[...] = la
    u_ref[...] = u


def _proj_in(h, w, wa, ba, cos, sin, *, tm, n_pad):
    n = h.shape[0]
    tb = cos.shape[0] // tm
    row = lambda width: pl.BlockSpec((tm, width), lambda i: (i, 0))
    const = lambda shape: pl.BlockSpec(shape, lambda i: (0, 0))
    tab = pl.BlockSpec((tm, LANES), lambda i: (i % tb, 0))
    widths = (A_WIDTH, KV_WIDTH, KV_WIDTH, GK_WIDTH, GK_WIDTH, GK_WIDTH, B_WIDTH, B_WIDTH, C_WIDTH)
    dtypes = (BF16,) + (F32,) * 8
    return pl.pallas_call(
        functools.partial(_proj_in_kernel, n_pad=n_pad),
        grid=(n // tm,),
        in_specs=[row(D_MODEL), const((D_MODEL, PROJ_PAD_WIDTH)), const((LANES, GK_WIDTH)), const((1, GK_WIDTH)),
                  tab, tab],
        out_specs=[row(wd) for wd in widths],
        out_shape=[jax.ShapeDtypeStruct((n, wd), dt) for wd, dt in zip(widths, dtypes)],
        compiler_params=_params("parallel"),
        name="proj_in",
    )(h, w, wa, ba, cos, sin)


def _attn_kernel(sink_ref, q_ref, km_ref, vm_ref, *rest, meta_mode):
    if meta_mode:
        (o_ref,) = rest
        k_all = km_ref[...]
        v_all = vm_ref[...]
    else:
        kp_ref, vp_ref, kc_ref, vc_ref, o_ref = rest
        k_all = jnp.concatenate([kp_ref[...], kc_ref[...], km_ref[...]], axis=0)
        v_all = jnp.concatenate([vp_ref[...], vc_ref[...], vm_ref[...]], axis=0)
    nk = k_all.shape[0]
    ki = lax.broadcasted_iota(jnp.int32, (nk, BLOCK), 0)
    qi = lax.broadcasted_iota(jnp.int32, (nk, BLOCK), 1)
    if meta_mode:
        ok = ki <= qi - META_PAD
    else:
        prev_lo = qi + jnp.where(pl.program_id(1) >= 1, 0, BLOCK)
        ok = ((ki >= prev_lo) & (ki < BLOCK)) | ((ki >= BLOCK) & (ki <= qi + BLOCK)) | (ki >= 2 * BLOCK)
    bias = jnp.where(ok, 0.0, NEG)
    q = q_ref[...]
    k_all = k_all.astype(BF16)
    v_all = v_all.astype(BF16)
    for kvh in range(A_KV_HEADS):
        heads = [kvh * A_GROUP + g for g in range(A_GROUP)]
        qs = jnp.concatenate([q[:, h * HEAD_DIM:(h + 1) * HEAD_DIM] for h in heads], axis=0)
        kk = k_all[:, kvh * HEAD_DIM:(kvh + 1) * HEAD_DIM]
        vv = v_all[:, kvh * HEAD_DIM:(kvh + 1) * HEAD_DIM]
        st = _dot_nt(kk, qs)
        ps, dens = [], []
        for g, h in enumerate(heads):
            s = st[:, g * BLOCK:(g + 1) * BLOCK] + bias
            sink = sink_ref[h]
            m = jnp.maximum(jnp.max(s, axis=0, keepdims=True), sink)
            p = jnp.exp(s - m)
            dens.append(jnp.sum(p, axis=0, keepdims=True) + jnp.exp(sink - m))
            ps.append(p.astype(BF16))
        ot = _dot_tn(vv, jnp.concatenate(ps, axis=1)) / jnp.concatenate(dens, axis=1)
        for pair in range(A_GROUP // 2):
            two = jnp.concatenate([ot[:, (2 * pair) * BLOCK:(2 * pair + 1) * BLOCK],
                                   ot[:, (2 * pair + 1) * BLOCK:(2 * pair + 2) * BLOCK]], axis=0)
            h0 = heads[2 * pair]
            o_ref[:, h0 * HEAD_DIM:(h0 + 2) * HEAD_DIM] = two.T.astype(BF16)


def _attn_prompt(sink, q, k, v, k_small, v_small, nbatch):
    n = q.shape[0]
    nblk = n // BLOCK // nbatch
    kvs = pl.BlockSpec((BLOCK, KV_WIDTH), lambda b, j: (b * nblk + j, 0))
    kvp = pl.BlockSpec((BLOCK, KV_WIDTH), lambda b, j: (b * nblk + jnp.maximum(j - 1, 0), 0))
    kvm = pl.BlockSpec((N_META, KV_WIDTH), lambda b, j: (META_PAD // N_META, 0))
    return pl.pallas_call(
        functools.partial(_attn_kernel, meta_mode=False),
        grid=(nbatch, nblk),
        in_specs=[pl.BlockSpec(memory_space=pltpu.SMEM),
                  pl.BlockSpec((BLOCK, A_WIDTH), lambda b, j: (b * nblk + j, 0)),
                  kvm, kvm, kvp, kvp, kvs, kvs],
        out_specs=pl.BlockSpec((BLOCK, A_WIDTH), lambda b, j: (b * nblk + j, 0)),
        out_shape=jax.ShapeDtypeStruct((n, A_WIDTH), BF16),
        compiler_params=_params("parallel", "parallel"),
        name="attn_prompt",
    )(sink, q, k_small, v_small, k, v, k, v)


def _attn_meta(sink, q_small, k_small, v_small):
    kvm = pl.BlockSpec((N_META, KV_WIDTH), lambda i: (META_PAD // N_META, 0))
    return pl.pallas_call(
        functools.partial(_attn_kernel, meta_mode=True),
        grid=(1,),
        in_specs=[pl.BlockSpec(memory_space=pltpu.SMEM),
                  pl.BlockSpec((BLOCK, A_WIDTH), lambda i: (0, 0)), kvm, kvm],
        out_specs=pl.BlockSpec((BLOCK, A_WIDTH), lambda i: (0, 0)),
        out_shape=jax.ShapeDtypeStruct((BLOCK, A_WIDTH), BF16),
        compiler_params=_params("arbitrary"),
        name="attn_meta",
    )(sink, q_small, k_small, v_small)


SAMPLE_BLOCK = 8
HEAD_ROWS = 16


def _attn_sample_kernel(sink_ref, q_ref, kn_ref, vn_ref, mk_ref, mv_ref, wk_ref, wv_ref,
                        o_ref, nwk_ref, nwv_ref):
    sb = q_ref.shape[0]
    kn = kn_ref[...]
    vn = vn_ref[...]
    rep = 8
    k_ext = jnp.concatenate([mk_ref[...], wk_ref[...], jnp.broadcast_to(kn, (sb, rep, KV_WIDTH))], axis=1)
    v_ext = jnp.concatenate([mv_ref[...], wv_ref[...], jnp.broadcast_to(vn, (sb, rep, KV_WIDTH))], axis=1)
    nk = k_ext.shape[1]
    n_real = nk - rep + 1
    s = jnp.einsum('bhd,bjd->bhj', q_ref[...], k_ext.astype(BF16), preferred_element_type=F32)
    col = lax.broadcasted_iota(jnp.int32, s.shape, 2)
    s = jnp.where(col < n_real, s, NEG)
    sink = sink_ref[...][None]
    m = jnp.maximum(jnp.max(s, axis=2, keepdims=True), sink)
    p = jnp.exp(s - m)
    denom = jnp.sum(p, axis=2, keepdims=True) + jnp.exp(sink - m)
    o = jnp.einsum('bhj,bjd->bhd', p.astype(BF16), v_ext.astype(BF16), preferred_element_type=F32)
    o_ref[...] = o / denom
    nwin = wk_ref.shape[1]
    nwk_ref[:, 0:nwin - 1, :] = wk_ref[:, 1:nwin, :]
    nwk_ref[:, nwin - 1:nwin, :] = kn
    nwv_ref[:, 0:nwin - 1, :] = wv_ref[:, 1:nwin, :]
    nwv_ref[:, nwin - 1:nwin, :] = vn


def _attn_sample(sink_col, q_bd, k_new, v_new, meta_k, meta_v, win_k, win_v):
    ns = q_bd.shape[0]
    sb = SAMPLE_BLOCK
    nwin = win_k.shape[1]
    blk = lambda rows: pl.BlockSpec((sb, rows, KV_WIDTH), lambda i: (i, 0, 0))
    return pl.pallas_call(
        _attn_sample_kernel,
        grid=(ns // sb,),
        in_specs=[pl.BlockSpec((HEAD_ROWS, 1), lambda i: (0, 0)),
                  blk(HEAD_ROWS), blk(1), blk(1), blk(N_META), blk(N_META), blk(nwin), blk(nwin)],
        out_specs=[blk(HEAD_ROWS), blk(nwin), blk(nwin)],
        out_shape=[jax.ShapeDtypeStruct((ns, HEAD_ROWS, KV_WIDTH), F32),
                   jax.ShapeDtypeStruct((ns, nwin, KV_WIDTH), F32),
                   jax.ShapeDtypeStruct((ns, nwin, KV_WIDTH), F32)],
        compiler_params=_params("parallel"),
        name="attn_sample",
    )(sink_col, q_bd, k_new, v_new, meta_k, meta_v, win_k, win_v)


def _seg_mean_sq(o, mseg):
    sq = o * o
    hi = sq.astype(BF16)
    lo = (sq - hi.astype(F32)).astype(BF16)
    return _dot(hi, mseg) + _dot(lo, mseg)


def _gla_kernel(q_ref, k_ref, la_ref, v_ref, r_ref, s0_ref, gg_ref, ob_ref, st_ref, st_scr, *, n_chunks):
    cs = GLA_CHUNK
    st_scr[...] = s0_ref[0]
    iota = lambda shape, axis: lax.broadcasted_iota(jnp.int32, shape, axis)
    lg_cs, lg_dk, lg_dv = cs.bit_length() - 1, B_DK.bit_length() - 1, B_DV.bit_length() - 1
    tri = jnp.where(iota((cs, cs), 1) <= iota((cs, cs), 0), 1.0, 0.0).astype(BF16)
    kd_mask = (iota((B_HEADS * cs, GK_WIDTH), 0) >> lg_cs) == (iota((B_HEADS * cs, GK_WIDTH), 1) >> lg_dk)
    vd_mask = (iota((B_HEADS * cs, B_WIDTH), 0) >> lg_cs) == (iota((B_HEADS * cs, B_WIDTH), 1) >> lg_dv)
    st_mask = (iota((B_WIDTH, GK_WIDTH), 0) >> lg_dv) == (iota((B_WIDTH, GK_WIDTH), 1) >> lg_dk)
    causal = (iota((cs, B_HEADS * cs), 1) & (cs - 1)) <= iota((cs, B_HEADS * cs), 0)
    mseg = jnp.where((iota((B_WIDTH, B_WIDTH), 0) >> lg_dv) == (iota((B_WIDTH, B_WIDTH), 1) >> lg_dv),
                     1.0 / B_DV, 0.0).astype(BF16)
    gg = gg_ref[...]

    def body(c, carry):
        rows = pl.ds(pl.multiple_of(c * cs, cs), cs)
        q = q_ref[0, rows, :]
        k = k_ref[0, rows, :]
        v = v_ref[0, rows, :]
        la_h, la_m, la_l = _split3(la_ref[0, rows, :])
        g = _dot(tri, la_h) + _dot(tri, la_m) + _dot(tri, la_l)
        g_end = g[cs - 1:cs, :]
        g_mid = g[cs // 2 - 1:cs // 2, :]
        qt = (q * jnp.exp(g - g_mid)).astype(BF16)
        kt = k * jnp.exp(g_mid - g)
        kbd = jnp.where(kd_mask, jnp.concatenate([kt] * B_HEADS, axis=0), 0.0).astype(BF16)
        att = jnp.where(causal, _dot_nt(qt, kbd), 0.0)
        vbd = jnp.where(vd_mask, jnp.concatenate([v] * B_HEADS, axis=0), 0.0).astype(BF16)
        st = st_scr[...]
        o = _dot(att.astype(BF16), vbd) + _dot_nt((q * jnp.exp(g)).astype(BF16), st.astype(BF16))
        kh = (k * jnp.exp(g_end - g)).astype(BF16)
        upd = _dot_tn(v.astype(BF16), kh)
        st_scr[...] = st * jnp.exp(g_end) + jnp.where(st_mask, upd, 0.0)
        on = o * lax.rsqrt(_seg_mean_sq(o, mseg) + LN_EPS)
        ob_ref[0, rows, :] = (on * gg * _silu(r_ref[0, rows, :])).astype(BF16)
        return carry

    lax.fori_loop(0, n_chunks, body, 0)
    st_ref[0] = st_scr[...]


def _gla_prompt(q, k, la, v, r, s0, gg, nbatch):
    t = q.shape[0] // nbatch
    seq = lambda a: a.reshape(nbatch, t, a.shape[-1])
    blk = lambda width: pl.BlockSpec((1, t, width), lambda b: (b, 0, 0))
    return pl.pallas_call(
        functools.partial(_gla_kernel, n_chunks=t // GLA_CHUNK),
        grid=(nbatch,),
        in_specs=[blk(GK_WIDTH), blk(GK_WIDTH), blk(GK_WIDTH), blk(B_WIDTH), blk(B_WIDTH),
                  pl.BlockSpec((1, B_WIDTH, GK_WIDTH), lambda b: (0, 0, 0)),
                  pl.BlockSpec((1, B_WIDTH), lambda b: (0, 0))],
        out_specs=[blk(B_WIDTH), pl.BlockSpec((1, B_WIDTH, GK_WIDTH), lambda b: (b, 0, 0))],
        out_shape=[jax.ShapeDtypeStruct((nbatch, t, B_WIDTH), BF16),
                   jax.ShapeDtypeStruct((nbatch, B_WIDTH, GK_WIDTH), F32)],
        scratch_shapes=[pltpu.VMEM((B_WIDTH, GK_WIDTH), F32)],
        compiler_params=_params("parallel"),
        name="gla_prompt",
    )(seq(q), seq(k), seq(la), seq(v), seq(r), s0, gg)


def _gla_sample_kernel(q_ref, k_ref, la_ref, v_ref, r_ref, s_ref, gg_ref, ob_ref, sn_ref):
    sb = q_ref.shape[0]
    qt = q_ref[...].T
    kt = k_ref[...].T
    at = jnp.exp(la_ref[...]).T
    v = v_ref[...]
    gg = gg_ref[...]
    for j in range(sb):
        vj = jnp.concatenate(
            [jnp.broadcast_to(v[j:j + 1, h * B_DV:(h + 1) * B_DV], (B_DK, B_DV)) for h in range(B_HEADS)], axis=0)
        s_new = at[:, j:j + 1] * s_ref[j] + kt[:, j:j + 1] * vj
        sn_ref[j] = s_new
        o = jnp.sum((qt[:, j:j + 1] * s_new).reshape(B_HEADS, B_DK, B_DV), axis=1)
        on = o * lax.rsqrt(jnp.mean(o * o, axis=-1, keepdims=True) + LN_EPS)
        ob_ref[j] = on * gg * _silu(r_ref[j])


def _gla_sample(q, k, la, v, r, state, gg):
    ns = q.shape[0]
    sb = SAMPLE_BLOCK
    row = lambda width: pl.BlockSpec((sb, width), lambda i: (i, 0))
    return pl.pallas_call(
        _gla_sample_kernel,
        grid=(ns // sb,),
        in_specs=[row(GK_WIDTH), row(GK_WIDTH), row(GK_WIDTH), row(B_WIDTH),
                  pl.BlockSpec((sb, B_HEADS, B_DV), lambda i: (i, 0, 0)),
                  pl.BlockSpec((sb, GK_WIDTH, B_DV), lambda i: (i, 0, 0)),
                  pl.BlockSpec((1, B_DV), lambda i: (0, 0))],
        out_specs=[pl.BlockSpec((sb, B_HEADS, B_DV), lambda i: (i, 0, 0)),
                   pl.BlockSpec((sb, GK_WIDTH, B_DV), lambda i: (i, 0, 0))],
        out_shape=[jax.ShapeDtypeStruct((ns, B_HEADS, B_DV), F32),
                   jax.ShapeDtypeStruct((ns, GK_WIDTH, B_DV), F32)],
        compiler_params=_params("parallel"),
        name="gla_sample",
    )(q, k, la, v, r, state, gg)


CONV_HIST = 32
CONV_CHUNK = 64


def _conv_post(y, dwb, lng, lnb, pww, pwb):
    y = _silu(_ln_rows(y + dwb, lng, lnb))
    return _dot(y.astype(BF16), pww) + pwb


def _conv_kernel(u_ref, hist_ref, dww_ref, dwb_ref, lng_ref, lnb_ref, pww_ref, pwb_ref,
                 oc_ref, tail_ref, ubuf, *, n_chunks):
    t = u_ref.shape[1]
    ubuf[0:CONV_HIST, :] = hist_ref[...]
    ubuf[CONV_HIST:CONV_HIST + t, :] = u_ref[0]
    off = CONV_HIST - (CONV_W - 1)

    def body(c, carry):
        r0 = pl.multiple_of(c * CONV_CHUNK, CONV_CHUNK)
        win = ubuf[pl.ds(r0, CONV_CHUNK + CONV_HIST), :]
        acc = jnp.zeros((CONV_CHUNK, C_WIDTH), F32)
        for j in range(CONV_W):
            acc = acc + win[j + off:j + off + CONV_CHUNK, :] * dww_ref[j:j + 1, :]
        oc = _conv_post(acc, dwb_ref[...], lng_ref[...], lnb_ref[...], pww_ref[...], pwb_ref[...])
        oc_ref[0, pl.ds(r0, CONV_CHUNK), :] = oc.astype(BF16)
        return carry

    lax.fori_loop(0, n_chunks, body, 0)
    tail_ref[0] = ubuf[t:t + CONV_HIST, :]


def _conv_prompt(u, hist, dww, dwb, lng, lnb, pww, pwb, nbatch):
    t = u.shape[0] // nbatch
    const = lambda shape: pl.BlockSpec(shape, lambda b: (0,) * len(shape))
    return pl.pallas_call(
        functools.partial(_conv_kernel, n_chunks=t // CONV_CHUNK),
        grid=(nbatch,),
        in_specs=[pl.BlockSpec((1, t, C_WIDTH), lambda b: (b, 0, 0)), const((CONV_HIST, C_WIDTH)),
                  const((CONV_HIST, C_WIDTH)), const((1, C_WIDTH)), const((1, C_WIDTH)), const((1, C_WIDTH)),
                  const((C_WIDTH, C_WIDTH)), const((1, C_WIDTH))],
        out_specs=[pl.BlockSpec((1, t, C_WIDTH), lambda b: (b, 0, 0)),
                   pl.BlockSpec((1, CONV_HIST, C_WIDTH), lambda b: (b, 0, 0))],
        out_shape=[jax.ShapeDtypeStruct((nbatch, t, C_WIDTH), BF16),
                   jax.ShapeDtypeStruct((nbatch, CONV_HIST, C_WIDTH), F32)],
        scratch_shapes=[pltpu.VMEM((CONV_HIST + t, C_WIDTH), F32)],
        compiler_params=_params("parallel"),
        name="conv_prompt",
    )(u.reshape(nbatch, t, C_WIDTH), hist, dww, dwb, lng, lnb, pww, pwb)


def _conv_sample_kernel(hist_ref, u_ref, dww_ref, dwb_ref, lng_ref, lnb_ref, pww_ref, pwb_ref, oc_ref, tail_ref):
    nh = CONV_W - 1
    hist = hist_ref[...]
    u = u_ref[...]
    y = jnp.sum(hist * dww_ref[0:nh, :][None], axis=1) + u[:, 0, :] * dww_ref[nh:nh + 1, :]
    oc = _conv_post(y, dwb_ref[...], lng_ref[...], lnb_ref[...], pww_ref[...], pwb_ref[...])
    oc_ref[...] = oc.astype(BF16)
    tail_ref[:, 0:nh - 1, :] = hist_ref[:, 1:nh, :]
    tail_ref[:, nh - 1:nh, :] = u


def _conv_sample(hist, u, dww, dwb, lng, lnb, pww, pwb):
    ns = hist.shape[0]
    nh = CONV_W - 1
    const = lambda shape: pl.BlockSpec(shape, lambda i: (0,) * len(shape))
    return pl.pallas_call(
        _conv_sample_kernel,
        grid=(1,),
        in_specs=[const((ns, nh, C_WIDTH)), const((ns, 1, C_WIDTH)),
                  const((CONV_HIST, C_WIDTH)), const((1, C_WIDTH)), const((1, C_WIDTH)), const((1, C_WIDTH)),
                  const((C_WIDTH, C_WIDTH)), const((1, C_WIDTH))],
        out_specs=[const((ns, C_WIDTH)), const((ns, nh, C_WIDTH))],
        out_shape=[jax.ShapeDtypeStruct((ns, C_WIDTH), BF16),
                   jax.ShapeDtypeStruct((ns, nh, C_WIDTH), F32)],
        compiler_params=_params("arbitrary"),
        name="conv_sample",
    )(hist, u, dww, dwb, lng, lnb, pww, pwb)


def _out_proj_kernel(oa_ref, ob_ref, oc_ref, h_ref, w_ref, g_ref, b_ref, o_ref):
    mix = (_dot(oa_ref[...], w_ref[0:A_WIDTH, :])
           + _dot(ob_ref[...], w_ref[A_WIDTH:A_WIDTH + B_WIDTH, :])
           + _dot(oc_ref[...], w_ref[A_WIDTH + B_WIDTH:D_MODEL, :]))
    o_ref[...] = _ln_rows(ALPHA * h_ref[...] + mix, g_ref[...], b_ref[...])


def _out_proj(oa, ob, oc, h, w, g, b, tm):
    n = h.shape[0]
    row = lambda width: pl.BlockSpec((tm, width), lambda i: (i, 0))
    const = lambda shape: pl.BlockSpec(shape, lambda i: (0, 0))
    return pl.pallas_call(
        _out_proj_kernel,
        grid=(n // tm,),
        in_specs=[row(A_WIDTH), row(B_WIDTH), row(C_WIDTH), row(D_MODEL),
                  const((D_MODEL, D_MODEL)), const((1, D_MODEL)), const((1, D_MODEL))],
        out_specs=row(D_MODEL),
        out_shape=jax.ShapeDtypeStruct((n, D_MODEL), F32),
        compiler_params=_params("parallel"),
        name="out_proj",
    )(oa, ob, oc, h, w, g, b)


ROUTE_LANE0 = N_GROUPS


def _route(x, wr_hi, wr_lo, br):
    x_hi = x.astype(BF16)
    x_lo = (x - x_hi.astype(F32)).astype(BF16)
    logits = _dot(x_hi, wr_hi) + _dot(x_lo, wr_hi) + _dot(x_hi, wr_lo) + br
    lane = lax.broadcasted_iota(jnp.int32, logits.shape, 1).astype(F32)
    far = 1e3
    glm = jnp.where(lane < N_GROUPS, logits, NEG)
    gmax = jnp.max(glm, axis=1, keepdims=True)
    gi = jnp.min(jnp.where(glm == gmax, lane, far), axis=1, keepdims=True)
    p_grp = 1.0 / jnp.sum(jnp.exp(glm - gmax), axis=1, keepdims=True)
    lo = ROUTE_LANE0 + EXP_PER_GROUP * gi
    in_sel = (lane >= lo) & (lane < lo + EXP_PER_GROUP)
    elm = jnp.where(in_sel, logits, NEG)
    v1 = jnp.max(elm, axis=1, keepdims=True)
    i1 = jnp.min(jnp.where(elm == v1, lane, far), axis=1, keepdims=True)
    elm2 = jnp.where(lane == i1, NEG, elm)
    v2 = jnp.max(elm2, axis=1, keepdims=True)
    i2 = jnp.min(jnp.where((elm2 == v2) & in_sel & (lane != i1), lane, far), axis=1, keepdims=True)
    t = jnp.exp(v2 - v1)
    w1 = p_grp / (1.0 + t)
    w2 = w1 * t
    return jnp.where(lane == i1, w1, 0.0) + jnp.where(lane == i2, w2, 0.0), gi


MOE_ROWS = 128
MOE_ALIGN = 16
GROUP_WIDTH = EXP_PER_GROUP * D_EXPERT


def _moe_sorted_rows(tm):
    need = tm + N_GROUPS * MOE_ALIGN + MOE_ROWS
    return -(-need // LANES) * LANES


def _moe_kernel(x_ref, wrh_ref, wrl_ref, br_ref, wg_ref, wu_ref, wd_ref, g_ref, b_ref, o_ref,
                xs_ref, ys_ref, ws_ref):
    x = x_ref[...]
    tm = x.shape[0]
    ns = xs_ref.shape[0]
    iota = lambda shape, axis: lax.broadcasted_iota(jnp.int32, shape, axis)
    dw, gi = _route(x, wrh_ref[...], wrl_ref[...], br_ref[...])
    lane_f = iota((tm, LANES), 1).astype(F32)
    onehot = jnp.where(lane_f == gi, 1.0, 0.0)
    tri = jnp.where(iota((tm, tm), 1) <= iota((tm, tm), 0), 1.0, 0.0).astype(BF16)
    cum = _dot(tri, onehot.astype(BF16))
    rank = jnp.sum(onehot * (cum - 1.0), axis=1, keepdims=True)
    counts = cum[tm - 1:tm, :]

    starts, tiles = [], []
    start = jnp.int32(0)
    for grp in range(N_GROUPS):
        n_g = counts[0, grp].astype(jnp.int32)
        starts.append(start)
        tiles.append((n_g + (MOE_ROWS - 1)) >> (MOE_ROWS.bit_length() - 1))
        start = start + ((n_g + (MOE_ALIGN - 1)) & -MOE_ALIGN)
    lane1 = iota((1, LANES), 1)
    start_v = jnp.zeros((1, LANES), F32)
    for grp in range(N_GROUPS):
        start_v = jnp.where(lane1 == grp, starts[grp].astype(F32), start_v)
    pos = jnp.sum(onehot * start_v, axis=1, keepdims=True) + rank

    pos_i = pos.astype(jnp.int32)
    digits = jnp.where(lane_f == 0.0, (pos_i >> 5).astype(F32),
                       jnp.where(lane_f == 1.0, (pos_i & 31).astype(F32), 0.0)).astype(BF16)
    lane8 = iota((8, LANES), 1)
    radix = jnp.where(lane8 == 0, 32.0, jnp.where(lane8 == 1, 1.0, 0.0)).astype(BF16)
    pos_row = _dot_nt(radix, digits)[0:1, :]

    perm = jnp.where(iota((ns, tm), 0).astype(F32) == pos_row, 1.0, 0.0).astype(BF16)
    xs_ref[...] = _dot(perm, x.astype(BF16)).astype(BF16)
    dw_hi = dw.astype(BF16)
    dw_lo = (dw - dw_hi.astype(F32)).astype(BF16)
    ws_ref[...] = _dot(perm, dw_hi) + _dot(perm, dw_lo)
    ys_ref[...] = jnp.zeros_like(ys_ref)

    for grp in range(N_GROUPS):
        def body(k, carry, grp=grp):
            r0 = pl.multiple_of(starts[grp] + k * MOE_ROWS, MOE_ALIGN)
            rows = pl.ds(r0, MOE_ROWS)
            xt = xs_ref[rows, :]
            w = ws_ref[rows, :]
            lane0 = ROUTE_LANE0 + grp * EXP_PER_GROUP
            w_exp = jnp.concatenate(
                [jnp.broadcast_to(w[:, lane0 + e:lane0 + e + 1], (MOE_ROWS, D_EXPERT)) for e in range(EXP_PER_GROUP)],
                axis=1)
            he = _silu(_dot(xt, wg_ref[grp])) * _dot(xt, wu_ref[grp]) * w_exp
            ys_ref[rows, :] = _dot(he.astype(BF16), wd_ref[grp]).astype(BF16)
            return carry
        lax.fori_loop(0, tiles[grp], body, 0)

    unperm = jnp.where(iota((tm, ns), 1).astype(F32) == pos, 1.0, 0.0).astype(BF16)
    y = _dot(unperm, ys_ref[...])
    o_ref[...] = _ln_rows(ALPHA * x + y, g_ref[...], b_ref[...])


def _moe(x, wrh, wrl, br, wg, wu, wd, g, b, tm):
    n = x.shape[0]
    ns = _moe_sorted_rows(tm)
    const = lambda shape: pl.BlockSpec(shape, lambda i: (0,) * len(shape))
    resident = lambda shape: pl.BlockSpec(shape, lambda i: (0,) * len(shape), pipeline_mode=pl.Buffered(1))
    return pl.pallas_call(
        _moe_kernel,
        grid=(n // tm,),
        in_specs=[pl.BlockSpec((tm, D_MODEL), lambda i: (i, 0)),
                  const((D_MODEL, LANES)), const((D_MODEL, LANES)), const((1, LANES)),
                  resident((N_GROUPS, D_MODEL, GROUP_WIDTH)), resident((N_GROUPS, D_MODEL, GROUP_WIDTH)),
                  resident((N_GROUPS, GROUP_WIDTH, D_MODEL)),
                  const((1, D_MODEL)), const((1, D_MODEL))],
        out_specs=pl.BlockSpec((tm, D_MODEL), lambda i: (i, 0)),
        out_shape=jax.ShapeDtypeStruct((n, D_MODEL), F32),
        scratch_shapes=[pltpu.VMEM((ns, D_MODEL), BF16), pltpu.VMEM((ns, D_MODEL), BF16),
                        pltpu.VMEM((ns, LANES), F32)],
        compiler_params=_params("parallel"),
        name="moe",
    )(x, wrh, wrl, br, wg, wu, wd, g, b)


def _rope_tables(pos):
    half = HEAD_DIM // 2
    inv = ROPE_THETA ** (-jnp.arange(half, dtype=F32) / half)
    ang = pos.astype(F32)[:, None] * inv[None, :]
    cos, sin = jnp.cos(ang), jnp.sin(ang)
    cos_t = jnp.concatenate([cos, cos] * (LANES // HEAD_DIM), axis=1)
    sin_t = jnp.concatenate([-sin, sin] * (LANES // HEAD_DIM), axis=1)
    return cos_t, sin_t


def _row(v):
    return v.reshape(1, -1)


def kernel(x_prompt, x_sample, cache_meta_k, cache_meta_v, cache_win_k, cache_win_v, state_gla, state_conv,
           meta_tokens, ln_in_g, ln_in_b, w_in, attn_sink, w_alpha, b_alpha, gla_norm_g,
           conv_dw_w, conv_dw_b, conv_ln_g, conv_ln_b, conv_pw_w, conv_pw_b, w_out, ln1_g, ln1_b,
           w_router_group, b_router_group, w_router_expert, b_router_expert,
           w_exp_gate, w_exp_up, w_exp_down, ln2_g, ln2_b):
    nb, seq, d = x_prompt.shape
    ns = x_sample.shape[0]
    nwin = cache_win_k.shape[2]
    n_big = nb * seq
    n_small = BLOCK + ns
    tm_big = min(512, seq)

    small_in = jnp.concatenate([jnp.zeros((META_PAD, d), F32), meta_tokens.astype(F32),
                                x_sample.reshape(ns, d)], axis=0)
    hb = _layer_norm(x_prompt.reshape(n_big, d), _row(ln_in_g), _row(ln_in_b), tm_big)
    hs = _layer_norm(small_in, _row(ln_in_g), _row(ln_in_b), n_small)

    cos_b, sin_b = _rope_tables(N_META + jnp.arange(seq))
    pos_small = jnp.concatenate([jnp.maximum(jnp.arange(BLOCK) - META_PAD, 0),
                                 jnp.full((ns,), PAST_LEN, jnp.int32)])
    cos_s, sin_s = _rope_tables(pos_small)

    outs = [[] for _ in range(12)]
    zeros_hist = jnp.zeros((CONV_HIST, C_WIDTH), F32)
    zeros_state = jnp.zeros((1, B_WIDTH, GK_WIDTH), F32)
    pad_cols = jnp.zeros((d, LANES - B_RANK), F32)

    for l in range(DEPTH):
        wi = w_in[l]
        w_pad = jnp.concatenate([wi[:, :C_AB + B_RANK], pad_cols, wi[:, C_AB + B_RANK:]], axis=1).astype(BF16)
        wa = jnp.concatenate([w_alpha[l], jnp.zeros((LANES - B_RANK, GK_WIDTH), F32)], axis=0).astype(BF16)
        ba = _row(b_alpha[l])
        sink = attn_sink[l].astype(F32)
        sink_col = jnp.concatenate([sink, jnp.zeros((HEAD_ROWS - A_HEADS,), F32)])[:, None]
        gg_t = _row(jnp.tile(gla_norm_g[l], B_HEADS))
        gg_h = _row(gla_norm_g[l])
        dww = jnp.concatenate([conv_dw_w[l], jnp.zeros((CONV_HIST - CONV_W, C_WIDTH), F32)], axis=0)
        dwb, clg, clb = _row(conv_dw_b[l]), _row(conv_ln_g[l]), _row(conv_ln_b[l])
        pww, pwb = conv_pw_w[l].astype(BF16), _row(conv_pw_b[l])
        wo = w_out[l].astype(BF16)
        wr = jnp.concatenate([w_router_group[l], w_router_expert[l],
                              jnp.zeros((d, LANES - N_GROUPS - N_EXPERTS), F32)], axis=1)
        wr_hi = wr.astype(BF16)
        wr_lo = (wr - wr_hi.astype(F32)).astype(BF16)
        br = _row(jnp.concatenate([b_router_group[l], b_router_expert[l],
                                   jnp.zeros((LANES - N_GROUPS - N_EXPERTS,), F32)]))
        side_by_side = lambda w: (w.astype(BF16).reshape(N_GROUPS, EXP_PER_GROUP, d, D_EXPERT)
                                  .transpose(0, 2, 1, 3).reshape(N_GROUPS, d, GROUP_WIDTH))
        wg, wu = side_by_side(w_exp_gate[l]), side_by_side(w_exp_up[l])
        wd = w_exp_down[l].astype(BF16).reshape(N_GROUPS, GROUP_WIDTH, d)

        qa_b, ka_b, va_b, qg_b, kg_b, la_b, vg_b, rg_b, u_b = _proj_in(
            hb, w_pad, wa, ba, cos_b, sin_b, tm=tm_big, n_pad=0)
        qa_s, ka_s, va_s, qg_s, kg_s, la_s, vg_s, rg_s, u_s = _proj_in(
            hs, w_pad, wa, ba, cos_s, sin_s, tm=n_small, n_pad=META_PAD)

        oa_b = _attn_prompt(sink, qa_b, ka_b, va_b, ka_s, va_s, nb)
        oa_m = _attn_meta(sink, qa_s, ka_s, va_s)
        q_smp = qa_s[BLOCK:].reshape(ns, A_KV_HEADS, A_GROUP, 1, HEAD_DIM)
        eye = jnp.eye(A_KV_HEADS, dtype=BF16)[None, :, None, :, None]
        q_bd = (q_smp * eye).reshape(ns, A_HEADS, KV_WIDTH)
        q_bd = jnp.pad(q_bd, ((0, 0), (0, HEAD_ROWS - A_HEADS), (0, 0)))
        o_bd, nwk, nwv = _attn_sample(
            sink_col, q_bd, ka_s[BLOCK:].reshape(ns, 1, KV_WIDTH), va_s[BLOCK:].reshape(ns, 1, KV_WIDTH),
            cache_meta_k[l].reshape(ns, N_META, KV_WIDTH), cache_meta_v[l].reshape(ns, N_META, KV_WIDTH),
            cache_win_k[l].reshape(ns, nwin, KV_WIDTH), cache_win_v[l].reshape(ns, nwin, KV_WIDTH))
        o_bd = o_bd[:, :A_HEADS].reshape(ns, A_KV_HEADS, A_GROUP, A_KV_HEADS, HEAD_DIM)
        oa_smp = jnp.stack([o_bd[:, c, :, c, :] for c in range(A_KV_HEADS)], axis=1).reshape(ns, A_WIDTH)
        oa_s = jnp.concatenate([oa_m, oa_smp.astype(BF16)], axis=0)

        ob_m, st_m = _gla_prompt(qg_s[:BLOCK], kg_s[:BLOCK], la_s[:BLOCK], vg_s[:BLOCK], rg_s[:BLOCK],
                                 zeros_state, gg_t, 1)
        ob_b, st_b = _gla_prompt(qg_b, kg_b, la_b, vg_b, rg_b, st_m, gg_t, nb)
        ob_smp, s_new = _gla_sample(qg_s[BLOCK:], kg_s[BLOCK:], la_s[BLOCK:], vg_s[BLOCK:],
                                    rg_s[BLOCK:].reshape(ns, B_HEADS, B_DV),
                                    state_gla[l].reshape(ns, GK_WIDTH, B_DV), gg_h)
        ob_s = jnp.concatenate([ob_m.reshape(BLOCK, B_WIDTH), ob_smp.reshape(ns, B_WIDTH).astype(BF16)], axis=0)
        st5 = st_b.reshape(nb, B_HEADS, B_DV, B_HEADS, B_DK)
        gla_p = jnp.stack([st5[:, h, :, h, :] for h in range(B_HEADS)], axis=1).transpose(0, 1, 3, 2)

        oc_m, _ = _conv_prompt(u_s[:BLOCK], zeros_hist, dww, dwb, clg, clb, pww, pwb, 1)
        oc_b, tail_b = _conv_prompt(u_b, u_s[BLOCK - CONV_HIST:BLOCK], dww, dwb, clg, clb, pww, pwb, nb)
        oc_smp, tail_s = _conv_sample(state_conv[l], u_s[BLOCK:].reshape(ns, 1, C_WIDTH),
                                      dww, dwb, clg, clb, pww, pwb)
        oc_s = jnp.concatenate([oc_m.reshape(BLOCK, C_WIDTH), oc_smp], axis=0)

        l1g, l1b, l2g, l2b = _row(ln1_g[l]), _row(ln1_b[l]), _row(ln2_g[l]), _row(ln2_b[l])
        hb1 = _out_proj(oa_b, ob_b.reshape(n_big, B_WIDTH), oc_b.reshape(n_big, C_WIDTH), hb, wo, l1g, l1b, tm_big)
        hs1 = _out_proj(oa_s, ob_s, oc_s, hs, wo, l1g, l1b, n_small)
        hb = _moe(hb1, wr_hi, wr_lo, br, wg, wu, wd, l2g, l2b, tm_big)
        hs = _moe(hs1, wr_hi, wr_lo, br, wg, wu, wd, l2g, l2b, n_small)

        kv4 = lambda a: a.reshape(a.shape[0], a.shape[1], A_KV_HEADS, HEAD_DIM)
        meta_k = jnp.broadcast_to(ka_s[META_PAD:BLOCK][None], (nb, N_META, KV_WIDTH))
        meta_v = jnp.broadcast_to(va_s[META_PAD:BLOCK][None], (nb, N_META, KV_WIDTH))
        win_k = ka_b.reshape(nb, seq, KV_WIDTH)[:, seq - nwin:]
        win_v = va_b.reshape(nb, seq, KV_WIDTH)[:, seq - nwin:]
        layer_out = (None, None, kv4(meta_k), kv4(meta_v), kv4(win_k), kv4(win_v), kv4(nwk), kv4(nwv),
                     gla_p, s_new.reshape(ns, B_HEADS, B_DK, B_DV),
                     tail_b[:, CONV_HIST - (CONV_W - 1):], tail_s)
        for i in range(2, 12):
            outs[i].append(layer_out[i])

    y_prompt = hb.reshape(nb, seq, d)
    y_sample = hs[BLOCK:].reshape(ns, 1, d)
    return (y_prompt, y_sample) + tuple(jnp.stack(o) for o in outs[2:])
```

```python
import functools

import jax
import jax.numpy as jnp
from jax import lax
from jax.experimental import pallas as pl
from jax.experimental.pallas import tpu as pltpu

F32 = jnp.float32
BF16 = jnp.bfloat16

D_MODEL = 1024
DEPTH = 2
PAST_LEN = 16384
N_META = 16
HEAD_DIM = 64
A_WIDTH = 512
A_HEADS = 8
A_KV_HEADS = 2
A_GROUP = 4
WINDOW = 128
BLOCK = 128
ROPE_THETA = 10000.0
B_WIDTH = 256
B_HEADS = 4
B_DV = 64
B_DK = 32
B_RANK = 16
GATE_TAU = 16.0
GLA_CHUNK = 64
C_WIDTH = 256
CONV_W = 31
N_GROUPS = 4
EXP_PER_GROUP = 4
N_EXPERTS = 16
D_EXPERT = 256
ALPHA = (2 * DEPTH) ** 0.25
LN_EPS = 1e-5

LANES = 128
SUBLANES = 8
META_PAD = BLOCK - N_META
KV_WIDTH = A_KV_HEADS * HEAD_DIM
GK_WIDTH = B_HEADS * B_DK
C_QA = 0
C_KA = C_QA + A_WIDTH
C_VA = C_KA + KV_WIDTH
C_QB = C_VA + KV_WIDTH
C_KB = C_QB + GK_WIDTH
C_VB = C_KB + GK_WIDTH
C_RB = C_VB + B_WIDTH
C_AB = C_RB + B_WIDTH
C_CG = C_AB + LANES
PROJ_PAD_WIDTH = C_CG + 2 * C_WIDTH
NEG = -1e30
VMEM_LIMIT = 56 * 1024 * 1024


def _dot(a, b):
    return jnp.dot(a, b, preferred_element_type=F32)


def _dot_nt(a, b):
    return lax.dot_general(a, b, (((1,), (1,)), ((), ())), preferred_element_type=F32)


def _dot_tn(a, b):
    return lax.dot_general(a, b, (((0,), (0,)), ((), ())), preferred_element_type=F32)


def _ln_rows(x, g, b):
    xc = x - jnp.mean(x, -1, keepdims=True)
    var = jnp.mean(xc * xc, -1, keepdims=True)
    return xc * lax.rsqrt(var + LN_EPS) * g + b


def _silu(x):
    return x * jax.nn.sigmoid(x)


def _split3(x):
    hi = x.astype(BF16)
    r1 = x - hi.astype(F32)
    mid = r1.astype(BF16)
    lo = (r1 - mid.astype(F32)).astype(BF16)
    return hi, mid, lo


def _params(*sem):
    return pltpu.CompilerParams(dimension_semantics=sem, vmem_limit_bytes=VMEM_LIMIT)


def _proj_in_kernel(*refs, n_pad, n_seq, pre_ln):
    if pre_ln:
        lg_ref, lb_ref, *refs = refs
    (x_ref, w_ref, wa_ref, ba_ref, cos_ref, sin_ref,
     qa_ref, ka_ref, va_ref, qg_ref, kg_ref, la_ref, vg_ref, rg_ref, u_ref) = refs
    x = x_ref[...]
    if pre_ln:
        x = _ln_rows(x, lg_ref[...], lb_ref[...])
    xb = x.astype(BF16)
    tm = xb.shape[0]
    cos = cos_ref[...]
    sin = sin_ref[...]
    lane = lax.broadcasted_iota(jnp.int32, (tm, LANES), 1)
    first_half = (lane & (HEAD_DIM // 2)) == 0

    def rope(z):
        rot = jnp.where(first_half, pltpu.roll(z, LANES - HEAD_DIM // 2, 1), pltpu.roll(z, HEAD_DIM // 2, 1))
        return z * cos + rot * sin

    if n_pad:
        valid = (lax.broadcasted_iota(jnp.int32, (tm, 1), 0) >= n_pad).astype(F32)
    else:
        valid = None

    za = _dot(xb, w_ref[:, C_QA:C_QB])
    for c in range(A_WIDTH // LANES):
        zq = za[:, c * LANES:(c + 1) * LANES]
        qa_ref[:, c * LANES:(c + 1) * LANES] = (rope(zq) * (HEAD_DIM ** -0.5)).astype(BF16)
    ka_ref[...] = rope(za[:, C_KA:C_VA])
    va_ref[...] = za[:, C_VA:C_QB]

    zb = _dot(xb, w_ref[:, C_QB:C_CG])
    o = C_QB
    qg_ref[...] = zb[:, C_QB - o:C_KB - o] * (B_DK ** -0.5)
    kg = zb[:, C_KB - o:C_VB - o]
    vg_ref[...] = zb[:, C_VB - o:C_RB - o]
    rg_ref[...] = zb[:, C_RB - o:C_AB - o]
    ab = zb[:, C_AB - o:C_CG - o].astype(BF16)
    xa = _dot(ab, wa_ref[...]) + ba_ref[...]
    la = (jnp.minimum(xa, 0.0) - jnp.log(1.0 + jnp.exp(-jnp.abs(xa)))) * (1.0 / GATE_TAU)

    zc = _dot(xb, w_ref[:, C_CG:PROJ_PAD_WIDTH])
    u = zc[:, :C_WIDTH] * jax.nn.sigmoid(zc[:, C_WIDTH:])
    if valid is not None:
        kg = kg * valid
        la = la * valid
        u = u * valid
    kg_ref[...] = kg
    u_ref[...] = u
    t_i = lax.broadcasted_iota(jnp.int32, (tm, tm), 0)
    s_i = lax.broadcasted_iota(jnp.int32, (tm, tm), 1)
    lg_chunk = GLA_CHUNK.bit_length() - 1
    in_chunk = ((t_i >> lg_chunk) == (s_i >> lg_chunk)) & (s_i <= t_i)
    if n_seq < tm:
        in_chunk = ((t_i < n_seq) & in_chunk) | ((t_i >= n_seq) & (s_i == t_i))
    csum = jnp.where(in_chunk, 1.0, 0.0).astype(BF16)
    la_h, la_m, la_l = _split3(la)
    la_ref[...] = _dot(csum, la_h) + _dot(csum, la_m) + _dot(csum, la_l)


def _proj_in(h, w, wa, ba, cos, sin, *, tm, n_pad, n_seq, ln=None):
    n = h.shape[0]
    tb = cos.shape[0] // tm
    row = lambda width: pl.BlockSpec((tm, width), lambda i: (i, 0))
    const = lambda shape: pl.BlockSpec(shape, lambda i: (0, 0))
    tab = pl.BlockSpec((tm, LANES), lambda i: (i % tb, 0))
    widths = (A_WIDTH, KV_WIDTH, KV_WIDTH, GK_WIDTH, GK_WIDTH, GK_WIDTH, B_WIDTH, B_WIDTH, C_WIDTH)
    dtypes = (BF16,) + (F32,) * 8
    ln_specs = [const((1, D_MODEL)), const((1, D_MODEL))] if ln else []
    return pl.pallas_call(
        functools.partial(_proj_in_kernel, n_pad=n_pad, n_seq=n_seq, pre_ln=bool(ln)),
        grid=(n // tm,),
        in_specs=ln_specs + [row(D_MODEL), const((D_MODEL, PROJ_PAD_WIDTH)), const((LANES, GK_WIDTH)),
                             const((1, GK_WIDTH)), tab, tab],
        out_specs=[row(wd) for wd in widths],
        out_shape=[jax.ShapeDtypeStruct((n, wd), dt) for wd, dt in zip(widths, dtypes)],
        compiler_params=_params("parallel"),
        name="proj_in",
    )(*(ln or ()), h, w, wa, ba, cos, sin)


def _attn_kernel(sink_ref, q_ref, km_ref, vm_ref, *rest, meta_mode):
    if meta_mode:
        (o_ref,) = rest
        _attn_block(sink_ref, q_ref[...], km_ref[...], vm_ref[...], o_ref, 0, None)
    else:
        kp_ref, vp_ref, kc_ref, vc_ref, o_ref = rest
        km, vm = km_ref[...], vm_ref[...]
        for sub in range(q_ref.shape[0] // BLOCK):
            cur = slice(sub * BLOCK, (sub + 1) * BLOCK)
            if sub == 0:
                kp, vp = kp_ref[...], vp_ref[...]
                has_prev = pl.program_id(1) >= 1
            else:
                prev = slice((sub - 1) * BLOCK, sub * BLOCK)
                kp, vp = kc_ref[prev, :], vc_ref[prev, :]
                has_prev = True
            k_all = jnp.concatenate([kp, kc_ref[cur, :], km], axis=0)
            v_all = jnp.concatenate([vp, vc_ref[cur, :], vm], axis=0)
            _attn_block(sink_ref, q_ref[cur, :], k_all, v_all, o_ref, sub * BLOCK, has_prev)


def _attn_block(sink_ref, q, k_all, v_all, o_ref, row0, has_prev):
    nk = k_all.shape[0]
    ki = lax.broadcasted_iota(jnp.int32, (nk, BLOCK), 0)
    qi = lax.broadcasted_iota(jnp.int32, (nk, BLOCK), 1)
    if has_prev is None:
        ok = ki <= qi - META_PAD
    else:
        prev_lo = qi if has_prev is True else qi + jnp.where(has_prev, 0, BLOCK)
        ok = ((ki >= prev_lo) & (ki < BLOCK)) | ((ki >= BLOCK) & (ki <= qi + BLOCK)) | (ki >= 2 * BLOCK)
    bias = jnp.where(ok, 0.0, NEG)
    k_all = k_all.astype(BF16)
    v_all = v_all.astype(BF16)
    for kvh in range(A_KV_HEADS):
        heads = [kvh * A_GROUP + g for g in range(A_GROUP)]
        qs = jnp.concatenate([q[:, h * HEAD_DIM:(h + 1) * HEAD_DIM] for h in heads], axis=0)
        kk = k_all[:, kvh * HEAD_DIM:(kvh + 1) * HEAD_DIM]
        vv = v_all[:, kvh * HEAD_DIM:(kvh + 1) * HEAD_DIM]
        st = _dot_nt(kk, qs)
        ps, dens = [], []
        for g, h in enumerate(heads):
            s = st[:, g * BLOCK:(g + 1) * BLOCK] + bias
            sink = sink_ref[h]
            m = jnp.maximum(jnp.max(s, axis=0, keepdims=True), sink)
            p = jnp.exp(s - m)
            dens.append(jnp.sum(p, axis=0, keepdims=True) + jnp.exp(sink - m))
            ps.append(p.astype(BF16))
        ot = _dot_tn(vv, jnp.concatenate(ps, axis=1)) / jnp.concatenate(dens, axis=1)
        for pair in range(A_GROUP // 2):
            two = jnp.concatenate([ot[:, (2 * pair) * BLOCK:(2 * pair + 1) * BLOCK],
                                   ot[:, (2 * pair + 1) * BLOCK:(2 * pair + 2) * BLOCK]], axis=0)
            h0 = heads[2 * pair]
            o_ref[row0:row0 + BLOCK, h0 * HEAD_DIM:(h0 + 2) * HEAD_DIM] = two.T.astype(BF16)


ATTN_SUB = 2


def _attn_prompt(sink, q, k, v, k_small, v_small, nbatch):
    n = q.shape[0]
    nblk = n // BLOCK // nbatch
    sub = ATTN_SUB if nblk % ATTN_SUB == 0 else 1
    nstep = nblk // sub
    kvs = pl.BlockSpec((sub * BLOCK, KV_WIDTH), lambda b, j: (b * nstep + j, 0))
    kvp = pl.BlockSpec((BLOCK, KV_WIDTH), lambda b, j: (b * nblk + jnp.maximum(j * sub - 1, 0), 0))
    kvm = pl.BlockSpec((N_META, KV_WIDTH), lambda b, j: (META_PAD // N_META, 0))
    return pl.pallas_call(
        functools.partial(_attn_kernel, meta_mode=False),
        grid=(nbatch, nstep),
        in_specs=[pl.BlockSpec(memory_space=pltpu.SMEM),
                  pl.BlockSpec((sub * BLOCK, A_WIDTH), lambda b, j: (b * nstep + j, 0)),
                  kvm, kvm, kvp, kvp, kvs, kvs],
        out_specs=pl.BlockSpec((sub * BLOCK, A_WIDTH), lambda b, j: (b * nstep + j, 0)),
        out_shape=jax.ShapeDtypeStruct((n, A_WIDTH), BF16),
        compiler_params=_params("parallel", "parallel"),
        name="attn_prompt",
    )(sink, q, k_small, v_small, k, v, k, v)


def _attn_meta(sink, q_small, k_small, v_small):
    kvm = pl.BlockSpec((N_META, KV_WIDTH), lambda i: (META_PAD // N_META, 0))
    return pl.pallas_call(
        functools.partial(_attn_kernel, meta_mode=True),
        grid=(1,),
        in_specs=[pl.BlockSpec(memory_space=pltpu.SMEM),
                  pl.BlockSpec((BLOCK, A_WIDTH), lambda i: (0, 0)), kvm, kvm],
        out_specs=pl.BlockSpec((BLOCK, A_WIDTH), lambda i: (0, 0)),
        out_shape=jax.ShapeDtypeStruct((BLOCK, A_WIDTH), BF16),
        compiler_params=_params("arbitrary"),
        name="attn_meta",
    )(sink, q_small, k_small, v_small)


SAMPLE_BLOCK = 32
HEAD_ROWS = 16


def _attn_sample_kernel(sink_ref, q_ref, kn_ref, vn_ref, mk_ref, mv_ref, wk_ref, wv_ref,
                        o_ref, nwk_ref, nwv_ref):
    sb = q_ref.shape[0]
    kn = kn_ref[...]
    vn = vn_ref[...]
    rep = 8
    k_ext = jnp.concatenate([mk_ref[...], wk_ref[...], jnp.broadcast_to(kn, (sb, rep, KV_WIDTH))], axis=1)
    v_ext = jnp.concatenate([mv_ref[...], wv_ref[...], jnp.broadcast_to(vn, (sb, rep, KV_WIDTH))], axis=1)
    nk = k_ext.shape[1]
    n_real = nk - rep + 1
    s = jnp.einsum('bhd,bjd->bhj', q_ref[...], k_ext.astype(BF16), preferred_element_type=F32)
    col = lax.broadcasted_iota(jnp.int32, s.shape, 2)
    s = jnp.where(col < n_real, s, NEG)
    sink = sink_ref[...][None]
    m = jnp.maximum(jnp.max(s, axis=2, keepdims=True), sink)
    p = jnp.exp(s - m)
    denom = jnp.sum(p, axis=2, keepdims=True) + jnp.exp(sink - m)
    o = jnp.einsum('bhj,bjd->bhd', p.astype(BF16), v_ext.astype(BF16), preferred_element_type=F32)
    o_ref[...] = o / denom
    nwin = wk_ref.shape[1]
    nwk_ref[:, 0:nwin - 1, :] = wk_ref[:, 1:nwin, :]
    nwk_ref[:, nwin - 1:nwin, :] = kn
    nwv_ref[:, 0:nwin - 1, :] = wv_ref[:, 1:nwin, :]
    nwv_ref[:, nwin - 1:nwin, :] = vn


def _attn_sample(sink_col, q_bd, k_new, v_new, meta_k, meta_v, win_k, win_v):
    ns = q_bd.shape[0]
    sb = min(SAMPLE_BLOCK, ns)
    nwin = win_k.shape[1]
    blk = lambda rows: pl.BlockSpec((sb, rows, KV_WIDTH), lambda i: (i, 0, 0))
    return pl.pallas_call(
        _attn_sample_kernel,
        grid=(ns // sb,),
        in_specs=[pl.BlockSpec((HEAD_ROWS, 1), lambda i: (0, 0)),
                  blk(HEAD_ROWS), blk(1), blk(1), blk(N_META), blk(N_META), blk(nwin), blk(nwin)],
        out_specs=[blk(HEAD_ROWS), blk(nwin), blk(nwin)],
        out_shape=[jax.ShapeDtypeStruct((ns, HEAD_ROWS, KV_WIDTH), F32),
                   jax.ShapeDtypeStruct((ns, nwin, KV_WIDTH), F32),
                   jax.ShapeDtypeStruct((ns, nwin, KV_WIDTH), F32)],
        compiler_params=_params("parallel"),
        name="attn_sample",
    )(sink_col, q_bd, k_new, v_new, meta_k, meta_v, win_k, win_v)


def _seg_mean_sq(o, mseg):
    sq = o * o
    hi = sq.astype(BF16)
    lo = (sq - hi.astype(F32)).astype(BF16)
    return _dot(hi, mseg) + _dot(lo, mseg)


def _gla_kernel(q_ref, k_ref, g_ref, v_ref, r_ref, s0_ref, gg_ref, ob_ref, st_ref, st_scr, *, n_chunks):
    cs = GLA_CHUNK
    nseq = q_ref.shape[0]

    @pl.when(pl.program_id(1) == 0)
    def _():
        for b in range(nseq):
            st_scr[b] = s0_ref[0]

    iota = lambda shape, axis: lax.broadcasted_iota(jnp.int32, shape, axis)
    lg_cs, lg_dk, lg_dv = cs.bit_length() - 1, B_DK.bit_length() - 1, B_DV.bit_length() - 1
    kd_mask = (iota((B_HEADS * cs, GK_WIDTH), 0) >> lg_cs) == (iota((B_HEADS * cs, GK_WIDTH), 1) >> lg_dk)
    vd_mask = (iota((B_HEADS * cs, B_WIDTH), 0) >> lg_cs) == (iota((B_HEADS * cs, B_WIDTH), 1) >> lg_dv)
    st_mask = (iota((B_WIDTH, GK_WIDTH), 0) >> lg_dv) == (iota((B_WIDTH, GK_WIDTH), 1) >> lg_dk)
    causal = (iota((cs, B_HEADS * cs), 1) & (cs - 1)) <= iota((cs, B_HEADS * cs), 0)
    mseg = jnp.where((iota((B_WIDTH, B_WIDTH), 0) >> lg_dv) == (iota((B_WIDTH, B_WIDTH), 1) >> lg_dv),
                     1.0 / B_DV, 0.0).astype(BF16)
    gg = gg_ref[...]

    def body(c, carry):
        rows = pl.ds(pl.multiple_of(c * cs, cs), cs)
        for b in range(nseq):
            q = q_ref[b, rows, :]
            k = k_ref[b, rows, :]
            v = v_ref[b, rows, :]
            g = g_ref[b, rows, :]
            g_end = g[cs - 1:cs, :]
            g_mid = g[cs // 2 - 1:cs // 2, :]
            qt = (q * jnp.exp(g - g_mid)).astype(BF16)
            kt = k * jnp.exp(g_mid - g)
            kbd = jnp.where(kd_mask, jnp.concatenate([kt] * B_HEADS, axis=0), 0.0).astype(BF16)
            att = jnp.where(causal, _dot_nt(qt, kbd), 0.0)
            vbd = jnp.where(vd_mask, jnp.concatenate([v] * B_HEADS, axis=0), 0.0).astype(BF16)
            st = st_scr[b]
            o = _dot(att.astype(BF16), vbd) + _dot_nt((q * jnp.exp(g)).astype(BF16), st.astype(BF16))
            kh = (k * jnp.exp(g_end - g)).astype(BF16)
            upd = _dot_tn(v.astype(BF16), kh)
            st_scr[b] = st * jnp.exp(g_end) + jnp.where(st_mask, upd, 0.0)
            on = o * lax.rsqrt(_seg_mean_sq(o, mseg) + LN_EPS)
            ob_ref[b, rows, :] = (on * gg * _silu(r_ref[b, rows, :])).astype(BF16)
        return carry

    lax.fori_loop(0, n_chunks, body, 0)

    @pl.when(pl.program_id(1) == pl.num_programs(1) - 1)
    def _():
        for b in range(nseq):
            st_ref[b] = st_scr[b]


GLA_SEQS = 4
GLA_ROWS = 512


def _gla_prompt(q, k, g, v, r, s0, gg, nbatch):
    t = q.shape[0] // nbatch
    nb = min(GLA_SEQS, nbatch)
    tr = min(GLA_ROWS, t)
    seq = lambda a: a.reshape(nbatch, t, a.shape[-1])
    blk = lambda width: pl.BlockSpec((nb, tr, width), lambda b, j: (b, j, 0))
    state = pl.BlockSpec((nb, B_WIDTH, GK_WIDTH), lambda b, j: (b, 0, 0))
    return pl.pallas_call(
        functools.partial(_gla_kernel, n_chunks=tr // GLA_CHUNK),
        grid=(nbatch // nb, t // tr),
        in_specs=[blk(GK_WIDTH), blk(GK_WIDTH), blk(GK_WIDTH), blk(B_WIDTH), blk(B_WIDTH),
                  pl.BlockSpec((1, B_WIDTH, GK_WIDTH), lambda b, j: (0, 0, 0)),
                  pl.BlockSpec((1, B_WIDTH), lambda b, j: (0, 0))],
        out_specs=[blk(B_WIDTH), state],
        out_shape=[jax.ShapeDtypeStruct((nbatch, t, B_WIDTH), BF16),
                   jax.ShapeDtypeStruct((nbatch, B_WIDTH, GK_WIDTH), F32)],
        scratch_shapes=[pltpu.VMEM((nb, B_WIDTH, GK_WIDTH), F32)],
        compiler_params=_params("parallel", "arbitrary"),
        name="gla_prompt",
    )(seq(q), seq(k), seq(g), seq(v), seq(r), s0, gg)


def _gla_sample_kernel(q_ref, k_ref, la_ref, v_ref, r_ref, s_ref, gg_ref, ob_ref, sn_ref):
    sb = q_ref.shape[0]
    qt = q_ref[...].T
    kt = k_ref[...].T
    at = jnp.exp(la_ref[...]).T
    v = v_ref[...]
    gg = gg_ref[...]
    for j in range(sb):
        vj = jnp.concatenate(
            [jnp.broadcast_to(v[j:j + 1, h * B_DV:(h + 1) * B_DV], (B_DK, B_DV)) for h in range(B_HEADS)], axis=0)
        s_new = at[:, j:j + 1] * s_ref[j] + kt[:, j:j + 1] * vj
        sn_ref[j] = s_new
        o = jnp.sum((qt[:, j:j + 1] * s_new).reshape(B_HEADS, B_DK, B_DV), axis=1)
        on = o * lax.rsqrt(jnp.mean(o * o, axis=-1, keepdims=True) + LN_EPS)
        ob_ref[j] = on * gg * _silu(r_ref[j])


def _gla_sample(q, k, la, v, r, state, gg):
    ns = q.shape[0]
    sb = min(SAMPLE_BLOCK, ns)
    row = lambda width: pl.BlockSpec((sb, width), lambda i: (i, 0))
    return pl.pallas_call(
        _gla_sample_kernel,
        grid=(ns // sb,),
        in_specs=[row(GK_WIDTH), row(GK_WIDTH), row(GK_WIDTH), row(B_WIDTH),
                  pl.BlockSpec((sb, B_HEADS, B_DV), lambda i: (i, 0, 0)),
                  pl.BlockSpec((sb, GK_WIDTH, B_DV), lambda i: (i, 0, 0)),
                  pl.BlockSpec((1, B_DV), lambda i: (0, 0))],
        out_specs=[pl.BlockSpec((sb, B_HEADS, B_DV), lambda i: (i, 0, 0)),
                   pl.BlockSpec((sb, GK_WIDTH, B_DV), lambda i: (i, 0, 0))],
        out_shape=[jax.ShapeDtypeStruct((ns, B_HEADS, B_DV), F32),
                   jax.ShapeDtypeStruct((ns, GK_WIDTH, B_DV), F32)],
        compiler_params=_params("parallel"),
        name="gla_sample",
    )(q, k, la, v, r, state, gg)


CONV_HIST = 32
CONV_CHUNK = 64


def _conv_post(y, dwb, lng, lnb, pww, pwb):
    y = _silu(_ln_rows(y + dwb, lng, lnb))
    return _dot(y.astype(BF16), pww) + pwb


def _conv_kernel(u_ref, hist_ref, dww_ref, dwb_ref, lng_ref, lnb_ref, pww_ref, pwb_ref,
                 oc_ref, tail_ref, ubuf, *, n_chunks):
    t = u_ref.shape[1]
    ubuf[0, 0:CONV_HIST, :] = hist_ref[...]
    ubuf[0, CONV_HIST:CONV_HIST + t, :] = u_ref[0]
    off = CONV_HIST - (CONV_W - 1)
    n_copy = t + CONV_HIST - SUBLANES
    for s in range(1, SUBLANES):
        ubuf[s, 0:n_copy, :] = ubuf[0, s:s + n_copy, :]

    def body(c, carry):
        r0 = pl.multiple_of(c * CONV_CHUNK, CONV_CHUNK)
        acc = jnp.zeros((CONV_CHUNK, C_WIDTH), F32)
        for j in range(CONV_W):
            a, s = divmod(j + off, SUBLANES)
            rows = pl.ds(pl.multiple_of(r0 + a * SUBLANES, SUBLANES), CONV_CHUNK)
            acc = acc + ubuf[s, rows, :] * dww_ref[j:j + 1, :]
        oc = _conv_post(acc, dwb_ref[...], lng_ref[...], lnb_ref[...], pww_ref[...], pwb_ref[...])
        oc_ref[0, pl.ds(r0, CONV_CHUNK), :] = oc.astype(BF16)
        return carry

    lax.fori_loop(0, n_chunks, body, 0, unroll=2)
    tail_ref[0] = ubuf[0, t:t + CONV_HIST, :]


def _conv_prompt(u, hist, dww, dwb, lng, lnb, pww, pwb, nbatch):
    t = u.shape[0] // nbatch
    const = lambda shape: pl.BlockSpec(shape, lambda b: (0,) * len(shape))
    return pl.pallas_call(
        functools.partial(_conv_kernel, n_chunks=t // CONV_CHUNK),
        grid=(nbatch,),
        in_specs=[pl.BlockSpec((1, t, C_WIDTH), lambda b: (b, 0, 0)), const((CONV_HIST, C_WIDTH)),
                  const((CONV_HIST, C_WIDTH)), const((1, C_WIDTH)), const((1, C_WIDTH)), const((1, C_WIDTH)),
                  const((C_WIDTH, C_WIDTH)), const((1, C_WIDTH))],
        out_specs=[pl.BlockSpec((1, t, C_WIDTH), lambda b: (b, 0, 0)),
                   pl.BlockSpec((1, CONV_HIST, C_WIDTH), lambda b: (b, 0, 0))],
        out_shape=[jax.ShapeDtypeStruct((nbatch, t, C_WIDTH), BF16),
                   jax.ShapeDtypeStruct((nbatch, CONV_HIST, C_WIDTH), F32)],
        scratch_shapes=[pltpu.VMEM((SUBLANES, CONV_HIST + t, C_WIDTH), F32)],
        compiler_params=_params("parallel"),
        name="conv_prompt",
    )(u.reshape(nbatch, t, C_WIDTH), hist, dww, dwb, lng, lnb, pww, pwb)


def _conv_sample_kernel(hist_ref, u_ref, dww_ref, dwb_ref, lng_ref, lnb_ref, pww_ref, pwb_ref, oc_ref, tail_ref):
    nh = CONV_W - 1
    hist = hist_ref[...]
    u = u_ref[...]
    y = jnp.sum(hist * dww_ref[0:nh, :][None], axis=1) + u[:, 0, :] * dww_ref[nh:nh + 1, :]
    oc = _conv_post(y, dwb_ref[...], lng_ref[...], lnb_ref[...], pww_ref[...], pwb_ref[...])
    oc_ref[...] = oc.astype(BF16)
    tail_ref[:, 0:nh - 1, :] = hist_ref[:, 1:nh, :]
    tail_ref[:, nh - 1:nh, :] = u


def _conv_sample(hist, u, dww, dwb, lng, lnb, pww, pwb):
    ns = hist.shape[0]
    nh = CONV_W - 1
    const = lambda shape: pl.BlockSpec(shape, lambda i: (0,) * len(shape))
    return pl.pallas_call(
        _conv_sample_kernel,
        grid=(1,),
        in_specs=[const((ns, nh, C_WIDTH)), const((ns, 1, C_WIDTH)),
                  const((CONV_HIST, C_WIDTH)), const((1, C_WIDTH)), const((1, C_WIDTH)), const((1, C_WIDTH)),
                  const((C_WIDTH, C_WIDTH)), const((1, C_WIDTH))],
        out_specs=[const((ns, C_WIDTH)), const((ns, nh, C_WIDTH))],
        out_shape=[jax.ShapeDtypeStruct((ns, C_WIDTH), BF16),
                   jax.ShapeDtypeStruct((ns, nh, C_WIDTH), F32)],
        compiler_params=_params("arbitrary"),
        name="conv_sample",
    )(hist, u, dww, dwb, lng, lnb, pww, pwb)


def _out_proj_kernel(*refs, pre_ln):
    if pre_ln:
        lg_ref, lb_ref, *refs = refs
    oa_ref, ob_ref, oc_ref, h_ref, w_ref, g_ref, b_ref, o_ref = refs
    h = h_ref[...]
    if pre_ln:
        h = _ln_rows(h, lg_ref[...], lb_ref[...])
    mix = (_dot(oa_ref[...], w_ref[0:A_WIDTH, :])
           + _dot(ob_ref[...], w_ref[A_WIDTH:A_WIDTH + B_WIDTH, :])
           + _dot(oc_ref[...], w_ref[A_WIDTH + B_WIDTH:D_MODEL, :]))
    o_ref[...] = _ln_rows(ALPHA * h + mix, g_ref[...], b_ref[...])


def _out_proj(oa, ob, oc, h, w, g, b, tm, ln=None):
    n = h.shape[0]
    row = lambda width: pl.BlockSpec((tm, width), lambda i: (i, 0))
    const = lambda shape: pl.BlockSpec(shape, lambda i: (0, 0))
    ln_specs = [const((1, D_MODEL)), const((1, D_MODEL))] if ln else []
    return pl.pallas_call(
        functools.partial(_out_proj_kernel, pre_ln=bool(ln)),
        grid=(n // tm,),
        in_specs=ln_specs + [row(A_WIDTH), row(B_WIDTH), row(C_WIDTH), row(D_MODEL),
                             const((D_MODEL, D_MODEL)), const((1, D_MODEL)), const((1, D_MODEL))],
        out_specs=row(D_MODEL),
        out_shape=jax.ShapeDtypeStruct((n, D_MODEL), F32),
        compiler_params=_params("parallel"),
        name="out_proj",
    )(*(ln or ()), oa, ob, oc, h, w, g, b)


ROUTE_LANE0 = N_GROUPS


def _route(x, wr_hi, wr_lo, br):
    x_hi = x.astype(BF16)
    x_lo = (x - x_hi.astype(F32)).astype(BF16)
    logits = _dot(x_hi, wr_hi) + _dot(x_lo, wr_hi) + _dot(x_hi, wr_lo) + br
    lane = lax.broadcasted_iota(jnp.int32, logits.shape, 1).astype(F32)
    far = 1e3
    glm = jnp.where(lane < N_GROUPS, logits, NEG)
    gmax = jnp.max(glm, axis=1, keepdims=True)
    gi = jnp.min(jnp.where(glm == gmax, lane, far), axis=1, keepdims=True)
    p_grp = 1.0 / jnp.sum(jnp.exp(glm - gmax), axis=1, keepdims=True)
    lo = ROUTE_LANE0 + EXP_PER_GROUP * gi
    in_sel = (lane >= lo) & (lane < lo + EXP_PER_GROUP)
    elm = jnp.where(in_sel, logits, NEG)
    v1 = jnp.max(elm, axis=1, keepdims=True)
    i1 = jnp.min(jnp.where(elm == v1, lane, far), axis=1, keepdims=True)
    elm2 = jnp.where(lane == i1, NEG, elm)
    v2 = jnp.max(elm2, axis=1, keepdims=True)
    i2 = jnp.min(jnp.where((elm2 == v2) & in_sel & (lane != i1), lane, far), axis=1, keepdims=True)
    t = jnp.exp(v2 - v1)
    w1 = p_grp / (1.0 + t)
    w2 = w1 * t
    return jnp.where(lane == i1, w1, 0.0) + jnp.where(lane == i2, w2, 0.0), gi


MOE_ROWS = 128
MOE_ALIGN = 16
GROUP_WIDTH = EXP_PER_GROUP * D_EXPERT


def _moe_sorted_rows(tm):
    need = tm + N_GROUPS * MOE_ALIGN + MOE_ROWS
    return -(-need // LANES) * LANES


def _moe_kernel(x_ref, wrh_ref, wrl_ref, br_ref, wg_ref, wu_ref, wd_ref, g_ref, b_ref, o_ref,
                xs_ref, ys_ref, ws_ref):
    x = x_ref[...]
    tm = x.shape[0]
    ns = xs_ref.shape[0]
    iota = lambda shape, axis: lax.broadcasted_iota(jnp.int32, shape, axis)
    dw, gi = _route(x, wrh_ref[...], wrl_ref[...], br_ref[...])
    lane_f = iota((tm, LANES), 1).astype(F32)
    onehot = jnp.where(lane_f == gi, 1.0, 0.0)
    tri = jnp.where(iota((tm, tm), 1) <= iota((tm, tm), 0), 1.0, 0.0).astype(BF16)
    cum = _dot(tri, onehot.astype(BF16))
    rank = jnp.sum(onehot * (cum - 1.0), axis=1, keepdims=True)
    counts = cum[tm - 1:tm, :]

    starts, tiles = [], []
    start = jnp.int32(0)
    for grp in range(N_GROUPS):
        n_g = counts[0, grp].astype(jnp.int32)
        starts.append(start)
        tiles.append((n_g + (MOE_ROWS - 1)) >> (MOE_ROWS.bit_length() - 1))
        start = start + ((n_g + (MOE_ALIGN - 1)) & -MOE_ALIGN)
    lane1 = iota((1, LANES), 1)
    start_v = jnp.zeros((1, LANES), F32)
    for grp in range(N_GROUPS):
        start_v = jnp.where(lane1 == grp, starts[grp].astype(F32), start_v)
    pos = jnp.sum(onehot * start_v, axis=1, keepdims=True) + rank

    pos_i = pos.astype(jnp.int32)
    digits = jnp.where(lane_f == 0.0, (pos_i >> 5).astype(F32),
                       jnp.where(lane_f == 1.0, (pos_i & 31).astype(F32), 0.0)).astype(BF16)
    lane8 = iota((8, LANES), 1)
    radix = jnp.where(lane8 == 0, 32.0, jnp.where(lane8 == 1, 1.0, 0.0)).astype(BF16)
    pos_row = _dot_nt(radix, digits)[0:1, :]

    perm = jnp.where(iota((ns, tm), 0).astype(F32) == pos_row, 1.0, 0.0).astype(BF16)
    xs_ref[...] = _dot(perm, x.astype(BF16)).astype(BF16)
    dw_hi = dw.astype(BF16)
    dw_lo = (dw - dw_hi.astype(F32)).astype(BF16)
    ws_ref[...] = _dot(perm, dw_hi) + _dot(perm, dw_lo)
    ys_ref[...] = jnp.zeros_like(ys_ref)

    for grp in range(N_GROUPS):
        def body(k, carry, grp=grp):
            r0 = pl.multiple_of(starts[grp] + k * MOE_ROWS, MOE_ALIGN)
            rows = pl.ds(r0, MOE_ROWS)
            xt = xs_ref[rows, :]
            w = ws_ref[rows, :]
            hes = []
            for e in range(EXP_PER_GROUP):
                ex = grp * EXP_PER_GROUP + e
                w_e = w[:, ROUTE_LANE0 + ex:ROUTE_LANE0 + ex + 1]
                hes.append((_silu(_dot(xt, wg_ref[ex])) * _dot(xt, wu_ref[ex]) * w_e).astype(BF16))
            ys_ref[rows, :] = _dot(jnp.concatenate(hes, axis=1), wd_ref[grp]).astype(BF16)
            return carry
        lax.fori_loop(0, tiles[grp], body, 0)

    unperm = jnp.where(iota((tm, ns), 1).astype(F32) == pos, 1.0, 0.0).astype(BF16)
    y = _dot(unperm, ys_ref[...])
    o_ref[...] = _ln_rows(ALPHA * x + y, g_ref[...], b_ref[...])


def _moe(x, wrh, wrl, br, wg, wu, wd, g, b, tm):
    n = x.shape[0]
    ns = _moe_sorted_rows(tm)
    const = lambda shape: pl.BlockSpec(shape, lambda i: (0,) * len(shape))
    resident = lambda shape: pl.BlockSpec(shape, lambda i: (0,) * len(shape), pipeline_mode=pl.Buffered(1))
    return pl.pallas_call(
        _moe_kernel,
        grid=(n // tm,),
        in_specs=[pl.BlockSpec((tm, D_MODEL), lambda i: (i, 0)),
                  const((D_MODEL, LANES)), const((D_MODEL, LANES)), const((1, LANES)),
                  resident((N_EXPERTS, D_MODEL, D_EXPERT)), resident((N_EXPERTS, D_MODEL, D_EXPERT)),
                  resident((N_GROUPS, GROUP_WIDTH, D_MODEL)),
                  const((1, D_MODEL)), const((1, D_MODEL))],
        out_specs=pl.BlockSpec((tm, D_MODEL), lambda i: (i, 0)),
        out_shape=jax.ShapeDtypeStruct((n, D_MODEL), F32),
        scratch_shapes=[pltpu.VMEM((ns, D_MODEL), BF16), pltpu.VMEM((ns, D_MODEL), BF16),
                        pltpu.VMEM((ns, LANES), F32)],
        compiler_params=_params("parallel"),
        name="moe",
    )(x, wrh, wrl, br, wg, wu, wd, g, b)


def _rope_tables(pos):
    half = HEAD_DIM // 2
    inv = ROPE_THETA ** (-jnp.arange(half, dtype=F32) / half)
    ang = pos.astype(F32)[:, None] * inv[None, :]
    cos, sin = jnp.cos(ang), jnp.sin(ang)
    cos_t = jnp.concatenate([cos, cos] * (LANES // HEAD_DIM), axis=1)
    sin_t = jnp.concatenate([-sin, sin] * (LANES // HEAD_DIM), axis=1)
    return cos_t, sin_t


def _row(v):
    return v.reshape(1, -1)


def kernel(x_prompt, x_sample, cache_meta_k, cache_meta_v, cache_win_k, cache_win_v, state_gla, state_conv,
           meta_tokens, ln_in_g, ln_in_b, w_in, attn_sink, w_alpha, b_alpha, gla_norm_g,
           conv_dw_w, conv_dw_b, conv_ln_g, conv_ln_b, conv_pw_w, conv_pw_b, w_out, ln1_g, ln1_b,
           w_router_group, b_router_group, w_router_expert, b_router_expert,
           w_exp_gate, w_exp_up, w_exp_down, ln2_g, ln2_b):
    nb, seq, d = x_prompt.shape
    ns = x_sample.shape[0]
    nwin = cache_win_k.shape[2]
    n_big = nb * seq
    n_small = BLOCK + ns
    tm_big = min(512, seq)

    small_in = jnp.concatenate([jnp.zeros((META_PAD, d), F32), meta_tokens.astype(F32),
                                x_sample.reshape(ns, d)], axis=0)
    hb = x_prompt.reshape(n_big, d)
    hs = small_in
    ln_in = (_row(ln_in_g), _row(ln_in_b))

    cos_b, sin_b = _rope_tables(N_META + jnp.arange(seq))
    pos_small = jnp.concatenate([jnp.maximum(jnp.arange(BLOCK) - META_PAD, 0),
                                 jnp.full((ns,), PAST_LEN, jnp.int32)])
    cos_s, sin_s = _rope_tables(pos_small)

    outs = [[] for _ in range(12)]
    zeros_hist = jnp.zeros((CONV_HIST, C_WIDTH), F32)
    zeros_state = jnp.zeros((1, B_WIDTH, GK_WIDTH), F32)
    pad_cols = jnp.zeros((d, LANES - B_RANK), F32)

    for l in range(DEPTH):
        wi = w_in[l]
        w_pad = jnp.concatenate([wi[:, :C_AB + B_RANK], pad_cols, wi[:, C_AB + B_RANK:]], axis=1).astype(BF16)
        wa = jnp.concatenate([w_alpha[l], jnp.zeros((LANES - B_RANK, GK_WIDTH), F32)], axis=0).astype(BF16)
        ba = _row(b_alpha[l])
        sink = attn_sink[l].astype(F32)
        sink_col = jnp.concatenate([sink, jnp.zeros((HEAD_ROWS - A_HEADS,), F32)])[:, None]
        gg_t = _row(jnp.tile(gla_norm_g[l], B_HEADS))
        gg_h = _row(gla_norm_g[l])
        dww = jnp.concatenate([conv_dw_w[l], jnp.zeros((CONV_HIST - CONV_W, C_WIDTH), F32)], axis=0)
        dwb, clg, clb = _row(conv_dw_b[l]), _row(conv_ln_g[l]), _row(conv_ln_b[l])
        pww, pwb = conv_pw_w[l].astype(BF16), _row(conv_pw_b[l])
        wo = w_out[l].astype(BF16)
        wr = jnp.concatenate([w_router_group[l], w_router_expert[l],
                              jnp.zeros((d, LANES - N_GROUPS - N_EXPERTS), F32)], axis=1)
        wr_hi = wr.astype(BF16)
        wr_lo = (wr - wr_hi.astype(F32)).astype(BF16)
        br = _row(jnp.concatenate([b_router_group[l], b_router_expert[l],
                                   jnp.zeros((LANES - N_GROUPS - N_EXPERTS,), F32)]))
        wg, wu = w_exp_gate[l].astype(BF16), w_exp_up[l].astype(BF16)
        wd = w_exp_down[l].astype(BF16).reshape(N_GROUPS, GROUP_WIDTH, d)

        ln = ln_in if l == 0 else None
        qa_b, ka_b, va_b, qg_b, kg_b, la_b, vg_b, rg_b, u_b = _proj_in(
            hb, w_pad, wa, ba, cos_b, sin_b, tm=tm_big, n_pad=0, n_seq=tm_big, ln=ln)
        qa_s, ka_s, va_s, qg_s, kg_s, la_s, vg_s, rg_s, u_s = _proj_in(
            hs, w_pad, wa, ba, cos_s, sin_s, tm=n_small, n_pad=META_PAD, n_seq=BLOCK, ln=ln)

        oa_b = _attn_prompt(sink, qa_b, ka_b, va_b, ka_s, va_s, nb)
        oa_m = _attn_meta(sink, qa_s, ka_s, va_s)
        q_smp = qa_s[BLOCK:].reshape(ns, A_KV_HEADS, A_GROUP, 1, HEAD_DIM)
        eye = jnp.eye(A_KV_HEADS, dtype=BF16)[None, :, None, :, None]
        q_bd = (q_smp * eye).reshape(ns, A_HEADS, KV_WIDTH)
        q_bd = jnp.pad(q_bd, ((0, 0), (0, HEAD_ROWS - A_HEADS), (0, 0)))
        o_bd, nwk, nwv = _attn_sample(
            sink_col, q_bd, ka_s[BLOCK:].reshape(ns, 1, KV_WIDTH), va_s[BLOCK:].reshape(ns, 1, KV_WIDTH),
            cache_meta_k[l].reshape(ns, N_META, KV_WIDTH), cache_meta_v[l].reshape(ns, N_META, KV_WIDTH),
            cache_win_k[l].reshape(ns, nwin, KV_WIDTH), cache_win_v[l].reshape(ns, nwin, KV_WIDTH))
        o_bd = o_bd[:, :A_HEADS].reshape(ns, A_KV_HEADS, A_GROUP, A_KV_HEADS, HEAD_DIM)
        oa_smp = jnp.stack([o_bd[:, c, :, c, :] for c in range(A_KV_HEADS)], axis=1).reshape(ns, A_WIDTH)
        oa_s = jnp.concatenate([oa_m, oa_smp.astype(BF16)], axis=0)

        ob_m, st_m = _gla_prompt(qg_s[:BLOCK], kg_s[:BLOCK], la_s[:BLOCK], vg_s[:BLOCK], rg_s[:BLOCK],
                                 zeros_state, gg_t, 1)
        ob_b, st_b = _gla_prompt(qg_b, kg_b, la_b, vg_b, rg_b, st_m, gg_t, nb)
        ob_smp, s_new = _gla_sample(qg_s[BLOCK:], kg_s[BLOCK:], la_s[BLOCK:], vg_s[BLOCK:],
                                    rg_s[BLOCK:].reshape(ns, B_HEADS, B_DV),
                                    state_gla[l].reshape(ns, GK_WIDTH, B_DV), gg_h)
        ob_s = jnp.concatenate([ob_m.reshape(BLOCK, B_WIDTH), ob_smp.reshape(ns, B_WIDTH).astype(BF16)], axis=0)
        st5 = st_b.reshape(nb, B_HEADS, B_DV, B_HEADS, B_DK)
        gla_p = jnp.stack([st5[:, h, :, h, :] for h in range(B_HEADS)], axis=1).transpose(0, 1, 3, 2)

        oc_m, _ = _conv_prompt(u_s[:BLOCK], zeros_hist, dww, dwb, clg, clb, pww, pwb, 1)
        oc_b, tail_b = _conv_prompt(u_b, u_s[BLOCK - CONV_HIST:BLOCK], dww, dwb, clg, clb, pww, pwb, nb)
        oc_smp, tail_s = _conv_sample(state_conv[l], u_s[BLOCK:].reshape(ns, 1, C_WIDTH),
                                      dww, dwb, clg, clb, pww, pwb)
        oc_s = jnp.concatenate([oc_m.reshape(BLOCK, C_WIDTH), oc_smp], axis=0)

        l1g, l1b, l2g, l2b = _row(ln1_g[l]), _row(ln1_b[l]), _row(ln2_g[l]), _row(ln2_b[l])
        hb1 = _out_proj(oa_b, ob_b.reshape(n_big, B_WIDTH), oc_b.reshape(n_big, C_WIDTH), hb, wo, l1g, l1b,
                        tm_big, ln=ln)
        hs1 = _out_proj(oa_s, ob_s, oc_s, hs, wo, l1g, l1b, n_small, ln=ln)
        hb = _moe(hb1, wr_hi, wr_lo, br, wg, wu, wd, l2g, l2b, tm_big)
        hs = _moe(hs1, wr_hi, wr_lo, br, wg, wu, wd, l2g, l2b, n_small)

        kv4 = lambda a: a.reshape(a.shape[0], a.shape[1], A_KV_HEADS, HEAD_DIM)
        meta_k = jnp.broadcast_to(ka_s[META_PAD:BLOCK][None], (nb, N_META, KV_WIDTH))
        meta_v = jnp.broadcast_to(va_s[META_PAD:BLOCK][None], (nb, N_META, KV_WIDTH))
        win_k = ka_b.reshape(nb, seq, KV_WIDTH)[:, seq - nwin:]
        win_v = va_b.reshape(nb, seq, KV_WIDTH)[:, seq - nwin:]
        layer_out = (None, None, kv4(meta_k), kv4(meta_v), kv4(win_k), kv4(win_v), kv4(nwk), kv4(nwv),
                     gla_p, s_new.reshape(ns, B_HEADS, B_DK, B_DV),
                     tail_b[:, CONV_HIST - (CONV_W - 1):], tail_s)
        for i in range(2, 12):
            outs[i].append(layer_out[i])

    y_prompt = hb.reshape(nb, seq, d)
    y_sample = hs[BLOCK:].reshape(ns, 1, d)
    return (y_prompt, y_sample) + tuple(jnp.stack(o) for o in outs[2:])
```

```python
import functools

import jax
import jax.numpy as jnp
from jax import lax
from jax.experimental import pallas as pl
from jax.experimental.pallas import tpu as pltpu

F32 = jnp.float32
BF16 = jnp.bfloat16

D_MODEL = 1024
DEPTH = 2
PAST_LEN = 16384
N_META = 16
HEAD_DIM = 64
A_WIDTH = 512
A_HEADS = 8
A_KV_HEADS = 2
A_GROUP = 4
WINDOW = 128
BLOCK = 128
ROPE_THETA = 10000.0
B_WIDTH = 256
B_HEADS = 4
B_DV = 64
B_DK = 32
B_RANK = 16
GATE_TAU = 16.0
GLA_CHUNK = 64
C_WIDTH = 256
CONV_W = 31
N_GROUPS = 4
EXP_PER_GROUP = 4
N_EXPERTS = 16
D_EXPERT = 256
ALPHA = (2 * DEPTH) ** 0.25
LN_EPS = 1e-5

LANES = 128
SUBLANES = 8
META_PAD = BLOCK - N_META
KV_WIDTH = A_KV_HEADS * HEAD_DIM
GK_WIDTH = B_HEADS * B_DK
C_QA = 0
C_KA = C_QA + A_WIDTH
C_VA = C_KA + KV_WIDTH
C_QB = C_VA + KV_WIDTH
C_KB = C_QB + GK_WIDTH
C_VB = C_KB + GK_WIDTH
C_RB = C_VB + B_WIDTH
C_AB = C_RB + B_WIDTH
C_CG = C_AB + LANES
PROJ_PAD_WIDTH = C_CG + 2 * C_WIDTH
NEG = -1e30
VMEM_LIMIT = 56 * 1024 * 1024


def _dot(a, b):
    return jnp.dot(a, b, preferred_element_type=F32)


def _dot_nt(a, b):
    return lax.dot_general(a, b, (((1,), (1,)), ((), ())), preferred_element_type=F32)


def _dot_tn(a, b):
    return lax.dot_general(a, b, (((0,), (0,)), ((), ())), preferred_element_type=F32)


def _ln_rows(x, g, b):
    xc = x - jnp.mean(x, -1, keepdims=True)
    var = jnp.mean(xc * xc, -1, keepdims=True)
    return xc * lax.rsqrt(var + LN_EPS) * g + b


def _silu(x):
    return x * jax.nn.sigmoid(x)


def _split3(x):
    hi = x.astype(BF16)
    r1 = x - hi.astype(F32)
    mid = r1.astype(BF16)
    lo = (r1 - mid.astype(F32)).astype(BF16)
    return hi, mid, lo


def _params(*sem):
    return pltpu.CompilerParams(dimension_semantics=sem, vmem_limit_bytes=VMEM_LIMIT)


def _proj_in_kernel(*refs, n_pad, n_seq, pre_ln):
    if pre_ln:
        lg_ref, lb_ref, *refs = refs
    (x_ref, w_ref, wa_ref, ba_ref, cos_ref, sin_ref,
     qa_ref, ka_ref, va_ref, qg_ref, kg_ref, la_ref, vg_ref, rg_ref, u_ref) = refs
    x = x_ref[...]
    if pre_ln:
        x = _ln_rows(x, lg_ref[...], lb_ref[...])
    xb = x.astype(BF16)
    tm = xb.shape[0]
    cos = cos_ref[...]
    sin = sin_ref[...]
    lane = lax.broadcasted_iota(jnp.int32, (tm, LANES), 1)
    first_half = (lane & (HEAD_DIM // 2)) == 0

    def rope(z):
        rot = jnp.where(first_half, pltpu.roll(z, LANES - HEAD_DIM // 2, 1), pltpu.roll(z, HEAD_DIM // 2, 1))
        return z * cos + rot * sin

    if n_pad:
        valid = (lax.broadcasted_iota(jnp.int32, (tm, 1), 0) >= n_pad).astype(F32)
    else:
        valid = None

    za = _dot(xb, w_ref[:, C_QA:C_QB])
    for c in range(A_WIDTH // LANES):
        zq = za[:, c * LANES:(c + 1) * LANES]
        qa_ref[:, c * LANES:(c + 1) * LANES] = (rope(zq) * (HEAD_DIM ** -0.5)).astype(BF16)
    ka_ref[...] = rope(za[:, C_KA:C_VA])
    va_ref[...] = za[:, C_VA:C_QB]

    zb = _dot(xb, w_ref[:, C_QB:C_CG])
    o = C_QB
    qg_ref[...] = zb[:, C_QB - o:C_KB - o] * (B_DK ** -0.5)
    kg = zb[:, C_KB - o:C_VB - o]
    vg_ref[...] = zb[:, C_VB - o:C_RB - o]
    rg_ref[...] = zb[:, C_RB - o:C_AB - o]
    ab = zb[:, C_AB - o:C_CG - o].astype(BF16)
    xa = _dot(ab, wa_ref[...]) + ba_ref[...]
    la = (jnp.minimum(xa, 0.0) - jnp.log(1.0 + jnp.exp(-jnp.abs(xa)))) * (1.0 / GATE_TAU)

    zc = _dot(xb, w_ref[:, C_CG:PROJ_PAD_WIDTH])
    u = zc[:, :C_WIDTH] * jax.nn.sigmoid(zc[:, C_WIDTH:])
    if valid is not None:
        kg = kg * valid
        la = la * valid
        u = u * valid
    kg_ref[...] = kg
    u_ref[...] = u
    cs = GLA_CHUNK
    tri = jnp.where(lax.broadcasted_iota(jnp.int32, (cs, cs), 1) <= lax.broadcasted_iota(jnp.int32, (cs, cs), 0),
                    1.0, 0.0).astype(BF16)
    la_h, la_m, la_l = _split3(la)
    for c in range(n_seq // cs):
        rows = slice(c * cs, (c + 1) * cs)
        la_ref[rows, :] = _dot(tri, la_h[rows]) + _dot(tri, la_m[rows]) + _dot(tri, la_l[rows])
    if n_seq < tm:
        la_ref[n_seq:tm, :] = la[n_seq:tm]


def _proj_in(h, w, wa, ba, cos, sin, *, tm, n_pad, n_seq, ln=None):
    n = h.shape[0]
    tb = cos.shape[0] // tm
    row = lambda width: pl.BlockSpec((tm, width), lambda i: (i, 0))
    const = lambda shape: pl.BlockSpec(shape, lambda i: (0, 0))
    tab = pl.BlockSpec((tm, LANES), lambda i: (i % tb, 0))
    widths = (A_WIDTH, KV_WIDTH, KV_WIDTH, GK_WIDTH, GK_WIDTH, GK_WIDTH, B_WIDTH, B_WIDTH, C_WIDTH)
    dtypes = (BF16,) + (F32,) * 8
    ln_specs = [const((1, D_MODEL)), const((1, D_MODEL))] if ln else []
    return pl.pallas_call(
        functools.partial(_proj_in_kernel, n_pad=n_pad, n_seq=n_seq, pre_ln=bool(ln)),
        grid=(n // tm,),
        in_specs=ln_specs + [row(D_MODEL), const((D_MODEL, PROJ_PAD_WIDTH)), const((LANES, GK_WIDTH)),
                             const((1, GK_WIDTH)), tab, tab],
        out_specs=[row(wd) for wd in widths],
        out_shape=[jax.ShapeDtypeStruct((n, wd), dt) for wd, dt in zip(widths, dtypes)],
        compiler_params=_params("parallel"),
        name="proj_in",
    )(*(ln or ()), h, w, wa, ba, cos, sin)


def _attn_kernel(sink_ref, q_ref, km_ref, vm_ref, *rest, meta_mode):
    if meta_mode:
        (o_ref,) = rest
        _attn_block(sink_ref, q_ref[...], km_ref[...], vm_ref[...], o_ref, 0, None)
    else:
        kp_ref, vp_ref, kc_ref, vc_ref, o_ref = rest
        km, vm = km_ref[...], vm_ref[...]
        for sub in range(q_ref.shape[0] // BLOCK):
            cur = slice(sub * BLOCK, (sub + 1) * BLOCK)
            if sub == 0:
                kp, vp = kp_ref[...], vp_ref[...]
                has_prev = pl.program_id(1) >= 1
            else:
                prev = slice((sub - 1) * BLOCK, sub * BLOCK)
                kp, vp = kc_ref[prev, :], vc_ref[prev, :]
                has_prev = True
            k_all = jnp.concatenate([kp, kc_ref[cur, :], km], axis=0)
            v_all = jnp.concatenate([vp, vc_ref[cur, :], vm], axis=0)
            _attn_block(sink_ref, q_ref[cur, :], k_all, v_all, o_ref, sub * BLOCK, has_prev)


def _attn_block(sink_ref, q, k_all, v_all, o_ref, row0, has_prev):
    nk = k_all.shape[0]
    ki = lax.broadcasted_iota(jnp.int32, (nk, BLOCK), 0)
    qi = lax.broadcasted_iota(jnp.int32, (nk, BLOCK), 1)
    if has_prev is None:
        ok = ki <= qi - META_PAD
    else:
        prev_lo = qi if has_prev is True else qi + jnp.where(has_prev, 0, BLOCK)
        ok = ((ki >= prev_lo) & (ki < BLOCK)) | ((ki >= BLOCK) & (ki <= qi + BLOCK)) | (ki >= 2 * BLOCK)
    bias = jnp.where(ok, 0.0, NEG)
    k_all = k_all.astype(BF16)
    v_all = v_all.astype(BF16)
    for kvh in range(A_KV_HEADS):
        heads = [kvh * A_GROUP + g for g in range(A_GROUP)]
        qs = jnp.concatenate([q[:, h * HEAD_DIM:(h + 1) * HEAD_DIM] for h in heads], axis=0)
        kk = k_all[:, kvh * HEAD_DIM:(kvh + 1) * HEAD_DIM]
        vv = v_all[:, kvh * HEAD_DIM:(kvh + 1) * HEAD_DIM]
        st = _dot_nt(kk, qs)
        ps, dens = [], []
        for g, h in enumerate(heads):
            s = st[:, g * BLOCK:(g + 1) * BLOCK] + bias
            sink = sink_ref[h]
            m = jnp.maximum(jnp.max(s, axis=0, keepdims=True), sink)
            p = jnp.exp(s - m)
            dens.append(jnp.sum(p, axis=0, keepdims=True) + jnp.exp(sink - m))
            ps.append(p.astype(BF16))
        ot = _dot_tn(vv, jnp.concatenate(ps, axis=1)) / jnp.concatenate(dens, axis=1)
        for pair in range(A_GROUP // 2):
            two = jnp.concatenate([ot[:, (2 * pair) * BLOCK:(2 * pair + 1) * BLOCK],
                                   ot[:, (2 * pair + 1) * BLOCK:(2 * pair + 2) * BLOCK]], axis=0)
            h0 = heads[2 * pair]
            o_ref[row0:row0 + BLOCK, h0 * HEAD_DIM:(h0 + 2) * HEAD_DIM] = two.T.astype(BF16)


ATTN_SUB = 2


def _attn_prompt(sink, q, k, v, k_small, v_small, nbatch):
    n = q.shape[0]
    nblk = n // BLOCK // nbatch
    sub = ATTN_SUB if nblk % ATTN_SUB == 0 else 1
    nstep = nblk // sub
    kvs = pl.BlockSpec((sub * BLOCK, KV_WIDTH), lambda b, j: (b * nstep + j, 0))
    kvp = pl.BlockSpec((BLOCK, KV_WIDTH), lambda b, j: (b * nblk + jnp.maximum(j * sub - 1, 0), 0))
    kvm = pl.BlockSpec((N_META, KV_WIDTH), lambda b, j: (META_PAD // N_META, 0))
    return pl.pallas_call(
        functools.partial(_attn_kernel, meta_mode=False),
        grid=(nbatch, nstep),
        in_specs=[pl.BlockSpec(memory_space=pltpu.SMEM),
                  pl.BlockSpec((sub * BLOCK, A_WIDTH), lambda b, j: (b * nstep + j, 0)),
                  kvm, kvm, kvp, kvp, kvs, kvs],
        out_specs=pl.BlockSpec((sub * BLOCK, A_WIDTH), lambda b, j: (b * nstep + j, 0)),
        out_shape=jax.ShapeDtypeStruct((n, A_WIDTH), BF16),
        compiler_params=_params("parallel", "parallel"),
        name="attn_prompt",
    )(sink, q, k_small, v_small, k, v, k, v)


def _attn_meta(sink, q_small, k_small, v_small):
    kvm = pl.BlockSpec((N_META, KV_WIDTH), lambda i: (META_PAD // N_META, 0))
    return pl.pallas_call(
        functools.partial(_attn_kernel, meta_mode=True),
        grid=(1,),
        in_specs=[pl.BlockSpec(memory_space=pltpu.SMEM),
                  pl.BlockSpec((BLOCK, A_WIDTH), lambda i: (0, 0)), kvm, kvm],
        out_specs=pl.BlockSpec((BLOCK, A_WIDTH), lambda i: (0, 0)),
        out_shape=jax.ShapeDtypeStruct((BLOCK, A_WIDTH), BF16),
        compiler_params=_params("arbitrary"),
        name="attn_meta",
    )(sink, q_small, k_small, v_small)


SAMPLE_BLOCK = 32
HEAD_ROWS = 16


def _attn_sample_kernel(sink_ref, q_ref, kn_ref, vn_ref, mk_ref, mv_ref, wk_ref, wv_ref,
                        o_ref, nwk_ref, nwv_ref):
    sb = q_ref.shape[0]
    kn = kn_ref[...]
    vn = vn_ref[...]
    rep = 8
    k_ext = jnp.concatenate([mk_ref[...], wk_ref[...], jnp.broadcast_to(kn, (sb, rep, KV_WIDTH))], axis=1)
    v_ext = jnp.concatenate([mv_ref[...], wv_ref[...], jnp.broadcast_to(vn, (sb, rep, KV_WIDTH))], axis=1)
    nk = k_ext.shape[1]
    n_real = nk - rep + 1
    s = jnp.einsum('bhd,bjd->bhj', q_ref[...], k_ext.astype(BF16), preferred_element_type=F32)
    col = lax.broadcasted_iota(jnp.int32, s.shape, 2)
    s = jnp.where(col < n_real, s, NEG)
    sink = sink_ref[...][None]
    m = jnp.maximum(jnp.max(s, axis=2, keepdims=True), sink)
    p = jnp.exp(s - m)
    denom = jnp.sum(p, axis=2, keepdims=True) + jnp.exp(sink - m)
    o = jnp.einsum('bhj,bjd->bhd', p.astype(BF16), v_ext.astype(BF16), preferred_element_type=F32)
    o_ref[...] = o / denom
    nwin = wk_ref.shape[1]
    nwk_ref[:, 0:nwin - 1, :] = wk_ref[:, 1:nwin, :]
    nwk_ref[:, nwin - 1:nwin, :] = kn
    nwv_ref[:, 0:nwin - 1, :] = wv_ref[:, 1:nwin, :]
    nwv_ref[:, nwin - 1:nwin, :] = vn


def _attn_sample(sink_col, q_bd, k_new, v_new, meta_k, meta_v, win_k, win_v):
    ns = q_bd.shape[0]
    sb = min(SAMPLE_BLOCK, ns)
    nwin = win_k.shape[1]
    blk = lambda rows: pl.BlockSpec((sb, rows, KV_WIDTH), lambda i: (i, 0, 0))
    return pl.pallas_call(
        _attn_sample_kernel,
        grid=(ns // sb,),
        in_specs=[pl.BlockSpec((HEAD_ROWS, 1), lambda i: (0, 0)),
                  blk(HEAD_ROWS), blk(1), blk(1), blk(N_META), blk(N_META), blk(nwin), blk(nwin)],
        out_specs=[blk(HEAD_ROWS), blk(nwin), blk(nwin)],
        out_shape=[jax.ShapeDtypeStruct((ns, HEAD_ROWS, KV_WIDTH), F32),
                   jax.ShapeDtypeStruct((ns, nwin, KV_WIDTH), F32),
                   jax.ShapeDtypeStruct((ns, nwin, KV_WIDTH), F32)],
        compiler_params=_params("parallel"),
        name="attn_sample",
    )(sink_col, q_bd, k_new, v_new, meta_k, meta_v, win_k, win_v)


def _seg_mean_sq(o, mseg):
    sq = o * o
    hi = sq.astype(BF16)
    lo = (sq - hi.astype(F32)).astype(BF16)
    return _dot(hi, mseg) + _dot(lo, mseg)


def _gla_kernel(q_ref, k_ref, g_ref, v_ref, r_ref, s0_ref, gg_ref, ob_ref, st_ref, st_scr, *, n_chunks):
    cs = GLA_CHUNK
    nseq = q_ref.shape[0]

    @pl.when(pl.program_id(1) == 0)
    def _():
        for b in range(nseq):
            st_scr[b] = s0_ref[0]

    iota = lambda shape, axis: lax.broadcasted_iota(jnp.int32, shape, axis)
    lg_cs, lg_dk, lg_dv = cs.bit_length() - 1, B_DK.bit_length() - 1, B_DV.bit_length() - 1
    kd_mask = (iota((B_HEADS * cs, GK_WIDTH), 0) >> lg_cs) == (iota((B_HEADS * cs, GK_WIDTH), 1) >> lg_dk)
    vd_mask = (iota((B_HEADS * cs, B_WIDTH), 0) >> lg_cs) == (iota((B_HEADS * cs, B_WIDTH), 1) >> lg_dv)
    st_mask = (iota((B_WIDTH, GK_WIDTH), 0) >> lg_dv) == (iota((B_WIDTH, GK_WIDTH), 1) >> lg_dk)
    causal = (iota((cs, B_HEADS * cs), 1) & (cs - 1)) <= iota((cs, B_HEADS * cs), 0)
    mseg = jnp.where((iota((B_WIDTH, B_WIDTH), 0) >> lg_dv) == (iota((B_WIDTH, B_WIDTH), 1) >> lg_dv),
                     1.0 / B_DV, 0.0).astype(BF16)
    gg = gg_ref[...]

    def body(c, carry):
        rows = pl.ds(pl.multiple_of(c * cs, cs), cs)
        for b in range(nseq):
            q = q_ref[b, rows, :]
            k = k_ref[b, rows, :]
            v = v_ref[b, rows, :]
            g = g_ref[b, rows, :]
            g_end = g[cs - 1:cs, :]
            g_mid = g[cs // 2 - 1:cs // 2, :]
            qt = (q * jnp.exp(g - g_mid)).astype(BF16)
            kt = k * jnp.exp(g_mid - g)
            kbd = jnp.where(kd_mask, jnp.concatenate([kt] * B_HEADS, axis=0), 0.0).astype(BF16)
            att = jnp.where(causal, _dot_nt(qt, kbd), 0.0)
            vbd = jnp.where(vd_mask, jnp.concatenate([v] * B_HEADS, axis=0), 0.0).astype(BF16)
            st = st_scr[b]
            o = _dot(att.astype(BF16), vbd) + _dot_nt((q * jnp.exp(g)).astype(BF16), st.astype(BF16))
            kh = (k * jnp.exp(g_end - g)).astype(BF16)
            upd = _dot_tn(v.astype(BF16), kh)
            st_scr[b] = st * jnp.exp(g_end) + jnp.where(st_mask, upd, 0.0)
            on = o * lax.rsqrt(_seg_mean_sq(o, mseg) + LN_EPS)
            ob_ref[b, rows, :] = (on * gg * _silu(r_ref[b, rows, :])).astype(BF16)
        return carry

    lax.fori_loop(0, n_chunks, body, 0)

    @pl.when(pl.program_id(1) == pl.num_programs(1) - 1)
    def _():
        for b in range(nseq):
            st_ref[b] = st_scr[b]


GLA_SEQS = 4
GLA_ROWS = 512


def _gla_prompt(q, k, g, v, r, s0, gg, nbatch):
    t = q.shape[0] // nbatch
    nb = min(GLA_SEQS, nbatch)
    tr = min(GLA_ROWS, t)
    seq = lambda a: a.reshape(nbatch, t, a.shape[-1])
    blk = lambda width: pl.BlockSpec((nb, tr, width), lambda b, j: (b, j, 0))
    state = pl.BlockSpec((nb, B_WIDTH, GK_WIDTH), lambda b, j: (b, 0, 0))
    return pl.pallas_call(
        functools.partial(_gla_kernel, n_chunks=tr // GLA_CHUNK),
        grid=(nbatch // nb, t // tr),
        in_specs=[blk(GK_WIDTH), blk(GK_WIDTH), blk(GK_WIDTH), blk(B_WIDTH), blk(B_WIDTH),
                  pl.BlockSpec((1, B_WIDTH, GK_WIDTH), lambda b, j: (0, 0, 0)),
                  pl.BlockSpec((1, B_WIDTH), lambda b, j: (0, 0))],
        out_specs=[blk(B_WIDTH), state],
        out_shape=[jax.ShapeDtypeStruct((nbatch, t, B_WIDTH), BF16),
                   jax.ShapeDtypeStruct((nbatch, B_WIDTH, GK_WIDTH), F32)],
        scratch_shapes=[pltpu.VMEM((nb, B_WIDTH, GK_WIDTH), F32)],
        compiler_params=_params("parallel", "arbitrary"),
        name="gla_prompt",
    )(seq(q), seq(k), seq(g), seq(v), seq(r), s0, gg)


def _gla_sample_kernel(q_ref, k_ref, la_ref, v_ref, r_ref, s_ref, gg_ref, ob_ref, sn_ref):
    sb = q_ref.shape[0]
    qt = q_ref[...].T
    kt = k_ref[...].T
    at = jnp.exp(la_ref[...]).T
    v = v_ref[...]
    gg = gg_ref[...]
    for j in range(sb):
        vj = jnp.concatenate(
            [jnp.broadcast_to(v[j:j + 1, h * B_DV:(h + 1) * B_DV], (B_DK, B_DV)) for h in range(B_HEADS)], axis=0)
        s_new = at[:, j:j + 1] * s_ref[j] + kt[:, j:j + 1] * vj
        sn_ref[j] = s_new
        o = jnp.sum((qt[:, j:j + 1] * s_new).reshape(B_HEADS, B_DK, B_DV), axis=1)
        on = o * lax.rsqrt(jnp.mean(o * o, axis=-1, keepdims=True) + LN_EPS)
        ob_ref[j] = on * gg * _silu(r_ref[j])


def _gla_sample(q, k, la, v, r, state, gg):
    ns = q.shape[0]
    sb = min(SAMPLE_BLOCK, ns)
    row = lambda width: pl.BlockSpec((sb, width), lambda i: (i, 0))
    return pl.pallas_call(
        _gla_sample_kernel,
        grid=(ns // sb,),
        in_specs=[row(GK_WIDTH), row(GK_WIDTH), row(GK_WIDTH), row(B_WIDTH),
                  pl.BlockSpec((sb, B_HEADS, B_DV), lambda i: (i, 0, 0)),
                  pl.BlockSpec((sb, GK_WIDTH, B_DV), lambda i: (i, 0, 0)),
                  pl.BlockSpec((1, B_DV), lambda i: (0, 0))],
        out_specs=[pl.BlockSpec((sb, B_HEADS, B_DV), lambda i: (i, 0, 0)),
                   pl.BlockSpec((sb, GK_WIDTH, B_DV), lambda i: (i, 0, 0))],
        out_shape=[jax.ShapeDtypeStruct((ns, B_HEADS, B_DV), F32),
                   jax.ShapeDtypeStruct((ns, GK_WIDTH, B_DV), F32)],
        compiler_params=_params("parallel"),
        name="gla_sample",
    )(q, k, la, v, r, state, gg)


CONV_HIST = 32
CONV_CHUNK = 64


def _conv_post(y, dwb, lng, lnb, pww, pwb):
    y = _silu(_ln_rows(y + dwb, lng, lnb))
    return _dot(y.astype(BF16), pww) + pwb


def _conv_kernel(u_ref, hist_ref, dww_ref, dwb_ref, lng_ref, lnb_ref, pww_ref, pwb_ref,
                 oc_ref, tail_ref, ubuf, *, n_chunks):
    t = u_ref.shape[1]
    ubuf[0, 0:CONV_HIST, :] = hist_ref[...]
    ubuf[0, CONV_HIST:CONV_HIST + t, :] = u_ref[0]
    off = CONV_HIST - (CONV_W - 1)
    n_copy = t + CONV_HIST - SUBLANES
    for s in range(1, SUBLANES):
        ubuf[s, 0:n_copy, :] = ubuf[0, s:s + n_copy, :]

    def body(c, carry):
        r0 = pl.multiple_of(c * CONV_CHUNK, CONV_CHUNK)
        acc = jnp.zeros((CONV_CHUNK, C_WIDTH), F32)
        for j in range(CONV_W):
            a, s = divmod(j + off, SUBLANES)
            rows = pl.ds(pl.multiple_of(r0 + a * SUBLANES, SUBLANES), CONV_CHUNK)
            acc = acc + ubuf[s, rows, :] * dww_ref[j:j + 1, :]
        oc = _conv_post(acc, dwb_ref[...], lng_ref[...], lnb_ref[...], pww_ref[...], pwb_ref[...])
        oc_ref[0, pl.ds(r0, CONV_CHUNK), :] = oc.astype(BF16)
        return carry

    lax.fori_loop(0, n_chunks, body, 0, unroll=2)
    tail_ref[0] = ubuf[0, t:t + CONV_HIST, :]


def _conv_prompt(u, hist, dww, dwb, lng, lnb, pww, pwb, nbatch):
    t = u.shape[0] // nbatch
    const = lambda shape: pl.BlockSpec(shape, lambda b: (0,) * len(shape))
    return pl.pallas_call(
        functools.partial(_conv_kernel, n_chunks=t // CONV_CHUNK),
        grid=(nbatch,),
        in_specs=[pl.BlockSpec((1, t, C_WIDTH), lambda b: (b, 0, 0)), const((CONV_HIST, C_WIDTH)),
                  const((CONV_HIST, C_WIDTH)), const((1, C_WIDTH)), const((1, C_WIDTH)), const((1, C_WIDTH)),
                  const((C_WIDTH, C_WIDTH)), const((1, C_WIDTH))],
        out_specs=[pl.BlockSpec((1, t, C_WIDTH), lambda b: (b, 0, 0)),
                   pl.BlockSpec((1, CONV_HIST, C_WIDTH), lambda b: (b, 0, 0))],
        out_shape=[jax.ShapeDtypeStruct((nbatch, t, C_WIDTH), BF16),
                   jax.ShapeDtypeStruct((nbatch, CONV_HIST, C_WIDTH), F32)],
        scratch_shapes=[pltpu.VMEM((SUBLANES, CONV_HIST + t, C_WIDTH), F32)],
        compiler_params=_params("parallel"),
        name="conv_prompt",
    )(u.reshape(nbatch, t, C_WIDTH), hist, dww, dwb, lng, lnb, pww, pwb)


def _conv_sample_kernel(hist_ref, u_ref, dww_ref, dwb_ref, lng_ref, lnb_ref, pww_ref, pwb_ref, oc_ref, tail_ref):
    nh = CONV_W - 1
    hist = hist_ref[...]
    u = u_ref[...]
    y = jnp.sum(hist * dww_ref[0:nh, :][None], axis=1) + u[:, 0, :] * dww_ref[nh:nh + 1, :]
    oc = _conv_post(y, dwb_ref[...], lng_ref[...], lnb_ref[...], pww_ref[...], pwb_ref[...])
    oc_ref[...] = oc.astype(BF16)
    tail_ref[:, 0:nh - 1, :] = hist_ref[:, 1:nh, :]
    tail_ref[:, nh - 1:nh, :] = u


def _conv_sample(hist, u, dww, dwb, lng, lnb, pww, pwb):
    ns = hist.shape[0]
    nh = CONV_W - 1
    const = lambda shape: pl.BlockSpec(shape, lambda i: (0,) * len(shape))
    return pl.pallas_call(
        _conv_sample_kernel,
        grid=(1,),
        in_specs=[const((ns, nh, C_WIDTH)), const((ns, 1, C_WIDTH)),
                  const((CONV_HIST, C_WIDTH)), const((1, C_WIDTH)), const((1, C_WIDTH)), const((1, C_WIDTH)),
                  const((C_WIDTH, C_WIDTH)), const((1, C_WIDTH))],
        out_specs=[const((ns, C_WIDTH)), const((ns, nh, C_WIDTH))],
        out_shape=[jax.ShapeDtypeStruct((ns, C_WIDTH), BF16),
                   jax.ShapeDtypeStruct((ns, nh, C_WIDTH), F32)],
        compiler_params=_params("arbitrary"),
        name="conv_sample",
    )(hist, u, dww, dwb, lng, lnb, pww, pwb)


ROUTE_LANE0 = N_GROUPS


def _route(x, wr2, br):
    x_hi = x.astype(BF16)
    x_lo = (x - x_hi.astype(F32)).astype(BF16)
    l_hi = _dot(x_hi, wr2)
    logits = l_hi[:, :LANES] + l_hi[:, LANES:] + _dot(x_lo, wr2)[:, :LANES] + br
    lane = lax.broadcasted_iota(jnp.int32, logits.shape, 1).astype(F32)
    far = 1e3
    glm = jnp.where(lane < N_GROUPS, logits, NEG)
    gmax = jnp.max(glm, axis=1, keepdims=True)
    gi = jnp.min(jnp.where(glm == gmax, lane, far), axis=1, keepdims=True)
    p_grp = 1.0 / jnp.sum(jnp.exp(glm - gmax), axis=1, keepdims=True)
    lo = ROUTE_LANE0 + EXP_PER_GROUP * gi
    in_sel = (lane >= lo) & (lane < lo + EXP_PER_GROUP)
    elm = jnp.where(in_sel, logits, NEG)
    v1 = jnp.max(elm, axis=1, keepdims=True)
    i1 = jnp.min(jnp.where(elm == v1, lane, far), axis=1, keepdims=True)
    elm2 = jnp.where(lane == i1, NEG, elm)
    v2 = jnp.max(elm2, axis=1, keepdims=True)
    i2 = jnp.min(jnp.where((elm2 == v2) & in_sel & (lane != i1), lane, far), axis=1, keepdims=True)
    t = jnp.exp(v2 - v1)
    w1 = p_grp / (1.0 + t)
    w2 = w1 * t
    return jnp.where(lane == i1, w1, 0.0) + jnp.where(lane == i2, w2, 0.0), gi


MOE_ROWS = 128
MOE_ALIGN = 16
GROUP_WIDTH = EXP_PER_GROUP * D_EXPERT


def _moe_sorted_rows(tm):
    need = tm + N_GROUPS * MOE_ALIGN + MOE_ROWS
    return -(-need // LANES) * LANES


def _ffn_kernel(*refs, pre_ln):
    if pre_ln:
        lg_ref, lb_ref, *refs = refs
    (oa_ref, ob_ref, oc_ref, h_ref, wo_ref, g1_ref, b1_ref, wr_ref, br_ref, wg_ref, wu_ref, wd_ref,
     g_ref, b_ref, o_ref, xs_ref, ys_ref, ws_ref) = refs
    h = h_ref[...]
    if pre_ln:
        h = _ln_rows(h, lg_ref[...], lb_ref[...])
    mix = (_dot(oa_ref[...], wo_ref[0:A_WIDTH, :])
           + _dot(ob_ref[...], wo_ref[A_WIDTH:A_WIDTH + B_WIDTH, :])
           + _dot(oc_ref[...], wo_ref[A_WIDTH + B_WIDTH:D_MODEL, :]))
    x = _ln_rows(ALPHA * h + mix, g1_ref[...], b1_ref[...])
    tm = x.shape[0]
    ns = xs_ref.shape[0]
    iota = lambda shape, axis: lax.broadcasted_iota(jnp.int32, shape, axis)
    dw, gi = _route(x, wr_ref[...], br_ref[...])
    lane_f = iota((tm, LANES), 1).astype(F32)
    onehot = jnp.where(lane_f == gi, 1.0, 0.0)
    tri = jnp.where(iota((tm, tm), 1) <= iota((tm, tm), 0), 1.0, 0.0).astype(BF16)
    cum = _dot(tri, onehot.astype(BF16))
    rank = jnp.sum(onehot * (cum - 1.0), axis=1, keepdims=True)
    counts = cum[tm - 1:tm, :]

    starts, tiles = [], []
    start = jnp.int32(0)
    for grp in range(N_GROUPS):
        n_g = counts[0, grp].astype(jnp.int32)
        starts.append(start)
        tiles.append((n_g + (MOE_ROWS - 1)) >> (MOE_ROWS.bit_length() - 1))
        start = start + ((n_g + (MOE_ALIGN - 1)) & -MOE_ALIGN)
    lane1 = iota((1, LANES), 1)
    start_v = jnp.zeros((1, LANES), F32)
    for grp in range(N_GROUPS):
        start_v = jnp.where(lane1 == grp, starts[grp].astype(F32), start_v)
    pos = jnp.sum(onehot * start_v, axis=1, keepdims=True) + rank

    pos_i = pos.astype(jnp.int32)
    digits = jnp.where(lane_f == 0.0, (pos_i >> 5).astype(F32),
                       jnp.where(lane_f == 1.0, (pos_i & 31).astype(F32), 0.0)).astype(BF16)
    lane8 = iota((8, LANES), 1)
    radix = jnp.where(lane8 == 0, 32.0, jnp.where(lane8 == 1, 1.0, 0.0)).astype(BF16)
    pos_row = _dot_nt(radix, digits)[0:1, :]

    perm = jnp.where(iota((ns, tm), 0).astype(F32) == pos_row, 1.0, 0.0).astype(BF16)
    dw_hi = dw.astype(BF16)
    dw_lo = (dw - dw_hi.astype(F32)).astype(BF16)
    srt = _dot(perm, jnp.concatenate([x.astype(BF16), dw_hi, dw_lo], axis=1))
    xs_ref[...] = srt[:, :D_MODEL].astype(BF16)
    ws_ref[...] = srt[:, D_MODEL:D_MODEL + LANES] + srt[:, D_MODEL + LANES:]
    ys_ref[...] = jnp.zeros_like(ys_ref)

    for grp in range(N_GROUPS):
        def body(k, carry, grp=grp):
            r0 = pl.multiple_of(starts[grp] + k * MOE_ROWS, MOE_ALIGN)
            rows = pl.ds(r0, MOE_ROWS)
            xt = xs_ref[rows, :]
            w = ws_ref[rows, :]
            hes = []
            for e in range(EXP_PER_GROUP):
                ex = grp * EXP_PER_GROUP + e
                w_e = w[:, ROUTE_LANE0 + ex:ROUTE_LANE0 + ex + 1]
                hes.append((_silu(_dot(xt, wg_ref[ex])) * _dot(xt, wu_ref[ex]) * w_e).astype(BF16))
            ys_ref[rows, :] = _dot(jnp.concatenate(hes, axis=1), wd_ref[grp]).astype(BF16)
            return carry
        lax.fori_loop(0, tiles[grp], body, 0)

    unperm = jnp.where(iota((tm, ns), 1).astype(F32) == pos, 1.0, 0.0).astype(BF16)
    y = _dot(unperm, ys_ref[...])
    o_ref[...] = _ln_rows(ALPHA * x + y, g_ref[...], b_ref[...])


def _ffn(oa, ob, oc, h, wo, g1, b1, wr2, br, wg, wu, wd, g2, b2, tm, ln=None):
    n = h.shape[0]
    ns = _moe_sorted_rows(tm)
    row = lambda width: pl.BlockSpec((tm, width), lambda i: (i, 0))
    const = lambda shape: pl.BlockSpec(shape, lambda i: (0,) * len(shape))
    resident = lambda shape: pl.BlockSpec(shape, lambda i: (0,) * len(shape), pipeline_mode=pl.Buffered(1))
    ln_specs = [const((1, D_MODEL)), const((1, D_MODEL))] if ln else []
    return pl.pallas_call(
        functools.partial(_ffn_kernel, pre_ln=bool(ln)),
        grid=(n // tm,),
        in_specs=ln_specs + [row(A_WIDTH), row(B_WIDTH), row(C_WIDTH), row(D_MODEL),
                             resident((D_MODEL, D_MODEL)), const((1, D_MODEL)), const((1, D_MODEL)),
                             resident((D_MODEL, 2 * LANES)), const((1, LANES)),
                             resident((N_EXPERTS, D_MODEL, D_EXPERT)), resident((N_EXPERTS, D_MODEL, D_EXPERT)),
                             resident((N_GROUPS, GROUP_WIDTH, D_MODEL)),
                             const((1, D_MODEL)), const((1, D_MODEL))],
        out_specs=row(D_MODEL),
        out_shape=jax.ShapeDtypeStruct((n, D_MODEL), F32),
        scratch_shapes=[pltpu.VMEM((ns, D_MODEL), BF16), pltpu.VMEM((ns, D_MODEL), BF16),
                        pltpu.VMEM((ns, LANES), F32)],
        compiler_params=_params("parallel"),
        name="ffn",
    )(*(ln or ()), oa, ob, oc, h, wo, g1, b1, wr2, br, wg, wu, wd, g2, b2)


def _rope_tables(pos):
    half = HEAD_DIM // 2
    inv = ROPE_THETA ** (-jnp.arange(half, dtype=F32) / half)
    ang = pos.astype(F32)[:, None] * inv[None, :]
    cos, sin = jnp.cos(ang), jnp.sin(ang)
    cos_t = jnp.concatenate([cos, cos] * (LANES // HEAD_DIM), axis=1)
    sin_t = jnp.concatenate([-sin, sin] * (LANES // HEAD_DIM), axis=1)
    return cos_t, sin_t


def _row(v):
    return v.reshape(1, -1)


def kernel(x_prompt, x_sample, cache_meta_k, cache_meta_v, cache_win_k, cache_win_v, state_gla, state_conv,
           meta_tokens, ln_in_g, ln_in_b, w_in, attn_sink, w_alpha, b_alpha, gla_norm_g,
           conv_dw_w, conv_dw_b, conv_ln_g, conv_ln_b, conv_pw_w, conv_pw_b, w_out, ln1_g, ln1_b,
           w_router_group, b_router_group, w_router_expert, b_router_expert,
           w_exp_gate, w_exp_up, w_exp_down, ln2_g, ln2_b):
    nb, seq, d = x_prompt.shape
    ns = x_sample.shape[0]
    nwin = cache_win_k.shape[2]
    n_big = nb * seq
    n_small = BLOCK + ns
    tm_big = min(512, seq)

    small_in = jnp.concatenate([jnp.zeros((META_PAD, d), F32), meta_tokens.astype(F32),
                                x_sample.reshape(ns, d)], axis=0)
    hb = x_prompt.reshape(n_big, d)
    hs = small_in
    ln_in = (_row(ln_in_g), _row(ln_in_b))

    cos_b, sin_b = _rope_tables(N_META + jnp.arange(seq))
    pos_small = jnp.concatenate([jnp.maximum(jnp.arange(BLOCK) - META_PAD, 0),
                                 jnp.full((ns,), PAST_LEN, jnp.int32)])
    cos_s, sin_s = _rope_tables(pos_small)

    outs = [[] for _ in range(12)]
    zeros_hist = jnp.zeros((CONV_HIST, C_WIDTH), F32)
    zeros_state = jnp.zeros((1, B_WIDTH, GK_WIDTH), F32)
    pad_cols = jnp.zeros((d, LANES - B_RANK), F32)

    for l in range(DEPTH):
        wi = w_in[l]
        w_pad = jnp.concatenate([wi[:, :C_AB + B_RANK], pad_cols, wi[:, C_AB + B_RANK:]], axis=1).astype(BF16)
        wa = jnp.concatenate([w_alpha[l], jnp.zeros((LANES - B_RANK, GK_WIDTH), F32)], axis=0).astype(BF16)
        ba = _row(b_alpha[l])
        sink = attn_sink[l].astype(F32)
        sink_col = jnp.concatenate([sink, jnp.zeros((HEAD_ROWS - A_HEADS,), F32)])[:, None]
        gg_t = _row(jnp.tile(gla_norm_g[l], B_HEADS))
        gg_h = _row(gla_norm_g[l])
        dww = jnp.concatenate([conv_dw_w[l], jnp.zeros((CONV_HIST - CONV_W, C_WIDTH), F32)], axis=0)
        dwb, clg, clb = _row(conv_dw_b[l]), _row(conv_ln_g[l]), _row(conv_ln_b[l])
        pww, pwb = conv_pw_w[l].astype(BF16), _row(conv_pw_b[l])
        wo = w_out[l].astype(BF16)
        wr = jnp.concatenate([w_router_group[l], w_router_expert[l],
                              jnp.zeros((d, LANES - N_GROUPS - N_EXPERTS), F32)], axis=1)
        wr_hi = wr.astype(BF16)
        wr2 = jnp.concatenate([wr_hi, (wr - wr_hi.astype(F32)).astype(BF16)], axis=1)
        br = _row(jnp.concatenate([b_router_group[l], b_router_expert[l],
                                   jnp.zeros((LANES - N_GROUPS - N_EXPERTS,), F32)]))
        wg, wu = w_exp_gate[l].astype(BF16), w_exp_up[l].astype(BF16)
        wd = w_exp_down[l].astype(BF16).reshape(N_GROUPS, GROUP_WIDTH, d)

        ln = ln_in if l == 0 else None
        qa_b, ka_b, va_b, qg_b, kg_b, la_b, vg_b, rg_b, u_b = _proj_in(
            hb, w_pad, wa, ba, cos_b, sin_b, tm=tm_big, n_pad=0, n_seq=tm_big, ln=ln)
        qa_s, ka_s, va_s, qg_s, kg_s, la_s, vg_s, rg_s, u_s = _proj_in(
            hs, w_pad, wa, ba, cos_s, sin_s, tm=n_small, n_pad=META_PAD, n_seq=BLOCK, ln=ln)

        oa_b = _attn_prompt(sink, qa_b, ka_b, va_b, ka_s, va_s, nb)
        oa_m = _attn_meta(sink, qa_s, ka_s, va_s)
        q_smp = qa_s[BLOCK:].reshape(ns, A_KV_HEADS, A_GROUP, 1, HEAD_DIM)
        eye = jnp.eye(A_KV_HEADS, dtype=BF16)[None, :, None, :, None]
        q_bd = (q_smp * eye).reshape(ns, A_HEADS, KV_WIDTH)
        q_bd = jnp.pad(q_bd, ((0, 0), (0, HEAD_ROWS - A_HEADS), (0, 0)))
        o_bd, nwk, nwv = _attn_sample(
            sink_col, q_bd, ka_s[BLOCK:].reshape(ns, 1, KV_WIDTH), va_s[BLOCK:].reshape(ns, 1, KV_WIDTH),
            cache_meta_k[l].reshape(ns, N_META, KV_WIDTH), cache_meta_v[l].reshape(ns, N_META, KV_WIDTH),
            cache_win_k[l].reshape(ns, nwin, KV_WIDTH), cache_win_v[l].reshape(ns, nwin, KV_WIDTH))
        o_bd = o_bd[:, :A_HEADS].reshape(ns, A_KV_HEADS, A_GROUP, A_KV_HEADS, HEAD_DIM)
        oa_smp = jnp.stack([o_bd[:, c, :, c, :] for c in range(A_KV_HEADS)], axis=1).reshape(ns, A_WIDTH)
        oa_s = jnp.concatenate([oa_m, oa_smp.astype(BF16)], axis=0)

        ob_m, st_m = _gla_prompt(qg_s[:BLOCK], kg_s[:BLOCK], la_s[:BLOCK], vg_s[:BLOCK], rg_s[:BLOCK],
                                 zeros_state, gg_t, 1)
        ob_b, st_b = _gla_prompt(qg_b, kg_b, la_b, vg_b, rg_b, st_m, gg_t, nb)
        ob_smp, s_new = _gla_sample(qg_s[BLOCK:], kg_s[BLOCK:], la_s[BLOCK:], vg_s[BLOCK:],
                                    rg_s[BLOCK:].reshape(ns, B_HEADS, B_DV),
                                    state_gla[l].reshape(ns, GK_WIDTH, B_DV), gg_h)
        ob_s = jnp.concatenate([ob_m.reshape(BLOCK, B_WIDTH), ob_smp.reshape(ns, B_WIDTH).astype(BF16)], axis=0)
        st5 = st_b.reshape(nb, B_HEADS, B_DV, B_HEADS, B_DK)
        gla_p = jnp.stack([st5[:, h, :, h, :] for h in range(B_HEADS)], axis=1).transpose(0, 1, 3, 2)

        oc_m, _ = _conv_prompt(u_s[:BLOCK], zeros_hist, dww, dwb, clg, clb, pww, pwb, 1)
        oc_b, tail_b = _conv_prompt(u_b, u_s[BLOCK - CONV_HIST:BLOCK], dww, dwb, clg, clb, pww, pwb, nb)
        oc_smp, tail_s = _conv_sample(state_conv[l], u_s[BLOCK:].reshape(ns, 1, C_WIDTH),
                                      dww, dwb, clg, clb, pww, pwb)
        oc_s = jnp.concatenate([oc_m.reshape(BLOCK, C_WIDTH), oc_smp], axis=0)

        l1g, l1b, l2g, l2b = _row(ln1_g[l]), _row(ln1_b[l]), _row(ln2_g[l]), _row(ln2_b[l])
        hb = _ffn(oa_b, ob_b.reshape(n_big, B_WIDTH), oc_b.reshape(n_big, C_WIDTH), hb, wo, l1g, l1b,
                  wr2, br, wg, wu, wd, l2g, l2b, tm_big, ln=ln)
        hs = _ffn(oa_s, ob_s, oc_s, hs, wo, l1g, l1b, wr2, br, wg, wu, wd, l2g, l2b, n_small, ln=ln)

        kv4 = lambda a: a.reshape(a.shape[0], a.shape[1], A_KV_HEADS, HEAD_DIM)
        meta_k = jnp.broadcast_to(ka_s[META_PAD:BLOCK][None], (nb, N_META, KV_WIDTH))
        meta_v = jnp.broadcast_to(va_s[META_PAD:BLOCK][None], (nb, N_META, KV_WIDTH))
        win_k = ka_b.reshape(nb, seq, KV_WIDTH)[:, seq - nwin:]
        win_v = va_b.reshape(nb, seq, KV_WIDTH)[:, seq - nwin:]
        layer_out = (None, None, kv4(meta_k), kv4(meta_v), kv4(win_k), kv4(win_v), kv4(nwk), kv4(nwv),
                     gla_p, s_new.reshape(ns, B_HEADS, B_DK, B_DV),
                     tail_b[:, CONV_HIST - (CONV_W - 1):], tail_s)
        for i in range(2, 12):
            outs[i].append(layer_out[i])

    y_prompt = hb.reshape(nb, seq, d)
    y_sample = hs[BLOCK:].reshape(ns, 1, d)
    return (y_prompt, y_sample) + tuple(jnp.stack(o) for o in outs[2:])
```

```python
import functools

import jax
import jax.numpy as jnp
from jax import lax
from jax.experimental import pallas as pl
from jax.experimental.pallas import tpu as pltpu

F32 = jnp.float32
BF16 = jnp.bfloat16

D_MODEL = 1024
DEPTH = 2
PAST_LEN = 16384
N_META = 16
HEAD_DIM = 64
A_WIDTH = 512
A_HEADS = 8
A_KV_HEADS = 2
A_GROUP = 4
WINDOW = 128
BLOCK = 128
ROPE_THETA = 10000.0
B_WIDTH = 256
B_HEADS = 4
B_DV = 64
B_DK = 32
B_RANK = 16
GATE_TAU = 16.0
GLA_CHUNK = 64
C_WIDTH = 256
CONV_W = 31
N_GROUPS = 4
EXP_PER_GROUP = 4
N_EXPERTS = 16
D_EXPERT = 256
ALPHA = (2 * DEPTH) ** 0.25
LN_EPS = 1e-5

LANES = 128
SUBLANES = 8
META_PAD = BLOCK - N_META
KV_WIDTH = A_KV_HEADS * HEAD_DIM
GK_WIDTH = B_HEADS * B_DK
C_QA = 0
C_KA = C_QA + A_WIDTH
C_VA = C_KA + KV_WIDTH
C_QB = C_VA + KV_WIDTH
C_KB = C_QB + GK_WIDTH
C_VB = C_KB + GK_WIDTH
C_RB = C_VB + B_WIDTH
C_AB = C_RB + B_WIDTH
C_CG = C_AB + LANES
PROJ_PAD_WIDTH = C_CG + 2 * C_WIDTH
NEG = -1e30
VMEM_LIMIT = 56 * 1024 * 1024


def _dot(a, b):
    return jnp.dot(a, b, preferred_element_type=F32)


def _dot_nt(a, b):
    return lax.dot_general(a, b, (((1,), (1,)), ((), ())), preferred_element_type=F32)


def _dot_tn(a, b):
    return lax.dot_general(a, b, (((0,), (0,)), ((), ())), preferred_element_type=F32)


def _ln_rows(x, g, b):
    xc = x - jnp.mean(x, -1, keepdims=True)
    var = jnp.mean(xc * xc, -1, keepdims=True)
    return xc * lax.rsqrt(var + LN_EPS) * g + b


def _silu(x):
    return x * jax.nn.sigmoid(x)


def _split3(x):
    hi = x.astype(BF16)
    r1 = x - hi.astype(F32)
    mid = r1.astype(BF16)
    lo = (r1 - mid.astype(F32)).astype(BF16)
    return hi, mid, lo


def _params(*sem):
    return pltpu.CompilerParams(dimension_semantics=sem, vmem_limit_bytes=VMEM_LIMIT)


def _proj_in_kernel(*refs, n_pad, n_seq, pre_ln):
    if pre_ln:
        lg_ref, lb_ref, *refs = refs
    (x_ref, w_ref, wa_ref, ba_ref, cos_ref, sin_ref,
     qa_ref, ka_ref, va_ref, qg_ref, kg_ref, la_ref, vg_ref, rg_ref, u_ref) = refs
    x = x_ref[...]
    if pre_ln:
        x = _ln_rows(x, lg_ref[...], lb_ref[...])
    xb = x.astype(BF16)
    tm = xb.shape[0]
    cos = cos_ref[...]
    sin = sin_ref[...]
    lane = lax.broadcasted_iota(jnp.int32, (tm, LANES), 1)
    first_half = (lane & (HEAD_DIM // 2)) == 0

    def rope(z):
        rot = jnp.where(first_half, pltpu.roll(z, LANES - HEAD_DIM // 2, 1), pltpu.roll(z, HEAD_DIM // 2, 1))
        return z * cos + rot * sin

    if n_pad:
        valid = (lax.broadcasted_iota(jnp.int32, (tm, 1), 0) >= n_pad).astype(F32)
    else:
        valid = None

    za = _dot(xb, w_ref[:, C_QA:C_QB])
    for c in range(A_WIDTH // LANES):
        zq = za[:, c * LANES:(c + 1) * LANES]
        qa_ref[:, c * LANES:(c + 1) * LANES] = (rope(zq) * (HEAD_DIM ** -0.5)).astype(BF16)
    ka_ref[...] = rope(za[:, C_KA:C_VA])
    va_ref[...] = za[:, C_VA:C_QB]

    zb = _dot(xb, w_ref[:, C_QB:C_CG])
    o = C_QB
    qg_ref[...] = zb[:, C_QB - o:C_KB - o] * (B_DK ** -0.5)
    kg = zb[:, C_KB - o:C_VB - o]
    vg_ref[...] = zb[:, C_VB - o:C_RB - o]
    rg_ref[...] = zb[:, C_RB - o:C_AB - o]
    ab = zb[:, C_AB - o:C_CG - o].astype(BF16)
    xa = _dot(ab, wa_ref[...]) + ba_ref[...]
    la = (jnp.minimum(xa, 0.0) - jnp.log(1.0 + jnp.exp(-jnp.abs(xa)))) * (1.0 / GATE_TAU)

    zc = _dot(xb, w_ref[:, C_CG:PROJ_PAD_WIDTH])
    u = zc[:, :C_WIDTH] * jax.nn.sigmoid(zc[:, C_WIDTH:])
    if valid is not None:
        kg = kg * valid
        la = la * valid
        u = u * valid
    kg_ref[...] = kg
    u_ref[...] = u
    cs = GLA_CHUNK
    tri = jnp.where(lax.broadcasted_iota(jnp.int32, (cs, cs), 1) <= lax.broadcasted_iota(jnp.int32, (cs, cs), 0),
                    1.0, 0.0).astype(BF16)
    la_h, la_m, la_l = _split3(la)
    for c in range(n_seq // cs):
        rows = slice(c * cs, (c + 1) * cs)
        la_ref[rows, :] = _dot(tri, la_h[rows]) + _dot(tri, la_m[rows]) + _dot(tri, la_l[rows])
    if n_seq < tm:
        la_ref[n_seq:tm, :] = la[n_seq:tm]


def _proj_in(h, w, wa, ba, cos, sin, *, layer, tm, n_pad, n_seq, ln=None):
    n = h.shape[0]
    tb = cos.shape[0] // tm
    row = lambda width: pl.BlockSpec((tm, width), lambda i: (i, 0))
    const = lambda shape: pl.BlockSpec(shape, lambda i: (0, 0))
    per_layer = lambda shape: pl.BlockSpec((None,) + shape, lambda i: (layer, 0, 0))
    tab = pl.BlockSpec((tm, LANES), lambda i: (i % tb, 0))
    widths = (A_WIDTH, KV_WIDTH, KV_WIDTH, GK_WIDTH, GK_WIDTH, GK_WIDTH, B_WIDTH, B_WIDTH, C_WIDTH)
    dtypes = (BF16,) + (F32,) * 8
    ln_specs = [const((1, D_MODEL)), const((1, D_MODEL))] if ln else []
    return pl.pallas_call(
        functools.partial(_proj_in_kernel, n_pad=n_pad, n_seq=n_seq, pre_ln=bool(ln)),
        grid=(n // tm,),
        in_specs=ln_specs + [row(D_MODEL), per_layer((D_MODEL, PROJ_PAD_WIDTH)), per_layer((LANES, GK_WIDTH)),
                             per_layer((1, GK_WIDTH)), tab, tab],
        out_specs=[row(wd) for wd in widths],
        out_shape=[jax.ShapeDtypeStruct((n, wd), dt) for wd, dt in zip(widths, dtypes)],
        compiler_params=_params("parallel"),
        name="proj_in",
    )(*(ln or ()), h, w, wa, ba, cos, sin)


def _attn_kernel(sink_ref, q_ref, km_ref, vm_ref, *rest, meta_mode):
    if meta_mode:
        (o_ref,) = rest
        _attn_block(sink_ref, q_ref[...], km_ref[...], vm_ref[...], o_ref, 0, None)
    else:
        kp_ref, vp_ref, kc_ref, vc_ref, o_ref = rest
        km, vm = km_ref[...], vm_ref[...]
        for sub in range(q_ref.shape[0] // BLOCK):
            cur = slice(sub * BLOCK, (sub + 1) * BLOCK)
            if sub == 0:
                kp, vp = kp_ref[...], vp_ref[...]
                has_prev = pl.program_id(1) >= 1
            else:
                prev = slice((sub - 1) * BLOCK, sub * BLOCK)
                kp, vp = kc_ref[prev, :], vc_ref[prev, :]
                has_prev = True
            k_all = jnp.concatenate([kp, kc_ref[cur, :], km], axis=0)
            v_all = jnp.concatenate([vp, vc_ref[cur, :], vm], axis=0)
            _attn_block(sink_ref, q_ref[cur, :], k_all, v_all, o_ref, sub * BLOCK, has_prev)


def _attn_block(sink_ref, q, k_all, v_all, o_ref, row0, has_prev):
    nk = k_all.shape[0]
    ki = lax.broadcasted_iota(jnp.int32, (nk, BLOCK), 0)
    qi = lax.broadcasted_iota(jnp.int32, (nk, BLOCK), 1)
    if has_prev is None:
        ok = ki <= qi - META_PAD
    else:
        prev_lo = qi if has_prev is True else qi + jnp.where(has_prev, 0, BLOCK)
        ok = ((ki >= prev_lo) & (ki < BLOCK)) | ((ki >= BLOCK) & (ki <= qi + BLOCK)) | (ki >= 2 * BLOCK)
    bias = jnp.where(ok, 0.0, NEG)
    k_all = k_all.astype(BF16)
    v_all = v_all.astype(BF16)
    for kvh in range(A_KV_HEADS):
        heads = [kvh * A_GROUP + g for g in range(A_GROUP)]
        qs = jnp.concatenate([q[:, h * HEAD_DIM:(h + 1) * HEAD_DIM] for h in heads], axis=0)
        kk = k_all[:, kvh * HEAD_DIM:(kvh + 1) * HEAD_DIM]
        vv = v_all[:, kvh * HEAD_DIM:(kvh + 1) * HEAD_DIM]
        st = _dot_nt(kk, qs)
        ps, dens = [], []
        for g, h in enumerate(heads):
            s = st[:, g * BLOCK:(g + 1) * BLOCK] + bias
            sink = sink_ref[h]
            m = jnp.maximum(jnp.max(s, axis=0, keepdims=True), sink)
            p = jnp.exp(s - m)
            dens.append(jnp.sum(p, axis=0, keepdims=True) + jnp.exp(sink - m))
            ps.append(p.astype(BF16))
        ot = _dot_tn(vv, jnp.concatenate(ps, axis=1)) / jnp.concatenate(dens, axis=1)
        for pair in range(A_GROUP // 2):
            two = jnp.concatenate([ot[:, (2 * pair) * BLOCK:(2 * pair + 1) * BLOCK],
                                   ot[:, (2 * pair + 1) * BLOCK:(2 * pair + 2) * BLOCK]], axis=0)
            h0 = heads[2 * pair]
            o_ref[row0:row0 + BLOCK, h0 * HEAD_DIM:(h0 + 2) * HEAD_DIM] = two.T.astype(BF16)


ATTN_SUB = 2


def _attn_prompt(sink, q, k, v, k_small, v_small, nbatch):
    n = q.shape[0]
    nblk = n // BLOCK // nbatch
    sub = ATTN_SUB if nblk % ATTN_SUB == 0 else 1
    nstep = nblk // sub
    kvs = pl.BlockSpec((sub * BLOCK, KV_WIDTH), lambda b, j: (b * nstep + j, 0))
    kvp = pl.BlockSpec((BLOCK, KV_WIDTH), lambda b, j: (b * nblk + jnp.maximum(j * sub - 1, 0), 0))
    kvm = pl.BlockSpec((N_META, KV_WIDTH), lambda b, j: (META_PAD // N_META, 0))
    return pl.pallas_call(
        functools.partial(_attn_kernel, meta_mode=False),
        grid=(nbatch, nstep),
        in_specs=[pl.BlockSpec(memory_space=pltpu.SMEM),
                  pl.BlockSpec((sub * BLOCK, A_WIDTH), lambda b, j: (b * nstep + j, 0)),
                  kvm, kvm, kvp, kvp, kvs, kvs],
        out_specs=pl.BlockSpec((sub * BLOCK, A_WIDTH), lambda b, j: (b * nstep + j, 0)),
        out_shape=jax.ShapeDtypeStruct((n, A_WIDTH), BF16),
        compiler_params=_params("parallel", "parallel"),
        name="attn_prompt",
    )(sink, q, k_small, v_small, k, v, k, v)


def _attn_meta(sink, q_small, k_small, v_small):
    kvm = pl.BlockSpec((N_META, KV_WIDTH), lambda i: (META_PAD // N_META, 0))
    return pl.pallas_call(
        functools.partial(_attn_kernel, meta_mode=True),
        grid=(1,),
        in_specs=[pl.BlockSpec(memory_space=pltpu.SMEM),
                  pl.BlockSpec((BLOCK, A_WIDTH), lambda i: (0, 0)), kvm, kvm],
        out_specs=pl.BlockSpec((BLOCK, A_WIDTH), lambda i: (0, 0)),
        out_shape=jax.ShapeDtypeStruct((BLOCK, A_WIDTH), BF16),
        compiler_params=_params("arbitrary"),
        name="attn_meta",
    )(sink, q_small, k_small, v_small)


SAMPLE_BLOCK = 32
HEAD_ROWS = 16


def _attn_sample_kernel(sink_ref, q_ref, kn_ref, vn_ref, mk_ref, mv_ref, wk_ref, wv_ref,
                        o_ref, nwk_ref, nwv_ref):
    sb = q_ref.shape[0]
    q = q_ref[...]
    kn = kn_ref[...]
    vn = vn_ref[...]
    s_win = jnp.einsum('bhl,blj->bhj', q, wk_ref[...].astype(BF16), preferred_element_type=F32)
    s_meta = jnp.einsum('bhl,bml->bhm', q, mk_ref[...].astype(BF16), preferred_element_type=F32)
    s_self = jnp.sum(q.astype(F32) * kn[:, None, :], axis=2, keepdims=True)
    sink = sink_ref[...][None]
    m = jnp.maximum(jnp.maximum(jnp.max(s_win, axis=2, keepdims=True), jnp.max(s_meta, axis=2, keepdims=True)),
                    jnp.maximum(s_self, sink))
    p_win = jnp.exp(s_win - m)
    p_meta = jnp.exp(s_meta - m)
    p_self = jnp.exp(s_self - m)
    denom = (jnp.sum(p_win, axis=2, keepdims=True) + jnp.sum(p_meta, axis=2, keepdims=True) + p_self
             + jnp.exp(sink - m))
    o = (jnp.einsum('bhj,blj->bhl', p_win.astype(BF16), wv_ref[...].astype(BF16), preferred_element_type=F32)
         + jnp.einsum('bhm,bml->bhl', p_meta.astype(BF16), mv_ref[...].astype(BF16), preferred_element_type=F32)
         + p_self * vn[:, None, :])
    o_ref[...] = o / denom
    nwin = wk_ref.shape[2]
    last = lax.broadcasted_iota(jnp.int32, (KV_WIDTH, nwin), 1) == nwin - 1
    knt = kn.T
    vnt = vn.T
    for j in range(sb):
        nwk_ref[j] = jnp.where(last, knt[:, j:j + 1], pltpu.roll(wk_ref[j], nwin - 1, 1))
        nwv_ref[j] = jnp.where(last, vnt[:, j:j + 1], pltpu.roll(wv_ref[j], nwin - 1, 1))


def _attn_sample(sink_col, q_bd, k_small, v_small, meta_k, meta_v, win_k_t, win_v_t, layer):
    ns = q_bd.shape[0]
    sb = min(SAMPLE_BLOCK, ns)
    nwin = win_k_t.shape[3]
    blk = lambda rows: pl.BlockSpec((sb, rows, KV_WIDTH), lambda i: (i, 0, 0))
    new = pl.BlockSpec((sb, KV_WIDTH), lambda i: (BLOCK // sb + i, 0))
    win = pl.BlockSpec((None, sb, KV_WIDTH, nwin), lambda i: (layer, i, 0, 0))
    win_out = pl.BlockSpec((sb, KV_WIDTH, nwin), lambda i: (i, 0, 0))
    return pl.pallas_call(
        _attn_sample_kernel,
        grid=(ns // sb,),
        in_specs=[pl.BlockSpec((HEAD_ROWS, 1), lambda i: (0, 0)),
                  blk(HEAD_ROWS), new, new, blk(N_META), blk(N_META), win, win],
        out_specs=[blk(HEAD_ROWS), win_out, win_out],
        out_shape=[jax.ShapeDtypeStruct((ns, HEAD_ROWS, KV_WIDTH), F32),
                   jax.ShapeDtypeStruct((ns, KV_WIDTH, nwin), F32),
                   jax.ShapeDtypeStruct((ns, KV_WIDTH, nwin), F32)],
        compiler_params=_params("parallel"),
        name="attn_sample",
    )(sink_col, q_bd, k_small, v_small, meta_k, meta_v, win_k_t, win_v_t)


def _seg_mean_sq(o, mseg):
    sq = o * o
    hi = sq.astype(BF16)
    lo = (sq - hi.astype(F32)).astype(BF16)
    return _dot(hi, mseg) + _dot(lo, mseg)


def _gla_kernel(q_ref, k_ref, g_ref, v_ref, r_ref, s0_ref, gg_ref, ob_ref, st_ref, st_scr, *, n_chunks):
    cs = GLA_CHUNK
    nseq = q_ref.shape[0]

    @pl.when(pl.program_id(1) == 0)
    def _():
        for b in range(nseq):
            st_scr[b] = s0_ref[0]

    iota = lambda shape, axis: lax.broadcasted_iota(jnp.int32, shape, axis)
    lg_cs, lg_dk, lg_dv = cs.bit_length() - 1, B_DK.bit_length() - 1, B_DV.bit_length() - 1
    kd_mask = (iota((B_HEADS * cs, GK_WIDTH), 0) >> lg_cs) == (iota((B_HEADS * cs, GK_WIDTH), 1) >> lg_dk)
    vd_mask = (iota((B_HEADS * cs, B_WIDTH), 0) >> lg_cs) == (iota((B_HEADS * cs, B_WIDTH), 1) >> lg_dv)
    st_mask = (iota((B_WIDTH, GK_WIDTH), 0) >> lg_dv) == (iota((B_WIDTH, GK_WIDTH), 1) >> lg_dk)
    causal = (iota((cs, B_HEADS * cs), 1) & (cs - 1)) <= iota((cs, B_HEADS * cs), 0)
    mseg = jnp.where((iota((B_WIDTH, B_WIDTH), 0) >> lg_dv) == (iota((B_WIDTH, B_WIDTH), 1) >> lg_dv),
                     1.0 / B_DV, 0.0).astype(BF16)
    gg = gg_ref[...]

    def body(c, carry):
        rows = pl.ds(pl.multiple_of(c * cs, cs), cs)
        for b in range(nseq):
            q = q_ref[b, rows, :]
            k = k_ref[b, rows, :]
            v = v_ref[b, rows, :]
            g = g_ref[b, rows, :]
            g_end = g[cs - 1:cs, :]
            g_mid = g[cs // 2 - 1:cs // 2, :]
            qt = (q * jnp.exp(g - g_mid)).astype(BF16)
            kt = k * jnp.exp(g_mid - g)
            kbd = jnp.where(kd_mask, jnp.concatenate([kt] * B_HEADS, axis=0), 0.0).astype(BF16)
            att = jnp.where(causal, _dot_nt(qt, kbd), 0.0)
            vbd = jnp.where(vd_mask, jnp.concatenate([v] * B_HEADS, axis=0), 0.0).astype(BF16)
            st = st_scr[b]
            o = _dot(att.astype(BF16), vbd) + _dot_nt((q * jnp.exp(g)).astype(BF16), st.astype(BF16))
            kh = (k * jnp.exp(g_end - g)).astype(BF16)
            upd = _dot_tn(v.astype(BF16), kh)
            st_scr[b] = st * jnp.exp(g_end) + jnp.where(st_mask, upd, 0.0)
            on = o * lax.rsqrt(_seg_mean_sq(o, mseg) + LN_EPS)
            ob_ref[b, rows, :] = (on * gg * _silu(r_ref[b, rows, :])).astype(BF16)
        return carry

    lax.fori_loop(0, n_chunks, body, 0)

    @pl.when(pl.program_id(1) == pl.num_programs(1) - 1)
    def _():
        for b in range(nseq):
            st_ref[b] = st_scr[b]


GLA_SEQS = 4
GLA_ROWS = 512


def _gla_prompt(q, k, g, v, r, s0, gg, nbatch):
    t = q.shape[0] // nbatch
    nb = min(GLA_SEQS, nbatch)
    tr = min(GLA_ROWS, t)
    seq = lambda a: a.reshape(nbatch, t, a.shape[-1])
    blk = lambda width: pl.BlockSpec((nb, tr, width), lambda b, j: (b, j, 0))
    state = pl.BlockSpec((nb, B_WIDTH, GK_WIDTH), lambda b, j: (b, 0, 0))
    return pl.pallas_call(
        functools.partial(_gla_kernel, n_chunks=tr // GLA_CHUNK),
        grid=(nbatch // nb, t // tr),
        in_specs=[blk(GK_WIDTH), blk(GK_WIDTH), blk(GK_WIDTH), blk(B_WIDTH), blk(B_WIDTH),
                  pl.BlockSpec((1, B_WIDTH, GK_WIDTH), lambda b, j: (0, 0, 0)),
                  pl.BlockSpec((1, B_WIDTH), lambda b, j: (0, 0))],
        out_specs=[blk(B_WIDTH), state],
        out_shape=[jax.ShapeDtypeStruct((nbatch, t, B_WIDTH), BF16),
                   jax.ShapeDtypeStruct((nbatch, B_WIDTH, GK_WIDTH), F32)],
        scratch_shapes=[pltpu.VMEM((nb, B_WIDTH, GK_WIDTH), F32)],
        compiler_params=_params("parallel", "arbitrary"),
        name="gla_prompt",
    )(seq(q), seq(k), seq(g), seq(v), seq(r), s0, gg)


def _gla_sample_kernel(q_ref, k_ref, la_ref, v_ref, r_ref, s_ref, gg_ref, ob_ref, sn_ref):
    qt = q_ref[...].T
    kt = k_ref[...].T
    at = jnp.exp(la_ref[...]).T
    vt = v_ref[...].T
    rt = r_ref[...].T
    gg = gg_ref[...]
    outs = []
    for h in range(B_HEADS):
        vh = vt[h * B_DV:(h + 1) * B_DV, :]
        o = jnp.zeros_like(vh)
        for d in range(B_DK):
            i = h * B_DK + d
            s_new = at[i:i + 1, :] * s_ref[i] + kt[i:i + 1, :] * vh
            sn_ref[i] = s_new
            o = o + qt[i:i + 1, :] * s_new
        on = o * lax.rsqrt(jnp.mean(o * o, axis=0, keepdims=True) + LN_EPS)
        outs.append(on * gg * _silu(rt[h * B_DV:(h + 1) * B_DV, :]))
    ob_ref[...] = jnp.concatenate(outs, axis=0).T.astype(BF16)


def _gla_sample(q, k, la, v, r, state_t, gg_col, layer):
    ns = state_t.shape[3]
    row = lambda width: pl.BlockSpec((ns, width), lambda i: (BLOCK // ns, 0))
    return pl.pallas_call(
        _gla_sample_kernel,
        grid=(1,),
        in_specs=[row(GK_WIDTH), row(GK_WIDTH), row(GK_WIDTH), row(B_WIDTH), row(B_WIDTH),
                  pl.BlockSpec((None, GK_WIDTH, B_DV, ns), lambda i: (layer, 0, 0, 0)),
                  pl.BlockSpec((B_DV, 1), lambda i: (0, 0))],
        out_specs=[pl.BlockSpec((ns, B_WIDTH), lambda i: (0, 0)),
                   pl.BlockSpec((GK_WIDTH, B_DV, ns), lambda i: (0, 0, 0))],
        out_shape=[jax.ShapeDtypeStruct((ns, B_WIDTH), BF16),
                   jax.ShapeDtypeStruct((GK_WIDTH, B_DV, ns), F32)],
        compiler_params=_params("arbitrary"),
        name="gla_sample",
    )(q, k, la, v, r, state_t, gg_col)


CONV_HIST = 32
CONV_CHUNK = 64


def _conv_post(y, dwb, lng, lnb, pww, pwb):
    y = _silu(_ln_rows(y + dwb, lng, lnb))
    return _dot(y.astype(BF16), pww) + pwb


def _conv_kernel(u_ref, hist_ref, dww_ref, dwb_ref, lng_ref, lnb_ref, pww_ref, pwb_ref,
                 oc_ref, tail_ref, ubuf, *, n_chunks):
    t = u_ref.shape[1]
    ubuf[0, 0:CONV_HIST, :] = hist_ref[...]
    ubuf[0, CONV_HIST:CONV_HIST + t, :] = u_ref[0]
    off = CONV_HIST - (CONV_W - 1)
    n_copy = t + CONV_HIST - SUBLANES
    for s in range(1, SUBLANES):
        ubuf[s, 0:n_copy, :] = ubuf[0, s:s + n_copy, :]

    def body(c, carry):
        r0 = pl.multiple_of(c * CONV_CHUNK, CONV_CHUNK)
        acc = jnp.zeros((CONV_CHUNK, C_WIDTH), F32)
        for j in range(CONV_W):
            a, s = divmod(j + off, SUBLANES)
            rows = pl.ds(pl.multiple_of(r0 + a * SUBLANES, SUBLANES), CONV_CHUNK)
            acc = acc + ubuf[s, rows, :] * dww_ref[j:j + 1, :]
        oc = _conv_post(acc, dwb_ref[...], lng_ref[...], lnb_ref[...], pww_ref[...], pwb_ref[...])
        oc_ref[0, pl.ds(r0, CONV_CHUNK), :] = oc.astype(BF16)
        return carry

    lax.fori_loop(0, n_chunks, body, 0, unroll=2)
    tail_ref[0] = ubuf[0, t:t + CONV_HIST, :]


def _conv_prompt(u, hist, dww, dwb, lng, lnb, pww, pwb, nbatch):
    t = u.shape[0] // nbatch
    const = lambda shape: pl.BlockSpec(shape, lambda b: (0,) * len(shape))
    return pl.pallas_call(
        functools.partial(_conv_kernel, n_chunks=t // CONV_CHUNK),
        grid=(nbatch,),
        in_specs=[pl.BlockSpec((1, t, C_WIDTH), lambda b: (b, 0, 0)), const((CONV_HIST, C_WIDTH)),
                  const((CONV_HIST, C_WIDTH)), const((1, C_WIDTH)), const((1, C_WIDTH)), const((1, C_WIDTH)),
                  const((C_WIDTH, C_WIDTH)), const((1, C_WIDTH))],
        out_specs=[pl.BlockSpec((1, t, C_WIDTH), lambda b: (b, 0, 0)),
                   pl.BlockSpec((1, CONV_HIST, C_WIDTH), lambda b: (b, 0, 0))],
        out_shape=[jax.ShapeDtypeStruct((nbatch, t, C_WIDTH), BF16),
                   jax.ShapeDtypeStruct((nbatch, CONV_HIST, C_WIDTH), F32)],
        scratch_shapes=[pltpu.VMEM((SUBLANES, CONV_HIST + t, C_WIDTH), F32)],
        compiler_params=_params("parallel"),
        name="conv_prompt",
    )(u.reshape(nbatch, t, C_WIDTH), hist, dww, dwb, lng, lnb, pww, pwb)


def _conv_sample_kernel(hist_ref, u_ref, dww_ref, dwb_ref, lng_ref, lnb_ref, pww_ref, pwb_ref, oc_ref, tail_ref):
    nh = CONV_W - 1
    u = u_ref[...]
    y = u * dww_ref[nh:nh + 1, :]
    for j in range(nh):
        y = y + hist_ref[j] * dww_ref[j:j + 1, :]
    oc = _conv_post(y, dwb_ref[...], lng_ref[...], lnb_ref[...], pww_ref[...], pwb_ref[...])
    oc_ref[...] = oc.astype(BF16)
    for j in range(nh - 1):
        tail_ref[j] = hist_ref[j + 1]
    tail_ref[nh - 1] = u


def _conv_sample(hist_t, u, dww, dwb, lng, lnb, pww, pwb, layer):
    ns = hist_t.shape[2]
    nh = CONV_W - 1
    const = lambda shape: pl.BlockSpec(shape, lambda i: (0,) * len(shape))
    return pl.pallas_call(
        _conv_sample_kernel,
        grid=(1,),
        in_specs=[pl.BlockSpec((None, nh, ns, C_WIDTH), lambda i: (layer, 0, 0, 0)),
                  pl.BlockSpec((ns, C_WIDTH), lambda i: (BLOCK // ns, 0)),
                  const((CONV_HIST, C_WIDTH)), const((1, C_WIDTH)), const((1, C_WIDTH)), const((1, C_WIDTH)),
                  const((C_WIDTH, C_WIDTH)), const((1, C_WIDTH))],
        out_specs=[const((ns, C_WIDTH)), const((nh, ns, C_WIDTH))],
        out_shape=[jax.ShapeDtypeStruct((ns, C_WIDTH), BF16),
                   jax.ShapeDtypeStruct((nh, ns, C_WIDTH), F32)],
        compiler_params=_params("arbitrary"),
        name="conv_sample",
    )(hist_t, u, dww, dwb, lng, lnb, pww, pwb)


ROUTE_LANE0 = N_GROUPS


def _route(x, wr2, br):
    x_hi = x.astype(BF16)
    x_lo = (x - x_hi.astype(F32)).astype(BF16)
    l_hi = _dot(x_hi, wr2)
    logits = l_hi[:, :LANES] + l_hi[:, LANES:] + _dot(x_lo, wr2)[:, :LANES] + br
    lane = lax.broadcasted_iota(jnp.int32, logits.shape, 1).astype(F32)
    far = 1e3
    glm = jnp.where(lane < N_GROUPS, logits, NEG)
    gmax = jnp.max(glm, axis=1, keepdims=True)
    gi = jnp.min(jnp.where(glm == gmax, lane, far), axis=1, keepdims=True)
    p_grp = 1.0 / jnp.sum(jnp.exp(glm - gmax), axis=1, keepdims=True)
    lo = ROUTE_LANE0 + EXP_PER_GROUP * gi
    in_sel = (lane >= lo) & (lane < lo + EXP_PER_GROUP)
    elm = jnp.where(in_sel, logits, NEG)
    v1 = jnp.max(elm, axis=1, keepdims=True)
    i1 = jnp.min(jnp.where(elm == v1, lane, far), axis=1, keepdims=True)
    elm2 = jnp.where(lane == i1, NEG, elm)
    v2 = jnp.max(elm2, axis=1, keepdims=True)
    i2 = jnp.min(jnp.where((elm2 == v2) & in_sel & (lane != i1), lane, far), axis=1, keepdims=True)
    t = jnp.exp(v2 - v1)
    w1 = p_grp / (1.0 + t)
    w2 = w1 * t
    return jnp.where(lane == i1, w1, 0.0) + jnp.where(lane == i2, w2, 0.0), gi


MOE_ROWS = 128
MOE_ALIGN = 16
GROUP_WIDTH = EXP_PER_GROUP * D_EXPERT


def _moe_sorted_rows(tm):
    need = tm + N_GROUPS * MOE_ALIGN + MOE_ROWS
    return -(-need // LANES) * LANES


def _ffn_kernel(*refs, pre_ln):
    if pre_ln:
        lg_ref, lb_ref, *refs = refs
    (oa_ref, ob_ref, oc_ref, h_ref, wo_ref, g1_ref, b1_ref, wr_ref, br_ref, wg_ref, wu_ref, wd_ref,
     g_ref, b_ref, o_ref, xs_ref, ys_ref, ws_ref) = refs
    h = h_ref[...]
    if pre_ln:
        h = _ln_rows(h, lg_ref[...], lb_ref[...])
    mix = (_dot(oa_ref[...], wo_ref[0:A_WIDTH, :])
           + _dot(ob_ref[...], wo_ref[A_WIDTH:A_WIDTH + B_WIDTH, :])
           + _dot(oc_ref[...], wo_ref[A_WIDTH + B_WIDTH:D_MODEL, :]))
    x = _ln_rows(ALPHA * h + mix, g1_ref[...], b1_ref[...])
    tm = x.shape[0]
    ns = xs_ref.shape[0]
    iota = lambda shape, axis: lax.broadcasted_iota(jnp.int32, shape, axis)
    dw, gi = _route(x, wr_ref[...], br_ref[...])
    lane_f = iota((tm, LANES), 1).astype(F32)
    onehot = jnp.where(lane_f == gi, 1.0, 0.0)
    tri = jnp.where(iota((tm, tm), 1) <= iota((tm, tm), 0), 1.0, 0.0).astype(BF16)
    cum = _dot(tri, onehot.astype(BF16))
    rank = jnp.sum(onehot * (cum - 1.0), axis=1, keepdims=True)
    counts = cum[tm - 1:tm, :]

    starts, tiles = [], []
    start = jnp.int32(0)
    for grp in range(N_GROUPS):
        n_g = counts[0, grp].astype(jnp.int32)
        starts.append(start)
        tiles.append((n_g + (MOE_ROWS - 1)) >> (MOE_ROWS.bit_length() - 1))
        start = start + ((n_g + (MOE_ALIGN - 1)) & -MOE_ALIGN)
    lane1 = iota((1, LANES), 1)
    start_v = jnp.zeros((1, LANES), F32)
    for grp in range(N_GROUPS):
        start_v = jnp.where(lane1 == grp, starts[grp].astype(F32), start_v)
    pos = jnp.sum(onehot * start_v, axis=1, keepdims=True) + rank

    pos_i = pos.astype(jnp.int32)
    digits = jnp.where(lane_f == 0.0, (pos_i >> 5).astype(F32),
                       jnp.where(lane_f == 1.0, (pos_i & 31).astype(F32), 0.0)).astype(BF16)
    lane8 = iota((8, LANES), 1)
    radix = jnp.where(lane8 == 0, 32.0, jnp.where(lane8 == 1, 1.0, 0.0)).astype(BF16)
    pos_row = _dot_nt(radix, digits)[0:1, :]

    perm = jnp.where(iota((ns, tm), 0).astype(F32) == pos_row, 1.0, 0.0).astype(BF16)
    dw_hi = dw.astype(BF16)
    dw_lo = (dw - dw_hi.astype(F32)).astype(BF16)
    srt = _dot(perm, jnp.concatenate([x.astype(BF16), dw_hi, dw_lo], axis=1))
    xs_ref[...] = srt[:, :D_MODEL].astype(BF16)
    ws_ref[...] = srt[:, D_MODEL:D_MODEL + LANES] + srt[:, D_MODEL + LANES:]
    ys_ref[...] = jnp.zeros_like(ys_ref)

    for grp in range(N_GROUPS):
        def body(k, carry, grp=grp):
            r0 = pl.multiple_of(starts[grp] + k * MOE_ROWS, MOE_ALIGN)
            rows = pl.ds(r0, MOE_ROWS)
            xt = xs_ref[rows, :]
            w = ws_ref[rows, :]
            hes = []
            for e in range(EXP_PER_GROUP):
                ex = grp * EXP_PER_GROUP + e
                w_e = w[:, ROUTE_LANE0 + ex:ROUTE_LANE0 + ex + 1]
                hes.append((_silu(_dot(xt, wg_ref[ex])) * _dot(xt, wu_ref[ex]) * w_e).astype(BF16))
            ys_ref[rows, :] = _dot(jnp.concatenate(hes, axis=1), wd_ref[grp]).astype(BF16)
            return carry
        lax.fori_loop(0, tiles[grp], body, 0)

    unperm = jnp.where(iota((tm, ns), 1).astype(F32) == pos, 1.0, 0.0).astype(BF16)
    y = _dot(unperm, ys_ref[...])
    o_ref[...] = _ln_rows(ALPHA * x + y, g_ref[...], b_ref[...])


def _ffn(oa, ob, oc, h, wo, g1, b1, wr2, br, wg, wu, wd, g2, b2, tm, layer, ln=None):
    n = h.shape[0]
    ns = _moe_sorted_rows(tm)
    row = lambda width: pl.BlockSpec((tm, width), lambda i: (i, 0))
    const = lambda shape: pl.BlockSpec(shape, lambda i: (0,) * len(shape))
    resident = lambda shape: pl.BlockSpec((None,) + shape, lambda i: (layer,) + (0,) * len(shape),
                                          pipeline_mode=pl.Buffered(1))
    ln_specs = [const((1, D_MODEL)), const((1, D_MODEL))] if ln else []
    return pl.pallas_call(
        functools.partial(_ffn_kernel, pre_ln=bool(ln)),
        grid=(n // tm,),
        in_specs=ln_specs + [row(A_WIDTH), row(B_WIDTH), row(C_WIDTH), row(D_MODEL),
                             resident((D_MODEL, D_MODEL)), const((1, D_MODEL)), const((1, D_MODEL)),
                             resident((D_MODEL, 2 * LANES)), const((1, LANES)),
                             resident((N_EXPERTS, D_MODEL, D_EXPERT)), resident((N_EXPERTS, D_MODEL, D_EXPERT)),
                             resident((N_GROUPS, GROUP_WIDTH, D_MODEL)),
                             const((1, D_MODEL)), const((1, D_MODEL))],
        out_specs=row(D_MODEL),
        out_shape=jax.ShapeDtypeStruct((n, D_MODEL), F32),
        scratch_shapes=[pltpu.VMEM((ns, D_MODEL), BF16), pltpu.VMEM((ns, D_MODEL), BF16),
                        pltpu.VMEM((ns, LANES), F32)],
        compiler_params=_params("parallel"),
        name="ffn",
    )(*(ln or ()), oa, ob, oc, h, wo, g1, b1, wr2, br, wg, wu, wd, g2, b2)


def _rope_tables(pos):
    half = HEAD_DIM // 2
    inv = ROPE_THETA ** (-jnp.arange(half, dtype=F32) / half)
    ang = pos.astype(F32)[:, None] * inv[None, :]
    cos, sin = jnp.cos(ang), jnp.sin(ang)
    cos_t = jnp.concatenate([cos, cos] * (LANES // HEAD_DIM), axis=1)
    sin_t = jnp.concatenate([-sin, sin] * (LANES // HEAD_DIM), axis=1)
    return cos_t, sin_t


def _row(v):
    return v.reshape(1, -1)


def kernel(x_prompt, x_sample, cache_meta_k, cache_meta_v, cache_win_k, cache_win_v, state_gla, state_conv,
           meta_tokens, ln_in_g, ln_in_b, w_in, attn_sink, w_alpha, b_alpha, gla_norm_g,
           conv_dw_w, conv_dw_b, conv_ln_g, conv_ln_b, conv_pw_w, conv_pw_b, w_out, ln1_g, ln1_b,
           w_router_group, b_router_group, w_router_expert, b_router_expert,
           w_exp_gate, w_exp_up, w_exp_down, ln2_g, ln2_b):
    nb, seq, d = x_prompt.shape
    ns = x_sample.shape[0]
    nwin = cache_win_k.shape[2]
    n_big = nb * seq
    n_small = BLOCK + ns
    tm_big = min(512, seq)

    small_in = jnp.concatenate([jnp.zeros((META_PAD, d), F32), meta_tokens.astype(F32),
                                x_sample.reshape(ns, d)], axis=0)
    hb = x_prompt.reshape(n_big, d)
    hs = small_in
    ln_in = (_row(ln_in_g), _row(ln_in_b))

    cos_b, sin_b = _rope_tables(N_META + jnp.arange(seq))
    pos_small = jnp.concatenate([jnp.maximum(jnp.arange(BLOCK) - META_PAD, 0),
                                 jnp.full((ns,), PAST_LEN, jnp.int32)])
    cos_s, sin_s = _rope_tables(pos_small)

    outs = [[] for _ in range(12)]
    zeros_hist = jnp.zeros((CONV_HIST, C_WIDTH), F32)
    zeros_state = jnp.zeros((1, B_WIDTH, GK_WIDTH), F32)

    nl = w_in.shape[0]
    w_pad = jnp.concatenate([w_in[:, :, :C_AB + B_RANK], jnp.zeros((nl, d, LANES - B_RANK), F32),
                             w_in[:, :, C_AB + B_RANK:]], axis=2).astype(BF16)
    wa_all = jnp.concatenate([w_alpha, jnp.zeros((nl, LANES - B_RANK, GK_WIDTH), F32)], axis=1).astype(BF16)
    ba_all = b_alpha[:, None, :]
    wo_all = w_out.astype(BF16)
    wr = jnp.concatenate([w_router_group, w_router_expert,
                          jnp.zeros((nl, d, LANES - N_GROUPS - N_EXPERTS), F32)], axis=2)
    wr_hi = wr.astype(BF16)
    wr2_all = jnp.concatenate([wr_hi, (wr - wr_hi.astype(F32)).astype(BF16)], axis=2)
    wg_all, wu_all = w_exp_gate.astype(BF16), w_exp_up.astype(BF16)
    wd_all = w_exp_down.astype(BF16).reshape(nl, N_GROUPS, GROUP_WIDTH, d)
    win_k_t = cache_win_k.transpose(0, 1, 3, 4, 2).reshape(nl, ns, KV_WIDTH, nwin)
    win_v_t = cache_win_v.transpose(0, 1, 3, 4, 2).reshape(nl, ns, KV_WIDTH, nwin)
    gla_t = state_gla.transpose(0, 2, 3, 4, 1).reshape(nl, GK_WIDTH, B_DV, ns)
    conv_t = state_conv.transpose(0, 2, 1, 3)

    for l in range(DEPTH):
        sink = attn_sink[l].astype(F32)
        sink_col = jnp.concatenate([sink, jnp.zeros((HEAD_ROWS - A_HEADS,), F32)])[:, None]
        gg_t = _row(jnp.tile(gla_norm_g[l], B_HEADS))
        gg_col = gla_norm_g[l][:, None]
        dww = jnp.concatenate([conv_dw_w[l], jnp.zeros((CONV_HIST - CONV_W, C_WIDTH), F32)], axis=0)
        dwb, clg, clb = _row(conv_dw_b[l]), _row(conv_ln_g[l]), _row(conv_ln_b[l])
        pww, pwb = conv_pw_w[l].astype(BF16), _row(conv_pw_b[l])
        br = _row(jnp.concatenate([b_router_group[l], b_router_expert[l],
                                   jnp.zeros((LANES - N_GROUPS - N_EXPERTS,), F32)]))

        ln = ln_in if l == 0 else None
        qa_b, ka_b, va_b, qg_b, kg_b, la_b, vg_b, rg_b, u_b = _proj_in(
            hb, w_pad, wa_all, ba_all, cos_b, sin_b, layer=l, tm=tm_big, n_pad=0, n_seq=tm_big, ln=ln)
        qa_s, ka_s, va_s, qg_s, kg_s, la_s, vg_s, rg_s, u_s = _proj_in(
            hs, w_pad, wa_all, ba_all, cos_s, sin_s, layer=l, tm=n_small, n_pad=META_PAD, n_seq=BLOCK, ln=ln)

        oa_b = _attn_prompt(sink, qa_b, ka_b, va_b, ka_s, va_s, nb)
        oa_m = _attn_meta(sink, qa_s, ka_s, va_s)
        q_smp = qa_s[BLOCK:].reshape(ns, A_KV_HEADS, A_GROUP, 1, HEAD_DIM)
        eye = jnp.eye(A_KV_HEADS, dtype=BF16)[None, :, None, :, None]
        q_bd = (q_smp * eye).reshape(ns, A_HEADS, KV_WIDTH)
        q_bd = jnp.pad(q_bd, ((0, 0), (0, HEAD_ROWS - A_HEADS), (0, 0)))
        o_bd, nwk_t, nwv_t = _attn_sample(
            sink_col, q_bd, ka_s, va_s,
            cache_meta_k[l].reshape(ns, N_META, KV_WIDTH), cache_meta_v[l].reshape(ns, N_META, KV_WIDTH),
            win_k_t, win_v_t, l)
        o_bd = o_bd[:, :A_HEADS].reshape(ns, A_KV_HEADS, A_GROUP, A_KV_HEADS, HEAD_DIM)
        oa_smp = jnp.stack([o_bd[:, c, :, c, :] for c in range(A_KV_HEADS)], axis=1).reshape(ns, A_WIDTH)
        oa_s = jnp.concatenate([oa_m, oa_smp.astype(BF16)], axis=0)

        ob_m, st_m = _gla_prompt(qg_s[:BLOCK], kg_s[:BLOCK], la_s[:BLOCK], vg_s[:BLOCK], rg_s[:BLOCK],
                                 zeros_state, gg_t, 1)
        ob_b, st_b = _gla_prompt(qg_b, kg_b, la_b, vg_b, rg_b, st_m, gg_t, nb)
        ob_smp, s_new_t = _gla_sample(qg_s, kg_s, la_s, vg_s, rg_s, gla_t, gg_col, l)
        ob_s = jnp.concatenate([ob_m.reshape(BLOCK, B_WIDTH), ob_smp], axis=0)
        st5 = st_b.reshape(nb, B_HEADS, B_DV, B_HEADS, B_DK)
        gla_p = jnp.stack([st5[:, h, :, h, :] for h in range(B_HEADS)], axis=1).transpose(0, 1, 3, 2)

        oc_m, _ = _conv_prompt(u_s[:BLOCK], zeros_hist, dww, dwb, clg, clb, pww, pwb, 1)
        oc_b, tail_b = _conv_prompt(u_b, u_s[BLOCK - CONV_HIST:BLOCK], dww, dwb, clg, clb, pww, pwb, nb)
        oc_smp, tail_s_t = _conv_sample(conv_t, u_s, dww, dwb, clg, clb, pww, pwb, l)
        oc_s = jnp.concatenate([oc_m.reshape(BLOCK, C_WIDTH), oc_smp], axis=0)

        l1g, l1b, l2g, l2b = _row(ln1_g[l]), _row(ln1_b[l]), _row(ln2_g[l]), _row(ln2_b[l])
        hb = _ffn(oa_b, ob_b.reshape(n_big, B_WIDTH), oc_b.reshape(n_big, C_WIDTH), hb, wo_all, l1g, l1b,
                  wr2_all, br, wg_all, wu_all, wd_all, l2g, l2b, tm_big, l, ln=ln)
        hs = _ffn(oa_s, ob_s, oc_s, hs, wo_all, l1g, l1b, wr2_all, br, wg_all, wu_all, wd_all, l2g, l2b,
                  n_small, l, ln=ln)

        kv4 = lambda a: a.reshape(a.shape[0], a.shape[1], A_KV_HEADS, HEAD_DIM)
        win_t = lambda a: a.reshape(ns, A_KV_HEADS, HEAD_DIM, nwin).transpose(0, 3, 1, 2)
        meta_k = jnp.broadcast_to(ka_s[META_PAD:BLOCK][None], (nb, N_META, KV_WIDTH))
        meta_v = jnp.broadcast_to(va_s[META_PAD:BLOCK][None], (nb, N_META, KV_WIDTH))
        win_k = ka_b.reshape(nb, seq, KV_WIDTH)[:, seq - nwin:]
        win_v = va_b.reshape(nb, seq, KV_WIDTH)[:, seq - nwin:]
        layer_out = (None, None, kv4(meta_k), kv4(meta_v), kv4(win_k), kv4(win_v), win_t(nwk_t), win_t(nwv_t),
                     gla_p, s_new_t.reshape(B_HEADS, B_DK, B_DV, ns).transpose(3, 0, 1, 2),
                     tail_b[:, CONV_HIST - (CONV_W - 1):], tail_s_t.transpose(1, 0, 2))
        for i in range(2, 12):
            outs[i].append(layer_out[i])

    y_prompt = hb.reshape(nb, seq, d)
    y_sample = hs[BLOCK:].reshape(ns, 1, d)
    return (y_prompt, y_sample) + tuple(jnp.stack(o) for o in outs[2:])
```

```python
import functools

import jax
import jax.numpy as jnp
from jax import lax
from jax.experimental import pallas as pl
from jax.experimental.pallas import tpu as pltpu

F32 = jnp.float32
BF16 = jnp.bfloat16

D_MODEL = 1024
DEPTH = 2
PAST_LEN = 16384
N_META = 16
HEAD_DIM = 64
A_WIDTH = 512
A_HEADS = 8
A_KV_HEADS = 2
A_GROUP = 4
WINDOW = 128
BLOCK = 128
ROPE_THETA = 10000.0
B_WIDTH = 256
B_HEADS = 4
B_DV = 64
B_DK = 32
B_RANK = 16
GATE_TAU = 16.0
GLA_CHUNK = 64
C_WIDTH = 256
CONV_W = 31
N_GROUPS = 4
EXP_PER_GROUP = 4
N_EXPERTS = 16
D_EXPERT = 256
ALPHA = (2 * DEPTH) ** 0.25
LN_EPS = 1e-5

LANES = 128
SUBLANES = 8
META_PAD = BLOCK - N_META
KV_WIDTH = A_KV_HEADS * HEAD_DIM
GK_WIDTH = B_HEADS * B_DK
C_QA = 0
C_KA = C_QA + A_WIDTH
C_VA = C_KA + KV_WIDTH
C_QB = C_VA + KV_WIDTH
C_KB = C_QB + GK_WIDTH
C_VB = C_KB + GK_WIDTH
C_RB = C_VB + B_WIDTH
C_AB = C_RB + B_WIDTH
C_CG = C_AB + LANES
PROJ_PAD_WIDTH = C_CG + 2 * C_WIDTH
NEG = -1e30
VMEM_LIMIT = 56 * 1024 * 1024


def _dot(a, b):
    return jnp.dot(a, b, preferred_element_type=F32)


def _dot_nt(a, b):
    return lax.dot_general(a, b, (((1,), (1,)), ((), ())), preferred_element_type=F32)


def _dot_tn(a, b):
    return lax.dot_general(a, b, (((0,), (0,)), ((), ())), preferred_element_type=F32)


def _ln_rows(x, g, b):
    xc = x - jnp.mean(x, -1, keepdims=True)
    var = jnp.mean(xc * xc, -1, keepdims=True)
    return xc * lax.rsqrt(var + LN_EPS) * g + b


def _silu(x):
    return x * jax.nn.sigmoid(x)


def _split3(x):
    hi = x.astype(BF16)
    r1 = x - hi.astype(F32)
    mid = r1.astype(BF16)
    lo = (r1 - mid.astype(F32)).astype(BF16)
    return hi, mid, lo


def _params(*sem):
    return pltpu.CompilerParams(dimension_semantics=sem, vmem_limit_bytes=VMEM_LIMIT)


def _proj_in_kernel(*refs, n_pad, n_seq, pre_ln):
    if pre_ln:
        lg_ref, lb_ref, *refs = refs
    (x_ref, w_ref, wa_ref, ba_ref, cos_ref, sin_ref,
     qa_ref, ka_ref, va_ref, qg_ref, kg_ref, la_ref, vg_ref, rg_ref, u_ref) = refs
    x = x_ref[...]
    if pre_ln:
        x = _ln_rows(x, lg_ref[...], lb_ref[...])
    xb = x.astype(BF16)
    tm = xb.shape[0]
    cos = cos_ref[...]
    sin = sin_ref[...]
    lane = lax.broadcasted_iota(jnp.int32, (tm, LANES), 1)
    first_half = (lane & (HEAD_DIM // 2)) == 0

    def rope(z):
        rot = jnp.where(first_half, pltpu.roll(z, LANES - HEAD_DIM // 2, 1), pltpu.roll(z, HEAD_DIM // 2, 1))
        return z * cos + rot * sin

    if n_pad:
        valid = (lax.broadcasted_iota(jnp.int32, (tm, 1), 0) >= n_pad).astype(F32)
    else:
        valid = None

    za = _dot(xb, w_ref[:, C_QA:C_QB])
    for c in range(A_WIDTH // LANES):
        zq = za[:, c * LANES:(c + 1) * LANES]
        qa_ref[:, c * LANES:(c + 1) * LANES] = (rope(zq) * (HEAD_DIM ** -0.5)).astype(BF16)
    ka_ref[...] = rope(za[:, C_KA:C_VA])
    va_ref[...] = za[:, C_VA:C_QB]

    zb = _dot(xb, w_ref[:, C_QB:C_CG])
    o = C_QB
    qg_ref[...] = zb[:, C_QB - o:C_KB - o] * (B_DK ** -0.5)
    kg = zb[:, C_KB - o:C_VB - o]
    vg_ref[...] = zb[:, C_VB - o:C_RB - o]
    rg_ref[...] = zb[:, C_RB - o:C_AB - o]
    ab = zb[:, C_AB - o:C_CG - o].astype(BF16)
    xa = _dot(ab, wa_ref[...]) + ba_ref[...]
    la = (jnp.minimum(xa, 0.0) - jnp.log(1.0 + jnp.exp(-jnp.abs(xa)))) * (1.0 / GATE_TAU)

    zc = _dot(xb, w_ref[:, C_CG:PROJ_PAD_WIDTH])
    u = zc[:, :C_WIDTH] * jax.nn.sigmoid(zc[:, C_WIDTH:])
    if valid is not None:
        kg = kg * valid
        la = la * valid
        u = u * valid
    kg_ref[...] = kg
    u_ref[...] = u
    cs = GLA_CHUNK
    tri = jnp.where(lax.broadcasted_iota(jnp.int32, (cs, cs), 1) <= lax.broadcasted_iota(jnp.int32, (cs, cs), 0),
                    1.0, 0.0).astype(BF16)
    la_h, la_m, la_l = _split3(la)
    for c in range(n_seq // cs):
        rows = slice(c * cs, (c + 1) * cs)
        la_ref[rows, :] = _dot(tri, la_h[rows]) + _dot(tri, la_m[rows]) + _dot(tri, la_l[rows])
    if n_seq < tm:
        la_ref[n_seq:tm, :] = la[n_seq:tm]


def _proj_in(h, w, wa, ba, cos, sin, *, layer, tm, n_pad, n_seq, ln=None):
    n = h.shape[0]
    tb = cos.shape[0] // tm
    row = lambda width: pl.BlockSpec((tm, width), lambda i: (i, 0))
    const = lambda shape: pl.BlockSpec(shape, lambda i: (0, 0))
    per_layer = lambda shape: pl.BlockSpec((None,) + shape, lambda i: (layer, 0, 0))
    tab = pl.BlockSpec((tm, LANES), lambda i: (i % tb, 0))
    widths = (A_WIDTH, KV_WIDTH, KV_WIDTH, GK_WIDTH, GK_WIDTH, GK_WIDTH, B_WIDTH, B_WIDTH, C_WIDTH)
    dtypes = (BF16,) + (F32,) * 8
    ln_specs = [const((1, D_MODEL)), const((1, D_MODEL))] if ln else []
    return pl.pallas_call(
        functools.partial(_proj_in_kernel, n_pad=n_pad, n_seq=n_seq, pre_ln=bool(ln)),
        grid=(n // tm,),
        in_specs=ln_specs + [row(D_MODEL), per_layer((D_MODEL, PROJ_PAD_WIDTH)), per_layer((LANES, GK_WIDTH)),
                             per_layer((1, GK_WIDTH)), tab, tab],
        out_specs=[row(wd) for wd in widths],
        out_shape=[jax.ShapeDtypeStruct((n, wd), dt) for wd, dt in zip(widths, dtypes)],
        compiler_params=_params("parallel"),
        name="proj_in",
    )(*(ln or ()), h, w, wa, ba, cos, sin)


def _attn_kernel(sink_ref, q_ref, km_ref, vm_ref, *rest, meta_mode):
    if meta_mode:
        (o_ref,) = rest
        blocks = [(0, q_ref[...], km_ref[...], vm_ref[...], None)]
    else:
        kp_ref, vp_ref, kc_ref, vc_ref, o_ref = rest
        km, vm = km_ref[...], vm_ref[...]
        blocks = []
        for sub in range(q_ref.shape[0] // BLOCK):
            cur = slice(sub * BLOCK, (sub + 1) * BLOCK)
            if sub == 0:
                kp, vp = kp_ref[...], vp_ref[...]
                has_prev = pl.program_id(1) >= 1
            else:
                prev = slice((sub - 1) * BLOCK, sub * BLOCK)
                kp, vp = kc_ref[prev, :], vc_ref[prev, :]
                has_prev = True
            blocks.append((sub * BLOCK, q_ref[cur, :], jnp.concatenate([kp, kc_ref[cur, :], km], axis=0),
                           jnp.concatenate([vp, vc_ref[cur, :], vm], axis=0), has_prev))

    units = []
    for row0, q, k_all, v_all, has_prev in blocks:
        nk = k_all.shape[0]
        ki = lax.broadcasted_iota(jnp.int32, (nk, BLOCK), 0)
        qi = lax.broadcasted_iota(jnp.int32, (nk, BLOCK), 1)
        if has_prev is None:
            ok = ki <= qi - META_PAD
        else:
            prev_lo = qi if has_prev is True else qi + jnp.where(has_prev, 0, BLOCK)
            ok = ((ki >= prev_lo) & (ki < BLOCK)) | ((ki >= BLOCK) & (ki <= qi + BLOCK)) | (ki >= 2 * BLOCK)
        bias = jnp.where(ok, 0.0, NEG)
        kb, vb = k_all.astype(BF16), v_all.astype(BF16)
        for kvh in range(A_KV_HEADS):
            lanes = slice(kvh * HEAD_DIM, (kvh + 1) * HEAD_DIM)
            qs = jnp.concatenate([q[:, (kvh * A_GROUP + g) * HEAD_DIM:(kvh * A_GROUP + g + 1) * HEAD_DIM]
                                  for g in range(A_GROUP)], axis=0)
            units.append((row0, kvh, bias, _dot_nt(kb[:, lanes], qs), vb[:, lanes]))

    outs = []
    for row0, kvh, bias, st, vv in units:
        ps, dens = [], []
        for g in range(A_GROUP):
            s = st[:, g * BLOCK:(g + 1) * BLOCK] + bias
            sink = sink_ref[kvh * A_GROUP + g]
            m = jnp.maximum(jnp.max(s, axis=0, keepdims=True), sink)
            p = jnp.exp(s - m)
            dens.append(jnp.sum(p, axis=0, keepdims=True) + jnp.exp(sink - m))
            ps.append(p.astype(BF16))
        outs.append((row0, kvh, _dot_tn(vv, jnp.concatenate(ps, axis=1)), jnp.concatenate(dens, axis=1)))

    for row0, kvh, ot, den in outs:
        ot = ot / den
        for pair in range(A_GROUP // 2):
            two = jnp.concatenate([ot[:, (2 * pair) * BLOCK:(2 * pair + 1) * BLOCK],
                                   ot[:, (2 * pair + 1) * BLOCK:(2 * pair + 2) * BLOCK]], axis=0)
            h0 = kvh * A_GROUP + 2 * pair
            o_ref[row0:row0 + BLOCK, h0 * HEAD_DIM:(h0 + 2) * HEAD_DIM] = two.T.astype(BF16)


ATTN_SUB = 4


def _attn_prompt(sink, q, k, v, k_small, v_small, nbatch):
    n = q.shape[0]
    nblk = n // BLOCK // nbatch
    sub = ATTN_SUB if nblk % ATTN_SUB == 0 else 1
    nstep = nblk // sub
    kvs = pl.BlockSpec((sub * BLOCK, KV_WIDTH), lambda b, j: (b * nstep + j, 0))
    kvp = pl.BlockSpec((BLOCK, KV_WIDTH), lambda b, j: (b * nblk + jnp.maximum(j * sub - 1, 0), 0))
    kvm = pl.BlockSpec((N_META, KV_WIDTH), lambda b, j: (META_PAD // N_META, 0))
    return pl.pallas_call(
        functools.partial(_attn_kernel, meta_mode=False),
        grid=(nbatch, nstep),
        in_specs=[pl.BlockSpec(memory_space=pltpu.SMEM),
                  pl.BlockSpec((sub * BLOCK, A_WIDTH), lambda b, j: (b * nstep + j, 0)),
                  kvm, kvm, kvp, kvp, kvs, kvs],
        out_specs=pl.BlockSpec((sub * BLOCK, A_WIDTH), lambda b, j: (b * nstep + j, 0)),
        out_shape=jax.ShapeDtypeStruct((n, A_WIDTH), BF16),
        compiler_params=_params("parallel", "parallel"),
        name="attn_prompt",
    )(sink, q, k_small, v_small, k, v, k, v)


def _attn_meta(sink, q_small, k_small, v_small):
    kvm = pl.BlockSpec((N_META, KV_WIDTH), lambda i: (META_PAD // N_META, 0))
    return pl.pallas_call(
        functools.partial(_attn_kernel, meta_mode=True),
        grid=(1,),
        in_specs=[pl.BlockSpec(memory_space=pltpu.SMEM),
                  pl.BlockSpec((BLOCK, A_WIDTH), lambda i: (0, 0)), kvm, kvm],
        out_specs=pl.BlockSpec((BLOCK, A_WIDTH), lambda i: (0, 0)),
        out_shape=jax.ShapeDtypeStruct((BLOCK, A_WIDTH), BF16),
        compiler_params=_params("arbitrary"),
        name="attn_meta",
    )(sink, q_small, k_small, v_small)


SAMPLE_BLOCK = 32
HEAD_ROWS = 16


def _attn_sample_kernel(sink_ref, q_ref, kn_ref, vn_ref, mk_ref, mv_ref, wk_ref, wv_ref,
                        o_ref, nwk_ref, nwv_ref):
    sb = q_ref.shape[0]
    q = q_ref[...]
    kn = kn_ref[...]
    vn = vn_ref[...]
    s_win = jnp.einsum('bhl,blj->bhj', q, wk_ref[...].astype(BF16), preferred_element_type=F32)
    s_meta = jnp.einsum('bhl,bml->bhm', q, mk_ref[...].astype(BF16), preferred_element_type=F32)
    s_self = jnp.sum(q.astype(F32) * kn[:, None, :], axis=2, keepdims=True)
    sink = sink_ref[...][None]
    m = jnp.maximum(jnp.maximum(jnp.max(s_win, axis=2, keepdims=True), jnp.max(s_meta, axis=2, keepdims=True)),
                    jnp.maximum(s_self, sink))
    p_win = jnp.exp(s_win - m)
    p_meta = jnp.exp(s_meta - m)
    p_self = jnp.exp(s_self - m)
    denom = (jnp.sum(p_win, axis=2, keepdims=True) + jnp.sum(p_meta, axis=2, keepdims=True) + p_self
             + jnp.exp(sink - m))
    o = (jnp.einsum('bhj,blj->bhl', p_win.astype(BF16), wv_ref[...].astype(BF16), preferred_element_type=F32)
         + jnp.einsum('bhm,bml->bhl', p_meta.astype(BF16), mv_ref[...].astype(BF16), preferred_element_type=F32)
         + p_self * vn[:, None, :])
    o_ref[...] = o / denom
    nwin = wk_ref.shape[2]
    last = lax.broadcasted_iota(jnp.int32, (KV_WIDTH, nwin), 1) == nwin - 1
    knt = kn.T
    vnt = vn.T
    for j in range(sb):
        nwk_ref[j] = jnp.where(last, knt[:, j:j + 1], pltpu.roll(wk_ref[j], nwin - 1, 1))
        nwv_ref[j] = jnp.where(last, vnt[:, j:j + 1], pltpu.roll(wv_ref[j], nwin - 1, 1))


def _attn_sample(sink_col, q_bd, k_small, v_small, meta_k, meta_v, win_k_t, win_v_t, layer):
    ns = q_bd.shape[0]
    sb = min(SAMPLE_BLOCK, ns)
    nwin = win_k_t.shape[3]
    blk = lambda rows: pl.BlockSpec((sb, rows, KV_WIDTH), lambda i: (i, 0, 0))
    new = pl.BlockSpec((sb, KV_WIDTH), lambda i: (BLOCK // sb + i, 0))
    win = pl.BlockSpec((None, sb, KV_WIDTH, nwin), lambda i: (layer, i, 0, 0))
    win_out = pl.BlockSpec((sb, KV_WIDTH, nwin), lambda i: (i, 0, 0))
    return pl.pallas_call(
        _attn_sample_kernel,
        grid=(ns // sb,),
        in_specs=[pl.BlockSpec((HEAD_ROWS, 1), lambda i: (0, 0)),
                  blk(HEAD_ROWS), new, new, blk(N_META), blk(N_META), win, win],
        out_specs=[blk(HEAD_ROWS), win_out, win_out],
        out_shape=[jax.ShapeDtypeStruct((ns, HEAD_ROWS, KV_WIDTH), F32),
                   jax.ShapeDtypeStruct((ns, KV_WIDTH, nwin), F32),
                   jax.ShapeDtypeStruct((ns, KV_WIDTH, nwin), F32)],
        compiler_params=_params("parallel"),
        name="attn_sample",
    )(sink_col, q_bd, k_small, v_small, meta_k, meta_v, win_k_t, win_v_t)


def _seg_mean_sq(o, mseg):
    sq = o * o
    hi = sq.astype(BF16)
    lo = (sq - hi.astype(F32)).astype(BF16)
    return _dot(hi, mseg) + _dot(lo, mseg)


def _gla_kernel(q_ref, k_ref, g_ref, v_ref, r_ref, s0_ref, gg_ref, ob_ref, st_ref, st_scr, *, n_chunks):
    cs = GLA_CHUNK
    nseq = q_ref.shape[0]

    @pl.when(pl.program_id(1) == 0)
    def _():
        for b in range(nseq):
            st_scr[b] = s0_ref[0]

    iota = lambda shape, axis: lax.broadcasted_iota(jnp.int32, shape, axis)
    lg_cs, lg_dk, lg_dv = cs.bit_length() - 1, B_DK.bit_length() - 1, B_DV.bit_length() - 1
    kd_mask = (iota((B_HEADS * cs, GK_WIDTH), 0) >> lg_cs) == (iota((B_HEADS * cs, GK_WIDTH), 1) >> lg_dk)
    vd_mask = (iota((B_HEADS * cs, B_WIDTH), 0) >> lg_cs) == (iota((B_HEADS * cs, B_WIDTH), 1) >> lg_dv)
    st_mask = (iota((B_WIDTH, GK_WIDTH), 0) >> lg_dv) == (iota((B_WIDTH, GK_WIDTH), 1) >> lg_dk)
    t_idx = iota((cs, B_HEADS * cs), 0)
    s_idx = iota((cs, B_HEADS * cs), 1) & (cs - 1)
    diff = t_idx ^ s_idx
    level = jnp.full((cs, B_HEADS * cs), lg_cs, jnp.int32)
    for bit in range(lg_cs):
        level = jnp.where(diff >= (1 << bit), bit, level)
    level = jnp.where(s_idx > t_idx, -1, level)
    odd_row = (iota((cs, GK_WIDTH), 0) & 1) == 1
    mseg =jnp.where((iota((B_WIDTH, B_WIDTH), 0) >> lg_dv) == (iota((B_WIDTH, B_WIDTH), 1) >> lg_dv),
                     1.0 / B_DV, 0.0).astype(BF16)
    gg = gg_ref[...]

    def scores(qt, kt):
        kbd = jnp.where(kd_mask, jnp.concatenate([kt.astype(BF16)] * B_HEADS, axis=0), 0.0)
        return _dot_nt(qt.astype(BF16), kbd)

    def level_ref(g, bit):
        if bit == 0:
            return jnp.where(odd_row, pltpu.roll(g, 1, 0), g)
        half = 1 << bit
        return jnp.concatenate([jnp.broadcast_to(g[p + half - 1:p + half, :], (2 * half, GK_WIDTH))
                                for p in range(0, cs, 2 * half)], axis=0)

    def body(c, carry):
        rows = pl.ds(pl.multiple_of(c * cs, cs), cs)
        seqs = range(nseq)
        q = [q_ref[b, rows, :] for b in seqs]
        k = [k_ref[b, rows, :] for b in seqs]
        g = [g_ref[b, rows, :] for b in seqs]
        att = [jnp.where(level == lg_cs, scores(q[b], k[b]), 0.0) for b in seqs]
        for bit in range(lg_cs):
            for b in seqs:
                decay = jnp.exp(-jnp.abs(g[b] - level_ref(g[b], bit)))
                att[b] = jnp.where(level == bit, scores(q[b] * decay, k[b] * decay), att[b])
        for b in seqs:
            g_end = g[b][cs - 1:cs, :]
            vb = v_ref[b, rows, :].astype(BF16)
            vbd = jnp.where(vd_mask, jnp.concatenate([vb] * B_HEADS, axis=0), 0.0)
            st = st_scr[b]
            o = _dot(att[b].astype(BF16), vbd) + _dot_nt((q[b] * jnp.exp(g[b])).astype(BF16), st.astype(BF16))
            kh = (k[b] * jnp.exp(g_end - g[b])).astype(BF16)
            upd = _dot_tn(vb, kh)
            st_scr[b] = st * jnp.exp(g_end) + jnp.where(st_mask, upd, 0.0)
            on = o * lax.rsqrt(_seg_mean_sq(o, mseg) + LN_EPS)
            ob_ref[b, rows, :] = (on * gg * _silu(r_ref[b, rows, :])).astype(BF16)
        return carry

    lax.fori_loop(0, n_chunks, body, 0)

    @pl.when(pl.program_id(1) == pl.num_programs(1) - 1)
    def _():
        for b in range(nseq):
            st_ref[b] = st_scr[b]


GLA_SEQS = 4
GLA_ROWS = 512


def _gla_prompt(q, k, g, v, r, s0, gg, nbatch):
    t = q.shape[0] // nbatch
    nb = min(GLA_SEQS, nbatch)
    tr = min(GLA_ROWS, t)
    seq = lambda a: a.reshape(nbatch, t, a.shape[-1])
    blk = lambda width: pl.BlockSpec((nb, tr, width), lambda b, j: (b, j, 0))
    state = pl.BlockSpec((nb, B_WIDTH, GK_WIDTH), lambda b, j: (b, 0, 0))
    return pl.pallas_call(
        functools.partial(_gla_kernel, n_chunks=tr // GLA_CHUNK),
        grid=(nbatch // nb, t // tr),
        in_specs=[blk(GK_WIDTH), blk(GK_WIDTH), blk(GK_WIDTH), blk(B_WIDTH), blk(B_WIDTH),
                  pl.BlockSpec((1, B_WIDTH, GK_WIDTH), lambda b, j: (0, 0, 0)),
                  pl.BlockSpec((1, B_WIDTH), lambda b, j: (0, 0))],
        out_specs=[blk(B_WIDTH), state],
        out_shape=[jax.ShapeDtypeStruct((nbatch, t, B_WIDTH), BF16),
                   jax.ShapeDtypeStruct((nbatch, B_WIDTH, GK_WIDTH), F32)],
        scratch_shapes=[pltpu.VMEM((nb, B_WIDTH, GK_WIDTH), F32)],
        compiler_params=_params("parallel", "arbitrary"),
        name="gla_prompt",
    )(seq(q), seq(k), seq(g), seq(v), seq(r), s0, gg)


def _gla_sample_kernel(q_ref, k_ref, la_ref, v_ref, r_ref, s_ref, gg_ref, ob_ref, sn_ref):
    qt = q_ref[...].T
    kt = k_ref[...].T
    at = jnp.exp(la_ref[...]).T
    vt = v_ref[...].T
    rt = r_ref[...].T
    gg = gg_ref[...]
    outs = []
    for h in range(B_HEADS):
        vh = vt[h * B_DV:(h + 1) * B_DV, :]
        o = jnp.zeros_like(vh)
        for d in range(B_DK):
            i = h * B_DK + d
            s_new = at[i:i + 1, :] * s_ref[i] + kt[i:i + 1, :] * vh
            sn_ref[i] = s_new
            o = o + qt[i:i + 1, :] * s_new
        on = o * lax.rsqrt(jnp.mean(o * o, axis=0, keepdims=True) + LN_EPS)
        outs.append(on * gg * _silu(rt[h * B_DV:(h + 1) * B_DV, :]))
    ob_ref[...] = jnp.concatenate(outs, axis=0).T.astype(BF16)


def _gla_sample(q, k, la, v, r, state_t, gg_col, layer):
    ns = state_t.shape[3]
    row = lambda width: pl.BlockSpec((ns, width), lambda i: (BLOCK // ns, 0))
    return pl.pallas_call(
        _gla_sample_kernel,
        grid=(1,),
        in_specs=[row(GK_WIDTH), row(GK_WIDTH), row(GK_WIDTH), row(B_WIDTH), row(B_WIDTH),
                  pl.BlockSpec((None, GK_WIDTH, B_DV, ns), lambda i: (layer, 0, 0, 0)),
                  pl.BlockSpec((B_DV, 1), lambda i: (0, 0))],
        out_specs=[pl.BlockSpec((ns, B_WIDTH), lambda i: (0, 0)),
                   pl.BlockSpec((GK_WIDTH, B_DV, ns), lambda i: (0, 0, 0))],
        out_shape=[jax.ShapeDtypeStruct((ns, B_WIDTH), BF16),
                   jax.ShapeDtypeStruct((GK_WIDTH, B_DV, ns), F32)],
        compiler_params=_params("arbitrary"),
        name="gla_sample",
    )(q, k, la, v, r, state_t, gg_col)


CONV_HIST = 32
CONV_CHUNK = 64
CONV_GROUP = 2


def _conv_post(y, dwb, lng, lnb, pww, pwb):
    y = _silu(_ln_rows(y + dwb, lng, lnb))
    return _dot(y.astype(BF16), pww) + pwb


def _conv_kernel(u_ref, hist_ref, dww_ref, dwb_ref, lng_ref, lnb_ref, pww_ref, pwb_ref,
                 oc_ref, tail_ref, ubuf, *, n_chunks):
    t = u_ref.shape[1]
    ubuf[0, 0:CONV_HIST, :] = hist_ref[...]
    ubuf[0, CONV_HIST:CONV_HIST + t, :] = u_ref[0]
    off = CONV_HIST - (CONV_W - 1)
    n_copy = t + CONV_HIST - SUBLANES
    for s in range(1, SUBLANES):
        ubuf[s, 0:n_copy, :] = ubuf[0, s:s + n_copy, :]

    group = CONV_GROUP if n_chunks % CONV_GROUP == 0 else 1

    def body(c, carry):
        starts = [pl.multiple_of((c * group + i) * CONV_CHUNK, CONV_CHUNK) for i in range(group)]
        accs = [jnp.zeros((CONV_CHUNK, C_WIDTH), F32) for _ in starts]
        for j in range(CONV_W):
            a, s = divmod(j + off, SUBLANES)
            w_j = dww_ref[j:j + 1, :]
            for i, r0 in enumerate(starts):
                rows = pl.ds(pl.multiple_of(r0 + a * SUBLANES, SUBLANES), CONV_CHUNK)
                accs[i] = accs[i] + ubuf[s, rows, :] * w_j
        for r0, acc in zip(starts, accs):
            oc = _conv_post(acc, dwb_ref[...], lng_ref[...], lnb_ref[...], pww_ref[...], pwb_ref[...])
            oc_ref[0, pl.ds(r0, CONV_CHUNK), :] = oc.astype(BF16)
        return carry

    lax.fori_loop(0, n_chunks // group, body, 0)
    tail_ref[0] = ubuf[0, t:t + CONV_HIST, :]


def _conv_prompt(u, hist, dww, dwb, lng, lnb, pww, pwb, nbatch):
    t = u.shape[0] // nbatch
    const = lambda shape: pl.BlockSpec(shape, lambda b: (0,) * len(shape))
    return pl.pallas_call(
        functools.partial(_conv_kernel, n_chunks=t // CONV_CHUNK),
        grid=(nbatch,),
        in_specs=[pl.BlockSpec((1, t, C_WIDTH), lambda b: (b, 0, 0)), const((CONV_HIST, C_WIDTH)),
                  const((CONV_HIST, C_WIDTH)), const((1, C_WIDTH)), const((1, C_WIDTH)), const((1, C_WIDTH)),
                  const((C_WIDTH, C_WIDTH)), const((1, C_WIDTH))],
        out_specs=[pl.BlockSpec((1, t, C_WIDTH), lambda b: (b, 0, 0)),
                   pl.BlockSpec((1, CONV_HIST, C_WIDTH), lambda b: (b, 0, 0))],
        out_shape=[jax.ShapeDtypeStruct((nbatch, t, C_WIDTH), BF16),
                   jax.ShapeDtypeStruct((nbatch, CONV_HIST, C_WIDTH), F32)],
        scratch_shapes=[pltpu.VMEM((SUBLANES, CONV_HIST + t, C_WIDTH), F32)],
        compiler_params=_params("parallel"),
        name="conv_prompt",
    )(u.reshape(nbatch, t, C_WIDTH), hist, dww, dwb, lng, lnb, pww, pwb)


def _conv_sample_kernel(hist_ref, u_ref, dww_ref, dwb_ref, lng_ref, lnb_ref, pww_ref, pwb_ref, oc_ref, tail_ref):
    nh = CONV_W - 1
    u = u_ref[...]
    y = u * dww_ref[nh:nh + 1, :]
    for j in range(nh):
        y = y + hist_ref[j] * dww_ref[j:j + 1, :]
    oc = _conv_post(y, dwb_ref[...], lng_ref[...], lnb_ref[...], pww_ref[...], pwb_ref[...])
    oc_ref[...] = oc.astype(BF16)
    for j in range(nh - 1):
        tail_ref[j] = hist_ref[j + 1]
    tail_ref[nh - 1] = u


def _conv_sample(hist_t, u, dww, dwb, lng, lnb, pww, pwb, layer):
    ns = hist_t.shape[2]
    nh = CONV_W - 1
    const = lambda shape: pl.BlockSpec(shape, lambda i: (0,) * len(shape))
    return pl.pallas_call(
        _conv_sample_kernel,
        grid=(1,),
        in_specs=[pl.BlockSpec((None, nh, ns, C_WIDTH), lambda i: (layer, 0, 0, 0)),
                  pl.BlockSpec((ns, C_WIDTH), lambda i: (BLOCK // ns, 0)),
                  const((CONV_HIST, C_WIDTH)), const((1, C_WIDTH)), const((1, C_WIDTH)), const((1, C_WIDTH)),
                  const((C_WIDTH, C_WIDTH)), const((1, C_WIDTH))],
        out_specs=[const((ns, C_WIDTH)), const((nh, ns, C_WIDTH))],
        out_shape=[jax.ShapeDtypeStruct((ns, C_WIDTH), BF16),
                   jax.ShapeDtypeStruct((nh, ns, C_WIDTH), F32)],
        compiler_params=_params("arbitrary"),
        name="conv_sample",
    )(hist_t, u, dww, dwb, lng, lnb, pww, pwb)


ROUTE_LANE0 = N_GROUPS


def _route(x, wr2, br):
    x_hi = x.astype(BF16)
    x_lo = (x - x_hi.astype(F32)).astype(BF16)
    l_hi = _dot(x_hi, wr2)
    logits = l_hi[:, :LANES] + l_hi[:, LANES:] + _dot(x_lo, wr2)[:, :LANES] + br
    lane = lax.broadcasted_iota(jnp.int32, logits.shape, 1).astype(F32)
    far = 1e3
    glm = jnp.where(lane < N_GROUPS, logits, NEG)
    gmax = jnp.max(glm, axis=1, keepdims=True)
    gi = jnp.min(jnp.where(glm == gmax, lane, far), axis=1, keepdims=True)
    p_grp = 1.0 / jnp.sum(jnp.exp(glm - gmax), axis=1, keepdims=True)
    lo = ROUTE_LANE0 + EXP_PER_GROUP * gi
    in_sel = (lane >= lo) & (lane < lo + EXP_PER_GROUP)
    elm = jnp.where(in_sel, logits, NEG)
    v1 = jnp.max(elm, axis=1, keepdims=True)
    i1 = jnp.min(jnp.where(elm == v1, lane, far), axis=1, keepdims=True)
    elm2 = jnp.where(lane == i1, NEG, elm)
    v2 = jnp.max(elm2, axis=1, keepdims=True)
    i2 = jnp.min(jnp.where((elm2 == v2) & in_sel & (lane != i1), lane, far), axis=1, keepdims=True)
    t = jnp.exp(v2 - v1)
    w1 = p_grp / (1.0 + t)
    w2 = w1 * t
    return jnp.where(lane == i1, w1, 0.0) + jnp.where(lane == i2, w2, 0.0), gi


MOE_ROWS = 128
MOE_ALIGN = 16
GROUP_WIDTH = EXP_PER_GROUP * D_EXPERT


def _moe_sorted_rows(tm):
    need = tm + N_GROUPS * MOE_ALIGN + MOE_ROWS
    return -(-need // LANES) * LANES


def _ffn_kernel(*refs, pre_ln):
    if pre_ln:
        lg_ref, lb_ref, *refs = refs
    (oa_ref, ob_ref, oc_ref, h_ref, wo_ref, g1_ref, b1_ref, wr_ref, br_ref, wg_ref, wu_ref, wd_ref,
     g_ref, b_ref, o_ref, xs_ref, ys_ref, ws_ref) = refs
    h = h_ref[...]
    if pre_ln:
        h = _ln_rows(h, lg_ref[...], lb_ref[...])
    mix = (_dot(oa_ref[...], wo_ref[0:A_WIDTH, :])
           + _dot(ob_ref[...], wo_ref[A_WIDTH:A_WIDTH + B_WIDTH, :])
           + _dot(oc_ref[...], wo_ref[A_WIDTH + B_WIDTH:D_MODEL, :]))
    x = _ln_rows(ALPHA * h + mix, g1_ref[...], b1_ref[...])
    tm = x.shape[0]
    ns = xs_ref.shape[0]
    iota = lambda shape, axis: lax.broadcasted_iota(jnp.int32, shape, axis)
    dw, gi = _route(x, wr_ref[...], br_ref[...])
    lane_f = iota((tm, LANES), 1).astype(F32)
    onehot = jnp.where(lane_f == gi, 1.0, 0.0)
    tri = jnp.where(iota((tm, tm), 1) <= iota((tm, tm), 0), 1.0, 0.0).astype(BF16)
    cum = _dot(tri, onehot.astype(BF16))
    rank = jnp.sum(onehot * (cum - 1.0), axis=1, keepdims=True)
    counts = cum[tm - 1:tm, :]

    starts, tiles = [], []
    start = jnp.int32(0)
    for grp in range(N_GROUPS):
        n_g = counts[0, grp].astype(jnp.int32)
        starts.append(start)
        tiles.append((n_g + (MOE_ROWS - 1)) >> (MOE_ROWS.bit_length() - 1))
        start = start + ((n_g + (MOE_ALIGN - 1)) & -MOE_ALIGN)
    lane1 = iota((1, LANES), 1)
    start_v = jnp.zeros((1, LANES), F32)
    for grp in range(N_GROUPS):
        start_v = jnp.where(lane1 == grp, starts[grp].astype(F32), start_v)
    pos = jnp.sum(onehot * start_v, axis=1, keepdims=True) + rank

    pos_i = pos.astype(jnp.int32)
    digits = jnp.where(lane_f == 0.0, (pos_i >> 5).astype(F32),
                       jnp.where(lane_f == 1.0, (pos_i & 31).astype(F32), 0.0)).astype(BF16)
    lane8 = iota((8, LANES), 1)
    radix = jnp.where(lane8 == 0, 32.0, jnp.where(lane8 == 1, 1.0, 0.0)).astype(BF16)
    pos_row = _dot_nt(radix, digits)[0:1, :]

    perm = jnp.where(iota((ns, tm), 0).astype(F32) == pos_row, 1.0, 0.0).astype(BF16)
    dw_hi = dw.astype(BF16)
    dw_lo = (dw - dw_hi.astype(F32)).astype(BF16)
    srt = _dot(perm, jnp.concatenate([x.astype(BF16), dw_hi, dw_lo], axis=1))
    xs_ref[...] = srt[:, :D_MODEL].astype(BF16)
    ws_ref[...] = srt[:, D_MODEL:D_MODEL + LANES] + srt[:, D_MODEL + LANES:]
    ys_ref[...] = jnp.zeros_like(ys_ref)

    for grp in range(N_GROUPS):
        def body(k, carry, grp=grp):
            r0 = pl.multiple_of(starts[grp] + k * MOE_ROWS, MOE_ALIGN)
            rows = pl.ds(r0, MOE_ROWS)
            xt = xs_ref[rows, :]
            w = ws_ref[rows, :]
            hes = []
            for e in range(EXP_PER_GROUP):
                ex = grp * EXP_PER_GROUP + e
                w_e = w[:, ROUTE_LANE0 + ex:ROUTE_LANE0 + ex + 1]
                hes.append((_silu(_dot(xt, wg_ref[ex])) * _dot(xt, wu_ref[ex]) * w_e).astype(BF16))
            ys_ref[rows, :] = _dot(jnp.concatenate(hes, axis=1), wd_ref[grp]).astype(BF16)
            return carry
        lax.fori_loop(0, tiles[grp], body, 0)

    unperm = jnp.where(iota((tm, ns), 1).astype(F32) == pos, 1.0, 0.0).astype(BF16)
    y = _dot(unperm, ys_ref[...])
    o_ref[...] = _ln_rows(ALPHA * x + y, g_ref[...], b_ref[...])


def _ffn(oa, ob, oc, h, wo, g1, b1, wr2, br, wg, wu, wd, g2, b2, tm, layer, ln=None):
    n = h.shape[0]
    ns = _moe_sorted_rows(tm)
    row = lambda width: pl.BlockSpec((tm, width), lambda i: (i, 0))
    const = lambda shape: pl.BlockSpec(shape, lambda i: (0,) * len(shape))
    resident = lambda shape: pl.BlockSpec((None,) + shape, lambda i: (layer,) + (0,) * len(shape),
                                          pipeline_mode=pl.Buffered(1))
    ln_specs = [const((1, D_MODEL)), const((1, D_MODEL))] if ln else []
    return pl.pallas_call(
        functools.partial(_ffn_kernel, pre_ln=bool(ln)),
        grid=(n // tm,),
        in_specs=ln_specs + [row(A_WIDTH), row(B_WIDTH), row(C_WIDTH), row(D_MODEL),
                             resident((D_MODEL, D_MODEL)), const((1, D_MODEL)), const((1, D_MODEL)),
                             resident((D_MODEL, 2 * LANES)), const((1, LANES)),
                             resident((N_EXPERTS, D_MODEL, D_EXPERT)), resident((N_EXPERTS, D_MODEL, D_EXPERT)),
                             resident((N_GROUPS, GROUP_WIDTH, D_MODEL)),
                             const((1, D_MODEL)), const((1, D_MODEL))],
        out_specs=row(D_MODEL),
        out_shape=jax.ShapeDtypeStruct((n, D_MODEL), F32),
        scratch_shapes=[pltpu.VMEM((ns, D_MODEL), BF16), pltpu.VMEM((ns, D_MODEL), BF16),
                        pltpu.VMEM((ns, LANES), F32)],
        compiler_params=_params("parallel"),
        name="ffn",
    )(*(ln or ()), oa, ob, oc, h, wo, g1, b1, wr2, br, wg, wu, wd, g2, b2)


def _rope_tables(pos):
    half = HEAD_DIM // 2
    inv = ROPE_THETA ** (-jnp.arange(half, dtype=F32) / half)
    ang = pos.astype(F32)[:, None] * inv[None, :]
    cos, sin = jnp.cos(ang), jnp.sin(ang)
    cos_t = jnp.concatenate([cos, cos] * (LANES // HEAD_DIM), axis=1)
    sin_t = jnp.concatenate([-sin, sin] * (LANES // HEAD_DIM), axis=1)
    return cos_t, sin_t


def _row(v):
    return v.reshape(1, -1)


def kernel(x_prompt, x_sample, cache_meta_k, cache_meta_v, cache_win_k, cache_win_v, state_gla, state_conv,
           meta_tokens, ln_in_g, ln_in_b, w_in, attn_sink, w_alpha, b_alpha, gla_norm_g,
           conv_dw_w, conv_dw_b, conv_ln_g, conv_ln_b, conv_pw_w, conv_pw_b, w_out, ln1_g, ln1_b,
           w_router_group, b_router_group, w_router_expert, b_router_expert,
           w_exp_gate, w_exp_up, w_exp_down, ln2_g, ln2_b):
    nb, seq, d = x_prompt.shape
    ns = x_sample.shape[0]
    nwin = cache_win_k.shape[2]
    n_big = nb * seq
    n_small = BLOCK + ns
    tm_big = min(512, seq)

    small_in = jnp.concatenate([jnp.zeros((META_PAD, d), F32), meta_tokens.astype(F32),
                                x_sample.reshape(ns, d)], axis=0)
    hb = x_prompt.reshape(n_big, d)
    hs = small_in
    ln_in = (_row(ln_in_g), _row(ln_in_b))

    cos_b, sin_b = _rope_tables(N_META + jnp.arange(seq))
    pos_small = jnp.concatenate([jnp.maximum(jnp.arange(BLOCK) - META_PAD, 0),
                                 jnp.full((ns,), PAST_LEN, jnp.int32)])
    cos_s, sin_s = _rope_tables(pos_small)

    outs = [[] for _ in range(12)]
    zeros_hist = jnp.zeros((CONV_HIST, C_WIDTH), F32)
    zeros_state = jnp.zeros((1, B_WIDTH, GK_WIDTH), F32)

    nl = w_in.shape[0]
    w_pad = jnp.concatenate([w_in[:, :, :C_AB + B_RANK], jnp.zeros((nl, d, LANES - B_RANK), F32),
                             w_in[:, :, C_AB + B_RANK:]], axis=2).astype(BF16)
    wa_all = jnp.concatenate([w_alpha, jnp.zeros((nl, LANES - B_RANK, GK_WIDTH), F32)], axis=1).astype(BF16)
    ba_all = b_alpha[:, None, :]
    wo_all = w_out.astype(BF16)
    wr = jnp.concatenate([w_router_group, w_router_expert,
                          jnp.zeros((nl, d, LANES - N_GROUPS - N_EXPERTS), F32)], axis=2)
    wr_hi = wr.astype(BF16)
    wr2_all = jnp.concatenate([wr_hi, (wr - wr_hi.astype(F32)).astype(BF16)], axis=2)
    wg_all, wu_all = w_exp_gate.astype(BF16), w_exp_up.astype(BF16)
    wd_all = w_exp_down.astype(BF16).reshape(nl, N_GROUPS, GROUP_WIDTH, d)
    win_k_t = cache_win_k.transpose(0, 1, 3, 4, 2).reshape(nl, ns, KV_WIDTH, nwin)
    win_v_t = cache_win_v.transpose(0, 1, 3, 4, 2).reshape(nl, ns, KV_WIDTH, nwin)
    gla_t = state_gla.transpose(0, 2, 3, 4, 1).reshape(nl, GK_WIDTH, B_DV, ns)
    conv_t = state_conv.transpose(0, 2, 1, 3)

    for l in range(DEPTH):
        sink = attn_sink[l].astype(F32)
        sink_col = jnp.concatenate([sink, jnp.zeros((HEAD_ROWS - A_HEADS,), F32)])[:, None]
        gg_t = _row(jnp.tile(gla_norm_g[l], B_HEADS))
        gg_col = gla_norm_g[l][:, None]
        dww = jnp.concatenate([conv_dw_w[l], jnp.zeros((CONV_HIST - CONV_W, C_WIDTH), F32)], axis=0)
        dwb, clg, clb = _row(conv_dw_b[l]), _row(conv_ln_g[l]), _row(conv_ln_b[l])
        pww, pwb = conv_pw_w[l].astype(BF16), _row(conv_pw_b[l])
        br = _row(jnp.concatenate([b_router_group[l], b_router_expert[l],
                                   jnp.zeros((LANES - N_GROUPS - N_EXPERTS,), F32)]))

        ln = ln_in if l == 0 else None
        qa_b, ka_b, va_b, qg_b, kg_b, la_b, vg_b, rg_b, u_b = _proj_in(
            hb, w_pad, wa_all, ba_all, cos_b, sin_b, layer=l, tm=tm_big, n_pad=0, n_seq=tm_big, ln=ln)
        qa_s, ka_s, va_s, qg_s, kg_s, la_s, vg_s, rg_s, u_s = _proj_in(
            hs, w_pad, wa_all, ba_all, cos_s, sin_s, layer=l, tm=n_small, n_pad=META_PAD, n_seq=BLOCK, ln=ln)

        oa_b = _attn_prompt(sink, qa_b, ka_b, va_b, ka_s, va_s, nb)
        oa_m = _attn_meta(sink, qa_s, ka_s, va_s)
        q_smp = qa_s[BLOCK:].reshape(ns, A_KV_HEADS, A_GROUP, 1, HEAD_DIM)
        eye = jnp.eye(A_KV_HEADS, dtype=BF16)[None, :, None, :, None]
        q_bd = (q_smp * eye).reshape(ns, A_HEADS, KV_WIDTH)
        q_bd = jnp.pad(q_bd, ((0, 0), (0, HEAD_ROWS - A_HEADS), (0, 0)))
        o_bd, nwk_t, nwv_t = _attn_sample(
            sink_col, q_bd, ka_s, va_s,
            cache_meta_k[l].reshape(ns, N_META, KV_WIDTH), cache_meta_v[l].reshape(ns, N_META, KV_WIDTH),
            win_k_t, win_v_t, l)
        o_bd = o_bd[:, :A_HEADS].reshape(ns, A_KV_HEADS, A_GROUP, A_KV_HEADS, HEAD_DIM)
        oa_smp = jnp.stack([o_bd[:, c, :, c, :] for c in range(A_KV_HEADS)], axis=1).reshape(ns, A_WIDTH)
        oa_s = jnp.concatenate([oa_m, oa_smp.astype(BF16)], axis=0)

        ob_m, st_m = _gla_prompt(qg_s[:BLOCK], kg_s[:BLOCK], la_s[:BLOCK], vg_s[:BLOCK], rg_s[:BLOCK],
                                 zeros_state, gg_t, 1)
        ob_b, st_b = _gla_prompt(qg_b, kg_b, la_b, vg_b, rg_b, st_m, gg_t, nb)
        ob_smp, s_new_t = _gla_sample(qg_s, kg_s, la_s, vg_s, rg_s, gla_t, gg_col, l)
        ob_s = jnp.concatenate([ob_m.reshape(BLOCK, B_WIDTH), ob_smp], axis=0)
        st5 = st_b.reshape(nb, B_HEADS, B_DV, B_HEADS, B_DK)
        gla_p = jnp.stack([st5[:, h, :, h, :] for h in range(B_HEADS)], axis=1).transpose(0, 1, 3, 2)

        oc_m, _ = _conv_prompt(u_s[:BLOCK], zeros_hist, dww, dwb, clg, clb, pww, pwb, 1)
        oc_b, tail_b = _conv_prompt(u_b, u_s[BLOCK - CONV_HIST:BLOCK], dww, dwb, clg, clb, pww, pwb, nb)
        oc_smp, tail_s_t = _conv_sample(conv_t, u_s, dww, dwb, clg, clb, pww, pwb, l)
        oc_s = jnp.concatenate([oc_m.reshape(BLOCK, C_WIDTH), oc_smp], axis=0)

        l1g, l1b, l2g, l2b = _row(ln1_g[l]), _row(ln1_b[l]), _row(ln2_g[l]), _row(ln2_b[l])
        hb = _ffn(oa_b, ob_b.reshape(n_big, B_WIDTH), oc_b.reshape(n_big, C_WIDTH), hb, wo_all, l1g, l1b,
                  wr2_all, br, wg_all, wu_all, wd_all, l2g, l2b, tm_big, l, ln=ln)
        hs = _ffn(oa_s, ob_s, oc_s, hs, wo_all, l1g, l1b, wr2_all, br, wg_all, wu_all, wd_all, l2g, l2b,
                  n_small, l, ln=ln)

        kv4 = lambda a: a.reshape(a.shape[0], a.shape[1], A_KV_HEADS, HEAD_DIM)
        win_t = lambda a: a.reshape(ns, A_KV_HEADS, HEAD_DIM, nwin).transpose(0, 3, 1, 2)
        meta_k = jnp.broadcast_to(ka_s[META_PAD:BLOCK][None], (nb, N_META, KV_WIDTH))
        meta_v = jnp.broadcast_to(va_s[META_PAD:BLOCK][None], (nb, N_META, KV_WIDTH))
        win_k = ka_b.reshape(nb, seq, KV_WIDTH)[:, seq - nwin:]
        win_v = va_b.reshape(nb, seq, KV_WIDTH)[:, seq - nwin:]
        layer_out = (None, None, kv4(meta_k), kv4(meta_v), kv4(win_k), kv4(win_v), win_t(nwk_t), win_t(nwv_t),
                     gla_p, s_new_t.reshape(B_HEADS, B_DK, B_DV, ns).transpose(3, 0, 1, 2),
                     tail_b[:, CONV_HIST - (CONV_W - 1):], tail_s_t.transpose(1, 0, 2))
        for i in range(2, 12):
            outs[i].append(layer_out[i])

    y_prompt = hb.reshape(nb, seq, d)
    y_sample = hs[BLOCK:].reshape(ns, 1, d)
    return (y_prompt, y_sample) + tuple(jnp.stack(o) for o in outs[2:])
```

```python
import functools

import jax
import jax.numpy as jnp
from jax import lax
from jax.experimental import pallas as pl
from jax.experimental.pallas import tpu as pltpu

F32 = jnp.float32
BF16 = jnp.bfloat16

D_MODEL = 1024
DEPTH = 2
PAST_LEN = 16384
N_META = 16
HEAD_DIM = 64
A_WIDTH = 512
A_HEADS = 8
A_KV_HEADS = 2
A_GROUP = 4
WINDOW = 128
BLOCK = 128
ROPE_THETA = 10000.0
B_WIDTH = 256
B_HEADS = 4
B_DV = 64
B_DK = 32
B_RANK = 16
GATE_TAU = 16.0
GLA_CHUNK = 64
C_WIDTH = 256
CONV_W = 31
N_GROUPS = 4
EXP_PER_GROUP = 4
N_EXPERTS = 16
D_EXPERT = 256
ALPHA = (2 * DEPTH) ** 0.25
LN_EPS = 1e-5

LANES = 128
SUBLANES = 8
META_PAD = BLOCK - N_META
KV_WIDTH = A_KV_HEADS * HEAD_DIM
GK_WIDTH = B_HEADS * B_DK
C_QA = 0
C_KA = C_QA + A_WIDTH
C_VA = C_KA + KV_WIDTH
C_QB = C_VA + KV_WIDTH
C_KB = C_QB + GK_WIDTH
C_VB = C_KB + GK_WIDTH
C_RB = C_VB + B_WIDTH
C_AB = C_RB + B_WIDTH
C_CG = C_AB + LANES
PROJ_PAD_WIDTH = C_CG + 2 * C_WIDTH
NEG = -1e30
VMEM_LIMIT = 56 * 1024 * 1024


def _dot(a, b):
    return jnp.dot(a, b, preferred_element_type=F32)


def _dot_nt(a, b):
    return lax.dot_general(a, b, (((1,), (1,)), ((), ())), preferred_element_type=F32)


def _dot_tn(a, b):
    return lax.dot_general(a, b, (((0,), (0,)), ((), ())), preferred_element_type=F32)


def _ln_rows(x, g, b):
    xc = x - jnp.mean(x, -1, keepdims=True)
    var = jnp.mean(xc * xc, -1, keepdims=True)
    return xc * lax.rsqrt(var + LN_EPS) * g + b


def _silu(x):
    return x * jax.nn.sigmoid(x)


def _split3(x):
    hi = x.astype(BF16)
    r1 = x - hi.astype(F32)
    mid = r1.astype(BF16)
    lo = (r1 - mid.astype(F32)).astype(BF16)
    return hi, mid, lo


def _params(*sem):
    return pltpu.CompilerParams(dimension_semantics=sem, vmem_limit_bytes=VMEM_LIMIT)


def _proj_in_kernel(*refs, n_pad, n_seq, pre_ln):
    if pre_ln:
        lg_ref, lb_ref, *refs = refs
    (x_ref, w_ref, wa_ref, ba_ref, cos_ref, sin_ref,
     qa_ref, ka_ref, va_ref, qg_ref, kg_ref, la_ref, vg_ref, rg_ref, u_ref) = refs
    x = x_ref[...]
    if pre_ln:
        x = _ln_rows(x, lg_ref[...], lb_ref[...])
    xb = x.astype(BF16)
    tm = xb.shape[0]
    za = _dot(xb, w_ref[:, C_QA:C_QB])
    cos = cos_ref[...]
    sin = sin_ref[...]
    lane = lax.broadcasted_iota(jnp.int32, (tm, LANES), 1)
    first_half = (lane & (HEAD_DIM // 2)) == 0

    def rope(z):
        rot = jnp.where(first_half, pltpu.roll(z, LANES - HEAD_DIM // 2, 1), pltpu.roll(z, HEAD_DIM // 2, 1))
        return z * cos + rot * sin

    if n_pad:
        valid = (lax.broadcasted_iota(jnp.int32, (tm, 1), 0) >= n_pad).astype(F32)
    else:
        valid = None

    zb = _dot(xb, w_ref[:, C_QB:C_CG])
    zc = _dot(xb, w_ref[:, C_CG:PROJ_PAD_WIDTH])
    for c in range(A_WIDTH // LANES):
        zq = za[:, c * LANES:(c + 1) * LANES]
        qa_ref[:, c * LANES:(c + 1) * LANES] = (rope(zq) * (HEAD_DIM ** -0.5)).astype(BF16)
    ka_ref[...] = rope(za[:, C_KA:C_VA])
    va_ref[...] = za[:, C_VA:C_QB]

    o = C_QB
    qg_ref[...] = zb[:, C_QB - o:C_KB - o] * (B_DK ** -0.5)
    kg = zb[:, C_KB - o:C_VB - o]
    vg_ref[...] = zb[:, C_VB - o:C_RB - o]
    rg_ref[...] = zb[:, C_RB - o:C_AB - o]
    ab = zb[:, C_AB - o:C_CG - o].astype(BF16)
    xa = _dot(ab, wa_ref[...]) + ba_ref[...]
    la = (jnp.minimum(xa, 0.0) - jnp.log(1.0 + jnp.exp(-jnp.abs(xa)))) * (1.0 / GATE_TAU)

    u = zc[:, :C_WIDTH] * jax.nn.sigmoid(zc[:, C_WIDTH:])
    if valid is not None:
        kg = kg * valid
        la = la * valid
        u = u * valid
    kg_ref[...] = kg
    u_ref[...] = u
    cs = GLA_CHUNK
    tri = jnp.where(lax.broadcasted_iota(jnp.int32, (cs, cs), 1) <= lax.broadcasted_iota(jnp.int32, (cs, cs), 0),
                    1.0, 0.0).astype(BF16)
    la_h, la_m, la_l = _split3(la)
    for c in range(n_seq // cs):
        rows = slice(c * cs, (c + 1) * cs)
        la_ref[rows, :] = _dot(tri, la_h[rows]) + _dot(tri, la_m[rows]) + _dot(tri, la_l[rows])
    if n_seq < tm:
        la_ref[n_seq:tm, :] = la[n_seq:tm]


def _proj_in(h, w, wa, ba, cos, sin, *, layer, tm, n_pad, n_seq, ln=None):
    n = h.shape[0]
    tb = cos.shape[0] // tm
    row = lambda width: pl.BlockSpec((tm, width), lambda i: (i, 0))
    const = lambda shape: pl.BlockSpec(shape, lambda i: (0, 0))
    per_layer = lambda shape: pl.BlockSpec((None,) + shape, lambda i: (layer, 0, 0))
    tab = pl.BlockSpec((tm, LANES), lambda i: (i % tb, 0))
    widths = (A_WIDTH, KV_WIDTH, KV_WIDTH, GK_WIDTH, GK_WIDTH, GK_WIDTH, B_WIDTH, B_WIDTH, C_WIDTH)
    dtypes = (BF16,) + (F32,) * 8
    ln_specs = [const((1, D_MODEL)), const((1, D_MODEL))] if ln else []
    return pl.pallas_call(
        functools.partial(_proj_in_kernel, n_pad=n_pad, n_seq=n_seq, pre_ln=bool(ln)),
        grid=(n // tm,),
        in_specs=ln_specs + [row(D_MODEL), per_layer((D_MODEL, PROJ_PAD_WIDTH)), per_layer((LANES, GK_WIDTH)),
                             per_layer((1, GK_WIDTH)), tab, tab],
        out_specs=[row(wd) for wd in widths],
        out_shape=[jax.ShapeDtypeStruct((n, wd), dt) for wd, dt in zip(widths, dtypes)],
        compiler_params=_params("parallel"),
        name="proj_in",
    )(*(ln or ()), h, w, wa, ba, cos, sin)


def _attn_kernel(sink_ref, q_ref, km_ref, vm_ref, *rest, meta_mode):
    if meta_mode:
        (o_ref,) = rest
        blocks = [(0, q_ref[...], km_ref[...], vm_ref[...], None)]
    else:
        kp_ref, vp_ref, kc_ref, vc_ref, o_ref = rest
        km, vm = km_ref[...], vm_ref[...]
        blocks = []
        for sub in range(q_ref.shape[0] // BLOCK):
            cur = slice(sub * BLOCK, (sub + 1) * BLOCK)
            if sub == 0:
                kp, vp = kp_ref[...], vp_ref[...]
                has_prev = pl.program_id(1) >= 1
            else:
                prev = slice((sub - 1) * BLOCK, sub * BLOCK)
                kp, vp = kc_ref[prev, :], vc_ref[prev, :]
                has_prev = True
            blocks.append((sub * BLOCK, q_ref[cur, :], jnp.concatenate([kp, kc_ref[cur, :], km], axis=0),
                           jnp.concatenate([vp, vc_ref[cur, :], vm], axis=0), has_prev))

    units = []
    for row0, q, k_all, v_all, has_prev in blocks:
        nk = k_all.shape[0]
        ki = lax.broadcasted_iota(jnp.int32, (nk, BLOCK), 0)
        qi = lax.broadcasted_iota(jnp.int32, (nk, BLOCK), 1)
        if has_prev is None:
            ok = ki <= qi - META_PAD
        else:
            prev_lo = qi if has_prev is True else qi + jnp.where(has_prev, 0, BLOCK)
            ok = ((ki >= prev_lo) & (ki < BLOCK)) | ((ki >= BLOCK) & (ki <= qi + BLOCK)) | (ki >= 2 * BLOCK)
        bias = jnp.where(ok, 0.0, NEG)
        kb, vb = k_all.astype(BF16), v_all.astype(BF16)
        for kvh in range(A_KV_HEADS):
            lanes = slice(kvh * HEAD_DIM, (kvh + 1) * HEAD_DIM)
            qs = jnp.concatenate([q[:, (kvh * A_GROUP + g) * HEAD_DIM:(kvh * A_GROUP + g + 1) * HEAD_DIM]
                                  for g in range(A_GROUP)], axis=0)
            units.append((row0, kvh, bias, _dot_nt(kb[:, lanes], qs), vb[:, lanes]))

    outs = []
    for row0, kvh, bias, st, vv in units:
        ps, dens = [], []
        for g in range(A_GROUP):
            s = st[:, g * BLOCK:(g + 1) * BLOCK] + bias
            sink = sink_ref[kvh * A_GROUP + g]
            m = jnp.maximum(jnp.max(s, axis=0, keepdims=True), sink)
            p = jnp.exp(s - m)
            dens.append(jnp.sum(p, axis=0, keepdims=True) + jnp.exp(sink - m))
            ps.append(p.astype(BF16))
        outs.append((row0, kvh, _dot_tn(vv, jnp.concatenate(ps, axis=1)), jnp.concatenate(dens, axis=1)))

    for row0, kvh, ot, den in outs:
        ot = ot / den
        for pair in range(A_GROUP // 2):
            two = jnp.concatenate([ot[:, (2 * pair) * BLOCK:(2 * pair + 1) * BLOCK],
                                   ot[:, (2 * pair + 1) * BLOCK:(2 * pair + 2) * BLOCK]], axis=0)
            h0 = kvh * A_GROUP + 2 * pair
            o_ref[row0:row0 + BLOCK, h0 * HEAD_DIM:(h0 + 2) * HEAD_DIM] = two.T.astype(BF16)


ATTN_SUB = 4


def _attn_prompt(sink, q, k, v, k_small, v_small, nbatch):
    n = q.shape[0]
    nblk = n // BLOCK // nbatch
    sub = ATTN_SUB if nblk % ATTN_SUB == 0 else 1
    nstep = nblk // sub
    kvs = pl.BlockSpec((sub * BLOCK, KV_WIDTH), lambda b, j: (b * nstep + j, 0))
    kvp = pl.BlockSpec((BLOCK, KV_WIDTH), lambda b, j: (b * nblk + jnp.maximum(j * sub - 1, 0), 0))
    kvm = pl.BlockSpec((N_META, KV_WIDTH), lambda b, j: (META_PAD // N_META, 0))
    return pl.pallas_call(
        functools.partial(_attn_kernel, meta_mode=False),
        grid=(nbatch, nstep),
        in_specs=[pl.BlockSpec(memory_space=pltpu.SMEM),
                  pl.BlockSpec((sub * BLOCK, A_WIDTH), lambda b, j: (b * nstep + j, 0)),
                  kvm, kvm, kvp, kvp, kvs, kvs],
        out_specs=pl.BlockSpec((sub * BLOCK, A_WIDTH), lambda b, j: (b * nstep + j, 0)),
        out_shape=jax.ShapeDtypeStruct((n, A_WIDTH), BF16),
        compiler_params=_params("parallel", "parallel"),
        name="attn_prompt",
    )(sink, q, k_small, v_small, k, v, k, v)


def _attn_meta(sink, q_small, k_small, v_small):
    kvm = pl.BlockSpec((N_META, KV_WIDTH), lambda i: (META_PAD // N_META, 0))
    return pl.pallas_call(
        functools.partial(_attn_kernel, meta_mode=True),
        grid=(1,),
        in_specs=[pl.BlockSpec(memory_space=pltpu.SMEM),
                  pl.BlockSpec((BLOCK, A_WIDTH), lambda i: (0, 0)), kvm, kvm],
        out_specs=pl.BlockSpec((BLOCK, A_WIDTH), lambda i: (0, 0)),
        out_shape=jax.ShapeDtypeStruct((BLOCK, A_WIDTH), BF16),
        compiler_params=_params("arbitrary"),
        name="attn_meta",
    )(sink, q_small, k_small, v_small)


SAMPLE_BLOCK = 32
HEAD_ROWS = 16


def _attn_sample_kernel(sink_ref, q_ref, kn_ref, vn_ref, mk_ref, mv_ref, wk_ref, wv_ref,
                        o_ref, nwk_ref, nwv_ref):
    sb = q_ref.shape[0]
    q = q_ref[...]
    kn = kn_ref[...]
    vn = vn_ref[...]
    s_win = jnp.einsum('bhl,blj->bhj', q, wk_ref[...].astype(BF16), preferred_element_type=F32)
    s_meta = jnp.einsum('bhl,bml->bhm', q, mk_ref[...].astype(BF16), preferred_element_type=F32)
    s_self = jnp.sum(q.astype(F32) * kn[:, None, :], axis=2, keepdims=True)
    sink = sink_ref[...][None]
    m = jnp.maximum(jnp.maximum(jnp.max(s_win, axis=2, keepdims=True), jnp.max(s_meta, axis=2, keepdims=True)),
                    jnp.maximum(s_self, sink))
    p_win = jnp.exp(s_win - m)
    p_meta = jnp.exp(s_meta - m)
    p_self = jnp.exp(s_self - m)
    denom = (jnp.sum(p_win, axis=2, keepdims=True) + jnp.sum(p_meta, axis=2, keepdims=True) + p_self
             + jnp.exp(sink - m))
    o = (jnp.einsum('bhj,blj->bhl', p_win.astype(BF16), wv_ref[...].astype(BF16), preferred_element_type=F32)
         + jnp.einsum('bhm,bml->bhl', p_meta.astype(BF16), mv_ref[...].astype(BF16), preferred_element_type=F32)
         + p_self * vn[:, None, :])
    o_ref[...] = o / denom
    nwin = wk_ref.shape[2]
    last = lax.broadcasted_iota(jnp.int32, (KV_WIDTH, nwin), 1) == nwin - 1
    knt = kn.T
    vnt = vn.T
    for j in range(sb):
        nwk_ref[j] = jnp.where(last, knt[:, j:j + 1], pltpu.roll(wk_ref[j], nwin - 1, 1))
        nwv_ref[j] = jnp.where(last, vnt[:, j:j + 1], pltpu.roll(wv_ref[j], nwin - 1, 1))


def _attn_sample(sink_col, q_bd, k_small, v_small, meta_k, meta_v, win_k_t, win_v_t, layer):
    ns = q_bd.shape[0]
    sb = min(SAMPLE_BLOCK, ns)
    nwin = win_k_t.shape[3]
    blk = lambda rows: pl.BlockSpec((sb, rows, KV_WIDTH), lambda i: (i, 0, 0))
    new = pl.BlockSpec((sb, KV_WIDTH), lambda i: (BLOCK // sb + i, 0))
    win = pl.BlockSpec((None, sb, KV_WIDTH, nwin), lambda i: (layer, i, 0, 0))
    win_out = pl.BlockSpec((sb, KV_WIDTH, nwin), lambda i: (i, 0, 0))
    return pl.pallas_call(
        _attn_sample_kernel,
        grid=(ns // sb,),
        in_specs=[pl.BlockSpec((HEAD_ROWS, 1), lambda i: (0, 0)),
                  blk(HEAD_ROWS), new, new, blk(N_META), blk(N_META), win, win],
        out_specs=[blk(HEAD_ROWS), win_out, win_out],
        out_shape=[jax.ShapeDtypeStruct((ns, HEAD_ROWS, KV_WIDTH), F32),
                   jax.ShapeDtypeStruct((ns, KV_WIDTH, nwin), F32),
                   jax.ShapeDtypeStruct((ns, KV_WIDTH, nwin), F32)],
        compiler_params=_params("parallel"),
        name="attn_sample",
    )(sink_col, q_bd, k_small, v_small, meta_k, meta_v, win_k_t, win_v_t)


def _seg_mean_sq(o, mseg):
    sq = o * o
    hi = sq.astype(BF16)
    lo = (sq - hi.astype(F32)).astype(BF16)
    return _dot(hi, mseg) + _dot(lo, mseg)


def _gla_kernel(q_ref, k_ref, g_ref, v_ref, r_ref, s0_ref, gg_ref, ob_ref, st_ref, st_scr, *, n_chunks):
    cs = GLA_CHUNK
    nseq = q_ref.shape[0]

    @pl.when(pl.program_id(1) == 0)
    def _():
        for b in range(nseq):
            st_scr[b] = s0_ref[0]

    iota = lambda shape, axis: lax.broadcasted_iota(jnp.int32, shape, axis)
    lg_cs, lg_dk, lg_dv = cs.bit_length() - 1, B_DK.bit_length() - 1, B_DV.bit_length() - 1
    kd_mask = (iota((B_HEADS * cs, GK_WIDTH), 0) >> lg_cs) == (iota((B_HEADS * cs, GK_WIDTH), 1) >> lg_dk)
    vd_mask = (iota((B_HEADS * cs, B_WIDTH), 0) >> lg_cs) == (iota((B_HEADS * cs, B_WIDTH), 1) >> lg_dv)
    st_mask = (iota((B_WIDTH, GK_WIDTH), 0) >> lg_dv) == (iota((B_WIDTH, GK_WIDTH), 1) >> lg_dk)
    t_idx = iota((cs, B_HEADS * cs), 0)
    s_idx = iota((cs, B_HEADS * cs), 1) & (cs - 1)
    diff = t_idx ^ s_idx
    level = jnp.full((cs, B_HEADS * cs), lg_cs, jnp.int32)
    for bit in range(lg_cs):
        level = jnp.where(diff >= (1 << bit), bit, level)
    level = jnp.where(s_idx > t_idx, -1, level)
    odd_row = (iota((cs, GK_WIDTH), 0) & 1) == 1
    mseg =jnp.where((iota((B_WIDTH, B_WIDTH), 0) >> lg_dv) == (iota((B_WIDTH, B_WIDTH), 1) >> lg_dv),
                     1.0 / B_DV, 0.0).astype(BF16)
    gg = gg_ref[...]

    def scores(qt, kt):
        kbd = jnp.where(kd_mask, jnp.concatenate([kt.astype(BF16)] * B_HEADS, axis=0), 0.0)
        return _dot_nt(qt.astype(BF16), kbd)

    def level_ref(g, bit):
        if bit == 0:
            return jnp.where(odd_row, pltpu.roll(g, 1, 0), g)
        half = 1 << bit
        return jnp.concatenate([jnp.broadcast_to(g[p + half - 1:p + half, :], (2 * half, GK_WIDTH))
                                for p in range(0, cs, 2 * half)], axis=0)

    def body(c, carry):
        rows = pl.ds(pl.multiple_of(c * cs, cs), cs)
        seqs = range(nseq)
        q = [q_ref[b, rows, :] for b in seqs]
        k = [k_ref[b, rows, :] for b in seqs]
        g = [g_ref[b, rows, :] for b in seqs]
        att = [jnp.where(level == lg_cs, scores(q[b], k[b]), 0.0) for b in seqs]
        for bit in range(lg_cs):
            for b in seqs:
                decay = jnp.exp(-jnp.abs(g[b] - level_ref(g[b], bit)))
                att[b] = jnp.where(level == bit, scores(q[b] * decay, k[b] * decay), att[b])
        for b in seqs:
            g_end = g[b][cs - 1:cs, :]
            vb = v_ref[b, rows, :].astype(BF16)
            vbd = jnp.where(vd_mask, jnp.concatenate([vb] * B_HEADS, axis=0), 0.0)
            st = st_scr[b]
            o = _dot(att[b].astype(BF16), vbd) + _dot_nt((q[b] * jnp.exp(g[b])).astype(BF16), st.astype(BF16))
            kh = (k[b] * jnp.exp(g_end - g[b])).astype(BF16)
            upd = _dot_tn(vb, kh)
            st_scr[b] = st * jnp.exp(g_end) + jnp.where(st_mask, upd, 0.0)
            on = o * lax.rsqrt(_seg_mean_sq(o, mseg) + LN_EPS)
            ob_ref[b, rows, :] = (on * gg * _silu(r_ref[b, rows, :])).astype(BF16)
        return carry

    lax.fori_loop(0, n_chunks, body, 0)

    @pl.when(pl.program_id(1) == pl.num_programs(1) - 1)
    def _():
        for b in range(nseq):
            st_ref[b] = st_scr[b]


GLA_SEQS = 4
GLA_ROWS = 512


def _gla_prompt(q, k, g, v, r, s0, gg, nbatch):
    t = q.shape[0] // nbatch
    nb = min(GLA_SEQS, nbatch)
    tr = min(GLA_ROWS, t)
    seq = lambda a: a.reshape(nbatch, t, a.shape[-1])
    blk = lambda width: pl.BlockSpec((nb, tr, width), lambda b, j: (b, j, 0))
    state = pl.BlockSpec((nb, B_WIDTH, GK_WIDTH), lambda b, j: (b, 0, 0))
    return pl.pallas_call(
        functools.partial(_gla_kernel, n_chunks=tr // GLA_CHUNK),
        grid=(nbatch // nb, t // tr),
        in_specs=[blk(GK_WIDTH), blk(GK_WIDTH), blk(GK_WIDTH), blk(B_WIDTH), blk(B_WIDTH),
                  pl.BlockSpec((1, B_WIDTH, GK_WIDTH), lambda b, j: (0, 0, 0)),
                  pl.BlockSpec((1, B_WIDTH), lambda b, j: (0, 0))],
        out_specs=[blk(B_WIDTH), state],
        out_shape=[jax.ShapeDtypeStruct((nbatch, t, B_WIDTH), BF16),
                   jax.ShapeDtypeStruct((nbatch, B_WIDTH, GK_WIDTH), F32)],
        scratch_shapes=[pltpu.VMEM((nb, B_WIDTH, GK_WIDTH), F32)],
        compiler_params=_params("parallel", "arbitrary"),
        name="gla_prompt",
    )(seq(q), seq(k), seq(g), seq(v), seq(r), s0, gg)


def _gla_sample_kernel(q_ref, k_ref, la_ref, v_ref, r_ref, s_ref, gg_ref, ob_ref, sn_ref):
    qt = q_ref[...].T
    kt = k_ref[...].T
    at = jnp.exp(la_ref[...]).T
    vt = v_ref[...].T
    rt = r_ref[...].T
    gg = gg_ref[...]
    outs = []
    for h in range(B_HEADS):
        vh = vt[h * B_DV:(h + 1) * B_DV, :]
        o = jnp.zeros_like(vh)
        for d in range(B_DK):
            i = h * B_DK + d
            s_new = at[i:i + 1, :] * s_ref[i] + kt[i:i + 1, :] * vh
            sn_ref[i] = s_new
            o = o + qt[i:i + 1, :] * s_new
        on = o * lax.rsqrt(jnp.mean(o * o, axis=0, keepdims=True) + LN_EPS)
        outs.append(on * gg * _silu(rt[h * B_DV:(h + 1) * B_DV, :]))
    ob_ref[...] = jnp.concatenate(outs, axis=0).T.astype(BF16)


def _gla_sample(q, k, la, v, r, state_t, gg_col, layer):
    ns = state_t.shape[3]
    row = lambda width: pl.BlockSpec((ns, width), lambda i: (BLOCK // ns, 0))
    return pl.pallas_call(
        _gla_sample_kernel,
        grid=(1,),
        in_specs=[row(GK_WIDTH), row(GK_WIDTH), row(GK_WIDTH), row(B_WIDTH), row(B_WIDTH),
                  pl.BlockSpec((None, GK_WIDTH, B_DV, ns), lambda i: (layer, 0, 0, 0)),
                  pl.BlockSpec((B_DV, 1), lambda i: (0, 0))],
        out_specs=[pl.BlockSpec((ns, B_WIDTH), lambda i: (0, 0)),
                   pl.BlockSpec((GK_WIDTH, B_DV, ns), lambda i: (0, 0, 0))],
        out_shape=[jax.ShapeDtypeStruct((ns, B_WIDTH), BF16),
                   jax.ShapeDtypeStruct((GK_WIDTH, B_DV, ns), F32)],
        compiler_params=_params("arbitrary"),
        name="gla_sample",
    )(q, k, la, v, r, state_t, gg_col)


CONV_HIST = 32
CONV_CHUNK = 64
CONV_GROUP = 2


def _conv_post(y, dwb, lng, lnb, pww, pwb):
    y = _silu(_ln_rows(y + dwb, lng, lnb))
    return _dot(y.astype(BF16), pww) + pwb


def _conv_kernel(u_ref, hist_ref, dww_ref, dwb_ref, lng_ref, lnb_ref, pww_ref, pwb_ref,
                 oc_ref, tail_ref, ubuf, *, n_chunks):
    t = u_ref.shape[1]
    ubuf[0, 0:CONV_HIST, :] = hist_ref[...]
    ubuf[0, CONV_HIST:CONV_HIST + t, :] = u_ref[0]
    off = CONV_HIST - (CONV_W - 1)
    n_copy = t + CONV_HIST - SUBLANES
    for s in range(1, SUBLANES):
        ubuf[s, 0:n_copy, :] = ubuf[0, s:s + n_copy, :]

    group = CONV_GROUP if n_chunks % CONV_GROUP == 0 else 1

    def body(c, carry):
        starts = [pl.multiple_of((c * group + i) * CONV_CHUNK, CONV_CHUNK) for i in range(group)]
        accs = [jnp.zeros((CONV_CHUNK, C_WIDTH), F32) for _ in starts]
        for j in range(CONV_W):
            a, s = divmod(j + off, SUBLANES)
            w_j = dww_ref[j:j + 1, :]
            for i, r0 in enumerate(starts):
                rows = pl.ds(pl.multiple_of(r0 + a * SUBLANES, SUBLANES), CONV_CHUNK)
                accs[i] = accs[i] + ubuf[s, rows, :] * w_j
        for r0, acc in zip(starts, accs):
            oc = _conv_post(acc, dwb_ref[...], lng_ref[...], lnb_ref[...], pww_ref[...], pwb_ref[...])
            oc_ref[0, pl.ds(r0, CONV_CHUNK), :] = oc.astype(BF16)
        return carry

    lax.fori_loop(0, n_chunks // group, body, 0)
    tail_ref[0] = ubuf[0, t:t + CONV_HIST, :]


def _conv_prompt(u, hist, dww, dwb, lng, lnb, pww, pwb, nbatch):
    t = u.shape[0] // nbatch
    const = lambda shape: pl.BlockSpec(shape, lambda b: (0,) * len(shape))
    return pl.pallas_call(
        functools.partial(_conv_kernel, n_chunks=t // CONV_CHUNK),
        grid=(nbatch,),
        in_specs=[pl.BlockSpec((1, t, C_WIDTH), lambda b: (b, 0, 0)), const((CONV_HIST, C_WIDTH)),
                  const((CONV_HIST, C_WIDTH)), const((1, C_WIDTH)), const((1, C_WIDTH)), const((1, C_WIDTH)),
                  const((C_WIDTH, C_WIDTH)), const((1, C_WIDTH))],
        out_specs=[pl.BlockSpec((1, t, C_WIDTH), lambda b: (b, 0, 0)),
                   pl.BlockSpec((1, CONV_HIST, C_WIDTH), lambda b: (b, 0, 0))],
        out_shape=[jax.ShapeDtypeStruct((nbatch, t, C_WIDTH), BF16),
                   jax.ShapeDtypeStruct((nbatch, CONV_HIST, C_WIDTH), F32)],
        scratch_shapes=[pltpu.VMEM((SUBLANES, CONV_HIST + t, C_WIDTH), F32)],
        compiler_params=_params("parallel"),
        name="conv_prompt",
    )(u.reshape(nbatch, t, C_WIDTH), hist, dww, dwb, lng, lnb, pww, pwb)


def _conv_sample_kernel(hist_ref, u_ref, dww_ref, dwb_ref, lng_ref, lnb_ref, pww_ref, pwb_ref, oc_ref, tail_ref):
    nh = CONV_W - 1
    u = u_ref[...]
    y = u * dww_ref[nh:nh + 1, :]
    for j in range(nh):
        y = y + hist_ref[j] * dww_ref[j:j + 1, :]
    oc = _conv_post(y, dwb_ref[...], lng_ref[...], lnb_ref[...], pww_ref[...], pwb_ref[...])
    oc_ref[...] = oc.astype(BF16)
    for j in range(nh - 1):
        tail_ref[j] = hist_ref[j + 1]
    tail_ref[nh - 1] = u


def _conv_sample(hist_t, u, dww, dwb, lng, lnb, pww, pwb, layer):
    ns = hist_t.shape[2]
    nh = CONV_W - 1
    const = lambda shape: pl.BlockSpec(shape, lambda i: (0,) * len(shape))
    return pl.pallas_call(
        _conv_sample_kernel,
        grid=(1,),
        in_specs=[pl.BlockSpec((None, nh, ns, C_WIDTH), lambda i: (layer, 0, 0, 0)),
                  pl.BlockSpec((ns, C_WIDTH), lambda i: (BLOCK // ns, 0)),
                  const((CONV_HIST, C_WIDTH)), const((1, C_WIDTH)), const((1, C_WIDTH)), const((1, C_WIDTH)),
                  const((C_WIDTH, C_WIDTH)), const((1, C_WIDTH))],
        out_specs=[const((ns, C_WIDTH)), const((nh, ns, C_WIDTH))],
        out_shape=[jax.ShapeDtypeStruct((ns, C_WIDTH), BF16),
                   jax.ShapeDtypeStruct((nh, ns, C_WIDTH), F32)],
        compiler_params=_params("arbitrary"),
        name="conv_sample",
    )(hist_t, u, dww, dwb, lng, lnb, pww, pwb)


ROUTE_LANE0 = N_GROUPS


def _route(x, wr2, br):
    x_hi = x.astype(BF16)
    x_lo = (x - x_hi.astype(F32)).astype(BF16)
    l_hi = _dot(x_hi, wr2)
    logits = l_hi[:, :LANES] + l_hi[:, LANES:] + _dot(x_lo, wr2)[:, :LANES] + br
    lane = lax.broadcasted_iota(jnp.int32, logits.shape, 1).astype(F32)
    far = 1e3
    glm = jnp.where(lane < N_GROUPS, logits, NEG)
    gmax = jnp.max(glm, axis=1, keepdims=True)
    gi = jnp.min(jnp.where(glm == gmax, lane, far), axis=1, keepdims=True)
    p_grp = 1.0 / jnp.sum(jnp.exp(glm - gmax), axis=1, keepdims=True)
    lo = ROUTE_LANE0 + EXP_PER_GROUP * gi
    in_sel = (lane >= lo) & (lane < lo + EXP_PER_GROUP)
    elm = jnp.where(in_sel, logits, NEG)
    v1 = jnp.max(elm, axis=1, keepdims=True)
    i1 = jnp.min(jnp.where(elm == v1, lane, far), axis=1, keepdims=True)
    elm2 = jnp.where(lane == i1, NEG, elm)
    v2 = jnp.max(elm2, axis=1, keepdims=True)
    i2 = jnp.min(jnp.where((elm2 == v2) & in_sel & (lane != i1), lane, far), axis=1, keepdims=True)
    t = jnp.exp(v2 - v1)
    w1 = p_grp / (1.0 + t)
    w2 = w1 * t
    return jnp.where(lane == i1, w1, 0.0) + jnp.where(lane == i2, w2, 0.0), gi


FFN_CHUNK = 128
MOE_ROWS = 144
MOE_ALIGN = 16
GROUP_WIDTH = EXP_PER_GROUP * D_EXPERT


def _moe_sorted_rows(tm):
    need = tm + N_GROUPS * MOE_ALIGN + MOE_ROWS
    return -(-need // LANES) * LANES


def _ffn_kernel(*refs, pre_ln):
    if pre_ln:
        lg_ref, lb_ref, *refs = refs
    (oa_ref, ob_ref, oc_ref, h_ref, wo_ref, g1_ref, b1_ref, wr_ref, br_ref, wg_ref, wu_ref, wd_ref,
     g_ref, b_ref, o_ref, x_ref, xs_ref, ys_ref, ws_ref) = refs
    tm = h_ref.shape[0]
    ns = xs_ref.shape[0]
    iota = lambda shape, axis: lax.broadcasted_iota(jnp.int32, shape, axis)
    ch = FFN_CHUNK if tm % FFN_CHUNK == 0 else tm
    chunks = [slice(r, r + ch) for r in range(0, tm, ch)]
    mixes = [(_dot(oa_ref[r, :], wo_ref[0:A_WIDTH, :])
              + _dot(ob_ref[r, :], wo_ref[A_WIDTH:A_WIDTH + B_WIDTH, :])
              + _dot(oc_ref[r, :], wo_ref[A_WIDTH + B_WIDTH:D_MODEL, :])) for r in chunks]
    xs_rows = []
    for r, mix in zip(chunks, mixes):
        h = h_ref[r, :]
        if pre_ln:
            h = _ln_rows(h, lg_ref[...], lb_ref[...])
        x_r = _ln_rows(ALPHA * h + mix, g1_ref[...], b1_ref[...])
        x_ref[r, :] = x_r
        xs_rows.append(x_r)
    routed = [_route(x_r, wr_ref[...], br_ref[...]) for x_r in xs_rows]
    x = jnp.concatenate(xs_rows, axis=0)
    dw = jnp.concatenate([d for d, _ in routed], axis=0)
    gi = jnp.concatenate([i for _, i in routed], axis=0)
    lane_f = iota((tm, LANES), 1).astype(F32)
    onehot = jnp.where(lane_f == gi, 1.0, 0.0)
    tri = jnp.where(iota((tm, tm), 1) <= iota((tm, tm), 0), 1.0, 0.0).astype(BF16)
    cum = _dot(tri, onehot.astype(BF16))
    rank = jnp.sum(onehot * (cum - 1.0), axis=1, keepdims=True)
    counts = cum[tm - 1:tm, :]

    starts, tiles = [], []
    start = jnp.int32(0)
    for grp in range(N_GROUPS):
        n_g = counts[0, grp].astype(jnp.int32)
        starts.append(start)
        tiles.append(sum((n_g > k * MOE_ROWS).astype(jnp.int32) for k in range(-(-tm // MOE_ROWS))))
        start = start + ((n_g + (MOE_ALIGN - 1)) & -MOE_ALIGN)
    lane1 = iota((1, LANES), 1)
    start_v = jnp.zeros((1, LANES), F32)
    for grp in range(N_GROUPS):
        start_v = jnp.where(lane1 == grp, starts[grp].astype(F32), start_v)
    pos = jnp.sum(onehot * start_v, axis=1, keepdims=True) + rank

    pos_i = pos.astype(jnp.int32)
    digits = jnp.where(lane_f == 0.0, (pos_i >> 5).astype(F32),
                       jnp.where(lane_f == 1.0, (pos_i & 31).astype(F32), 0.0)).astype(BF16)
    lane8 = iota((8, LANES), 1)
    radix = jnp.where(lane8 == 0, 32.0, jnp.where(lane8 == 1, 1.0, 0.0)).astype(BF16)
    pos_row = _dot_nt(radix, digits)[0:1, :]

    perm = jnp.where(iota((ns, tm), 0).astype(F32) == pos_row, 1.0, 0.0).astype(BF16)
    dw_hi = dw.astype(BF16)
    dw_lo = (dw - dw_hi.astype(F32)).astype(BF16)
    srt = _dot(perm, jnp.concatenate([x.astype(BF16), dw_hi, dw_lo], axis=1))
    xs_ref[...] = srt[:, :D_MODEL].astype(BF16)
    ws_ref[...] = srt[:, D_MODEL:D_MODEL + LANES] + srt[:, D_MODEL + LANES:]
    ys_ref[...] = jnp.zeros_like(ys_ref)

    for grp in range(N_GROUPS):
        def body(k, carry, grp=grp):
            r0 = pl.multiple_of(starts[grp] + k * MOE_ROWS, MOE_ALIGN)
            rows = pl.ds(r0, MOE_ROWS)
            xt = xs_ref[rows, :]
            w = ws_ref[rows, :]
            hes = []
            for e in range(EXP_PER_GROUP):
                ex = grp * EXP_PER_GROUP + e
                w_e = w[:, ROUTE_LANE0 + ex:ROUTE_LANE0 + ex + 1]
                hes.append((_silu(_dot(xt, wg_ref[ex])) * _dot(xt, wu_ref[ex]) * w_e).astype(BF16))
            ys_ref[rows, :] = _dot(jnp.concatenate(hes, axis=1), wd_ref[grp]).astype(BF16)
            return carry
        lax.fori_loop(0, tiles[grp], body, 0)

    lane_ns = iota((ch, ns), 1).astype(F32)
    ys = ys_ref[...]
    y_rows = [_dot(jnp.where(lane_ns == pos[r], 1.0, 0.0).astype(BF16), ys) for r in chunks]
    for r, y in zip(chunks, y_rows):
        o_ref[r, :] = _ln_rows(ALPHA * x_ref[r, :] + y, g_ref[...], b_ref[...])


def _ffn(oa, ob, oc, h, wo, g1, b1, wr2, br, wg, wu, wd, g2, b2, tm, layer, ln=None):
    n = h.shape[0]
    ns = _moe_sorted_rows(tm)
    row = lambda width: pl.BlockSpec((tm, width), lambda i: (i, 0))
    const = lambda shape: pl.BlockSpec(shape, lambda i: (0,) * len(shape))
    resident = lambda shape: pl.BlockSpec((None,) + shape, lambda i: (layer,) + (0,) * len(shape),
                                          pipeline_mode=pl.Buffered(1))
    ln_specs = [const((1, D_MODEL)), const((1, D_MODEL))] if ln else []
    return pl.pallas_call(
        functools.partial(_ffn_kernel, pre_ln=bool(ln)),
        grid=(n // tm,),
        in_specs=ln_specs + [row(A_WIDTH), row(B_WIDTH), row(C_WIDTH), row(D_MODEL),
                             resident((D_MODEL, D_MODEL)), const((1, D_MODEL)), const((1, D_MODEL)),
                             resident((D_MODEL, 2 * LANES)), const((1, LANES)),
                             resident((N_EXPERTS, D_MODEL, D_EXPERT)), resident((N_EXPERTS, D_MODEL, D_EXPERT)),
                             resident((N_GROUPS, GROUP_WIDTH, D_MODEL)),
                             const((1, D_MODEL)), const((1, D_MODEL))],
        out_specs=row(D_MODEL),
        out_shape=jax.ShapeDtypeStruct((n, D_MODEL), F32),
        scratch_shapes=[pltpu.VMEM((tm, D_MODEL), F32), pltpu.VMEM((ns, D_MODEL), BF16),
                        pltpu.VMEM((ns, D_MODEL), BF16), pltpu.VMEM((ns, LANES), F32)],
        compiler_params=_params("parallel"),
        name="ffn",
    )(*(ln or ()), oa, ob, oc, h, wo, g1, b1, wr2, br, wg, wu, wd, g2, b2)


def _rope_tables(pos):
    half = HEAD_DIM // 2
    inv = ROPE_THETA ** (-jnp.arange(half, dtype=F32) / half)
    ang = pos.astype(F32)[:, None] * inv[None, :]
    cos, sin = jnp.cos(ang), jnp.sin(ang)
    cos_t = jnp.concatenate([cos, cos] * (LANES // HEAD_DIM), axis=1)
    sin_t = jnp.concatenate([-sin, sin] * (LANES // HEAD_DIM), axis=1)
    return cos_t, sin_t


def _row(v):
    return v.reshape(1, -1)


def kernel(x_prompt, x_sample, cache_meta_k, cache_meta_v, cache_win_k, cache_win_v, state_gla, state_conv,
           meta_tokens, ln_in_g, ln_in_b, w_in, attn_sink, w_alpha, b_alpha, gla_norm_g,
           conv_dw_w, conv_dw_b, conv_ln_g, conv_ln_b, conv_pw_w, conv_pw_b, w_out, ln1_g, ln1_b,
           w_router_group, b_router_group, w_router_expert, b_router_expert,
           w_exp_gate, w_exp_up, w_exp_down, ln2_g, ln2_b):
    nb, seq, d = x_prompt.shape
    ns = x_sample.shape[0]
    nwin = cache_win_k.shape[2]
    n_big = nb * seq
    n_small = BLOCK + ns
    tm_big = min(512, seq)

    small_in = jnp.concatenate([jnp.zeros((META_PAD, d), F32), meta_tokens.astype(F32),
                                x_sample.reshape(ns, d)], axis=0)
    hb = x_prompt.reshape(n_big, d)
    hs = small_in
    ln_in = (_row(ln_in_g), _row(ln_in_b))

    cos_b, sin_b = _rope_tables(N_META + jnp.arange(seq))
    pos_small = jnp.concatenate([jnp.maximum(jnp.arange(BLOCK) - META_PAD, 0),
                                 jnp.full((ns,), PAST_LEN, jnp.int32)])
    cos_s, sin_s = _rope_tables(pos_small)

    outs = [[] for _ in range(12)]
    zeros_hist = jnp.zeros((CONV_HIST, C_WIDTH), F32)
    zeros_state = jnp.zeros((1, B_WIDTH, GK_WIDTH), F32)

    nl = w_in.shape[0]
    w_pad = jnp.concatenate([w_in[:, :, :C_AB + B_RANK], jnp.zeros((nl, d, LANES - B_RANK), F32),
                             w_in[:, :, C_AB + B_RANK:]], axis=2).astype(BF16)
    wa_all = jnp.concatenate([w_alpha, jnp.zeros((nl, LANES - B_RANK, GK_WIDTH), F32)], axis=1).astype(BF16)
    ba_all = b_alpha[:, None, :]
    wo_all = w_out.astype(BF16)
    wr = jnp.concatenate([w_router_group, w_router_expert,
                          jnp.zeros((nl, d, LANES - N_GROUPS - N_EXPERTS), F32)], axis=2)
    wr_hi = wr.astype(BF16)
    wr2_all = jnp.concatenate([wr_hi, (wr - wr_hi.astype(F32)).astype(BF16)], axis=2)
    wg_all, wu_all = w_exp_gate.astype(BF16), w_exp_up.astype(BF16)
    wd_all = w_exp_down.astype(BF16).reshape(nl, N_GROUPS, GROUP_WIDTH, d)
    win_k_t = cache_win_k.transpose(0, 1, 3, 4, 2).reshape(nl, ns, KV_WIDTH, nwin)
    win_v_t = cache_win_v.transpose(0, 1, 3, 4, 2).reshape(nl, ns, KV_WIDTH, nwin)
    gla_t = state_gla.transpose(0, 2, 3, 4, 1).reshape(nl, GK_WIDTH, B_DV, ns)
    conv_t = state_conv.transpose(0, 2, 1, 3)

    for l in range(DEPTH):
        sink = attn_sink[l].astype(F32)
        sink_col = jnp.concatenate([sink, jnp.zeros((HEAD_ROWS - A_HEADS,), F32)])[:, None]
        gg_t = _row(jnp.tile(gla_norm_g[l], B_HEADS))
        gg_col = gla_norm_g[l][:, None]
        dww = jnp.concatenate([conv_dw_w[l], jnp.zeros((CONV_HIST - CONV_W, C_WIDTH), F32)], axis=0)
        dwb, clg, clb = _row(conv_dw_b[l]), _row(conv_ln_g[l]), _row(conv_ln_b[l])
        pww, pwb = conv_pw_w[l].astype(BF16), _row(conv_pw_b[l])
        br = _row(jnp.concatenate([b_router_group[l], b_router_expert[l],
                                   jnp.zeros((LANES - N_GROUPS - N_EXPERTS,), F32)]))

        ln = ln_in if l == 0 else None
        qa_b, ka_b, va_b, qg_b, kg_b, la_b, vg_b, rg_b, u_b = _proj_in(
            hb, w_pad, wa_all, ba_all, cos_b, sin_b, layer=l, tm=tm_big, n_pad=0, n_seq=tm_big, ln=ln)
        qa_s, ka_s, va_s, qg_s, kg_s, la_s, vg_s, rg_s, u_s = _proj_in(
            hs, w_pad, wa_all, ba_all, cos_s, sin_s, layer=l, tm=n_small, n_pad=META_PAD, n_seq=BLOCK, ln=ln)

        oa_b = _attn_prompt(sink, qa_b, ka_b, va_b, ka_s, va_s, nb)
        oa_m = _attn_meta(sink, qa_s, ka_s, va_s)
        q_smp = qa_s[BLOCK:].reshape(ns, A_KV_HEADS, A_GROUP, 1, HEAD_DIM)
        eye = jnp.eye(A_KV_HEADS, dtype=BF16)[None, :, None, :, None]
        q_bd = (q_smp * eye).reshape(ns, A_HEADS, KV_WIDTH)
        q_bd = jnp.pad(q_bd, ((0, 0), (0, HEAD_ROWS - A_HEADS), (0, 0)))
        o_bd, nwk_t, nwv_t = _attn_sample(
            sink_col, q_bd, ka_s, va_s,
            cache_meta_k[l].reshape(ns, N_META, KV_WIDTH), cache_meta_v[l].reshape(ns, N_META, KV_WIDTH),
            win_k_t, win_v_t, l)
        o_bd = o_bd[:, :A_HEADS].reshape(ns, A_KV_HEADS, A_GROUP, A_KV_HEADS, HEAD_DIM)
        oa_smp = jnp.stack([o_bd[:, c, :, c, :] for c in range(A_KV_HEADS)], axis=1).reshape(ns, A_WIDTH)
        oa_s = jnp.concatenate([oa_m, oa_smp.astype(BF16)], axis=0)

        ob_m, st_m = _gla_prompt(qg_s[:BLOCK], kg_s[:BLOCK], la_s[:BLOCK], vg_s[:BLOCK], rg_s[:BLOCK],
                                 zeros_state, gg_t, 1)
        ob_b, st_b = _gla_prompt(qg_b, kg_b, la_b, vg_b, rg_b, st_m, gg_t, nb)
        ob_smp, s_new_t = _gla_sample(qg_s, kg_s, la_s, vg_s, rg_s, gla_t, gg_col, l)
        ob_s = jnp.concatenate([ob_m.reshape(BLOCK, B_WIDTH), ob_smp], axis=0)
        st5 = st_b.reshape(nb, B_HEADS, B_DV, B_HEADS, B_DK)
        gla_p = jnp.stack([st5[:, h, :, h, :] for h in range(B_HEADS)], axis=1).transpose(0, 1, 3, 2)

        oc_m, _ = _conv_prompt(u_s[:BLOCK], zeros_hist, dww, dwb, clg, clb, pww, pwb, 1)
        oc_b, tail_b = _conv_prompt(u_b, u_s[BLOCK - CONV_HIST:BLOCK], dww, dwb, clg, clb, pww, pwb, nb)
        oc_smp, tail_s_t = _conv_sample(conv_t, u_s, dww, dwb, clg, clb, pww, pwb, l)
        oc_s = jnp.concatenate([oc_m.reshape(BLOCK, C_WIDTH), oc_smp], axis=0)

        l1g, l1b, l2g, l2b = _row(ln1_g[l]), _row(ln1_b[l]), _row(ln2_g[l]), _row(ln2_b[l])
        hb = _ffn(oa_b, ob_b.reshape(n_big, B_WIDTH), oc_b.reshape(n_big, C_WIDTH), hb, wo_all, l1g, l1b,
                  wr2_all, br, wg_all, wu_all, wd_all, l2g, l2b, tm_big, l, ln=ln)
        hs = _ffn(oa_s, ob_s, oc_s, hs, wo_all, l1g, l1b, wr2_all, br, wg_all, wu_all, wd_all, l2g, l2b,
                  n_small, l, ln=ln)

        kv4 = lambda a: a.reshape(a.shape[0], a.shape[1], A_KV_HEADS, HEAD_DIM)
        win_t = lambda a: a.reshape(ns, A_KV_HEADS, HEAD_DIM, nwin).transpose(0, 3, 1, 2)
        meta_k = jnp.broadcast_to(ka_s[META_PAD:BLOCK][None], (nb, N_META, KV_WIDTH))
        meta_v = jnp.broadcast_to(va_s[META_PAD:BLOCK][None], (nb, N_META, KV_WIDTH))
        win_k = ka_b.reshape(nb, seq, KV_WIDTH)[:, seq - nwin:]
        win_v = va_b.reshape(nb, seq, KV_WIDTH)[:, seq - nwin:]
        layer_out = (None, None, kv4(meta_k), kv4(meta_v), kv4(win_k), kv4(win_v), win_t(nwk_t), win_t(nwv_t),
                     gla_p, s_new_t.reshape(B_HEADS, B_DK, B_DV, ns).transpose(3, 0, 1, 2),
                     tail_b[:, CONV_HIST - (CONV_W - 1):], tail_s_t.transpose(1, 0, 2))
        for i in range(2, 12):
            outs[i].append(layer_out[i])

    y_prompt = hb.reshape(nb, seq, d)
    y_sample = hs[BLOCK:].reshape(ns, 1, d)
    return (y_prompt, y_sample) + tuple(jnp.stack(o) for o in outs[2:])
```

```python
import functools

import jax
import jax.numpy as jnp
from jax import lax
from jax.experimental import pallas as pl
from jax.experimental.pallas import tpu as pltpu

F32 = jnp.float32
BF16 = jnp.bfloat16

D_MODEL = 1024
DEPTH = 2
PAST_LEN = 16384
N_META = 16
HEAD_DIM = 64
A_WIDTH = 512
A_HEADS = 8
A_KV_HEADS = 2
A_GROUP = 4
WINDOW = 128
BLOCK = 128
ROPE_THETA = 10000.0
B_WIDTH = 256
B_HEADS = 4
B_DV = 64
B_DK = 32
B_RANK = 16
GATE_TAU = 16.0
GLA_CHUNK = 64
C_WIDTH = 256
CONV_W = 31
N_GROUPS = 4
EXP_PER_GROUP = 4
N_EXPERTS = 16
D_EXPERT = 256
ALPHA = (2 * DEPTH) ** 0.25
LN_EPS = 1e-5

LANES = 128
SUBLANES = 8
META_PAD = BLOCK - N_META
KV_WIDTH = A_KV_HEADS * HEAD_DIM
GK_WIDTH = B_HEADS * B_DK
C_QA = 0
C_KA = C_QA + A_WIDTH
C_VA = C_KA + KV_WIDTH
C_QB = C_VA + KV_WIDTH
C_KB = C_QB + GK_WIDTH
C_VB = C_KB + GK_WIDTH
C_RB = C_VB + B_WIDTH
C_AB = C_RB + B_WIDTH
C_CG = C_AB + LANES
PROJ_PAD_WIDTH = C_CG + 2 * C_WIDTH
NEG = -1e30
VMEM_LIMIT = 56 * 1024 * 1024


def _dot(a, b):
    return jnp.dot(a, b, preferred_element_type=F32)


def _dot_nt(a, b):
    return lax.dot_general(a, b, (((1,), (1,)), ((), ())), preferred_element_type=F32)


def _dot_tn(a, b):
    return lax.dot_general(a, b, (((0,), (0,)), ((), ())), preferred_element_type=F32)


def _ln_rows(x, g, b):
    xc = x - jnp.mean(x, -1, keepdims=True)
    var = jnp.mean(xc * xc, -1, keepdims=True)
    return xc * lax.rsqrt(var + LN_EPS) * g + b


def _silu(x):
    return x * jax.nn.sigmoid(x)


def _split3(x):
    hi = x.astype(BF16)
    r1 = x - hi.astype(F32)
    mid = r1.astype(BF16)
    lo = (r1 - mid.astype(F32)).astype(BF16)
    return hi, mid, lo


def _params(*sem):
    return pltpu.CompilerParams(dimension_semantics=sem, vmem_limit_bytes=VMEM_LIMIT)


def _proj_in_kernel(*refs, n_pad, n_seq, pre_ln):
    if pre_ln:
        lg_ref, lb_ref, *refs = refs
    (x_ref, w_ref, wa_ref, ba_ref, cos_ref, sin_ref,
     qa_ref, ka_ref, va_ref, qg_ref, kg_ref, la_ref, vg_ref, rg_ref, u_ref) = refs
    x = x_ref[...]
    if pre_ln:
        x = _ln_rows(x, lg_ref[...], lb_ref[...])
    xb = x.astype(BF16)
    tm = xb.shape[0]
    za = _dot(xb, w_ref[:, C_QA:C_QB])
    cos = cos_ref[...]
    sin = sin_ref[...]
    lane = lax.broadcasted_iota(jnp.int32, (tm, LANES), 1)
    first_half = (lane & (HEAD_DIM // 2)) == 0

    def rope(z):
        rot = jnp.where(first_half, pltpu.roll(z, LANES - HEAD_DIM // 2, 1), pltpu.roll(z, HEAD_DIM // 2, 1))
        return z * cos + rot * sin

    if n_pad:
        valid = (lax.broadcasted_iota(jnp.int32, (tm, 1), 0) >= n_pad).astype(F32)
    else:
        valid = None

    zb = _dot(xb, w_ref[:, C_QB:C_CG])
    zc = _dot(xb, w_ref[:, C_CG:PROJ_PAD_WIDTH])
    for c in range(A_WIDTH // LANES):
        zq = za[:, c * LANES:(c + 1) * LANES]
        qa_ref[:, c * LANES:(c + 1) * LANES] = (rope(zq) * (HEAD_DIM ** -0.5)).astype(BF16)
    ka_ref[...] = rope(za[:, C_KA:C_VA])
    va_ref[...] = za[:, C_VA:C_QB]

    o = C_QB
    qg_ref[...] = zb[:, C_QB - o:C_KB - o] * (B_DK ** -0.5)
    kg = zb[:, C_KB - o:C_VB - o]
    vg_ref[...] = zb[:, C_VB - o:C_RB - o]
    rg_ref[...] = zb[:, C_RB - o:C_AB - o]
    ab = zb[:, C_AB - o:C_CG - o].astype(BF16)
    xa = _dot(ab, wa_ref[...]) + ba_ref[...]
    la = (jnp.minimum(xa, 0.0) - jnp.log(1.0 + jnp.exp(-jnp.abs(xa)))) * (1.0 / GATE_TAU)

    u = zc[:, :C_WIDTH] * jax.nn.sigmoid(zc[:, C_WIDTH:])
    if valid is not None:
        kg = kg * valid
        la = la * valid
        u = u * valid
    kg_ref[...] = kg
    u_ref[...] = u
    cs = GLA_CHUNK
    tri = jnp.where(lax.broadcasted_iota(jnp.int32, (cs, cs), 1) <= lax.broadcasted_iota(jnp.int32, (cs, cs), 0),
                    1.0, 0.0).astype(BF16)
    la_h, la_m, la_l = _split3(la)
    for c in range(n_seq // cs):
        rows = slice(c * cs, (c + 1) * cs)
        la_ref[rows, :] = _dot(tri, la_h[rows]) + _dot(tri, la_m[rows]) + _dot(tri, la_l[rows])
    if n_seq < tm:
        la_ref[n_seq:tm, :] = la[n_seq:tm]


def _proj_in(h, w, wa, ba, cos, sin, *, layer, tm, n_pad, n_seq, ln=None):
    n = h.shape[0]
    tb = cos.shape[0] // tm
    row = lambda width: pl.BlockSpec((tm, width), lambda i: (i, 0))
    const = lambda shape: pl.BlockSpec(shape, lambda i: (0, 0))
    per_layer = lambda shape: pl.BlockSpec((None,) + shape, lambda i: (layer, 0, 0))
    tab = pl.BlockSpec((tm, LANES), lambda i: (i % tb, 0))
    widths = (A_WIDTH, KV_WIDTH, KV_WIDTH, GK_WIDTH, GK_WIDTH, GK_WIDTH, B_WIDTH, B_WIDTH, C_WIDTH)
    dtypes = (BF16,) + (F32,) * 8
    ln_specs = [const((1, D_MODEL)), const((1, D_MODEL))] if ln else []
    return pl.pallas_call(
        functools.partial(_proj_in_kernel, n_pad=n_pad, n_seq=n_seq, pre_ln=bool(ln)),
        grid=(n // tm,),
        in_specs=ln_specs + [row(D_MODEL), per_layer((D_MODEL, PROJ_PAD_WIDTH)), per_layer((LANES, GK_WIDTH)),
                             per_layer((1, GK_WIDTH)), tab, tab],
        out_specs=[row(wd) for wd in widths],
        out_shape=[jax.ShapeDtypeStruct((n, wd), dt) for wd, dt in zip(widths, dtypes)],
        compiler_params=_params("parallel"),
        name="proj_in",
    )(*(ln or ()), h, w, wa, ba, cos, sin)


def _attn_kernel(sink_ref, q_ref, km_ref, vm_ref, *rest, meta_mode):
    if meta_mode:
        (o_ref,) = rest
        blocks = [(0, q_ref[...], km_ref[...], vm_ref[...], None)]
    else:
        kp_ref, vp_ref, kc_ref, vc_ref, o_ref = rest
        km, vm = km_ref[...], vm_ref[...]
        blocks = []
        for sub in range(q_ref.shape[0] // BLOCK):
            cur = slice(sub * BLOCK, (sub + 1) * BLOCK)
            if sub == 0:
                kp, vp = kp_ref[...], vp_ref[...]
                has_prev = pl.program_id(1) >= 1
            else:
                prev = slice((sub - 1) * BLOCK, sub * BLOCK)
                kp, vp = kc_ref[prev, :], vc_ref[prev, :]
                has_prev = True
            blocks.append((sub * BLOCK, q_ref[cur, :], jnp.concatenate([kp, kc_ref[cur, :], km], axis=0),
                           jnp.concatenate([vp, vc_ref[cur, :], vm], axis=0), has_prev))

    units = []
    for row0, q, k_all, v_all, has_prev in blocks:
        nk = k_all.shape[0]
        ki = lax.broadcasted_iota(jnp.int32, (nk, BLOCK), 0)
        qi = lax.broadcasted_iota(jnp.int32, (nk, BLOCK), 1)
        if has_prev is None:
            ok = ki <= qi - META_PAD
        else:
            prev_lo = qi if has_prev is True else qi + jnp.where(has_prev, 0, BLOCK)
            ok = ((ki >= prev_lo) & (ki < BLOCK)) | ((ki >= BLOCK) & (ki <= qi + BLOCK)) | (ki >= 2 * BLOCK)
        bias = jnp.where(ok, 0.0, NEG)
        kb, vb = k_all.astype(BF16), v_all.astype(BF16)
        for kvh in range(A_KV_HEADS):
            lanes = slice(kvh * HEAD_DIM, (kvh + 1) * HEAD_DIM)
            qs = jnp.concatenate([q[:, (kvh * A_GROUP + g) * HEAD_DIM:(kvh * A_GROUP + g + 1) * HEAD_DIM]
                                  for g in range(A_GROUP)], axis=0)
            units.append((row0, kvh, bias, _dot_nt(kb[:, lanes], qs), vb[:, lanes]))

    outs = []
    for row0, kvh, bias, st, vv in units:
        ps, dens = [], []
        for g in range(A_GROUP):
            s = st[:, g * BLOCK:(g + 1) * BLOCK] + bias
            sink = sink_ref[kvh * A_GROUP + g]
            m = jnp.maximum(jnp.max(s, axis=0, keepdims=True), sink)
            p = jnp.exp(s - m)
            dens.append(jnp.sum(p, axis=0, keepdims=True) + jnp.exp(sink - m))
            ps.append(p.astype(BF16))
        outs.append((row0, kvh, _dot_tn(vv, jnp.concatenate(ps, axis=1)), jnp.concatenate(dens, axis=1)))

    for row0, kvh, ot, den in outs:
        ot = ot / den
        for pair in range(A_GROUP // 2):
            two = jnp.concatenate([ot[:, (2 * pair) * BLOCK:(2 * pair + 1) * BLOCK],
                                   ot[:, (2 * pair + 1) * BLOCK:(2 * pair + 2) * BLOCK]], axis=0)
            h0 = kvh * A_GROUP + 2 * pair
            o_ref[row0:row0 + BLOCK, h0 * HEAD_DIM:(h0 + 2) * HEAD_DIM] = two.T.astype(BF16)


ATTN_SUB = 4


def _attn_prompt(sink, q, k, v, k_small, v_small, nbatch):
    n = q.shape[0]
    nblk = n // BLOCK // nbatch
    sub = ATTN_SUB if nblk % ATTN_SUB == 0 else 1
    nstep = nblk // sub
    kvs = pl.BlockSpec((sub * BLOCK, KV_WIDTH), lambda b, j: (b * nstep + j, 0))
    kvp = pl.BlockSpec((BLOCK, KV_WIDTH), lambda b, j: (b * nblk + jnp.maximum(j * sub - 1, 0), 0))
    kvm = pl.BlockSpec((N_META, KV_WIDTH), lambda b, j: (META_PAD // N_META, 0))
    return pl.pallas_call(
        functools.partial(_attn_kernel, meta_mode=False),
        grid=(nbatch, nstep),
        in_specs=[pl.BlockSpec(memory_space=pltpu.SMEM),
                  pl.BlockSpec((sub * BLOCK, A_WIDTH), lambda b, j: (b * nstep + j, 0)),
                  kvm, kvm, kvp, kvp, kvs, kvs],
        out_specs=pl.BlockSpec((sub * BLOCK, A_WIDTH), lambda b, j: (b * nstep + j, 0)),
        out_shape=jax.ShapeDtypeStruct((n, A_WIDTH), BF16),
        compiler_params=_params("parallel", "parallel"),
        name="attn_prompt",
    )(sink, q, k_small, v_small, k, v, k, v)


def _attn_meta(sink, q_small, k_small, v_small):
    kvm = pl.BlockSpec((N_META, KV_WIDTH), lambda i: (META_PAD // N_META, 0))
    return pl.pallas_call(
        functools.partial(_attn_kernel, meta_mode=True),
        grid=(1,),
        in_specs=[pl.BlockSpec(memory_space=pltpu.SMEM),
                  pl.BlockSpec((BLOCK, A_WIDTH), lambda i: (0, 0)), kvm, kvm],
        out_specs=pl.BlockSpec((BLOCK, A_WIDTH), lambda i: (0, 0)),
        out_shape=jax.ShapeDtypeStruct((BLOCK, A_WIDTH), BF16),
        compiler_params=_params("arbitrary"),
        name="attn_meta",
    )(sink, q_small, k_small, v_small)


SAMPLE_BLOCK = 32
HEAD_ROWS = 16


def _attn_sample_kernel(sink_ref, q_ref, kn_ref, vn_ref, mk_ref, mv_ref, wk_ref, wv_ref,
                        o_ref, nwk_ref, nwv_ref):
    sb = q_ref.shape[0]
    q = q_ref[...]
    kn = kn_ref[...]
    vn = vn_ref[...]
    s_win = jnp.einsum('bhl,blj->bhj', q, wk_ref[...].astype(BF16), preferred_element_type=F32)
    s_meta = jnp.einsum('bhl,bml->bhm', q, mk_ref[...].astype(BF16), preferred_element_type=F32)
    s_self = jnp.sum(q.astype(F32) * kn[:, None, :], axis=2, keepdims=True)
    sink = sink_ref[...][None]
    m = jnp.maximum(jnp.maximum(jnp.max(s_win, axis=2, keepdims=True), jnp.max(s_meta, axis=2, keepdims=True)),
                    jnp.maximum(s_self, sink))
    p_win = jnp.exp(s_win - m)
    p_meta = jnp.exp(s_meta - m)
    p_self = jnp.exp(s_self - m)
    denom = (jnp.sum(p_win, axis=2, keepdims=True) + jnp.sum(p_meta, axis=2, keepdims=True) + p_self
             + jnp.exp(sink - m))
    o = (jnp.einsum('bhj,blj->bhl', p_win.astype(BF16), wv_ref[...].astype(BF16), preferred_element_type=F32)
         + jnp.einsum('bhm,bml->bhl', p_meta.astype(BF16), mv_ref[...].astype(BF16), preferred_element_type=F32)
         + p_self * vn[:, None, :])
    o_ref[...] = o / denom
    nwin = wk_ref.shape[2]
    last = lax.broadcasted_iota(jnp.int32, (KV_WIDTH, nwin), 1) == nwin - 1
    knt = kn.T
    vnt = vn.T
    for j in range(sb):
        nwk_ref[j] = jnp.where(last, knt[:, j:j + 1], pltpu.roll(wk_ref[j], nwin - 1, 1))
        nwv_ref[j] = jnp.where(last, vnt[:, j:j + 1], pltpu.roll(wv_ref[j], nwin - 1, 1))


def _attn_sample(sink_col, q_bd, k_small, v_small, meta_k, meta_v, win_k_t, win_v_t, layer):
    ns = q_bd.shape[0]
    sb = min(SAMPLE_BLOCK, ns)
    nwin = win_k_t.shape[3]
    blk = lambda rows: pl.BlockSpec((sb, rows, KV_WIDTH), lambda i: (i, 0, 0))
    new = pl.BlockSpec((sb, KV_WIDTH), lambda i: (BLOCK // sb + i, 0))
    win = pl.BlockSpec((None, sb, KV_WIDTH, nwin), lambda i: (layer, i, 0, 0))
    win_out = pl.BlockSpec((sb, KV_WIDTH, nwin), lambda i: (i, 0, 0))
    return pl.pallas_call(
        _attn_sample_kernel,
        grid=(ns // sb,),
        in_specs=[pl.BlockSpec((HEAD_ROWS, 1), lambda i: (0, 0)),
                  blk(HEAD_ROWS), new, new, blk(N_META), blk(N_META), win, win],
        out_specs=[blk(HEAD_ROWS), win_out, win_out],
        out_shape=[jax.ShapeDtypeStruct((ns, HEAD_ROWS, KV_WIDTH), F32),
                   jax.ShapeDtypeStruct((ns, KV_WIDTH, nwin), F32),
                   jax.ShapeDtypeStruct((ns, KV_WIDTH, nwin), F32)],
        compiler_params=_params("parallel"),
        name="attn_sample",
    )(sink_col, q_bd, k_small, v_small, meta_k, meta_v, win_k_t, win_v_t)


def _seg_mean_sq(o, mseg):
    sq = o * o
    hi = sq.astype(BF16)
    lo = (sq - hi.astype(F32)).astype(BF16)
    return _dot(hi, mseg) + _dot(lo, mseg)


def _gla_kernel(q_ref, k_ref, g_ref, v_ref, r_ref, s0_ref, gg_ref, ob_ref, st_ref, st_scr, *, n_chunks):
    cs = GLA_CHUNK
    nseq = q_ref.shape[0]

    @pl.when(pl.program_id(1) == 0)
    def _():
        for b in range(nseq):
            st_scr[b] = s0_ref[0]

    iota = lambda shape, axis: lax.broadcasted_iota(jnp.int32, shape, axis)
    lg_cs, lg_dk, lg_dv = cs.bit_length() - 1, B_DK.bit_length() - 1, B_DV.bit_length() - 1
    kd_mask = (iota((B_HEADS * cs, GK_WIDTH), 0) >> lg_cs) == (iota((B_HEADS * cs, GK_WIDTH), 1) >> lg_dk)
    vd_mask = (iota((B_HEADS * cs, B_WIDTH), 0) >> lg_cs) == (iota((B_HEADS * cs, B_WIDTH), 1) >> lg_dv)
    st_mask = (iota((B_WIDTH, GK_WIDTH), 0) >> lg_dv) == (iota((B_WIDTH, GK_WIDTH), 1) >> lg_dk)
    t_idx = iota((cs, B_HEADS * cs), 0)
    s_idx = iota((cs, B_HEADS * cs), 1) & (cs - 1)
    diff = t_idx ^ s_idx
    level = jnp.full((cs, B_HEADS * cs), lg_cs, jnp.int32)
    for bit in range(lg_cs):
        level = jnp.where(diff >= (1 << bit), bit, level)
    level = jnp.where(s_idx > t_idx, -1, level)
    odd_row = (iota((cs, GK_WIDTH), 0) & 1) == 1
    mseg =jnp.where((iota((B_WIDTH, B_WIDTH), 0) >> lg_dv) == (iota((B_WIDTH, B_WIDTH), 1) >> lg_dv),
                     1.0 / B_DV, 0.0).astype(BF16)
    gg = gg_ref[...]

    def scores(qt, kt):
        kbd = jnp.where(kd_mask, jnp.concatenate([kt.astype(BF16)] * B_HEADS, axis=0), 0.0)
        return _dot_nt(qt.astype(BF16), kbd)

    def level_ref(g, bit):
        if bit == 0:
            return jnp.where(odd_row, pltpu.roll(g, 1, 0), g)
        half = 1 << bit
        return jnp.concatenate([jnp.broadcast_to(g[p + half - 1:p + half, :], (2 * half, GK_WIDTH))
                                for p in range(0, cs, 2 * half)], axis=0)

    def body(c, carry):
        rows = pl.ds(pl.multiple_of(c * cs, cs), cs)
        seqs = range(nseq)
        q = [q_ref[b, rows, :] for b in seqs]
        k = [k_ref[b, rows, :] for b in seqs]
        g = [g_ref[b, rows, :] for b in seqs]
        att = [jnp.where(level == lg_cs, scores(q[b], k[b]), 0.0) for b in seqs]
        for bit in range(lg_cs):
            for b in seqs:
                decay = jnp.exp(-jnp.abs(g[b] - level_ref(g[b], bit)))
                att[b] = jnp.where(level == bit, scores(q[b] * decay, k[b] * decay), att[b])
        for b in seqs:
            g_end = g[b][cs - 1:cs, :]
            vb = v_ref[b, rows, :].astype(BF16)
            vbd = jnp.where(vd_mask, jnp.concatenate([vb] * B_HEADS, axis=0), 0.0)
            st = st_scr[b]
            o = _dot(att[b].astype(BF16), vbd) + _dot_nt((q[b] * jnp.exp(g[b])).astype(BF16), st.astype(BF16))
            kh = (k[b] * jnp.exp(g_end - g[b])).astype(BF16)
            upd = _dot_tn(vb, kh)
            st_scr[b] = st * jnp.exp(g_end) + jnp.where(st_mask, upd, 0.0)
            on = o * lax.rsqrt(_seg_mean_sq(o, mseg) + LN_EPS)
            ob_ref[b, rows, :] = (on * gg * _silu(r_ref[b, rows, :])).astype(BF16)
        return carry

    lax.fori_loop(0, n_chunks, body, 0)

    @pl.when(pl.program_id(1) == pl.num_programs(1) - 1)
    def _():
        for b in range(nseq):
            st_ref[b] = st_scr[b]


GLA_SEQS = 4
GLA_ROWS = 512


def _gla_prompt(q, k, g, v, r, s0, gg, nbatch):
    t = q.shape[0] // nbatch
    nb = min(GLA_SEQS, nbatch)
    tr = min(GLA_ROWS, t)
    seq = lambda a: a.reshape(nbatch, t, a.shape[-1])
    blk = lambda width: pl.BlockSpec((nb, tr, width), lambda b, j: (b, j, 0))
    state = pl.BlockSpec((nb, B_WIDTH, GK_WIDTH), lambda b, j: (b, 0, 0))
    return pl.pallas_call(
        functools.partial(_gla_kernel, n_chunks=tr // GLA_CHUNK),
        grid=(nbatch // nb, t // tr),
        in_specs=[blk(GK_WIDTH), blk(GK_WIDTH), blk(GK_WIDTH), blk(B_WIDTH), blk(B_WIDTH),
                  pl.BlockSpec((1, B_WIDTH, GK_WIDTH), lambda b, j: (0, 0, 0)),
                  pl.BlockSpec((1, B_WIDTH), lambda b, j: (0, 0))],
        out_specs=[blk(B_WIDTH), state],
        out_shape=[jax.ShapeDtypeStruct((nbatch, t, B_WIDTH), BF16),
                   jax.ShapeDtypeStruct((nbatch, B_WIDTH, GK_WIDTH), F32)],
        scratch_shapes=[pltpu.VMEM((nb, B_WIDTH, GK_WIDTH), F32)],
        compiler_params=_params("parallel", "arbitrary"),
        name="gla_prompt",
    )(seq(q), seq(k), seq(g), seq(v), seq(r), s0, gg)


def _gla_sample_kernel(q_ref, k_ref, la_ref, v_ref, r_ref, s_ref, gg_ref, ob_ref, sn_ref):
    qt = q_ref[...].T
    kt = k_ref[...].T
    at = jnp.exp(la_ref[...]).T
    vt = v_ref[...].T
    rt = r_ref[...].T
    gg = gg_ref[...]
    outs = []
    for h in range(B_HEADS):
        vh = vt[h * B_DV:(h + 1) * B_DV, :]
        o = jnp.zeros_like(vh)
        for d in range(B_DK):
            i = h * B_DK + d
            s_new = at[i:i + 1, :] * s_ref[i] + kt[i:i + 1, :] * vh
            sn_ref[i] = s_new
            o = o + qt[i:i + 1, :] * s_new
        on = o * lax.rsqrt(jnp.mean(o * o, axis=0, keepdims=True) + LN_EPS)
        outs.append(on * gg * _silu(rt[h * B_DV:(h + 1) * B_DV, :]))
    ob_ref[...] = jnp.concatenate(outs, axis=0).T.astype(BF16)


def _gla_sample(q, k, la, v, r, state_t, gg_col, layer):
    ns = state_t.shape[3]
    row = lambda width: pl.BlockSpec((ns, width), lambda i: (BLOCK // ns, 0))
    return pl.pallas_call(
        _gla_sample_kernel,
        grid=(1,),
        in_specs=[row(GK_WIDTH), row(GK_WIDTH), row(GK_WIDTH), row(B_WIDTH), row(B_WIDTH),
                  pl.BlockSpec((None, GK_WIDTH, B_DV, ns), lambda i: (layer, 0, 0, 0)),
                  pl.BlockSpec((B_DV, 1), lambda i: (0, 0))],
        out_specs=[pl.BlockSpec((ns, B_WIDTH), lambda i: (0, 0)),
                   pl.BlockSpec((GK_WIDTH, B_DV, ns), lambda i: (0, 0, 0))],
        out_shape=[jax.ShapeDtypeStruct((ns, B_WIDTH), BF16),
                   jax.ShapeDtypeStruct((GK_WIDTH, B_DV, ns), F32)],
        compiler_params=_params("arbitrary"),
        name="gla_sample",
    )(q, k, la, v, r, state_t, gg_col)


CONV_HIST = 32
CONV_CHUNK = 64
CONV_GROUP = 2


def _conv_post(y, dwb, lng, lnb, pww, pwb):
    y = _silu(_ln_rows(y + dwb, lng, lnb))
    return _dot(y.astype(BF16), pww) + pwb


def _conv_kernel(u_ref, hist_ref, dww_ref, dwb_ref, lng_ref, lnb_ref, pww_ref, pwb_ref,
                 oc_ref, tail_ref, ubuf, *, n_chunks):
    t = u_ref.shape[1]
    ubuf[0, 0:CONV_HIST, :] = hist_ref[...]
    ubuf[0, CONV_HIST:CONV_HIST + t, :] = u_ref[0]
    off = CONV_HIST - (CONV_W - 1)
    n_copy = t + CONV_HIST - SUBLANES
    for s in range(1, SUBLANES):
        ubuf[s, 0:n_copy, :] = ubuf[0, s:s + n_copy, :]

    group = CONV_GROUP if n_chunks % CONV_GROUP == 0 else 1

    def body(c, carry):
        starts = [pl.multiple_of((c * group + i) * CONV_CHUNK, CONV_CHUNK) for i in range(group)]
        accs = [jnp.zeros((CONV_CHUNK, C_WIDTH), F32) for _ in starts]
        for j in range(CONV_W):
            a, s = divmod(j + off, SUBLANES)
            w_j = dww_ref[j:j + 1, :]
            for i, r0 in enumerate(starts):
                rows = pl.ds(pl.multiple_of(r0 + a * SUBLANES, SUBLANES), CONV_CHUNK)
                accs[i] = accs[i] + ubuf[s, rows, :] * w_j
        for r0, acc in zip(starts, accs):
            oc = _conv_post(acc, dwb_ref[...], lng_ref[...], lnb_ref[...], pww_ref[...], pwb_ref[...])
            oc_ref[0, pl.ds(r0, CONV_CHUNK), :] = oc.astype(BF16)
        return carry

    lax.fori_loop(0, n_chunks // group, body, 0)
    tail_ref[0] = ubuf[0, t:t + CONV_HIST, :]


def _conv_prompt(u, hist, dww, dwb, lng, lnb, pww, pwb, nbatch):
    t = u.shape[0] // nbatch
    const = lambda shape: pl.BlockSpec(shape, lambda b: (0,) * len(shape))
    return pl.pallas_call(
        functools.partial(_conv_kernel, n_chunks=t // CONV_CHUNK),
        grid=(nbatch,),
        in_specs=[pl.BlockSpec((1, t, C_WIDTH), lambda b: (b, 0, 0)), const((CONV_HIST, C_WIDTH)),
                  const((CONV_HIST, C_WIDTH)), const((1, C_WIDTH)), const((1, C_WIDTH)), const((1, C_WIDTH)),
                  const((C_WIDTH, C_WIDTH)), const((1, C_WIDTH))],
        out_specs=[pl.BlockSpec((1, t, C_WIDTH), lambda b: (b, 0, 0)),
                   pl.BlockSpec((1, CONV_HIST, C_WIDTH), lambda b: (b, 0, 0))],
        out_shape=[jax.ShapeDtypeStruct((nbatch, t, C_WIDTH), BF16),
                   jax.ShapeDtypeStruct((nbatch, CONV_HIST, C_WIDTH), F32)],
        scratch_shapes=[pltpu.VMEM((SUBLANES, CONV_HIST + t, C_WIDTH), F32)],
        compiler_params=_params("parallel"),
        name="conv_prompt",
    )(u.reshape(nbatch, t, C_WIDTH), hist, dww, dwb, lng, lnb, pww, pwb)


def _conv_sample_kernel(hist_ref, u_ref, dww_ref, dwb_ref, lng_ref, lnb_ref, pww_ref, pwb_ref, oc_ref, tail_ref):
    nh = CONV_W - 1
    u = u_ref[...]
    y = u * dww_ref[nh:nh + 1, :]
    for j in range(nh):
        y = y + hist_ref[j] * dww_ref[j:j + 1, :]
    oc = _conv_post(y, dwb_ref[...], lng_ref[...], lnb_ref[...], pww_ref[...], pwb_ref[...])
    oc_ref[...] = oc.astype(BF16)
    for j in range(nh - 1):
        tail_ref[j] = hist_ref[j + 1]
    tail_ref[nh - 1] = u


def _conv_sample(hist_t, u, dww, dwb, lng, lnb, pww, pwb, layer):
    ns = hist_t.shape[2]
    nh = CONV_W - 1
    const = lambda shape: pl.BlockSpec(shape, lambda i: (0,) * len(shape))
    return pl.pallas_call(
        _conv_sample_kernel,
        grid=(1,),
        in_specs=[pl.BlockSpec((None, nh, ns, C_WIDTH), lambda i: (layer, 0, 0, 0)),
                  pl.BlockSpec((ns, C_WIDTH), lambda i: (BLOCK // ns, 0)),
                  const((CONV_HIST, C_WIDTH)), const((1, C_WIDTH)), const((1, C_WIDTH)), const((1, C_WIDTH)),
                  const((C_WIDTH, C_WIDTH)), const((1, C_WIDTH))],
        out_specs=[const((ns, C_WIDTH)), const((nh, ns, C_WIDTH))],
        out_shape=[jax.ShapeDtypeStruct((ns, C_WIDTH), BF16),
                   jax.ShapeDtypeStruct((nh, ns, C_WIDTH), F32)],
        compiler_params=_params("arbitrary"),
        name="conv_sample",
    )(hist_t, u, dww, dwb, lng, lnb, pww, pwb)


ROUTE_LANE0 = N_GROUPS


def _route(x, wr2, br):
    x_hi = x.astype(BF16)
    x_lo = (x - x_hi.astype(F32)).astype(BF16)
    l_hi = _dot(x_hi, wr2)
    logits = l_hi[:, :LANES] + l_hi[:, LANES:] + _dot(x_lo, wr2)[:, :LANES] + br
    lane = lax.broadcasted_iota(jnp.int32, logits.shape, 1).astype(F32)
    far = 1e3
    glm = jnp.where(lane < N_GROUPS, logits, NEG)
    gmax = jnp.max(glm, axis=1, keepdims=True)
    gi = jnp.min(jnp.where(glm == gmax, lane, far), axis=1, keepdims=True)
    p_grp = 1.0 / jnp.sum(jnp.exp(glm - gmax), axis=1, keepdims=True)
    lo = ROUTE_LANE0 + EXP_PER_GROUP * gi
    in_sel = (lane >= lo) & (lane < lo + EXP_PER_GROUP)
    elm = jnp.where(in_sel, logits, NEG)
    v1 = jnp.max(elm, axis=1, keepdims=True)
    i1 = jnp.min(jnp.where(elm == v1, lane, far), axis=1, keepdims=True)
    elm2 = jnp.where(lane == i1, NEG, elm)
    v2 = jnp.max(elm2, axis=1, keepdims=True)
    i2 = jnp.min(jnp.where((elm2 == v2) & in_sel & (lane != i1), lane, far), axis=1, keepdims=True)
    t = jnp.exp(v2 - v1)
    w1 = p_grp / (1.0 + t)
    w2 = w1 * t
    return jnp.where(lane == i1, w1, 0.0) + jnp.where(lane == i2, w2, 0.0), gi


FFN_CHUNK = 128
MOE_ROWS = 128
MOE_SINGLE = (144, 176)
MOE_ALIGN = 16
GROUP_WIDTH = EXP_PER_GROUP * D_EXPERT


def _moe_sorted_rows(tm):
    need = tm + N_GROUPS * MOE_ALIGN + max(MOE_SINGLE + (MOE_ROWS,))
    return -(-need // LANES) * LANES


def _ffn_kernel(*refs, pre_ln):
    if pre_ln:
        lg_ref, lb_ref, *refs = refs
    (oa_ref, ob_ref, oc_ref, h_ref, wo_ref, g1_ref, b1_ref, wr_ref, br_ref, wg_ref, wu_ref, wd_ref,
     g_ref, b_ref, o_ref, x_ref, xs_ref, ys_ref, ws_ref) = refs
    tm = h_ref.shape[0]
    ns = xs_ref.shape[0]
    iota = lambda shape, axis: lax.broadcasted_iota(jnp.int32, shape, axis)
    ch = FFN_CHUNK if tm % FFN_CHUNK == 0 else tm
    chunks = [slice(r, r + ch) for r in range(0, tm, ch)]
    mixes = [(_dot(oa_ref[r, :], wo_ref[0:A_WIDTH, :])
              + _dot(ob_ref[r, :], wo_ref[A_WIDTH:A_WIDTH + B_WIDTH, :])
              + _dot(oc_ref[r, :], wo_ref[A_WIDTH + B_WIDTH:D_MODEL, :])) for r in chunks]
    xs_rows = []
    for r, mix in zip(chunks, mixes):
        h = h_ref[r, :]
        if pre_ln:
            h = _ln_rows(h, lg_ref[...], lb_ref[...])
        x_r = _ln_rows(ALPHA * h + mix, g1_ref[...], b1_ref[...])
        x_ref[r, :] = x_r
        xs_rows.append(x_r)
    routed = [_route(x_r, wr_ref[...], br_ref[...]) for x_r in xs_rows]
    x = jnp.concatenate(xs_rows, axis=0)
    dw = jnp.concatenate([d for d, _ in routed], axis=0)
    gi = jnp.concatenate([i for _, i in routed], axis=0)
    lane_f = iota((tm, LANES), 1).astype(F32)
    onehot = jnp.where(lane_f == gi, 1.0, 0.0)
    tri = jnp.where(iota((tm, tm), 1) <= iota((tm, tm), 0), 1.0, 0.0).astype(BF16)
    cum = _dot(tri, onehot.astype(BF16))
    rank = jnp.sum(onehot * (cum - 1.0), axis=1, keepdims=True)
    counts = cum[tm - 1:tm, :]

    starts, tiles, counts_i = [], [], []
    start = jnp.int32(0)
    for grp in range(N_GROUPS):
        n_g = counts[0, grp].astype(jnp.int32)
        counts_i.append(n_g)
        starts.append(start)
        tiles.append(sum((n_g > k * MOE_ROWS).astype(jnp.int32) for k in range(-(-tm // MOE_ROWS))))
        start = start + ((n_g + (MOE_ALIGN - 1)) & -MOE_ALIGN)
    lane1 = iota((1, LANES), 1)
    start_v = jnp.zeros((1, LANES), F32)
    for grp in range(N_GROUPS):
        start_v = jnp.where(lane1 == grp, starts[grp].astype(F32), start_v)
    pos = jnp.sum(onehot * start_v, axis=1, keepdims=True) + rank

    pos_i = pos.astype(jnp.int32)
    digits = jnp.where(lane_f == 0.0, (pos_i >> 5).astype(F32),
                       jnp.where(lane_f == 1.0, (pos_i & 31).astype(F32), 0.0)).astype(BF16)
    lane8 = iota((8, LANES), 1)
    radix = jnp.where(lane8 == 0, 32.0, jnp.where(lane8 == 1, 1.0, 0.0)).astype(BF16)
    pos_row = _dot_nt(radix, digits)[0:1, :]

    perm = jnp.where(iota((ns, tm), 0).astype(F32) == pos_row, 1.0, 0.0).astype(BF16)
    dw_hi = dw.astype(BF16)
    dw_lo = (dw - dw_hi.astype(F32)).astype(BF16)
    srt = _dot(perm, jnp.concatenate([x.astype(BF16), dw_hi, dw_lo], axis=1))
    xs_ref[...] = srt[:, :D_MODEL].astype(BF16)
    ws_ref[...] = srt[:, D_MODEL:D_MODEL + LANES] + srt[:, D_MODEL + LANES:]
    ys_ref[...] = jnp.zeros_like(ys_ref)

    def expert_tile(grp, r0, n_rows):
        rows = pl.ds(pl.multiple_of(r0, MOE_ALIGN), n_rows)
        xt = xs_ref[rows, :]
        w = ws_ref[rows, :]
        hes = []
        for e in range(EXP_PER_GROUP):
            ex = grp * EXP_PER_GROUP + e
            w_e = w[:, ROUTE_LANE0 + ex:ROUTE_LANE0 + ex + 1]
            hes.append((_silu(_dot(xt, wg_ref[ex])) * _dot(xt, wu_ref[ex]) * w_e).astype(BF16))
        ys_ref[rows, :] = _dot(jnp.concatenate(hes, axis=1), wd_ref[grp]).astype(BF16)

    for grp in range(N_GROUPS):
        n_g, lo = counts_i[grp], 0
        for size in MOE_SINGLE:
            pl.when((n_g > lo) & (n_g <= size))(functools.partial(expert_tile, grp, starts[grp], size))
            lo = size

        @pl.when(n_g > lo)
        def _(grp=grp):
            lax.fori_loop(0, tiles[grp],
                          lambda k, c: (expert_tile(grp, starts[grp] + k * MOE_ROWS, MOE_ROWS), c)[1], 0)

    lane_ns = iota((ch, ns), 1).astype(F32)
    ys = ys_ref[...]
    y_rows = [_dot(jnp.where(lane_ns == pos[r], 1.0, 0.0).astype(BF16), ys) for r in chunks]
    for r, y in zip(chunks, y_rows):
        o_ref[r, :] = _ln_rows(ALPHA * x_ref[r, :] + y, g_ref[...], b_ref[...])


def _ffn(oa, ob, oc, h, wo, g1, b1, wr2, br, wg, wu, wd, g2, b2, tm, layer, ln=None):
    n = h.shape[0]
    ns = _moe_sorted_rows(tm)
    row = lambda width: pl.BlockSpec((tm, width), lambda i: (i, 0))
    const = lambda shape: pl.BlockSpec(shape, lambda i: (0,) * len(shape))
    resident = lambda shape: pl.BlockSpec((None,) + shape, lambda i: (layer,) + (0,) * len(shape),
                                          pipeline_mode=pl.Buffered(1))
    ln_specs = [const((1, D_MODEL)), const((1, D_MODEL))] if ln else []
    return pl.pallas_call(
        functools.partial(_ffn_kernel, pre_ln=bool(ln)),
        grid=(n // tm,),
        in_specs=ln_specs + [row(A_WIDTH), row(B_WIDTH), row(C_WIDTH), row(D_MODEL),
                             resident((D_MODEL, D_MODEL)), const((1, D_MODEL)), const((1, D_MODEL)),
                             resident((D_MODEL, 2 * LANES)), const((1, LANES)),
                             resident((N_EXPERTS, D_MODEL, D_EXPERT)), resident((N_EXPERTS, D_MODEL, D_EXPERT)),
                             resident((N_GROUPS, GROUP_WIDTH, D_MODEL)),
                             const((1, D_MODEL)), const((1, D_MODEL))],
        out_specs=row(D_MODEL),
        out_shape=jax.ShapeDtypeStruct((n, D_MODEL), F32),
        scratch_shapes=[pltpu.VMEM((tm, D_MODEL), F32), pltpu.VMEM((ns, D_MODEL), BF16),
                        pltpu.VMEM((ns, D_MODEL), BF16), pltpu.VMEM((ns, LANES), F32)],
        compiler_params=_params("parallel"),
        name="ffn",
    )(*(ln or ()), oa, ob, oc, h, wo, g1, b1, wr2, br, wg, wu, wd, g2, b2)


def _rope_tables(pos):
    half = HEAD_DIM // 2
    inv = ROPE_THETA ** (-jnp.arange(half, dtype=F32) / half)
    ang = pos.astype(F32)[:, None] * inv[None, :]
    cos, sin = jnp.cos(ang), jnp.sin(ang)
    cos_t = jnp.concatenate([cos, cos] * (LANES // HEAD_DIM), axis=1)
    sin_t = jnp.concatenate([-sin, sin] * (LANES // HEAD_DIM), axis=1)
    return cos_t, sin_t


def _row(v):
    return v.reshape(1, -1)


def kernel(x_prompt, x_sample, cache_meta_k, cache_meta_v, cache_win_k, cache_win_v, state_gla, state_conv,
           meta_tokens, ln_in_g, ln_in_b, w_in, attn_sink, w_alpha, b_alpha, gla_norm_g,
           conv_dw_w, conv_dw_b, conv_ln_g, conv_ln_b, conv_pw_w, conv_pw_b, w_out, ln1_g, ln1_b,
           w_router_group, b_router_group, w_router_expert, b_router_expert,
           w_exp_gate, w_exp_up, w_exp_down, ln2_g, ln2_b):
    nb, seq, d = x_prompt.shape
    ns = x_sample.shape[0]
    nwin = cache_win_k.shape[2]
    n_big = nb * seq
    n_small = BLOCK + ns
    tm_big = min(512, seq)

    small_in = jnp.concatenate([jnp.zeros((META_PAD, d), F32), meta_tokens.astype(F32),
                                x_sample.reshape(ns, d)], axis=0)
    hb = x_prompt.reshape(n_big, d)
    hs = small_in
    ln_in = (_row(ln_in_g), _row(ln_in_b))

    cos_b, sin_b = _rope_tables(N_META + jnp.arange(seq))
    pos_small = jnp.concatenate([jnp.maximum(jnp.arange(BLOCK) - META_PAD, 0),
                                 jnp.full((ns,), PAST_LEN, jnp.int32)])
    cos_s, sin_s = _rope_tables(pos_small)

    outs = [[] for _ in range(12)]
    zeros_hist = jnp.zeros((CONV_HIST, C_WIDTH), F32)
    zeros_state = jnp.zeros((1, B_WIDTH, GK_WIDTH), F32)

    nl = w_in.shape[0]
    w_pad = jnp.concatenate([w_in[:, :, :C_AB + B_RANK], jnp.zeros((nl, d, LANES - B_RANK), F32),
                             w_in[:, :, C_AB + B_RANK:]], axis=2).astype(BF16)
    wa_all = jnp.concatenate([w_alpha, jnp.zeros((nl, LANES - B_RANK, GK_WIDTH), F32)], axis=1).astype(BF16)
    ba_all = b_alpha[:, None, :]
    wo_all = w_out.astype(BF16)
    wr = jnp.concatenate([w_router_group, w_router_expert,
                          jnp.zeros((nl, d, LANES - N_GROUPS - N_EXPERTS), F32)], axis=2)
    wr_hi = wr.astype(BF16)
    wr2_all = jnp.concatenate([wr_hi, (wr - wr_hi.astype(F32)).astype(BF16)], axis=2)
    wg_all, wu_all = w_exp_gate.astype(BF16), w_exp_up.astype(BF16)
    wd_all = w_exp_down.astype(BF16).reshape(nl, N_GROUPS, GROUP_WIDTH, d)
    win_k_t = cache_win_k.transpose(0, 1, 3, 4, 2).reshape(nl, ns, KV_WIDTH, nwin)
    win_v_t = cache_win_v.transpose(0, 1, 3, 4, 2).reshape(nl, ns, KV_WIDTH, nwin)
    gla_t = state_gla.transpose(0, 2, 3, 4, 1).reshape(nl, GK_WIDTH, B_DV, ns)
    conv_t = state_conv.transpose(0, 2, 1, 3)

    for l in range(DEPTH):
        sink = attn_sink[l].astype(F32)
        sink_col = jnp.concatenate([sink, jnp.zeros((HEAD_ROWS - A_HEADS,), F32)])[:, None]
        gg_t = _row(jnp.tile(gla_norm_g[l], B_HEADS))
        gg_col = gla_norm_g[l][:, None]
        dww = jnp.concatenate([conv_dw_w[l], jnp.zeros((CONV_HIST - CONV_W, C_WIDTH), F32)], axis=0)
        dwb, clg, clb = _row(conv_dw_b[l]), _row(conv_ln_g[l]), _row(conv_ln_b[l])
        pww, pwb = conv_pw_w[l].astype(BF16), _row(conv_pw_b[l])
        br = _row(jnp.concatenate([b_router_group[l], b_router_expert[l],
                                   jnp.zeros((LANES - N_GROUPS - N_EXPERTS,), F32)]))

        ln = ln_in if l == 0 else None
        qa_b, ka_b, va_b, qg_b, kg_b, la_b, vg_b, rg_b, u_b = _proj_in(
            hb, w_pad, wa_all, ba_all, cos_b, sin_b, layer=l, tm=tm_big, n_pad=0, n_seq=tm_big, ln=ln)
        qa_s, ka_s, va_s, qg_s, kg_s, la_s, vg_s, rg_s, u_s = _proj_in(
            hs, w_pad, wa_all, ba_all, cos_s, sin_s, layer=l, tm=n_small, n_pad=META_PAD, n_seq=BLOCK, ln=ln)

        oa_b = _attn_prompt(sink, qa_b, ka_b, va_b, ka_s, va_s, nb)
        oa_m = _attn_meta(sink, qa_s, ka_s, va_s)
        q_smp = qa_s[BLOCK:].reshape(ns, A_KV_HEADS, A_GROUP, 1, HEAD_DIM)
        eye = jnp.eye(A_KV_HEADS, dtype=BF16)[None, :, None, :, None]
        q_bd = (q_smp * eye).reshape(ns, A_HEADS, KV_WIDTH)
        q_bd = jnp.pad(q_bd, ((0, 0), (0, HEAD_ROWS - A_HEADS), (0, 0)))
        o_bd, nwk_t, nwv_t = _attn_sample(
            sink_col, q_bd, ka_s, va_s,
            cache_meta_k[l].reshape(ns, N_META, KV_WIDTH), cache_meta_v[l].reshape(ns, N_META, KV_WIDTH),
            win_k_t, win_v_t, l)
        o_bd = o_bd[:, :A_HEADS].reshape(ns, A_KV_HEADS, A_GROUP, A_KV_HEADS, HEAD_DIM)
        oa_smp = jnp.stack([o_bd[:, c, :, c, :] for c in range(A_KV_HEADS)], axis=1).reshape(ns, A_WIDTH)
        oa_s = jnp.concatenate([oa_m, oa_smp.astype(BF16)], axis=0)

        ob_m, st_m = _gla_prompt(qg_s[:BLOCK], kg_s[:BLOCK], la_s[:BLOCK], vg_s[:BLOCK], rg_s[:BLOCK],
                                 zeros_state, gg_t, 1)
        ob_b, st_b = _gla_prompt(qg_b, kg_b, la_b, vg_b, rg_b, st_m, gg_t, nb)
        ob_smp, s_new_t = _gla_sample(qg_s, kg_s, la_s, vg_s, rg_s, gla_t, gg_col, l)
        ob_s = jnp.concatenate([ob_m.reshape(BLOCK, B_WIDTH), ob_smp], axis=0)
        st5 = st_b.reshape(nb, B_HEADS, B_DV, B_HEADS, B_DK)
        gla_p = jnp.stack([st5[:, h, :, h, :] for h in range(B_HEADS)], axis=1).transpose(0, 1, 3, 2)

        oc_m, _ = _conv_prompt(u_s[:BLOCK], zeros_hist, dww, dwb, clg, clb, pww, pwb, 1)
        oc_b, tail_b = _conv_prompt(u_b, u_s[BLOCK - CONV_HIST:BLOCK], dww, dwb, clg, clb, pww, pwb, nb)
        oc_smp, tail_s_t = _conv_sample(conv_t, u_s, dww, dwb, clg, clb, pww, pwb, l)
        oc_s = jnp.concatenate([oc_m.reshape(BLOCK, C_WIDTH), oc_smp], axis=0)

        l1g, l1b, l2g, l2b = _row(ln1_g[l]), _row(ln1_b[l]), _row(ln2_g[l]), _row(ln2_b[l])
        hb = _ffn(oa_b, ob_b.reshape(n_big, B_WIDTH), oc_b.reshape(n_big, C_WIDTH), hb, wo_all, l1g, l1b,
                  wr2_all, br, wg_all, wu_all, wd_all, l2g, l2b, tm_big, l, ln=ln)
        hs = _ffn(oa_s, ob_s, oc_s, hs, wo_all, l1g, l1b, wr2_all, br, wg_all, wu_all, wd_all, l2g, l2b,
                  n_small, l, ln=ln)

        kv4 = lambda a: a.reshape(a.shape[0], a.shape[1], A_KV_HEADS, HEAD_DIM)
        win_t = lambda a: a.reshape(ns, A_KV_HEADS, HEAD_DIM, nwin).transpose(0, 3, 1, 2)
        meta_k = jnp.broadcast_to(ka_s[META_PAD:BLOCK][None], (nb, N_META, KV_WIDTH))
        meta_v = jnp.broadcast_to(va_s[META_PAD:BLOCK][None], (nb, N_META, KV_WIDTH))
        win_k = ka_b.reshape(nb, seq, KV_WIDTH)[:, seq - nwin:]
        win_v = va_b.reshape(nb, seq, KV_WIDTH)[:, seq - nwin:]
        layer_out = (None, None, kv4(meta_k), kv4(meta_v), kv4(win_k), kv4(win_v), win_t(nwk_t), win_t(nwv_t),
                     gla_p, s_new_t.reshape(B_HEADS, B_DK, B_DV, ns).transpose(3, 0, 1, 2),
                     tail_b[:, CONV_HIST - (CONV_W - 1):], tail_s_t.transpose(1, 0, 2))
        for i in range(2, 12):
            outs[i].append(layer_out[i])

    y_prompt = hb.reshape(nb, seq, d)
    y_sample = hs[BLOCK:].reshape(ns, 1, d)
    return (y_prompt, y_sample) + tuple(jnp.stack(o) for o in outs[2:])
```

```python
import functools

import jax
import jax.numpy as jnp
from jax import lax
from jax.experimental import pallas as pl
from jax.experimental.pallas import tpu as pltpu

F32 = jnp.float32
BF16 = jnp.bfloat16

D_MODEL = 1024
DEPTH = 2
PAST_LEN = 16384
N_META = 16
HEAD_DIM = 64
A_WIDTH = 512
A_HEADS = 8
A_KV_HEADS = 2
A_GROUP = 4
WINDOW = 128
BLOCK = 128
ROPE_THETA = 10000.0
B_WIDTH = 256
B_HEADS = 4
B_DV = 64
B_DK = 32
B_RANK = 16
GATE_TAU = 16.0
GLA_CHUNK = 64
C_WIDTH = 256
CONV_W = 31
N_GROUPS = 4
EXP_PER_GROUP = 4
N_EXPERTS = 16
D_EXPERT = 256
ALPHA = (2 * DEPTH) ** 0.25
LN_EPS = 1e-5

LANES = 128
SUBLANES = 8
META_PAD = BLOCK - N_META
KV_WIDTH = A_KV_HEADS * HEAD_DIM
GK_WIDTH = B_HEADS * B_DK
C_QA = 0
C_KA = C_QA + A_WIDTH
C_VA = C_KA + KV_WIDTH
C_QB = C_VA + KV_WIDTH
C_KB = C_QB + GK_WIDTH
C_VB = C_KB + GK_WIDTH
C_RB = C_VB + B_WIDTH
C_AB = C_RB + B_WIDTH
C_CG = C_AB + LANES
PROJ_PAD_WIDTH = C_CG + 2 * C_WIDTH
NEG = -1e30
VMEM_LIMIT = 56 * 1024 * 1024


def _dot(a, b):
    return jnp.dot(a, b, preferred_element_type=F32)


def _dot_nt(a, b):
    return lax.dot_general(a, b, (((1,), (1,)), ((), ())), preferred_element_type=F32)


def _dot_tn(a, b):
    return lax.dot_general(a, b, (((0,), (0,)), ((), ())), preferred_element_type=F32)


def _ln_rows(x, g, b):
    xc = x - jnp.mean(x, -1, keepdims=True)
    var = jnp.mean(xc * xc, -1, keepdims=True)
    return xc * lax.rsqrt(var + LN_EPS) * g + b


def _silu(x):
    return x * jax.nn.sigmoid(x)


def _split3(x):
    hi = x.astype(BF16)
    r1 = x - hi.astype(F32)
    mid = r1.astype(BF16)
    lo = (r1 - mid.astype(F32)).astype(BF16)
    return hi, mid, lo


def _params(*sem):
    return pltpu.CompilerParams(dimension_semantics=sem, vmem_limit_bytes=VMEM_LIMIT)


def _proj_in_kernel(*refs, n_pad, n_seq, pre_ln):
    if pre_ln:
        lg_ref, lb_ref, *refs = refs
    (x_ref, w_ref, wa_ref, ba_ref, cos_ref, sin_ref,
     qa_ref, ka_ref, va_ref, qg_ref, kg_ref, la_ref, vg_ref, rg_ref, u_ref) = refs
    x = x_ref[...]
    if pre_ln:
        x = _ln_rows(x, lg_ref[...], lb_ref[...])
    xb = x.astype(BF16)
    tm = xb.shape[0]
    za = _dot(xb, w_ref[:, C_QA:C_QB])
    cos = cos_ref[...]
    sin = sin_ref[...]
    lane = lax.broadcasted_iota(jnp.int32, (tm, LANES), 1)
    first_half = (lane & (HEAD_DIM // 2)) == 0

    def rope(z):
        rot = jnp.where(first_half, pltpu.roll(z, LANES - HEAD_DIM // 2, 1), pltpu.roll(z, HEAD_DIM // 2, 1))
        return z * cos + rot * sin

    if n_pad:
        valid = (lax.broadcasted_iota(jnp.int32, (tm, 1), 0) >= n_pad).astype(F32)
    else:
        valid = None

    zb = _dot(xb, w_ref[:, C_QB:C_CG])
    zc = _dot(xb, w_ref[:, C_CG:PROJ_PAD_WIDTH])
    for c in range(A_WIDTH // LANES):
        zq = za[:, c * LANES:(c + 1) * LANES]
        qa_ref[:, c * LANES:(c + 1) * LANES] = (rope(zq) * (HEAD_DIM ** -0.5)).astype(BF16)
    ka_ref[...] = rope(za[:, C_KA:C_VA])
    va_ref[...] = za[:, C_VA:C_QB]

    o = C_QB
    qg_ref[...] = zb[:, C_QB - o:C_KB - o] * (B_DK ** -0.5)
    kg = zb[:, C_KB - o:C_VB - o]
    vg_ref[...] = zb[:, C_VB - o:C_RB - o]
    rg_ref[...] = zb[:, C_RB - o:C_AB - o]
    ab = zb[:, C_AB - o:C_CG - o].astype(BF16)
    xa = _dot(ab, wa_ref[...]) + ba_ref[...]
    la = (jnp.minimum(xa, 0.0) - jnp.log(1.0 + jnp.exp(-jnp.abs(xa)))) * (1.0 / GATE_TAU)

    u = zc[:, :C_WIDTH] * jax.nn.sigmoid(zc[:, C_WIDTH:])
    if valid is not None:
        kg = kg * valid
        la = la * valid
        u = u * valid
    kg_ref[...] = kg
    u_ref[...] = u
    cs = GLA_CHUNK
    tri = jnp.where(lax.broadcasted_iota(jnp.int32, (cs, cs), 1) <= lax.broadcasted_iota(jnp.int32, (cs, cs), 0),
                    1.0, 0.0).astype(BF16)
    la_h, la_m, la_l = _split3(la)
    for c in range(n_seq // cs):
        rows = slice(c * cs, (c + 1) * cs)
        la_ref[rows, :] = _dot(tri, la_h[rows]) + _dot(tri, la_m[rows]) + _dot(tri, la_l[rows])
    if n_seq < tm:
        la_ref[n_seq:tm, :] = la[n_seq:tm]


def _proj_in(h, w, wa, ba, cos, sin, *, layer, tm, n_pad, n_seq, ln=None):
    n = h.shape[0]
    tb = cos.shape[0] // tm
    row = lambda width: pl.BlockSpec((tm, width), lambda i: (i, 0))
    const = lambda shape: pl.BlockSpec(shape, lambda i: (0, 0))
    per_layer = lambda shape: pl.BlockSpec((None,) + shape, lambda i: (layer, 0, 0))
    tab = pl.BlockSpec((tm, LANES), lambda i: (i % tb, 0))
    widths = (A_WIDTH, KV_WIDTH, KV_WIDTH, GK_WIDTH, GK_WIDTH, GK_WIDTH, B_WIDTH, B_WIDTH, C_WIDTH)
    dtypes = (BF16,) + (F32,) * 8
    ln_specs = [const((1, D_MODEL)), const((1, D_MODEL))] if ln else []
    return pl.pallas_call(
        functools.partial(_proj_in_kernel, n_pad=n_pad, n_seq=n_seq, pre_ln=bool(ln)),
        grid=(n // tm,),
        in_specs=ln_specs + [row(D_MODEL), per_layer((D_MODEL, PROJ_PAD_WIDTH)), per_layer((LANES, GK_WIDTH)),
                             per_layer((1, GK_WIDTH)), tab, tab],
        out_specs=[row(wd) for wd in widths],
        out_shape=[jax.ShapeDtypeStruct((n, wd), dt) for wd, dt in zip(widths, dtypes)],
        compiler_params=_params("parallel"),
        name="proj_in",
    )(*(ln or ()), h, w, wa, ba, cos, sin)


def _attn_kernel(sink_ref, q_ref, km_ref, vm_ref, *rest, meta_mode):
    if meta_mode:
        (o_ref,) = rest
        blocks = [(0, q_ref[...], km_ref[...], vm_ref[...], None)]
    else:
        kp_ref, vp_ref, kc_ref, vc_ref, o_ref = rest
        km, vm = km_ref[...], vm_ref[...]
        blocks = []
        for sub in range(q_ref.shape[0] // BLOCK):
            cur = slice(sub * BLOCK, (sub + 1) * BLOCK)
            if sub == 0:
                kp, vp = kp_ref[...], vp_ref[...]
                has_prev = pl.program_id(1) >= 1
            else:
                prev = slice((sub - 1) * BLOCK, sub * BLOCK)
                kp, vp = kc_ref[prev, :], vc_ref[prev, :]
                has_prev = True
            blocks.append((sub * BLOCK, q_ref[cur, :], jnp.concatenate([kp, kc_ref[cur, :], km], axis=0),
                           jnp.concatenate([vp, vc_ref[cur, :], vm], axis=0), has_prev))

    units = []
    for row0, q, k_all, v_all, has_prev in blocks:
        nk = k_all.shape[0]
        ki = lax.broadcasted_iota(jnp.int32, (nk, BLOCK), 0)
        qi = lax.broadcasted_iota(jnp.int32, (nk, BLOCK), 1)
        if has_prev is None:
            ok = ki <= qi - META_PAD
        else:
            prev_lo = qi if has_prev is True else qi + jnp.where(has_prev, 0, BLOCK)
            ok = ((ki >= prev_lo) & (ki < BLOCK)) | ((ki >= BLOCK) & (ki <= qi + BLOCK)) | (ki >= 2 * BLOCK)
        bias = jnp.where(ok, 0.0, NEG)
        kb, vb = k_all.astype(BF16), v_all.astype(BF16)
        for kvh in range(A_KV_HEADS):
            lanes = slice(kvh * HEAD_DIM, (kvh + 1) * HEAD_DIM)
            qs = jnp.concatenate([q[:, (kvh * A_GROUP + g) * HEAD_DIM:(kvh * A_GROUP + g + 1) * HEAD_DIM]
                                  for g in range(A_GROUP)], axis=0)
            units.append((row0, kvh, bias, _dot_nt(kb[:, lanes], qs), vb[:, lanes]))

    outs = []
    for row0, kvh, bias, st, vv in units:
        ps, dens = [], []
        for g in range(A_GROUP):
            s = st[:, g * BLOCK:(g + 1) * BLOCK] + bias
            sink = sink_ref[kvh * A_GROUP + g]
            m = jnp.maximum(jnp.max(s, axis=0, keepdims=True), sink)
            p = jnp.exp(s - m)
            dens.append(jnp.sum(p, axis=0, keepdims=True) + jnp.exp(sink - m))
            ps.append(p.astype(BF16))
        outs.append((row0, kvh, _dot_tn(vv, jnp.concatenate(ps, axis=1)), jnp.concatenate(dens, axis=1)))

    for row0, kvh, ot, den in outs:
        ot = ot / den
        for pair in range(A_GROUP // 2):
            two = jnp.concatenate([ot[:, (2 * pair) * BLOCK:(2 * pair + 1) * BLOCK],
                                   ot[:, (2 * pair + 1) * BLOCK:(2 * pair + 2) * BLOCK]], axis=0)
            h0 = kvh * A_GROUP + 2 * pair
            o_ref[row0:row0 + BLOCK, h0 * HEAD_DIM:(h0 + 2) * HEAD_DIM] = two.T.astype(BF16)


ATTN_SUB = 4


def _attn_prompt(sink, q, k, v, k_small, v_small, nbatch):
    n = q.shape[0]
    nblk = n // BLOCK // nbatch
    sub = ATTN_SUB if nblk % ATTN_SUB == 0 else 1
    nstep = nblk // sub
    kvs = pl.BlockSpec((sub * BLOCK, KV_WIDTH), lambda b, j: (b * nstep + j, 0))
    kvp = pl.BlockSpec((BLOCK, KV_WIDTH), lambda b, j: (b * nblk + jnp.maximum(j * sub - 1, 0), 0))
    kvm = pl.BlockSpec((N_META, KV_WIDTH), lambda b, j: (META_PAD // N_META, 0))
    return pl.pallas_call(
        functools.partial(_attn_kernel, meta_mode=False),
        grid=(nbatch, nstep),
        in_specs=[pl.BlockSpec(memory_space=pltpu.SMEM),
                  pl.BlockSpec((sub * BLOCK, A_WIDTH), lambda b, j: (b * nstep + j, 0)),
                  kvm, kvm, kvp, kvp, kvs, kvs],
        out_specs=pl.BlockSpec((sub * BLOCK, A_WIDTH), lambda b, j: (b * nstep + j, 0)),
        out_shape=jax.ShapeDtypeStruct((n, A_WIDTH), BF16),
        compiler_params=_params("parallel", "parallel"),
        name="attn_prompt",
    )(sink, q, k_small, v_small, k, v, k, v)


def _attn_meta(sink, q_small, k_small, v_small):
    kvm = pl.BlockSpec((N_META, KV_WIDTH), lambda i: (META_PAD // N_META, 0))
    return pl.pallas_call(
        functools.partial(_attn_kernel, meta_mode=True),
        grid=(1,),
        in_specs=[pl.BlockSpec(memory_space=pltpu.SMEM),
                  pl.BlockSpec((BLOCK, A_WIDTH), lambda i: (0, 0)), kvm, kvm],
        out_specs=pl.BlockSpec((BLOCK, A_WIDTH), lambda i: (0, 0)),
        out_shape=jax.ShapeDtypeStruct((BLOCK, A_WIDTH), BF16),
        compiler_params=_params("arbitrary"),
        name="attn_meta",
    )(sink, q_small, k_small, v_small)


SAMPLE_BLOCK = 32
HEAD_ROWS = 16


def _attn_sample_kernel(sink_ref, q_ref, kn_ref, vn_ref, mk_ref, mv_ref, wk_ref, wv_ref,
                        o_ref, nwk_ref, nwv_ref):
    sb = q_ref.shape[0]
    q = q_ref[...]
    kn = kn_ref[...]
    vn = vn_ref[...]
    s_win = jnp.einsum('bhl,blj->bhj', q, wk_ref[...].astype(BF16), preferred_element_type=F32)
    s_meta = jnp.einsum('bhl,bml->bhm', q, mk_ref[...].astype(BF16), preferred_element_type=F32)
    s_self = jnp.sum(q.astype(F32) * kn[:, None, :], axis=2, keepdims=True)
    sink = sink_ref[...][None]
    m = jnp.maximum(jnp.maximum(jnp.max(s_win, axis=2, keepdims=True), jnp.max(s_meta, axis=2, keepdims=True)),
                    jnp.maximum(s_self, sink))
    p_win = jnp.exp(s_win - m)
    p_meta = jnp.exp(s_meta - m)
    p_self = jnp.exp(s_self - m)
    denom = (jnp.sum(p_win, axis=2, keepdims=True) + jnp.sum(p_meta, axis=2, keepdims=True) + p_self
             + jnp.exp(sink - m))
    o = (jnp.einsum('bhj,blj->bhl', p_win.astype(BF16), wv_ref[...].astype(BF16), preferred_element_type=F32)
         + jnp.einsum('bhm,bml->bhl', p_meta.astype(BF16), mv_ref[...].astype(BF16), preferred_element_type=F32)
         + p_self * vn[:, None, :])
    o_ref[...] = o / denom
    nwin = wk_ref.shape[2]
    last = lax.broadcasted_iota(jnp.int32, (KV_WIDTH, nwin), 1) == nwin - 1
    knt = kn.T
    vnt = vn.T
    for j in range(sb):
        nwk_ref[j] = jnp.where(last, knt[:, j:j + 1], pltpu.roll(wk_ref[j], nwin - 1, 1))
        nwv_ref[j] = jnp.where(last, vnt[:, j:j + 1], pltpu.roll(wv_ref[j], nwin - 1, 1))


def _attn_sample(sink_col, q_bd, k_small, v_small, meta_k, meta_v, win_k_t, win_v_t, layer):
    ns = q_bd.shape[0]
    sb = min(SAMPLE_BLOCK, ns)
    nwin = win_k_t.shape[3]
    blk = lambda rows: pl.BlockSpec((sb, rows, KV_WIDTH), lambda i: (i, 0, 0))
    new = pl.BlockSpec((sb, KV_WIDTH), lambda i: (BLOCK // sb + i, 0))
    win = pl.BlockSpec((None, sb, KV_WIDTH, nwin), lambda i: (layer, i, 0, 0))
    win_out = pl.BlockSpec((sb, KV_WIDTH, nwin), lambda i: (i, 0, 0))
    return pl.pallas_call(
        _attn_sample_kernel,
        grid=(ns // sb,),
        in_specs=[pl.BlockSpec((HEAD_ROWS, 1), lambda i: (0, 0)),
                  blk(HEAD_ROWS), new, new, blk(N_META), blk(N_META), win, win],
        out_specs=[blk(HEAD_ROWS), win_out, win_out],
        out_shape=[jax.ShapeDtypeStruct((ns, HEAD_ROWS, KV_WIDTH), F32),
                   jax.ShapeDtypeStruct((ns, KV_WIDTH, nwin), F32),
                   jax.ShapeDtypeStruct((ns, KV_WIDTH, nwin), F32)],
        compiler_params=_params("parallel"),
        name="attn_sample",
    )(sink_col, q_bd, k_small, v_small, meta_k, meta_v, win_k_t, win_v_t)


def _seg_mean_sq(o, mseg):
    sq = o * o
    hi = sq.astype(BF16)
    lo = (sq - hi.astype(F32)).astype(BF16)
    return _dot(hi, mseg) + _dot(lo, mseg)


def _gla_kernel(q_ref, k_ref, g_ref, v_ref, r_ref, s0_ref, gg_ref, ob_ref, st_ref, st_scr, *, n_chunks):
    cs = GLA_CHUNK
    nseq = q_ref.shape[0]

    @pl.when(pl.program_id(1) == 0)
    def _():
        for b in range(nseq):
            st_scr[b] = s0_ref[0]

    iota = lambda shape, axis: lax.broadcasted_iota(jnp.int32, shape, axis)
    lg_cs, lg_dk, lg_dv = cs.bit_length() - 1, B_DK.bit_length() - 1, B_DV.bit_length() - 1
    kd_mask = (iota((B_HEADS * cs, GK_WIDTH), 0) >> lg_cs) == (iota((B_HEADS * cs, GK_WIDTH), 1) >> lg_dk)
    vd_mask = (iota((B_HEADS * cs, B_WIDTH), 0) >> lg_cs) == (iota((B_HEADS * cs, B_WIDTH), 1) >> lg_dv)
    st_mask = (iota((B_WIDTH, GK_WIDTH), 0) >> lg_dv) == (iota((B_WIDTH, GK_WIDTH), 1) >> lg_dk)
    t_idx = iota((cs, B_HEADS * cs), 0)
    s_idx = iota((cs, B_HEADS * cs), 1) & (cs - 1)
    diff = t_idx ^ s_idx
    level = jnp.full((cs, B_HEADS * cs), lg_cs, jnp.int32)
    for bit in range(lg_cs):
        level = jnp.where(diff >= (1 << bit), bit, level)
    level = jnp.where(s_idx > t_idx, -1, level)
    odd_row = (iota((cs, GK_WIDTH), 0) & 1) == 1
    mseg =jnp.where((iota((B_WIDTH, B_WIDTH), 0) >> lg_dv) == (iota((B_WIDTH, B_WIDTH), 1) >> lg_dv),
                     1.0 / B_DV, 0.0).astype(BF16)
    gg = gg_ref[...]

    def scores(qt, kt):
        kbd = jnp.where(kd_mask, jnp.concatenate([kt.astype(BF16)] * B_HEADS, axis=0), 0.0)
        return _dot_nt(qt.astype(BF16), kbd)

    def level_ref(g, bit):
        if bit == 0:
            return jnp.where(odd_row, pltpu.roll(g, 1, 0), g)
        half = 1 << bit
        return jnp.concatenate([jnp.broadcast_to(g[p + half - 1:p + half, :], (2 * half, GK_WIDTH))
                                for p in range(0, cs, 2 * half)], axis=0)

    def body(c, carry):
        rows = pl.ds(pl.multiple_of(c * cs, cs), cs)
        seqs = range(nseq)
        q = [q_ref[b, rows, :] for b in seqs]
        k = [k_ref[b, rows, :] for b in seqs]
        g = [g_ref[b, rows, :] for b in seqs]
        att = [jnp.where(level == lg_cs, scores(q[b], k[b]), 0.0) for b in seqs]
        for bit in range(lg_cs):
            for b in seqs:
                decay = jnp.exp(-jnp.abs(g[b] - level_ref(g[b], bit)))
                att[b] = jnp.where(level == bit, scores(q[b] * decay, k[b] * decay), att[b])
        for b in seqs:
            g_end = g[b][cs - 1:cs, :]
            vb = v_ref[b, rows, :].astype(BF16)
            vbd = jnp.where(vd_mask, jnp.concatenate([vb] * B_HEADS, axis=0), 0.0)
            st = st_scr[b]
            o = _dot(att[b].astype(BF16), vbd) + _dot_nt((q[b] * jnp.exp(g[b])).astype(BF16), st.astype(BF16))
            kh = (k[b] * jnp.exp(g_end - g[b])).astype(BF16)
            upd = _dot_tn(vb, kh)
            st_scr[b] = st * jnp.exp(g_end) + jnp.where(st_mask, upd, 0.0)
            on = o * lax.rsqrt(_seg_mean_sq(o, mseg) + LN_EPS)
            ob_ref[b, rows, :] = (on * gg * _silu(r_ref[b, rows, :])).astype(BF16)
        return carry

    lax.fori_loop(0, n_chunks, body, 0)

    @pl.when(pl.program_id(1) == pl.num_programs(1) - 1)
    def _():
        for b in range(nseq):
            st_ref[b] = st_scr[b]


GLA_SEQS = 4
GLA_ROWS = 512


def _gla_prompt(q, k, g, v, r, s0, gg, nbatch):
    t = q.shape[0] // nbatch
    nb = min(GLA_SEQS, nbatch)
    tr = min(GLA_ROWS, t)
    seq = lambda a: a.reshape(nbatch, t, a.shape[-1])
    blk = lambda width: pl.BlockSpec((nb, tr, width), lambda b, j: (b, j, 0))
    state = pl.BlockSpec((nb, B_WIDTH, GK_WIDTH), lambda b, j: (b, 0, 0))
    return pl.pallas_call(
        functools.partial(_gla_kernel, n_chunks=tr // GLA_CHUNK),
        grid=(nbatch // nb, t // tr),
        in_specs=[blk(GK_WIDTH), blk(GK_WIDTH), blk(GK_WIDTH), blk(B_WIDTH), blk(B_WIDTH),
                  pl.BlockSpec((1, B_WIDTH, GK_WIDTH), lambda b, j: (0, 0, 0)),
                  pl.BlockSpec((1, B_WIDTH), lambda b, j: (0, 0))],
        out_specs=[blk(B_WIDTH), state],
        out_shape=[jax.ShapeDtypeStruct((nbatch, t, B_WIDTH), BF16),
                   jax.ShapeDtypeStruct((nbatch, B_WIDTH, GK_WIDTH), F32)],
        scratch_shapes=[pltpu.VMEM((nb, B_WIDTH, GK_WIDTH), F32)],
        compiler_params=_params("parallel", "arbitrary"),
        name="gla_prompt",
    )(seq(q), seq(k), seq(g), seq(v), seq(r), s0, gg)


def _gla_sample_kernel(q_ref, k_ref, la_ref, v_ref, r_ref, s_ref, gg_ref, ob_ref, sn_ref):
    qt = q_ref[...].T
    kt = k_ref[...].T
    at = jnp.exp(la_ref[...]).T
    vt = v_ref[...].T
    rt = r_ref[...].T
    gg = gg_ref[...]
    outs = []
    for h in range(B_HEADS):
        vh = vt[h * B_DV:(h + 1) * B_DV, :]
        o = jnp.zeros_like(vh)
        for d in range(B_DK):
            i = h * B_DK + d
            s_new = at[i:i + 1, :] * s_ref[i] + kt[i:i + 1, :] * vh
            sn_ref[i] = s_new
            o = o + qt[i:i + 1, :] * s_new
        on = o * lax.rsqrt(jnp.mean(o * o, axis=0, keepdims=True) + LN_EPS)
        outs.append(on * gg * _silu(rt[h * B_DV:(h + 1) * B_DV, :]))
    ob_ref[...] = jnp.concatenate(outs, axis=0).T.astype(BF16)


def _gla_sample(q, k, la, v, r, state_t, gg_col, layer):
    ns = state_t.shape[3]
    row = lambda width: pl.BlockSpec((ns, width), lambda i: (BLOCK // ns, 0))
    return pl.pallas_call(
        _gla_sample_kernel,
        grid=(1,),
        in_specs=[row(GK_WIDTH), row(GK_WIDTH), row(GK_WIDTH), row(B_WIDTH), row(B_WIDTH),
                  pl.BlockSpec((None, GK_WIDTH, B_DV, ns), lambda i: (layer, 0, 0, 0)),
                  pl.BlockSpec((B_DV, 1), lambda i: (0, 0))],
        out_specs=[pl.BlockSpec((ns, B_WIDTH), lambda i: (0, 0)),
                   pl.BlockSpec((GK_WIDTH, B_DV, ns), lambda i: (0, 0, 0))],
        out_shape=[jax.ShapeDtypeStruct((ns, B_WIDTH), BF16),
                   jax.ShapeDtypeStruct((GK_WIDTH, B_DV, ns), F32)],
        compiler_params=_params("arbitrary"),
        name="gla_sample",
    )(q, k, la, v, r, state_t, gg_col)


CONV_HIST = 32
CONV_CHUNK = 64
CONV_GROUP = 2


def _conv_post(y, dwb, lng, lnb, pww, pwb):
    y = _silu(_ln_rows(y + dwb, lng, lnb))
    return _dot(y.astype(BF16), pww) + pwb


def _conv_kernel(u_ref, hist_ref, dww_ref, dwb_ref, lng_ref, lnb_ref, pww_ref, pwb_ref,
                 oc_ref, tail_ref, ubuf, *, n_chunks):
    t = u_ref.shape[1]
    ubuf[0, 0:CONV_HIST, :] = hist_ref[...]
    ubuf[0, CONV_HIST:CONV_HIST + t, :] = u_ref[0]
    off = CONV_HIST - (CONV_W - 1)
    n_copy = t + CONV_HIST - SUBLANES
    for s in range(1, SUBLANES):
        ubuf[s, 0:n_copy, :] = ubuf[0, s:s + n_copy, :]

    group = CONV_GROUP if n_chunks % CONV_GROUP == 0 else 1

    def body(c, carry):
        starts = [pl.multiple_of((c * group + i) * CONV_CHUNK, CONV_CHUNK) for i in range(group)]
        accs = [jnp.zeros((CONV_CHUNK, C_WIDTH), F32) for _ in starts]
        for j in range(CONV_W):
            a, s = divmod(j + off, SUBLANES)
            w_j = dww_ref[j:j + 1, :]
            for i, r0 in enumerate(starts):
                rows = pl.ds(pl.multiple_of(r0 + a * SUBLANES, SUBLANES), CONV_CHUNK)
                accs[i] = accs[i] + ubuf[s, rows, :] * w_j
        for r0, acc in zip(starts, accs):
            oc = _conv_post(acc, dwb_ref[...], lng_ref[...], lnb_ref[...], pww_ref[...], pwb_ref[...])
            oc_ref[0, pl.ds(r0, CONV_CHUNK), :] = oc.astype(BF16)
        return carry

    lax.fori_loop(0, n_chunks // group, body, 0)
    tail_ref[0] = ubuf[0, t:t + CONV_HIST, :]


def _conv_prompt(u, hist, dww, dwb, lng, lnb, pww, pwb, nbatch):
    t = u.shape[0] // nbatch
    const = lambda shape: pl.BlockSpec(shape, lambda b: (0,) * len(shape))
    return pl.pallas_call(
        functools.partial(_conv_kernel, n_chunks=t // CONV_CHUNK),
        grid=(nbatch,),
        in_specs=[pl.BlockSpec((1, t, C_WIDTH), lambda b: (b, 0, 0)), const((CONV_HIST, C_WIDTH)),
                  const((CONV_HIST, C_WIDTH)), const((1, C_WIDTH)), const((1, C_WIDTH)), const((1, C_WIDTH)),
                  const((C_WIDTH, C_WIDTH)), const((1, C_WIDTH))],
        out_specs=[pl.BlockSpec((1, t, C_WIDTH), lambda b: (b, 0, 0)),
                   pl.BlockSpec((1, CONV_HIST, C_WIDTH), lambda b: (b, 0, 0))],
        out_shape=[jax.ShapeDtypeStruct((nbatch, t, C_WIDTH), BF16),
                   jax.ShapeDtypeStruct((nbatch, CONV_HIST, C_WIDTH), F32)],
        scratch_shapes=[pltpu.VMEM((SUBLANES, CONV_HIST + t, C_WIDTH), F32)],
        compiler_params=_params("parallel"),
        name="conv_prompt",
    )(u.reshape(nbatch, t, C_WIDTH), hist, dww, dwb, lng, lnb, pww, pwb)


def _conv_sample_kernel(hist_ref, u_ref, dww_ref, dwb_ref, lng_ref, lnb_ref, pww_ref, pwb_ref, oc_ref, tail_ref):
    nh = CONV_W - 1
    u = u_ref[...]
    y = u * dww_ref[nh:nh + 1, :]
    for j in range(nh):
        y = y + hist_ref[j] * dww_ref[j:j + 1, :]
    oc = _conv_post(y, dwb_ref[...], lng_ref[...], lnb_ref[...], pww_ref[...], pwb_ref[...])
    oc_ref[...] = oc.astype(BF16)
    for j in range(nh - 1):
        tail_ref[j] = hist_ref[j + 1]
    tail_ref[nh - 1] = u


def _conv_sample(hist_t, u, dww, dwb, lng, lnb, pww, pwb, layer):
    ns = hist_t.shape[2]
    nh = CONV_W - 1
    const = lambda shape: pl.BlockSpec(shape, lambda i: (0,) * len(shape))
    return pl.pallas_call(
        _conv_sample_kernel,
        grid=(1,),
        in_specs=[pl.BlockSpec((None, nh, ns, C_WIDTH), lambda i: (layer, 0, 0, 0)),
                  pl.BlockSpec((ns, C_WIDTH), lambda i: (BLOCK // ns, 0)),
                  const((CONV_HIST, C_WIDTH)), const((1, C_WIDTH)), const((1, C_WIDTH)), const((1, C_WIDTH)),
                  const((C_WIDTH, C_WIDTH)), const((1, C_WIDTH))],
        out_specs=[const((ns, C_WIDTH)), const((nh, ns, C_WIDTH))],
        out_shape=[jax.ShapeDtypeStruct((ns, C_WIDTH), BF16),
                   jax.ShapeDtypeStruct((nh, ns, C_WIDTH), F32)],
        compiler_params=_params("arbitrary"),
        name="conv_sample",
    )(hist_t, u, dww, dwb, lng, lnb, pww, pwb)


ROUTE_LANE0 = N_GROUPS


def _route(x, wr2, br):
    x_hi = x.astype(BF16)
    x_lo = (x - x_hi.astype(F32)).astype(BF16)
    l_hi = _dot(x_hi, wr2)
    logits = l_hi[:, :LANES] + l_hi[:, LANES:] + _dot(x_lo, wr2)[:, :LANES] + br
    lane = lax.broadcasted_iota(jnp.int32, logits.shape, 1).astype(F32)
    far = 1e3
    glm = jnp.where(lane < N_GROUPS, logits, NEG)
    gmax = jnp.max(glm, axis=1, keepdims=True)
    gi = jnp.min(jnp.where(glm == gmax, lane, far), axis=1, keepdims=True)
    p_grp = 1.0 / jnp.sum(jnp.exp(glm - gmax), axis=1, keepdims=True)
    lo = ROUTE_LANE0 + EXP_PER_GROUP * gi
    in_sel = (lane >= lo) & (lane < lo + EXP_PER_GROUP)
    elm = jnp.where(in_sel, logits, NEG)
    v1 = jnp.max(elm, axis=1, keepdims=True)
    i1 = jnp.min(jnp.where(elm == v1, lane, far), axis=1, keepdims=True)
    elm2 = jnp.where(lane == i1, NEG, elm)
    v2 = jnp.max(elm2, axis=1, keepdims=True)
    i2 = jnp.min(jnp.where((elm2 == v2) & in_sel & (lane != i1), lane, far), axis=1, keepdims=True)
    t = jnp.exp(v2 - v1)
    w1 = p_grp / (1.0 + t)
    w2 = w1 * t
    return jnp.where(lane == i1, w1, 0.0) + jnp.where(lane == i2, w2, 0.0), gi


FFN_CHUNK = 128
MOE_ROWS = 128
MOE_SINGLE = (128, 144, 176)
MOE_ALIGN = 16
GROUP_WIDTH = EXP_PER_GROUP * D_EXPERT


def _moe_sorted_rows(tm):
    need = tm + N_GROUPS * MOE_ALIGN + max(MOE_SINGLE + (MOE_ROWS,))
    return -(-need // LANES) * LANES


def _ffn_kernel(*refs, pre_ln):
    if pre_ln:
        lg_ref, lb_ref, *refs = refs
    (oa_ref, ob_ref, oc_ref, h_ref, wo_ref, g1_ref, b1_ref, wr_ref, br_ref, wg_ref, wu_ref, wd_ref,
     g_ref, b_ref, o_ref, x_ref, xs_ref, ys_ref, ws_ref) = refs
    tm = h_ref.shape[0]
    ns = xs_ref.shape[0]
    iota = lambda shape, axis: lax.broadcasted_iota(jnp.int32, shape, axis)
    ch = FFN_CHUNK if tm % FFN_CHUNK == 0 else tm
    chunks = [slice(r, r + ch) for r in range(0, tm, ch)]
    mixes = [(_dot(oa_ref[r, :], wo_ref[0:A_WIDTH, :])
              + _dot(ob_ref[r, :], wo_ref[A_WIDTH:A_WIDTH + B_WIDTH, :])
              + _dot(oc_ref[r, :], wo_ref[A_WIDTH + B_WIDTH:D_MODEL, :])) for r in chunks]
    xs_rows = []
    for r, mix in zip(chunks, mixes):
        h = h_ref[r, :]
        if pre_ln:
            h = _ln_rows(h, lg_ref[...], lb_ref[...])
        x_r = _ln_rows(ALPHA * h + mix, g1_ref[...], b1_ref[...])
        x_ref[r, :] = x_r
        xs_rows.append(x_r)
    routed = [_route(x_r, wr_ref[...], br_ref[...]) for x_r in xs_rows]
    x = jnp.concatenate(xs_rows, axis=0)
    dw = jnp.concatenate([d for d, _ in routed], axis=0)
    gi = jnp.concatenate([i for _, i in routed], axis=0)
    lane_f = iota((tm, LANES), 1).astype(F32)
    onehot = jnp.where(lane_f == gi, 1.0, 0.0)
    tri = jnp.where(iota((ch, ch), 1) <= iota((ch, ch), 0), 1.0, 0.0).astype(BF16)
    onehot_b = onehot.astype(BF16)
    carry = jnp.zeros((1, LANES), F32)
    cums = []
    for r in chunks:
        cums.append(_dot(tri, onehot_b[r]) + carry)
        carry = cums[-1][ch - 1:ch, :]
    cum = jnp.concatenate(cums, axis=0)
    rank = jnp.sum(onehot * (cum - 1.0), axis=1, keepdims=True)
    counts = cum[tm - 1:tm, :]

    starts, tiles, counts_i = [], [], []
    start = jnp.int32(0)
    for grp in range(N_GROUPS):
        n_g = counts[0, grp].astype(jnp.int32)
        counts_i.append(n_g)
        starts.append(start)
        tiles.append(sum((n_g > k * MOE_ROWS).astype(jnp.int32) for k in range(-(-tm // MOE_ROWS))))
        start = start + ((n_g + (MOE_ALIGN - 1)) & -MOE_ALIGN)
    lane1 = iota((1, LANES), 1)
    start_v = jnp.zeros((1, LANES), F32)
    for grp in range(N_GROUPS):
        start_v = jnp.where(lane1 == grp, starts[grp].astype(F32), start_v)
    pos = jnp.sum(onehot * start_v, axis=1, keepdims=True) + rank

    pos_i = pos.astype(jnp.int32)
    digits = jnp.where(lane_f == 0.0, (pos_i >> 5).astype(F32),
                       jnp.where(lane_f == 1.0, (pos_i & 31).astype(F32), 0.0)).astype(BF16)
    lane8 = iota((8, LANES), 1)
    radix = jnp.where(lane8 == 0, 32.0, jnp.where(lane8 == 1, 1.0, 0.0)).astype(BF16)
    pos_row = _dot_nt(radix, digits)[0:1, :]

    perm = jnp.where(iota((ns, tm), 0).astype(F32) == pos_row, 1.0, 0.0).astype(BF16)
    dw_hi = dw.astype(BF16)
    dw_lo = (dw - dw_hi.astype(F32)).astype(BF16)
    srt = _dot(perm, jnp.concatenate([x.astype(BF16), dw_hi, dw_lo], axis=1))
    xs_ref[...] = srt[:, :D_MODEL].astype(BF16)
    ws_ref[...] = srt[:, D_MODEL:D_MODEL + LANES] + srt[:, D_MODEL + LANES:]
    ys_ref[...] = jnp.zeros_like(ys_ref)

    def expert_tile(grp, r0, n_rows):
        rows = pl.ds(pl.multiple_of(r0, MOE_ALIGN), n_rows)
        xt = xs_ref[rows, :]
        w = ws_ref[rows, :]
        hes = []
        for e in range(EXP_PER_GROUP):
            ex = grp * EXP_PER_GROUP + e
            w_e = w[:, ROUTE_LANE0 + ex:ROUTE_LANE0 + ex + 1]
            hes.append((_silu(_dot(xt, wg_ref[ex])) * _dot(xt, wu_ref[ex]) * w_e).astype(BF16))
        ys_ref[rows, :] = _dot(jnp.concatenate(hes, axis=1), wd_ref[grp]).astype(BF16)

    for grp in range(N_GROUPS):
        n_g, lo = counts_i[grp], 0
        for size in MOE_SINGLE:
            pl.when((n_g > lo) & (n_g <= size))(functools.partial(expert_tile, grp, starts[grp], size))
            lo = size

        @pl.when(n_g > lo)
        def _(grp=grp):
            lax.fori_loop(0, tiles[grp],
                          lambda k, c: (expert_tile(grp, starts[grp] + k * MOE_ROWS, MOE_ROWS), c)[1], 0)

    lane_ns = iota((ch, ns), 1).astype(F32)
    ys = ys_ref[...]
    y_rows = [_dot(jnp.where(lane_ns == pos[r], 1.0, 0.0).astype(BF16), ys) for r in chunks]
    for r, y in zip(chunks, y_rows):
        o_ref[r, :] = _ln_rows(ALPHA * x_ref[r, :] + y, g_ref[...], b_ref[...])


def _ffn(oa, ob, oc, h, wo, g1, b1, wr2, br, wg, wu, wd, g2, b2, tm, layer, ln=None):
    n = h.shape[0]
    ns = _moe_sorted_rows(tm)
    row = lambda width: pl.BlockSpec((tm, width), lambda i: (i, 0))
    const = lambda shape: pl.BlockSpec(shape, lambda i: (0,) * len(shape))
    resident = lambda shape: pl.BlockSpec((None,) + shape, lambda i: (layer,) + (0,) * len(shape),
                                          pipeline_mode=pl.Buffered(1))
    ln_specs = [const((1, D_MODEL)), const((1, D_MODEL))] if ln else []
    return pl.pallas_call(
        functools.partial(_ffn_kernel, pre_ln=bool(ln)),
        grid=(n // tm,),
        in_specs=ln_specs + [row(A_WIDTH), row(B_WIDTH), row(C_WIDTH), row(D_MODEL),
                             resident((D_MODEL, D_MODEL)), const((1, D_MODEL)), const((1, D_MODEL)),
                             resident((D_MODEL, 2 * LANES)), const((1, LANES)),
                             resident((N_EXPERTS, D_MODEL, D_EXPERT)), resident((N_EXPERTS, D_MODEL, D_EXPERT)),
                             resident((N_GROUPS, GROUP_WIDTH, D_MODEL)),
                             const((1, D_MODEL)), const((1, D_MODEL))],
        out_specs=row(D_MODEL),
        out_shape=jax.ShapeDtypeStruct((n, D_MODEL), F32),
        scratch_shapes=[pltpu.VMEM((tm, D_MODEL), F32), pltpu.VMEM((ns, D_MODEL), BF16),
                        pltpu.VMEM((ns, D_MODEL), BF16), pltpu.VMEM((ns, LANES), F32)],
        compiler_params=_params("parallel"),
        name="ffn",
    )(*(ln or ()), oa, ob, oc, h, wo, g1, b1, wr2, br, wg, wu, wd, g2, b2)


def _rope_tables(pos):
    half = HEAD_DIM // 2
    inv = ROPE_THETA ** (-jnp.arange(half, dtype=F32) / half)
    ang = pos.astype(F32)[:, None] * inv[None, :]
    cos, sin = jnp.cos(ang), jnp.sin(ang)
    cos_t = jnp.concatenate([cos, cos] * (LANES // HEAD_DIM), axis=1)
    sin_t = jnp.concatenate([-sin, sin] * (LANES // HEAD_DIM), axis=1)
    return cos_t, sin_t


def _row(v):
    return v.reshape(1, -1)


def kernel(x_prompt, x_sample, cache_meta_k, cache_meta_v, cache_win_k, cache_win_v, state_gla, state_conv,
           meta_tokens, ln_in_g, ln_in_b, w_in, attn_sink, w_alpha, b_alpha, gla_norm_g,
           conv_dw_w, conv_dw_b, conv_ln_g, conv_ln_b, conv_pw_w, conv_pw_b, w_out, ln1_g, ln1_b,
           w_router_group, b_router_group, w_router_expert, b_router_expert,
           w_exp_gate, w_exp_up, w_exp_down, ln2_g, ln2_b):
    nb, seq, d = x_prompt.shape
    ns = x_sample.shape[0]
    nwin = cache_win_k.shape[2]
    n_big = nb * seq
    n_small = BLOCK + ns
    tm_big = min(512, seq)

    small_in = jnp.concatenate([jnp.zeros((META_PAD, d), F32), meta_tokens.astype(F32),
                                x_sample.reshape(ns, d)], axis=0)
    hb = x_prompt.reshape(n_big, d)
    hs = small_in
    ln_in = (_row(ln_in_g), _row(ln_in_b))

    cos_b, sin_b = _rope_tables(N_META + jnp.arange(seq))
    pos_small = jnp.concatenate([jnp.maximum(jnp.arange(BLOCK) - META_PAD, 0),
                                 jnp.full((ns,), PAST_LEN, jnp.int32)])
    cos_s, sin_s = _rope_tables(pos_small)

    outs = [[] for _ in range(12)]
    zeros_hist = jnp.zeros((CONV_HIST, C_WIDTH), F32)
    zeros_state = jnp.zeros((1, B_WIDTH, GK_WIDTH), F32)

    nl = w_in.shape[0]
    w_pad = jnp.concatenate([w_in[:, :, :C_AB + B_RANK], jnp.zeros((nl, d, LANES - B_RANK), F32),
                             w_in[:, :, C_AB + B_RANK:]], axis=2).astype(BF16)
    wa_all = jnp.concatenate([w_alpha, jnp.zeros((nl, LANES - B_RANK, GK_WIDTH), F32)], axis=1).astype(BF16)
    ba_all = b_alpha[:, None, :]
    wo_all = w_out.astype(BF16)
    wr = jnp.concatenate([w_router_group, w_router_expert,
                          jnp.zeros((nl, d, LANES - N_GROUPS - N_EXPERTS), F32)], axis=2)
    wr_hi = wr.astype(BF16)
    wr2_all = jnp.concatenate([wr_hi, (wr - wr_hi.astype(F32)).astype(BF16)], axis=2)
    wg_all, wu_all = w_exp_gate.astype(BF16), w_exp_up.astype(BF16)
    wd_all = w_exp_down.astype(BF16).reshape(nl, N_GROUPS, GROUP_WIDTH, d)
    win_k_t = cache_win_k.transpose(0, 1, 3, 4, 2).reshape(nl, ns, KV_WIDTH, nwin)
    win_v_t = cache_win_v.transpose(0, 1, 3, 4, 2).reshape(nl, ns, KV_WIDTH, nwin)
    gla_t = state_gla.transpose(0, 2, 3, 4, 1).reshape(nl, GK_WIDTH, B_DV, ns)
    conv_t = state_conv.transpose(0, 2, 1, 3)

    for l in range(DEPTH):
        sink = attn_sink[l].astype(F32)
        sink_col = jnp.concatenate([sink, jnp.zeros((HEAD_ROWS - A_HEADS,), F32)])[:, None]
        gg_t = _row(jnp.tile(gla_norm_g[l], B_HEADS))
        gg_col = gla_norm_g[l][:, None]
        dww = jnp.concatenate([conv_dw_w[l], jnp.zeros((CONV_HIST - CONV_W, C_WIDTH), F32)], axis=0)
        dwb, clg, clb = _row(conv_dw_b[l]), _row(conv_ln_g[l]), _row(conv_ln_b[l])
        pww, pwb = conv_pw_w[l].astype(BF16), _row(conv_pw_b[l])
        br = _row(jnp.concatenate([b_router_group[l], b_router_expert[l],
                                   jnp.zeros((LANES - N_GROUPS - N_EXPERTS,), F32)]))

        ln = ln_in if l == 0 else None
        qa_b, ka_b, va_b, qg_b, kg_b, la_b, vg_b, rg_b, u_b = _proj_in(
            hb, w_pad, wa_all, ba_all, cos_b, sin_b, layer=l, tm=tm_big, n_pad=0, n_seq=tm_big, ln=ln)
        qa_s, ka_s, va_s, qg_s, kg_s, la_s, vg_s, rg_s, u_s = _proj_in(
            hs, w_pad, wa_all, ba_all, cos_s, sin_s, layer=l, tm=n_small, n_pad=META_PAD, n_seq=BLOCK, ln=ln)

        oa_b = _attn_prompt(sink, qa_b, ka_b, va_b, ka_s, va_s, nb)
        oa_m = _attn_meta(sink, qa_s, ka_s, va_s)
        q_smp = qa_s[BLOCK:].reshape(ns, A_KV_HEADS, A_GROUP, 1, HEAD_DIM)
        eye = jnp.eye(A_KV_HEADS, dtype=BF16)[None, :, None, :, None]
        q_bd = (q_smp * eye).reshape(ns, A_HEADS, KV_WIDTH)
        q_bd = jnp.pad(q_bd, ((0, 0), (0, HEAD_ROWS - A_HEADS), (0, 0)))
        o_bd, nwk_t, nwv_t = _attn_sample(
            sink_col, q_bd, ka_s, va_s,
            cache_meta_k[l].reshape(ns, N_META, KV_WIDTH), cache_meta_v[l].reshape(ns, N_META, KV_WIDTH),
            win_k_t, win_v_t, l)
        o_bd = o_bd[:, :A_HEADS].reshape(ns, A_KV_HEADS, A_GROUP, A_KV_HEADS, HEAD_DIM)
        oa_smp = jnp.stack([o_bd[:, c, :, c, :] for c in range(A_KV_HEADS)], axis=1).reshape(ns, A_WIDTH)
        oa_s = jnp.concatenate([oa_m, oa_smp.astype(BF16)], axis=0)

        ob_m, st_m = _gla_prompt(qg_s[:BLOCK], kg_s[:BLOCK], la_s[:BLOCK], vg_s[:BLOCK], rg_s[:BLOCK],
                                 zeros_state, gg_t, 1)
        ob_b, st_b = _gla_prompt(qg_b, kg_b, la_b, vg_b, rg_b, st_m, gg_t, nb)
        ob_smp, s_new_t = _gla_sample(qg_s, kg_s, la_s, vg_s, rg_s, gla_t, gg_col, l)
        ob_s = jnp.concatenate([ob_m.reshape(BLOCK, B_WIDTH), ob_smp], axis=0)
        st5 = st_b.reshape(nb, B_HEADS, B_DV, B_HEADS, B_DK)
        gla_p = jnp.stack([st5[:, h, :, h, :] for h in range(B_HEADS)], axis=1).transpose(0, 1, 3, 2)

        oc_m, _ = _conv_prompt(u_s[:BLOCK], zeros_hist, dww, dwb, clg, clb, pww, pwb, 1)
        oc_b, tail_b = _conv_prompt(u_b, u_s[BLOCK - CONV_HIST:BLOCK], dww, dwb, clg, clb, pww, pwb, nb)
        oc_smp, tail_s_t = _conv_sample(conv_t, u_s, dww, dwb, clg, clb, pww, pwb, l)
        oc_s = jnp.concatenate([oc_m.reshape(BLOCK, C_WIDTH), oc_smp], axis=0)

        l1g, l1b, l2g, l2b = _row(ln1_g[l]), _row(ln1_b[l]), _row(ln2_g[l]), _row(ln2_b[l])
        hb = _ffn(oa_b, ob_b.reshape(n_big, B_WIDTH), oc_b.reshape(n_big, C_WIDTH), hb, wo_all, l1g, l1b,
                  wr2_all, br, wg_all, wu_all, wd_all, l2g, l2b, tm_big, l, ln=ln)
        hs = _ffn(oa_s, ob_s, oc_s, hs, wo_all, l1g, l1b, wr2_all, br, wg_all, wu_all, wd_all, l2g, l2b,
                  n_small, l, ln=ln)

        kv4 = lambda a: a.reshape(a.shape[0], a.shape[1], A_KV_HEADS, HEAD_DIM)
        win_t = lambda a: a.reshape(ns, A_KV_HEADS, HEAD_DIM, nwin).transpose(0, 3, 1, 2)
        meta_k = jnp.broadcast_to(ka_s[META_PAD:BLOCK][None], (nb, N_META, KV_WIDTH))
        meta_v = jnp.broadcast_to(va_s[META_PAD:BLOCK][None], (nb, N_META, KV_WIDTH))
        win_k = ka_b.reshape(nb, seq, KV_WIDTH)[:, seq - nwin:]
        win_v = va_b.reshape(nb, seq, KV_WIDTH)[:, seq - nwin:]
        layer_out = (None, None, kv4(meta_k), kv4(meta_v), kv4(win_k), kv4(win_v), win_t(nwk_t), win_t(nwv_t),
                     gla_p, s_new_t.reshape(B_HEADS, B_DK, B_DV, ns).transpose(3, 0, 1, 2),
                     tail_b[:, CONV_HIST - (CONV_W - 1):], tail_s_t.transpose(1, 0, 2))
        for i in range(2, 12):
            outs[i].append(layer_out[i])

    y_prompt = hb.reshape(nb, seq, d)
    y_sample = hs[BLOCK:].reshape(ns, 1, d)
    return (y_prompt, y_sample) + tuple(jnp.stack(o) for o in outs[2:])
```

```python
import functools

import jax
import jax.numpy as jnp
from jax import lax
from jax.experimental import pallas as pl
from jax.experimental.pallas import tpu as pltpu

F32 = jnp.float32
BF16 = jnp.bfloat16

D_MODEL = 1024
DEPTH = 2
PAST_LEN = 16384
N_META = 16
HEAD_DIM = 64
A_WIDTH = 512
A_HEADS = 8
A_KV_HEADS = 2
A_GROUP = 4
WINDOW = 128
BLOCK = 128
ROPE_THETA = 10000.0
B_WIDTH = 256
B_HEADS = 4
B_DV = 64
B_DK = 32
B_RANK = 16
GATE_TAU = 16.0
GLA_CHUNK = 64
C_WIDTH = 256
CONV_W = 31
N_GROUPS = 4
EXP_PER_GROUP = 4
N_EXPERTS = 16
D_EXPERT = 256
ALPHA = (2 * DEPTH) ** 0.25
LN_EPS = 1e-5

LANES = 128
SUBLANES = 8
META_PAD = BLOCK - N_META
KV_WIDTH = A_KV_HEADS * HEAD_DIM
GK_WIDTH = B_HEADS * B_DK
C_QA = 0
C_KA = C_QA + A_WIDTH
C_VA = C_KA + KV_WIDTH
C_QB = C_VA + KV_WIDTH
C_KB = C_QB + GK_WIDTH
C_VB = C_KB + GK_WIDTH
C_RB = C_VB + B_WIDTH
C_AB = C_RB + B_WIDTH
C_CG = C_AB + LANES
PROJ_PAD_WIDTH = C_CG + 2 * C_WIDTH
NEG = -1e30
VMEM_LIMIT = 56 * 1024 * 1024


def _dot(a, b):
    return jnp.dot(a, b, preferred_element_type=F32)


def _dot_nt(a, b):
    return lax.dot_general(a, b, (((1,), (1,)), ((), ())), preferred_element_type=F32)


def _dot_tn(a, b):
    return lax.dot_general(a, b, (((0,), (0,)), ((), ())), preferred_element_type=F32)


def _ln_rows(x, g, b):
    xc = x - jnp.mean(x, -1, keepdims=True)
    var = jnp.mean(xc * xc, -1, keepdims=True)
    return xc * lax.rsqrt(var + LN_EPS) * g + b


def _silu(x):
    return x * jax.nn.sigmoid(x)


def _split3(x):
    hi = x.astype(BF16)
    r1 = x - hi.astype(F32)
    mid = r1.astype(BF16)
    lo = (r1 - mid.astype(F32)).astype(BF16)
    return hi, mid, lo


def _params(*sem):
    return pltpu.CompilerParams(dimension_semantics=sem, vmem_limit_bytes=VMEM_LIMIT)


def _proj_in_kernel(*refs, n_pad, n_seq, pre_ln):
    if pre_ln:
        lg_ref, lb_ref, *refs = refs
    (x_ref, w_ref, wa_ref, ba_ref, cos_ref, sin_ref,
     qa_ref, ka_ref, va_ref, qg_ref, kg_ref, la_ref, vg_ref, rg_ref, u_ref) = refs
    x = x_ref[...]
    if pre_ln:
        x = _ln_rows(x, lg_ref[...], lb_ref[...])
    xb = x.astype(BF16)
    tm = xb.shape[0]
    za = _dot(xb, w_ref[:, C_QA:C_QB])
    cos = cos_ref[...]
    sin = sin_ref[...]
    lane = lax.broadcasted_iota(jnp.int32, (tm, LANES), 1)
    first_half = (lane & (HEAD_DIM // 2)) == 0

    def rope(z):
        rot = jnp.where(first_half, pltpu.roll(z, LANES - HEAD_DIM // 2, 1), pltpu.roll(z, HEAD_DIM // 2, 1))
        return z * cos + rot * sin

    if n_pad:
        valid = (lax.broadcasted_iota(jnp.int32, (tm, 1), 0) >= n_pad).astype(F32)
    else:
        valid = None

    zb = _dot(xb, w_ref[:, C_QB:C_CG])
    zc = _dot(xb, w_ref[:, C_CG:PROJ_PAD_WIDTH])
    for c in range(A_WIDTH // LANES):
        zq = za[:, c * LANES:(c + 1) * LANES]
        qa_ref[:, c * LANES:(c + 1) * LANES] = (rope(zq) * (HEAD_DIM ** -0.5)).astype(BF16)
    ka_ref[...] = rope(za[:, C_KA:C_VA])
    va_ref[...] = za[:, C_VA:C_QB]

    o = C_QB
    qg_ref[...] = zb[:, C_QB - o:C_KB - o] * (B_DK ** -0.5)
    kg = zb[:, C_KB - o:C_VB - o]
    vg_ref[...] = zb[:, C_VB - o:C_RB - o]
    rg_ref[...] = zb[:, C_RB - o:C_AB - o]
    ab = zb[:, C_AB - o:C_CG - o].astype(BF16)
    xa = _dot(ab, wa_ref[...]) + ba_ref[...]
    la = (jnp.minimum(xa, 0.0) - jnp.log(1.0 + jnp.exp(-jnp.abs(xa)))) * (1.0 / GATE_TAU)

    u = zc[:, :C_WIDTH] * jax.nn.sigmoid(zc[:, C_WIDTH:])
    if valid is not None:
        kg = kg * valid
        la = la * valid
        u = u * valid
    kg_ref[...] = kg
    u_ref[...] = u
    cs = GLA_CHUNK
    tri = jnp.where(lax.broadcasted_iota(jnp.int32, (cs, cs), 1) <= lax.broadcasted_iota(jnp.int32, (cs, cs), 0),
                    1.0, 0.0).astype(BF16)
    la_h, la_m, la_l = _split3(la)
    for c in range(n_seq // cs):
        rows = slice(c * cs, (c + 1) * cs)
        la_ref[rows, :] = _dot(tri, la_h[rows]) + _dot(tri, la_m[rows]) + _dot(tri, la_l[rows])
    if n_seq < tm:
        la_ref[n_seq:tm, :] = la[n_seq:tm]


def _proj_in(h, w, wa, ba, cos, sin, *, layer, tm, n_pad, n_seq, ln=None):
    n = h.shape[0]
    tb = cos.shape[0] // tm
    row = lambda width: pl.BlockSpec((tm, width), lambda i: (i, 0))
    const = lambda shape: pl.BlockSpec(shape, lambda i: (0, 0))
    per_layer = lambda shape: pl.BlockSpec((None,) + shape, lambda i: (layer, 0, 0))
    tab = pl.BlockSpec((tm, LANES), lambda i: (i % tb, 0))
    widths = (A_WIDTH, KV_WIDTH, KV_WIDTH, GK_WIDTH, GK_WIDTH, GK_WIDTH, B_WIDTH, B_WIDTH, C_WIDTH)
    dtypes = (BF16,) + (F32,) * 8
    ln_specs = [const((1, D_MODEL)), const((1, D_MODEL))] if ln else []
    return pl.pallas_call(
        functools.partial(_proj_in_kernel, n_pad=n_pad, n_seq=n_seq, pre_ln=bool(ln)),
        grid=(n // tm,),
        in_specs=ln_specs + [row(D_MODEL), per_layer((D_MODEL, PROJ_PAD_WIDTH)), per_layer((LANES, GK_WIDTH)),
                             per_layer((1, GK_WIDTH)), tab, tab],
        out_specs=[row(wd) for wd in widths],
        out_shape=[jax.ShapeDtypeStruct((n, wd), dt) for wd, dt in zip(widths, dtypes)],
        compiler_params=_params("parallel"),
        name="proj_in",
    )(*(ln or ()), h, w, wa, ba, cos, sin)


def _attn_kernel(sink_ref, q_ref, km_ref, vm_ref, *rest, meta_mode):
    if meta_mode:
        (o_ref,) = rest
        blocks = [(0, q_ref[...], km_ref[...], vm_ref[...], None)]
    else:
        kp_ref, vp_ref, kc_ref, vc_ref, o_ref = rest
        km, vm = km_ref[...], vm_ref[...]
        blocks = []
        for sub in range(q_ref.shape[0] // BLOCK):
            cur = slice(sub * BLOCK, (sub + 1) * BLOCK)
            if sub == 0:
                kp, vp = kp_ref[...], vp_ref[...]
                has_prev = pl.program_id(1) >= 1
            else:
                prev = slice((sub - 1) * BLOCK, sub * BLOCK)
                kp, vp = kc_ref[prev, :], vc_ref[prev, :]
                has_prev = True
            blocks.append((sub * BLOCK, q_ref[cur, :], jnp.concatenate([kp, kc_ref[cur, :], km], axis=0),
                           jnp.concatenate([vp, vc_ref[cur, :], vm], axis=0), has_prev))

    units = []
    for row0, q, k_all, v_all, has_prev in blocks:
        nk = k_all.shape[0]
        ki = lax.broadcasted_iota(jnp.int32, (nk, BLOCK), 0)
        qi = lax.broadcasted_iota(jnp.int32, (nk, BLOCK), 1)
        if has_prev is None:
            ok = ki <= qi - META_PAD
        else:
            prev_lo = qi if has_prev is True else qi + jnp.where(has_prev, 0, BLOCK)
            ok = ((ki >= prev_lo) & (ki < BLOCK)) | ((ki >= BLOCK) & (ki <= qi + BLOCK)) | (ki >= 2 * BLOCK)
        bias = jnp.where(ok, 0.0, NEG)
        kb, vb = k_all.astype(BF16), v_all.astype(BF16)
        for kvh in range(A_KV_HEADS):
            lanes = slice(kvh * HEAD_DIM, (kvh + 1) * HEAD_DIM)
            qs = jnp.concatenate([q[:, (kvh * A_GROUP + g) * HEAD_DIM:(kvh * A_GROUP + g + 1) * HEAD_DIM]
                                  for g in range(A_GROUP)], axis=0)
            units.append((row0, kvh, bias, _dot_nt(kb[:, lanes], qs), vb[:, lanes]))

    outs = []
    for row0, kvh, bias, st, vv in units:
        ps, dens = [], []
        for g in range(A_GROUP):
            s = st[:, g * BLOCK:(g + 1) * BLOCK] + bias
            sink = sink_ref[kvh * A_GROUP + g]
            m = jnp.maximum(jnp.max(s, axis=0, keepdims=True), sink)
            p = jnp.exp(s - m)
            dens.append(jnp.sum(p, axis=0, keepdims=True) + jnp.exp(sink - m))
            ps.append(p.astype(BF16))
        outs.append((row0, kvh, _dot_tn(vv, jnp.concatenate(ps, axis=1)), jnp.concatenate(dens, axis=1)))

    for row0, kvh, ot, den in outs:
        ot = ot / den
        for pair in range(A_GROUP // 2):
            two = jnp.concatenate([ot[:, (2 * pair) * BLOCK:(2 * pair + 1) * BLOCK],
                                   ot[:, (2 * pair + 1) * BLOCK:(2 * pair + 2) * BLOCK]], axis=0)
            h0 = kvh * A_GROUP + 2 * pair
            o_ref[row0:row0 + BLOCK, h0 * HEAD_DIM:(h0 + 2) * HEAD_DIM] = two.T.astype(BF16)


ATTN_SUB = 8


def _attn_prompt(sink, q, k, v, k_small, v_small, nbatch):
    n = q.shape[0]
    nblk = n // BLOCK // nbatch
    sub = ATTN_SUB if nblk % ATTN_SUB == 0 else 1
    nstep = nblk // sub
    kvs = pl.BlockSpec((sub * BLOCK, KV_WIDTH), lambda b, j: (b * nstep + j, 0))
    kvp = pl.BlockSpec((BLOCK, KV_WIDTH), lambda b, j: (b * nblk + jnp.maximum(j * sub - 1, 0), 0))
    kvm = pl.BlockSpec((N_META, KV_WIDTH), lambda b, j: (META_PAD // N_META, 0))
    return pl.pallas_call(
        functools.partial(_attn_kernel, meta_mode=False),
        grid=(nbatch, nstep),
        in_specs=[pl.BlockSpec(memory_space=pltpu.SMEM),
                  pl.BlockSpec((sub * BLOCK, A_WIDTH), lambda b, j: (b * nstep + j, 0)),
                  kvm, kvm, kvp, kvp, kvs, kvs],
        out_specs=pl.BlockSpec((sub * BLOCK, A_WIDTH), lambda b, j: (b * nstep + j, 0)),
        out_shape=jax.ShapeDtypeStruct((n, A_WIDTH), BF16),
        compiler_params=_params("parallel", "parallel"),
        name="attn_prompt",
    )(sink, q, k_small, v_small, k, v, k, v)


def _attn_meta(sink, q_small, k_small, v_small):
    kvm = pl.BlockSpec((N_META, KV_WIDTH), lambda i: (META_PAD // N_META, 0))
    return pl.pallas_call(
        functools.partial(_attn_kernel, meta_mode=True),
        grid=(1,),
        in_specs=[pl.BlockSpec(memory_space=pltpu.SMEM),
                  pl.BlockSpec((BLOCK, A_WIDTH), lambda i: (0, 0)), kvm, kvm],
        out_specs=pl.BlockSpec((BLOCK, A_WIDTH), lambda i: (0, 0)),
        out_shape=jax.ShapeDtypeStruct((BLOCK, A_WIDTH), BF16),
        compiler_params=_params("arbitrary"),
        name="attn_meta",
    )(sink, q_small, k_small, v_small)


SAMPLE_BLOCK = 32
HEAD_ROWS = 16


def _attn_sample_kernel(sink_ref, q_ref, kn_ref, vn_ref, mk_ref, mv_ref, wk_ref, wv_ref,
                        o_ref, nwk_ref, nwv_ref):
    sb = q_ref.shape[0]
    q = q_ref[...]
    kn = kn_ref[...]
    vn = vn_ref[...]
    s_win = jnp.einsum('bhl,blj->bhj', q, wk_ref[...].astype(BF16), preferred_element_type=F32)
    s_meta = jnp.einsum('bhl,bml->bhm', q, mk_ref[...].astype(BF16), preferred_element_type=F32)
    s_self = jnp.sum(q.astype(F32) * kn[:, None, :], axis=2, keepdims=True)
    sink = sink_ref[...][None]
    m = jnp.maximum(jnp.maximum(jnp.max(s_win, axis=2, keepdims=True), jnp.max(s_meta, axis=2, keepdims=True)),
                    jnp.maximum(s_self, sink))
    p_win = jnp.exp(s_win - m)
    p_meta = jnp.exp(s_meta - m)
    p_self = jnp.exp(s_self - m)
    denom = (jnp.sum(p_win, axis=2, keepdims=True) + jnp.sum(p_meta, axis=2, keepdims=True) + p_self
             + jnp.exp(sink - m))
    o = (jnp.einsum('bhj,blj->bhl', p_win.astype(BF16), wv_ref[...].astype(BF16), preferred_element_type=F32)
         + jnp.einsum('bhm,bml->bhl', p_meta.astype(BF16), mv_ref[...].astype(BF16), preferred_element_type=F32)
         + p_self * vn[:, None, :])
    o_ref[...] = o / denom
    nwin = wk_ref.shape[2]
    last = lax.broadcasted_iota(jnp.int32, (KV_WIDTH, nwin), 1) == nwin - 1
    knt = kn.T
    vnt = vn.T
    for j in range(sb):
        nwk_ref[j] = jnp.where(last, knt[:, j:j + 1], pltpu.roll(wk_ref[j], nwin - 1, 1))
        nwv_ref[j] = jnp.where(last, vnt[:, j:j + 1], pltpu.roll(wv_ref[j], nwin - 1, 1))


def _attn_sample(sink_col, q_bd, k_small, v_small, meta_k, meta_v, win_k_t, win_v_t, layer):
    ns = q_bd.shape[0]
    sb = min(SAMPLE_BLOCK, ns)
    nwin = win_k_t.shape[3]
    blk = lambda rows: pl.BlockSpec((sb, rows, KV_WIDTH), lambda i: (i, 0, 0))
    new = pl.BlockSpec((sb, KV_WIDTH), lambda i: (BLOCK // sb + i, 0))
    win = pl.BlockSpec((None, sb, KV_WIDTH, nwin), lambda i: (layer, i, 0, 0))
    win_out = pl.BlockSpec((sb, KV_WIDTH, nwin), lambda i: (i, 0, 0))
    return pl.pallas_call(
        _attn_sample_kernel,
        grid=(ns // sb,),
        in_specs=[pl.BlockSpec((HEAD_ROWS, 1), lambda i: (0, 0)),
                  blk(HEAD_ROWS), new, new, blk(N_META), blk(N_META), win, win],
        out_specs=[blk(HEAD_ROWS), win_out, win_out],
        out_shape=[jax.ShapeDtypeStruct((ns, HEAD_ROWS, KV_WIDTH), F32),
                   jax.ShapeDtypeStruct((ns, KV_WIDTH, nwin), F32),
                   jax.ShapeDtypeStruct((ns, KV_WIDTH, nwin), F32)],
        compiler_params=_params("parallel"),
        name="attn_sample",
    )(sink_col, q_bd, k_small, v_small, meta_k, meta_v, win_k_t, win_v_t)


def _seg_mean_sq(o, mseg):
    sq = o * o
    hi = sq.astype(BF16)
    lo = (sq - hi.astype(F32)).astype(BF16)
    return _dot(hi, mseg) + _dot(lo, mseg)


def _gla_kernel(q_ref, k_ref, g_ref, v_ref, r_ref, s0_ref, gg_ref, ob_ref, st_ref, st_scr, *, n_chunks):
    cs = GLA_CHUNK
    nseq = q_ref.shape[0]

    @pl.when(pl.program_id(1) == 0)
    def _():
        for b in range(nseq):
            st_scr[b] = s0_ref[0]

    iota = lambda shape, axis: lax.broadcasted_iota(jnp.int32, shape, axis)
    lg_cs, lg_dk, lg_dv = cs.bit_length() - 1, B_DK.bit_length() - 1, B_DV.bit_length() - 1
    kd_mask = (iota((B_HEADS * cs, GK_WIDTH), 0) >> lg_cs) == (iota((B_HEADS * cs, GK_WIDTH), 1) >> lg_dk)
    vd_mask = (iota((B_HEADS * cs, B_WIDTH), 0) >> lg_cs) == (iota((B_HEADS * cs, B_WIDTH), 1) >> lg_dv)
    st_mask = (iota((B_WIDTH, GK_WIDTH), 0) >> lg_dv) == (iota((B_WIDTH, GK_WIDTH), 1) >> lg_dk)
    t_idx = iota((cs, B_HEADS * cs), 0)
    s_idx = iota((cs, B_HEADS * cs), 1) & (cs - 1)
    diff = t_idx ^ s_idx
    level = jnp.full((cs, B_HEADS * cs), lg_cs, jnp.int32)
    for bit in range(lg_cs):
        level = jnp.where(diff >= (1 << bit), bit, level)
    level = jnp.where(s_idx > t_idx, -1, level)
    odd_row = (iota((cs, GK_WIDTH), 0) & 1) == 1
    mseg =jnp.where((iota((B_WIDTH, B_WIDTH), 0) >> lg_dv) == (iota((B_WIDTH, B_WIDTH), 1) >> lg_dv),
                     1.0 / B_DV, 0.0).astype(BF16)
    gg = gg_ref[...]

    def scores(qt, kt):
        kbd = jnp.where(kd_mask, jnp.concatenate([kt.astype(BF16)] * B_HEADS, axis=0), 0.0)
        return _dot_nt(qt.astype(BF16), kbd)

    def level_ref(g, bit):
        if bit == 0:
            return jnp.where(odd_row, pltpu.roll(g, 1, 0), g)
        half = 1 << bit
        return jnp.concatenate([jnp.broadcast_to(g[p + half - 1:p + half, :], (2 * half, GK_WIDTH))
                                for p in range(0, cs, 2 * half)], axis=0)

    def body(c, carry):
        rows = pl.ds(pl.multiple_of(c * cs, cs), cs)
        seqs = range(nseq)
        q = [q_ref[b, rows, :] for b in seqs]
        k = [k_ref[b, rows, :] for b in seqs]
        g = [g_ref[b, rows, :] for b in seqs]
        att = [jnp.where(level == lg_cs, scores(q[b], k[b]), 0.0) for b in seqs]
        for bit in range(lg_cs):
            for b in seqs:
                decay = jnp.exp(-jnp.abs(g[b] - level_ref(g[b], bit)))
                att[b] = jnp.where(level == bit, scores(q[b] * decay, k[b] * decay), att[b])
        for b in seqs:
            g_end = g[b][cs - 1:cs, :]
            vb = v_ref[b, rows, :].astype(BF16)
            vbd = jnp.where(vd_mask, jnp.concatenate([vb] * B_HEADS, axis=0), 0.0)
            st = st_scr[b]
            o = _dot(att[b].astype(BF16), vbd) + _dot_nt((q[b] * jnp.exp(g[b])).astype(BF16), st.astype(BF16))
            kh = (k[b] * jnp.exp(g_end - g[b])).astype(BF16)
            upd = _dot_tn(vb, kh)
            st_scr[b] = st * jnp.exp(g_end) + jnp.where(st_mask, upd, 0.0)
            on = o * lax.rsqrt(_seg_mean_sq(o, mseg) + LN_EPS)
            ob_ref[b, rows, :] = (on * gg * _silu(r_ref[b, rows, :])).astype(BF16)
        return carry

    lax.fori_loop(0, n_chunks, body, 0)

    @pl.when(pl.program_id(1) == pl.num_programs(1) - 1)
    def _():
        for b in range(nseq):
            st_ref[b] = st_scr[b]


GLA_SEQS = 8
GLA_ROWS = 512


def _gla_prompt(q, k, g, v, r, s0, gg, nbatch):
    t = q.shape[0] // nbatch
    nb = min(GLA_SEQS, nbatch)
    tr = min(GLA_ROWS, t)
    seq = lambda a: a.reshape(nbatch, t, a.shape[-1])
    blk = lambda width: pl.BlockSpec((nb, tr, width), lambda b, j: (b, j, 0))
    state = pl.BlockSpec((nb, B_WIDTH, GK_WIDTH), lambda b, j: (b, 0, 0))
    return pl.pallas_call(
        functools.partial(_gla_kernel, n_chunks=tr // GLA_CHUNK),
        grid=(nbatch // nb, t // tr),
        in_specs=[blk(GK_WIDTH), blk(GK_WIDTH), blk(GK_WIDTH), blk(B_WIDTH), blk(B_WIDTH),
                  pl.BlockSpec((1, B_WIDTH, GK_WIDTH), lambda b, j: (0, 0, 0)),
                  pl.BlockSpec((1, B_WIDTH), lambda b, j: (0, 0))],
        out_specs=[blk(B_WIDTH), state],
        out_shape=[jax.ShapeDtypeStruct((nbatch, t, B_WIDTH), BF16),
                   jax.ShapeDtypeStruct((nbatch, B_WIDTH, GK_WIDTH), F32)],
        scratch_shapes=[pltpu.VMEM((nb, B_WIDTH, GK_WIDTH), F32)],
        compiler_params=_params("parallel", "arbitrary"),
        name="gla_prompt",
    )(seq(q), seq(k), seq(g), seq(v), seq(r), s0, gg)


def _gla_sample_kernel(q_ref, k_ref, la_ref, v_ref, r_ref, s_ref, gg_ref, ob_ref, sn_ref):
    qt = q_ref[...].T
    kt = k_ref[...].T
    at = jnp.exp(la_ref[...]).T
    vt = v_ref[...].T
    rt = r_ref[...].T
    gg = gg_ref[...]
    outs = []
    for h in range(B_HEADS):
        vh = vt[h * B_DV:(h + 1) * B_DV, :]
        o = jnp.zeros_like(vh)
        for d in range(B_DK):
            i = h * B_DK + d
            s_new = at[i:i + 1, :] * s_ref[i] + kt[i:i + 1, :] * vh
            sn_ref[i] = s_new
            o = o + qt[i:i + 1, :] * s_new
        on = o * lax.rsqrt(jnp.mean(o * o, axis=0, keepdims=True) + LN_EPS)
        outs.append(on * gg * _silu(rt[h * B_DV:(h + 1) * B_DV, :]))
    ob_ref[...] = jnp.concatenate(outs, axis=0).T.astype(BF16)


def _gla_sample(q, k, la, v, r, state_t, gg_col, layer):
    ns = state_t.shape[3]
    row = lambda width: pl.BlockSpec((ns, width), lambda i: (BLOCK // ns, 0))
    return pl.pallas_call(
        _gla_sample_kernel,
        grid=(1,),
        in_specs=[row(GK_WIDTH), row(GK_WIDTH), row(GK_WIDTH), row(B_WIDTH), row(B_WIDTH),
                  pl.BlockSpec((None, GK_WIDTH, B_DV, ns), lambda i: (layer, 0, 0, 0)),
                  pl.BlockSpec((B_DV, 1), lambda i: (0, 0))],
        out_specs=[pl.BlockSpec((ns, B_WIDTH), lambda i: (0, 0)),
                   pl.BlockSpec((GK_WIDTH, B_DV, ns), lambda i: (0, 0, 0))],
        out_shape=[jax.ShapeDtypeStruct((ns, B_WIDTH), BF16),
                   jax.ShapeDtypeStruct((GK_WIDTH, B_DV, ns), F32)],
        compiler_params=_params("arbitrary"),
        name="gla_sample",
    )(q, k, la, v, r, state_t, gg_col)


CONV_HIST = 32
CONV_CHUNK = 64
CONV_GROUP = 2


def _conv_post(y, dwb, lng, lnb, pww, pwb):
    y = _silu(_ln_rows(y + dwb, lng, lnb))
    return _dot(y.astype(BF16), pww) + pwb


def _conv_kernel(u_ref, hist_ref, dww_ref, dwb_ref, lng_ref, lnb_ref, pww_ref, pwb_ref,
                 oc_ref, tail_ref, ubuf, *, n_chunks):
    t = u_ref.shape[1]
    ubuf[0, 0:CONV_HIST, :] = hist_ref[...]
    ubuf[0, CONV_HIST:CONV_HIST + t, :] = u_ref[0]
    off = CONV_HIST - (CONV_W - 1)
    n_copy = t + CONV_HIST - SUBLANES
    for s in range(1, SUBLANES):
        ubuf[s, 0:n_copy, :] = ubuf[0, s:s + n_copy, :]

    group = CONV_GROUP if n_chunks % CONV_GROUP == 0 else 1

    def body(c, carry):
        starts = [pl.multiple_of((c * group + i) * CONV_CHUNK, CONV_CHUNK) for i in range(group)]
        accs = [jnp.zeros((CONV_CHUNK, C_WIDTH), F32) for _ in starts]
        for j in range(CONV_W):
            a, s = divmod(j + off, SUBLANES)
            w_j = dww_ref[j:j + 1, :]
            for i, r0 in enumerate(starts):
                rows = pl.ds(pl.multiple_of(r0 + a * SUBLANES, SUBLANES), CONV_CHUNK)
                accs[i] = accs[i] + ubuf[s, rows, :] * w_j
        for r0, acc in zip(starts, accs):
            oc = _conv_post(acc, dwb_ref[...], lng_ref[...], lnb_ref[...], pww_ref[...], pwb_ref[...])
            oc_ref[0, pl.ds(r0, CONV_CHUNK), :] = oc.astype(BF16)
        return carry

    lax.fori_loop(0, n_chunks // group, body, 0)
    tail_ref[0] = ubuf[0, t:t + CONV_HIST, :]


def _conv_prompt(u, hist, dww, dwb, lng, lnb, pww, pwb, nbatch):
    t = u.shape[0] // nbatch
    const = lambda shape: pl.BlockSpec(shape, lambda b: (0,) * len(shape))
    return pl.pallas_call(
        functools.partial(_conv_kernel, n_chunks=t // CONV_CHUNK),
        grid=(nbatch,),
        in_specs=[pl.BlockSpec((1, t, C_WIDTH), lambda b: (b, 0, 0)), const((CONV_HIST, C_WIDTH)),
                  const((CONV_HIST, C_WIDTH)), const((1, C_WIDTH)), const((1, C_WIDTH)), const((1, C_WIDTH)),
                  const((C_WIDTH, C_WIDTH)), const((1, C_WIDTH))],
        out_specs=[pl.BlockSpec((1, t, C_WIDTH), lambda b: (b, 0, 0)),
                   pl.BlockSpec((1, CONV_HIST, C_WIDTH), lambda b: (b, 0, 0))],
        out_shape=[jax.ShapeDtypeStruct((nbatch, t, C_WIDTH), BF16),
                   jax.ShapeDtypeStruct((nbatch, CONV_HIST, C_WIDTH), F32)],
        scratch_shapes=[pltpu.VMEM((SUBLANES, CONV_HIST + t, C_WIDTH), F32)],
        compiler_params=_params("parallel"),
        name="conv_prompt",
    )(u.reshape(nbatch, t, C_WIDTH), hist, dww, dwb, lng, lnb, pww, pwb)


def _conv_sample_kernel(hist_ref, u_ref, dww_ref, dwb_ref, lng_ref, lnb_ref, pww_ref, pwb_ref, oc_ref, tail_ref):
    nh = CONV_W - 1
    u = u_ref[...]
    y = u * dww_ref[nh:nh + 1, :]
    for j in range(nh):
        y = y + hist_ref[j] * dww_ref[j:j + 1, :]
    oc = _conv_post(y, dwb_ref[...], lng_ref[...], lnb_ref[...], pww_ref[...], pwb_ref[...])
    oc_ref[...] = oc.astype(BF16)
    for j in range(nh - 1):
        tail_ref[j] = hist_ref[j + 1]
    tail_ref[nh - 1] = u


def _conv_sample(hist_t, u, dww, dwb, lng, lnb, pww, pwb, layer):
    ns = hist_t.shape[2]
    nh = CONV_W - 1
    const = lambda shape: pl.BlockSpec(shape, lambda i: (0,) * len(shape))
    return pl.pallas_call(
        _conv_sample_kernel,
        grid=(1,),
        in_specs=[pl.BlockSpec((None, nh, ns, C_WIDTH), lambda i: (layer, 0, 0, 0)),
                  pl.BlockSpec((ns, C_WIDTH), lambda i: (BLOCK // ns, 0)),
                  const((CONV_HIST, C_WIDTH)), const((1, C_WIDTH)), const((1, C_WIDTH)), const((1, C_WIDTH)),
                  const((C_WIDTH, C_WIDTH)), const((1, C_WIDTH))],
        out_specs=[const((ns, C_WIDTH)), const((nh, ns, C_WIDTH))],
        out_shape=[jax.ShapeDtypeStruct((ns, C_WIDTH), BF16),
                   jax.ShapeDtypeStruct((nh, ns, C_WIDTH), F32)],
        compiler_params=_params("arbitrary"),
        name="conv_sample",
    )(hist_t, u, dww, dwb, lng, lnb, pww, pwb)


ROUTE_LANE0 = N_GROUPS


def _route(x, wr2, br):
    x_hi = x.astype(BF16)
    x_lo = (x - x_hi.astype(F32)).astype(BF16)
    l_hi = _dot(x_hi, wr2)
    logits = l_hi[:, :LANES] + l_hi[:, LANES:] + _dot(x_lo, wr2)[:, :LANES] + br
    lane = lax.broadcasted_iota(jnp.int32, logits.shape, 1).astype(F32)
    far = 1e3
    glm = jnp.where(lane < N_GROUPS, logits, NEG)
    gmax = jnp.max(glm, axis=1, keepdims=True)
    gi = jnp.min(jnp.where(glm == gmax, lane, far), axis=1, keepdims=True)
    p_grp = 1.0 / jnp.sum(jnp.exp(glm - gmax), axis=1, keepdims=True)
    lo = ROUTE_LANE0 + EXP_PER_GROUP * gi
    in_sel = (lane >= lo) & (lane < lo + EXP_PER_GROUP)
    elm = jnp.where(in_sel, logits, NEG)
    v1 = jnp.max(elm, axis=1, keepdims=True)
    i1 = jnp.min(jnp.where(elm == v1, lane, far), axis=1, keepdims=True)
    elm2 = jnp.where(lane == i1, NEG, elm)
    v2 = jnp.max(elm2, axis=1, keepdims=True)
    i2 = jnp.min(jnp.where((elm2 == v2) & in_sel & (lane != i1), lane, far), axis=1, keepdims=True)
    t = jnp.exp(v2 - v1)
    w1 = p_grp / (1.0 + t)
    w2 = w1 * t
    return jnp.where(lane == i1, w1, 0.0) + jnp.where(lane == i2, w2, 0.0), gi


FFN_CHUNK = 128
MOE_ROWS = 128
MOE_SINGLE = (128, 144, 176)
MOE_ALIGN = 16
GROUP_WIDTH = EXP_PER_GROUP * D_EXPERT


def _moe_sorted_rows(tm):
    need = tm + N_GROUPS * MOE_ALIGN + max(MOE_SINGLE + (MOE_ROWS,))
    return -(-need // LANES) * LANES


def _ffn_kernel(*refs, pre_ln):
    if pre_ln:
        lg_ref, lb_ref, *refs = refs
    (oa_ref, ob_ref, oc_ref, h_ref, wo_ref, g1_ref, b1_ref, wr_ref, br_ref, wg_ref, wu_ref, wd_ref,
     g_ref, b_ref, o_ref, x_ref, xs_ref, ys_ref, ws_ref) = refs
    tm = h_ref.shape[0]
    ns = xs_ref.shape[0]
    iota = lambda shape, axis: lax.broadcasted_iota(jnp.int32, shape, axis)
    ch = FFN_CHUNK if tm % FFN_CHUNK == 0 else tm
    chunks = [slice(r, r + ch) for r in range(0, tm, ch)]
    mixes = [(_dot(oa_ref[r, :], wo_ref[0:A_WIDTH, :])
              + _dot(ob_ref[r, :], wo_ref[A_WIDTH:A_WIDTH + B_WIDTH, :])
              + _dot(oc_ref[r, :], wo_ref[A_WIDTH + B_WIDTH:D_MODEL, :])) for r in chunks]
    xs_rows = []
    for r, mix in zip(chunks, mixes):
        h = h_ref[r, :]
        if pre_ln:
            h = _ln_rows(h, lg_ref[...], lb_ref[...])
        x_r = _ln_rows(ALPHA * h + mix, g1_ref[...], b1_ref[...])
        x_ref[r, :] = x_r
        xs_rows.append(x_r)
    routed = [_route(x_r, wr_ref[...], br_ref[...]) for x_r in xs_rows]
    x = jnp.concatenate(xs_rows, axis=0)
    dw = jnp.concatenate([d for d, _ in routed], axis=0)
    gi = jnp.concatenate([i for _, i in routed], axis=0)
    lane_f = iota((tm, LANES), 1).astype(F32)
    onehot = jnp.where(lane_f == gi, 1.0, 0.0)
    tri = jnp.where(iota((ch, ch), 1) <= iota((ch, ch), 0), 1.0, 0.0).astype(BF16)
    onehot_b = onehot.astype(BF16)
    carry = jnp.zeros((1, LANES), F32)
    cums = []
    for r in chunks:
        cums.append(_dot(tri, onehot_b[r]) + carry)
        carry = cums[-1][ch - 1:ch, :]
    cum = jnp.concatenate(cums, axis=0)
    rank = jnp.sum(onehot * (cum - 1.0), axis=1, keepdims=True)
    counts = cum[tm - 1:tm, :]

    starts, tiles, counts_i = [], [], []
    start = jnp.int32(0)
    for grp in range(N_GROUPS):
        n_g = counts[0, grp].astype(jnp.int32)
        counts_i.append(n_g)
        starts.append(start)
        tiles.append(sum((n_g > k * MOE_ROWS).astype(jnp.int32) for k in range(-(-tm // MOE_ROWS))))
        start = start + ((n_g + (MOE_ALIGN - 1)) & -MOE_ALIGN)
    lane1 = iota((1, LANES), 1)
    start_v = jnp.zeros((1, LANES), F32)
    for grp in range(N_GROUPS):
        start_v = jnp.where(lane1 == grp, starts[grp].astype(F32), start_v)
    pos = jnp.sum(onehot * start_v, axis=1, keepdims=True) + rank

    pos_i = pos.astype(jnp.int32)
    digits = jnp.where(lane_f == 0.0, (pos_i >> 5).astype(F32),
                       jnp.where(lane_f == 1.0, (pos_i & 31).astype(F32), 0.0)).astype(BF16)
    lane8 = iota((8, LANES), 1)
    radix = jnp.where(lane8 == 0, 32.0, jnp.where(lane8 == 1, 1.0, 0.0)).astype(BF16)
    pos_row = _dot_nt(radix, digits)[0:1, :]

    perm = jnp.where(iota((ns, tm), 0).astype(F32) == pos_row, 1.0, 0.0).astype(BF16)
    dw_hi = dw.astype(BF16)
    dw_lo = (dw - dw_hi.astype(F32)).astype(BF16)
    srt = _dot(perm, jnp.concatenate([x.astype(BF16), dw_hi, dw_lo], axis=1))
    xs_ref[...] = srt[:, :D_MODEL].astype(BF16)
    ws_ref[...] = srt[:, D_MODEL:D_MODEL + LANES] + srt[:, D_MODEL + LANES:]
    ys_ref[...] = jnp.zeros_like(ys_ref)

    def expert_tile(grp, r0, n_rows):
        rows = pl.ds(pl.multiple_of(r0, MOE_ALIGN), n_rows)
        xt = xs_ref[rows, :]
        w = ws_ref[rows, :]
        hes = []
        for e in range(EXP_PER_GROUP):
            ex = grp * EXP_PER_GROUP + e
            w_e = w[:, ROUTE_LANE0 + ex:ROUTE_LANE0 + ex + 1]
            hes.append((_silu(_dot(xt, wg_ref[ex])) * _dot(xt, wu_ref[ex]) * w_e).astype(BF16))
        ys_ref[rows, :] = _dot(jnp.concatenate(hes, axis=1), wd_ref[grp]).astype(BF16)

    for grp in range(N_GROUPS):
        n_g, lo = counts_i[grp], 0
        for size in MOE_SINGLE:
            pl.when((n_g > lo) & (n_g <= size))(functools.partial(expert_tile, grp, starts[grp], size))
            lo = size

        @pl.when(n_g > lo)
        def _(grp=grp):
            lax.fori_loop(0, tiles[grp],
                          lambda k, c: (expert_tile(grp, starts[grp] + k * MOE_ROWS, MOE_ROWS), c)[1], 0)

    lane_ns = iota((ch, ns), 1).astype(F32)
    ys = ys_ref[...]
    y_rows = [_dot(jnp.where(lane_ns == pos[r], 1.0, 0.0).astype(BF16), ys) for r in chunks]
    for r, y in zip(chunks, y_rows):
        o_ref[r, :] = _ln_rows(ALPHA * x_ref[r, :] + y, g_ref[...], b_ref[...])


def _ffn(oa, ob, oc, h, wo, g1, b1, wr2, br, wg, wu, wd, g2, b2, tm, layer, ln=None):
    n = h.shape[0]
    ns = _moe_sorted_rows(tm)
    row = lambda width: pl.BlockSpec((tm, width), lambda i: (i, 0))
    const = lambda shape: pl.BlockSpec(shape, lambda i: (0,) * len(shape))
    resident = lambda shape: pl.BlockSpec((None,) + shape, lambda i: (layer,) + (0,) * len(shape),
                                          pipeline_mode=pl.Buffered(1))
    ln_specs = [const((1, D_MODEL)), const((1, D_MODEL))] if ln else []
    return pl.pallas_call(
        functools.partial(_ffn_kernel, pre_ln=bool(ln)),
        grid=(n // tm,),
        in_specs=ln_specs + [row(A_WIDTH), row(B_WIDTH), row(C_WIDTH), row(D_MODEL),
                             resident((D_MODEL, D_MODEL)), const((1, D_MODEL)), const((1, D_MODEL)),
                             resident((D_MODEL, 2 * LANES)), const((1, LANES)),
                             resident((N_EXPERTS, D_MODEL, D_EXPERT)), resident((N_EXPERTS, D_MODEL, D_EXPERT)),
                             resident((N_GROUPS, GROUP_WIDTH, D_MODEL)),
                             const((1, D_MODEL)), const((1, D_MODEL))],
        out_specs=row(D_MODEL),
        out_shape=jax.ShapeDtypeStruct((n, D_MODEL), F32),
        scratch_shapes=[pltpu.VMEM((tm, D_MODEL), F32), pltpu.VMEM((ns, D_MODEL), BF16),
                        pltpu.VMEM((ns, D_MODEL), BF16), pltpu.VMEM((ns, LANES), F32)],
        compiler_params=_params("parallel"),
        name="ffn",
    )(*(ln or ()), oa, ob, oc, h, wo, g1, b1, wr2, br, wg, wu, wd, g2, b2)


def _rope_tables(pos):
    half = HEAD_DIM // 2
    inv = ROPE_THETA ** (-jnp.arange(half, dtype=F32) / half)
    ang = pos.astype(F32)[:, None] * inv[None, :]
    cos, sin = jnp.cos(ang), jnp.sin(ang)
    cos_t = jnp.concatenate([cos, cos] * (LANES // HEAD_DIM), axis=1)
    sin_t = jnp.concatenate([-sin, sin] * (LANES // HEAD_DIM), axis=1)
    return cos_t, sin_t


def _row(v):
    return v.reshape(1, -1)


def kernel(x_prompt, x_sample, cache_meta_k, cache_meta_v, cache_win_k, cache_win_v, state_gla, state_conv,
           meta_tokens, ln_in_g, ln_in_b, w_in, attn_sink, w_alpha, b_alpha, gla_norm_g,
           conv_dw_w, conv_dw_b, conv_ln_g, conv_ln_b, conv_pw_w, conv_pw_b, w_out, ln1_g, ln1_b,
           w_router_group, b_router_group, w_router_expert, b_router_expert,
           w_exp_gate, w_exp_up, w_exp_down, ln2_g, ln2_b):
    nb, seq, d = x_prompt.shape
    ns = x_sample.shape[0]
    nwin = cache_win_k.shape[2]
    n_big = nb * seq
    n_small = BLOCK + ns
    tm_big = min(512, seq)

    small_in = jnp.concatenate([jnp.zeros((META_PAD, d), F32), meta_tokens.astype(F32),
                                x_sample.reshape(ns, d)], axis=0)
    hb = x_prompt.reshape(n_big, d)
    hs = small_in
    ln_in = (_row(ln_in_g), _row(ln_in_b))

    cos_b, sin_b = _rope_tables(N_META + jnp.arange(seq))
    pos_small = jnp.concatenate([jnp.maximum(jnp.arange(BLOCK) - META_PAD, 0),
                                 jnp.full((ns,), PAST_LEN, jnp.int32)])
    cos_s, sin_s = _rope_tables(pos_small)

    outs = [[] for _ in range(12)]
    zeros_hist = jnp.zeros((CONV_HIST, C_WIDTH), F32)
    zeros_state = jnp.zeros((1, B_WIDTH, GK_WIDTH), F32)

    nl = w_in.shape[0]
    w_pad = jnp.concatenate([w_in[:, :, :C_AB + B_RANK], jnp.zeros((nl, d, LANES - B_RANK), F32),
                             w_in[:, :, C_AB + B_RANK:]], axis=2).astype(BF16)
    wa_all = jnp.concatenate([w_alpha, jnp.zeros((nl, LANES - B_RANK, GK_WIDTH), F32)], axis=1).astype(BF16)
    ba_all = b_alpha[:, None, :]
    wo_all = w_out.astype(BF16)
    wr = jnp.concatenate([w_router_group, w_router_expert,
                          jnp.zeros((nl, d, LANES - N_GROUPS - N_EXPERTS), F32)], axis=2)
    wr_hi = wr.astype(BF16)
    wr2_all = jnp.concatenate([wr_hi, (wr - wr_hi.astype(F32)).astype(BF16)], axis=2)
    wg_all, wu_all = w_exp_gate.astype(BF16), w_exp_up.astype(BF16)
    wd_all = w_exp_down.astype(BF16).reshape(nl, N_GROUPS, GROUP_WIDTH, d)
    win_k_t = cache_win_k.transpose(0, 1, 3, 4, 2).reshape(nl, ns, KV_WIDTH, nwin)
    win_v_t = cache_win_v.transpose(0, 1, 3, 4, 2).reshape(nl, ns, KV_WIDTH, nwin)
    gla_t = state_gla.transpose(0, 2, 3, 4, 1).reshape(nl, GK_WIDTH, B_DV, ns)
    conv_t = state_conv.transpose(0, 2, 1, 3)

    for l in range(DEPTH):
        sink = attn_sink[l].astype(F32)
        sink_col = jnp.concatenate([sink, jnp.zeros((HEAD_ROWS - A_HEADS,), F32)])[:, None]
        gg_t = _row(jnp.tile(gla_norm_g[l], B_HEADS))
        gg_col = gla_norm_g[l][:, None]
        dww = jnp.concatenate([conv_dw_w[l], jnp.zeros((CONV_HIST - CONV_W, C_WIDTH), F32)], axis=0)
        dwb, clg, clb = _row(conv_dw_b[l]), _row(conv_ln_g[l]), _row(conv_ln_b[l])
        pww, pwb = conv_pw_w[l].astype(BF16), _row(conv_pw_b[l])
        br = _row(jnp.concatenate([b_router_group[l], b_router_expert[l],
                                   jnp.zeros((LANES - N_GROUPS - N_EXPERTS,), F32)]))

        ln = ln_in if l == 0 else None
        qa_b, ka_b, va_b, qg_b, kg_b, la_b, vg_b, rg_b, u_b = _proj_in(
            hb, w_pad, wa_all, ba_all, cos_b, sin_b, layer=l, tm=tm_big, n_pad=0, n_seq=tm_big, ln=ln)
        qa_s, ka_s, va_s, qg_s, kg_s, la_s, vg_s, rg_s, u_s = _proj_in(
            hs, w_pad, wa_all, ba_all, cos_s, sin_s, layer=l, tm=n_small, n_pad=META_PAD, n_seq=BLOCK, ln=ln)

        oa_b = _attn_prompt(sink, qa_b, ka_b, va_b, ka_s, va_s, nb)
        oa_m = _attn_meta(sink, qa_s, ka_s, va_s)
        q_smp = qa_s[BLOCK:].reshape(ns, A_KV_HEADS, A_GROUP, 1, HEAD_DIM)
        eye = jnp.eye(A_KV_HEADS, dtype=BF16)[None, :, None, :, None]
        q_bd = (q_smp * eye).reshape(ns, A_HEADS, KV_WIDTH)
        q_bd = jnp.pad(q_bd, ((0, 0), (0, HEAD_ROWS - A_HEADS), (0, 0)))
        o_bd, nwk_t, nwv_t = _attn_sample(
            sink_col, q_bd, ka_s, va_s,
            cache_meta_k[l].reshape(ns, N_META, KV_WIDTH), cache_meta_v[l].reshape(ns, N_META, KV_WIDTH),
            win_k_t, win_v_t, l)
        o_bd = o_bd[:, :A_HEADS].reshape(ns, A_KV_HEADS, A_GROUP, A_KV_HEADS, HEAD_DIM)
        oa_smp = jnp.stack([o_bd[:, c, :, c, :] for c in range(A_KV_HEADS)], axis=1).reshape(ns, A_WIDTH)
        oa_s = jnp.concatenate([oa_m, oa_smp.astype(BF16)], axis=0)

        ob_m, st_m = _gla_prompt(qg_s[:BLOCK], kg_s[:BLOCK], la_s[:BLOCK], vg_s[:BLOCK], rg_s[:BLOCK],
                                 zeros_state, gg_t, 1)
        ob_b, st_b = _gla_prompt(qg_b, kg_b, la_b, vg_b, rg_b, st_m, gg_t, nb)
        ob_smp, s_new_t = _gla_sample(qg_s, kg_s, la_s, vg_s, rg_s, gla_t, gg_col, l)
        ob_s = jnp.concatenate([ob_m.reshape(BLOCK, B_WIDTH), ob_smp], axis=0)
        st5 = st_b.reshape(nb, B_HEADS, B_DV, B_HEADS, B_DK)
        gla_p = jnp.stack([st5[:, h, :, h, :] for h in range(B_HEADS)], axis=1).transpose(0, 1, 3, 2)

        oc_m, _ = _conv_prompt(u_s[:BLOCK], zeros_hist, dww, dwb, clg, clb, pww, pwb, 1)
        oc_b, tail_b = _conv_prompt(u_b, u_s[BLOCK - CONV_HIST:BLOCK], dww, dwb, clg, clb, pww, pwb, nb)
        oc_smp, tail_s_t = _conv_sample(conv_t, u_s, dww, dwb, clg, clb, pww, pwb, l)
        oc_s = jnp.concatenate([oc_m.reshape(BLOCK, C_WIDTH), oc_smp], axis=0)

        l1g, l1b, l2g, l2b = _row(ln1_g[l]), _row(ln1_b[l]), _row(ln2_g[l]), _row(ln2_b[l])
        hb = _ffn(oa_b, ob_b.reshape(n_big, B_WIDTH), oc_b.reshape(n_big, C_WIDTH), hb, wo_all, l1g, l1b,
                  wr2_all, br, wg_all, wu_all, wd_all, l2g, l2b, tm_big, l, ln=ln)
        hs = _ffn(oa_s, ob_s, oc_s, hs, wo_all, l1g, l1b, wr2_all, br, wg_all, wu_all, wd_all, l2g, l2b,
                  n_small, l, ln=ln)

        kv4 = lambda a: a.reshape(a.shape[0], a.shape[1], A_KV_HEADS, HEAD_DIM)
        win_t = lambda a: a.reshape(ns, A_KV_HEADS, HEAD_DIM, nwin).transpose(0, 3, 1, 2)
        meta_k = jnp.broadcast_to(ka_s[META_PAD:BLOCK][None], (nb, N_META, KV_WIDTH))
        meta_v = jnp.broadcast_to(va_s[META_PAD:BLOCK][None], (nb, N_META, KV_WIDTH))
        win_k = ka_b.reshape(nb, seq, KV_WIDTH)[:, seq - nwin:]
        win_v = va_b.reshape(nb, seq, KV_WIDTH)[:, seq - nwin:]
        layer_out = (None, None, kv4(meta_k), kv4(meta_v), kv4(win_k), kv4(win_v), win_t(nwk_t), win_t(nwv_t),
                     gla_p, s_new_t.reshape(B_HEADS, B_DK, B_DV, ns).transpose(3, 0, 1, 2),
                     tail_b[:, CONV_HIST - (CONV_W - 1):], tail_s_t.transpose(1, 0, 2))
        for i in range(2, 12):
            outs[i].append(layer_out[i])

    y_prompt = hb.reshape(nb, seq, d)
    y_sample = hs[BLOCK:].reshape(ns, 1, d)
    return (y_prompt, y_sample) + tuple(jnp.stack(o) for o in outs[2:])
```

```python
import functools

import jax
import jax.numpy as jnp
from jax import lax
from jax.experimental import pallas as pl
from jax.experimental.pallas import tpu as pltpu

F32 = jnp.float32
BF16 = jnp.bfloat16

D_MODEL = 1024
DEPTH = 2
PAST_LEN = 16384
N_META = 16
HEAD_DIM = 64
A_WIDTH = 512
A_HEADS = 8
A_KV_HEADS = 2
A_GROUP = 4
WINDOW = 128
BLOCK = 128
ROPE_THETA = 10000.0
B_WIDTH = 256
B_HEADS = 4
B_DV = 64
B_DK = 32
B_RANK = 16
GATE_TAU = 16.0
GLA_CHUNK = 64
C_WIDTH = 256
CONV_W = 31
N_GROUPS = 4
EXP_PER_GROUP = 4
N_EXPERTS = 16
D_EXPERT = 256
ALPHA = (2 * DEPTH) ** 0.25
LN_EPS = 1e-5

LANES = 128
SUBLANES = 8
META_PAD = BLOCK - N_META
KV_WIDTH = A_KV_HEADS * HEAD_DIM
GK_WIDTH = B_HEADS * B_DK
C_QA = 0
C_KA = C_QA + A_WIDTH
C_VA = C_KA + KV_WIDTH
C_QB = C_VA + KV_WIDTH
C_KB = C_QB + GK_WIDTH
C_VB = C_KB + GK_WIDTH
C_RB = C_VB + B_WIDTH
C_AB = C_RB + B_WIDTH
C_CG = C_AB + LANES
PROJ_PAD_WIDTH = C_CG + 2 * C_WIDTH
NEG = -1e30
VMEM_LIMIT = 56 * 1024 * 1024


def _dot(a, b):
    return jnp.dot(a, b, preferred_element_type=F32)


def _dot_nt(a, b):
    return lax.dot_general(a, b, (((1,), (1,)), ((), ())), preferred_element_type=F32)


def _dot_tn(a, b):
    return lax.dot_general(a, b, (((0,), (0,)), ((), ())), preferred_element_type=F32)


def _ln_rows(x, g, b):
    xc = x - jnp.mean(x, -1, keepdims=True)
    var = jnp.mean(xc * xc, -1, keepdims=True)
    return xc * lax.rsqrt(var + LN_EPS) * g + b


def _silu(x):
    return x * jax.nn.sigmoid(x)


def _split3(x):
    hi = x.astype(BF16)
    r1 = x - hi.astype(F32)
    mid = r1.astype(BF16)
    lo = (r1 - mid.astype(F32)).astype(BF16)
    return hi, mid, lo


def _params(*sem):
    return pltpu.CompilerParams(dimension_semantics=sem, vmem_limit_bytes=VMEM_LIMIT)


def _proj_in_kernel(*refs, n_pad, n_seq, pre_ln):
    if pre_ln:
        lg_ref, lb_ref, *refs = refs
    (x_ref, w_ref, wa_ref, ba_ref, cos_ref, sin_ref,
     qa_ref, ka_ref, va_ref, qg_ref, kg_ref, la_ref, vg_ref, rg_ref, u_ref) = refs
    x = x_ref[...]
    if pre_ln:
        x = _ln_rows(x, lg_ref[...], lb_ref[...])
    xb = x.astype(BF16)
    tm = xb.shape[0]
    za = _dot(xb, w_ref[:, C_QA:C_QB])
    cos = cos_ref[...]
    sin = sin_ref[...]
    lane = lax.broadcasted_iota(jnp.int32, (tm, LANES), 1)
    first_half = (lane & (HEAD_DIM // 2)) == 0

    def rope(z):
        rot = jnp.where(first_half, pltpu.roll(z, LANES - HEAD_DIM // 2, 1), pltpu.roll(z, HEAD_DIM // 2, 1))
        return z * cos + rot * sin

    if n_pad:
        valid = (lax.broadcasted_iota(jnp.int32, (tm, 1), 0) >= n_pad).astype(F32)
    else:
        valid = None

    zb = _dot(xb, w_ref[:, C_QB:C_CG])
    zc = _dot(xb, w_ref[:, C_CG:PROJ_PAD_WIDTH])
    for c in range(A_WIDTH // LANES):
        zq = za[:, c * LANES:(c + 1) * LANES]
        qa_ref[:, c * LANES:(c + 1) * LANES] = (rope(zq) * (HEAD_DIM ** -0.5)).astype(BF16)
    ka_ref[...] = rope(za[:, C_KA:C_VA])
    va_ref[...] = za[:, C_VA:C_QB]

    o = C_QB
    qg_ref[...] = zb[:, C_QB - o:C_KB - o] * (B_DK ** -0.5)
    kg = zb[:, C_KB - o:C_VB - o]
    vg_ref[...] = zb[:, C_VB - o:C_RB - o]
    rg_ref[...] = zb[:, C_RB - o:C_AB - o]
    ab = zb[:, C_AB - o:C_CG - o].astype(BF16)
    xa = _dot(ab, wa_ref[...]) + ba_ref[...]
    la = (jnp.minimum(xa, 0.0) - jnp.log(1.0 + jnp.exp(-jnp.abs(xa)))) * (1.0 / GATE_TAU)

    u = zc[:, :C_WIDTH] * jax.nn.sigmoid(zc[:, C_WIDTH:])
    if valid is not None:
        kg = kg * valid
        la = la * valid
        u = u * valid
    kg_ref[...] = kg
    u_ref[...] = u
    cs = GLA_CHUNK
    tri = jnp.where(lax.broadcasted_iota(jnp.int32, (cs, cs), 1) <= lax.broadcasted_iota(jnp.int32, (cs, cs), 0),
                    1.0, 0.0).astype(BF16)
    la_h, la_m, la_l = _split3(la)
    for c in range(n_seq // cs):
        rows = slice(c * cs, (c + 1) * cs)
        la_ref[rows, :] = _dot(tri, la_h[rows]) + _dot(tri, la_m[rows]) + _dot(tri, la_l[rows])
    if n_seq < tm:
        la_ref[n_seq:tm, :] = la[n_seq:tm]


def _proj_in(h, w, wa, ba, cos, sin, *, layer, tm, n_pad, n_seq, ln=None):
    n = h.shape[0]
    tb = cos.shape[0] // tm
    row = lambda width: pl.BlockSpec((tm, width), lambda i: (i, 0))
    const = lambda shape: pl.BlockSpec(shape, lambda i: (0, 0))
    per_layer = lambda shape: pl.BlockSpec((None,) + shape, lambda i: (layer, 0, 0))
    tab = pl.BlockSpec((tm, LANES), lambda i: (i % tb, 0))
    widths = (A_WIDTH, KV_WIDTH, KV_WIDTH, GK_WIDTH, GK_WIDTH, GK_WIDTH, B_WIDTH, B_WIDTH, C_WIDTH)
    dtypes = (BF16,) + (F32,) * 8
    ln_specs = [const((1, D_MODEL)), const((1, D_MODEL))] if ln else []
    return pl.pallas_call(
        functools.partial(_proj_in_kernel, n_pad=n_pad, n_seq=n_seq, pre_ln=bool(ln)),
        grid=(n // tm,),
        in_specs=ln_specs + [row(D_MODEL), per_layer((D_MODEL, PROJ_PAD_WIDTH)), per_layer((LANES, GK_WIDTH)),
                             per_layer((1, GK_WIDTH)), tab, tab],
        out_specs=[row(wd) for wd in widths],
        out_shape=[jax.ShapeDtypeStruct((n, wd), dt) for wd, dt in zip(widths, dtypes)],
        compiler_params=_params("parallel"),
        name="proj_in",
    )(*(ln or ()), h, w, wa, ba, cos, sin)


def _attn_kernel(sink_ref, q_ref, km_ref, vm_ref, *rest, meta_mode):
    if meta_mode:
        (o_ref,) = rest
        blocks = [(0, q_ref[...], km_ref[...], vm_ref[...], None)]
    else:
        kp_ref, vp_ref, kc_ref, vc_ref, o_ref = rest
        km, vm = km_ref[...], vm_ref[...]
        blocks = []
        for sub in range(q_ref.shape[0] // BLOCK):
            cur = slice(sub * BLOCK, (sub + 1) * BLOCK)
            if sub == 0:
                kp, vp = kp_ref[...], vp_ref[...]
                has_prev = pl.program_id(1) >= 1
            else:
                prev = slice((sub - 1) * BLOCK, sub * BLOCK)
                kp, vp = kc_ref[prev, :], vc_ref[prev, :]
                has_prev = True
            blocks.append((sub * BLOCK, q_ref[cur, :], jnp.concatenate([kp, kc_ref[cur, :], km], axis=0),
                           jnp.concatenate([vp, vc_ref[cur, :], vm], axis=0), has_prev))

    units = []
    for row0, q, k_all, v_all, has_prev in blocks:
        nk = k_all.shape[0]
        ki = lax.broadcasted_iota(jnp.int32, (nk, BLOCK), 0)
        qi = lax.broadcasted_iota(jnp.int32, (nk, BLOCK), 1)
        if has_prev is None:
            ok = ki <= qi - META_PAD
        else:
            prev_lo = qi if has_prev is True else qi + jnp.where(has_prev, 0, BLOCK)
            ok = ((ki >= prev_lo) & (ki < BLOCK)) | ((ki >= BLOCK) & (ki <= qi + BLOCK)) | (ki >= 2 * BLOCK)
        bias = jnp.where(ok, 0.0, NEG)
        kb, vb = k_all.astype(BF16), v_all.astype(BF16)
        for kvh in range(A_KV_HEADS):
            lanes = slice(kvh * HEAD_DIM, (kvh + 1) * HEAD_DIM)
            qs = jnp.concatenate([q[:, (kvh * A_GROUP + g) * HEAD_DIM:(kvh * A_GROUP + g + 1) * HEAD_DIM]
                                  for g in range(A_GROUP)], axis=0)
            units.append((row0, kvh, bias, _dot_nt(kb[:, lanes], qs), vb[:, lanes]))

    outs = []
    for row0, kvh, bias, st, vv in units:
        ps, dens = [], []
        for g in range(A_GROUP):
            s = st[:, g * BLOCK:(g + 1) * BLOCK] + bias
            sink = sink_ref[kvh * A_GROUP + g]
            m = jnp.maximum(jnp.max(s, axis=0, keepdims=True), sink)
            p = jnp.exp(s - m)
            dens.append(jnp.sum(p, axis=0, keepdims=True) + jnp.exp(sink - m))
            ps.append(p.astype(BF16))
        outs.append((row0, kvh, _dot_tn(vv, jnp.concatenate(ps, axis=1)), jnp.concatenate(dens, axis=1)))

    for row0, kvh, ot, den in outs:
        ot = ot / den
        for pair in range(A_GROUP // 2):
            two = jnp.concatenate([ot[:, (2 * pair) * BLOCK:(2 * pair + 1) * BLOCK],
                                   ot[:, (2 * pair + 1) * BLOCK:(2 * pair + 2) * BLOCK]], axis=0)
            h0 = kvh * A_GROUP + 2 * pair
            o_ref[row0:row0 + BLOCK, h0 * HEAD_DIM:(h0 + 2) * HEAD_DIM] = two.T.astype(BF16)


ATTN_SUB = 8


def _attn_prompt(sink, q, k, v, k_small, v_small, nbatch):
    n = q.shape[0]
    nblk = n // BLOCK // nbatch
    sub = ATTN_SUB if nblk % ATTN_SUB == 0 else 1
    nstep = nblk // sub
    kvs = pl.BlockSpec((sub * BLOCK, KV_WIDTH), lambda b, j: (b * nstep + j, 0))
    kvp = pl.BlockSpec((BLOCK, KV_WIDTH), lambda b, j: (b * nblk + jnp.maximum(j * sub - 1, 0), 0))
    kvm = pl.BlockSpec((N_META, KV_WIDTH), lambda b, j: (META_PAD // N_META, 0))
    return pl.pallas_call(
        functools.partial(_attn_kernel, meta_mode=False),
        grid=(nbatch, nstep),
        in_specs=[pl.BlockSpec(memory_space=pltpu.SMEM),
                  pl.BlockSpec((sub * BLOCK, A_WIDTH), lambda b, j: (b * nstep + j, 0)),
                  kvm, kvm, kvp, kvp, kvs, kvs],
        out_specs=pl.BlockSpec((sub * BLOCK, A_WIDTH), lambda b, j: (b * nstep + j, 0)),
        out_shape=jax.ShapeDtypeStruct((n, A_WIDTH), BF16),
        compiler_params=_params("parallel", "parallel"),
        name="attn_prompt",
    )(sink, q, k_small, v_small, k, v, k, v)


def _attn_meta(sink, q_small, k_small, v_small):
    kvm = pl.BlockSpec((N_META, KV_WIDTH), lambda i: (META_PAD // N_META, 0))
    return pl.pallas_call(
        functools.partial(_attn_kernel, meta_mode=True),
        grid=(1,),
        in_specs=[pl.BlockSpec(memory_space=pltpu.SMEM),
                  pl.BlockSpec((BLOCK, A_WIDTH), lambda i: (0, 0)), kvm, kvm],
        out_specs=pl.BlockSpec((BLOCK, A_WIDTH), lambda i: (0, 0)),
        out_shape=jax.ShapeDtypeStruct((BLOCK, A_WIDTH), BF16),
        compiler_params=_params("arbitrary"),
        name="attn_meta",
    )(sink, q_small, k_small, v_small)


SAMPLE_BLOCK = 32
HEAD_ROWS = 16


def _attn_sample_kernel(sink_ref, q_ref, kn_ref, vn_ref, mk_ref, mv_ref, wk_ref, wv_ref,
                        o_ref, nwk_ref, nwv_ref):
    sb = q_ref.shape[0]
    q = q_ref[...]
    kn = kn_ref[...]
    vn = vn_ref[...]
    s_win = jnp.einsum('bhl,blj->bhj', q, wk_ref[...].astype(BF16), preferred_element_type=F32)
    s_meta = jnp.einsum('bhl,bml->bhm', q, mk_ref[...].astype(BF16), preferred_element_type=F32)
    s_self = jnp.sum(q.astype(F32) * kn[:, None, :], axis=2, keepdims=True)
    sink = sink_ref[...][None]
    m = jnp.maximum(jnp.maximum(jnp.max(s_win, axis=2, keepdims=True), jnp.max(s_meta, axis=2, keepdims=True)),
                    jnp.maximum(s_self, sink))
    p_win = jnp.exp(s_win - m)
    p_meta = jnp.exp(s_meta - m)
    p_self = jnp.exp(s_self - m)
    denom = (jnp.sum(p_win, axis=2, keepdims=True) + jnp.sum(p_meta, axis=2, keepdims=True) + p_self
             + jnp.exp(sink - m))
    o = (jnp.einsum('bhj,blj->bhl', p_win.astype(BF16), wv_ref[...].astype(BF16), preferred_element_type=F32)
         + jnp.einsum('bhm,bml->bhl', p_meta.astype(BF16), mv_ref[...].astype(BF16), preferred_element_type=F32)
         + p_self * vn[:, None, :])
    o_ref[...] = o / denom
    nwin = wk_ref.shape[2]
    last = lax.broadcasted_iota(jnp.int32, (KV_WIDTH, nwin), 1) == nwin - 1
    knt = kn.T
    vnt = vn.T
    for j in range(sb):
        nwk_ref[j] = jnp.where(last, knt[:, j:j + 1], pltpu.roll(wk_ref[j], nwin - 1, 1))
        nwv_ref[j] = jnp.where(last, vnt[:, j:j + 1], pltpu.roll(wv_ref[j], nwin - 1, 1))


def _attn_sample(sink_col, q_bd, k_small, v_small, meta_k, meta_v, win_k_t, win_v_t, layer):
    ns = q_bd.shape[0]
    sb = min(SAMPLE_BLOCK, ns)
    nwin = win_k_t.shape[3]
    blk = lambda rows: pl.BlockSpec((sb, rows, KV_WIDTH), lambda i: (i, 0, 0))
    new = pl.BlockSpec((sb, KV_WIDTH), lambda i: (BLOCK // sb + i, 0))
    win = pl.BlockSpec((None, sb, KV_WIDTH, nwin), lambda i: (layer, i, 0, 0))
    win_out = pl.BlockSpec((sb, KV_WIDTH, nwin), lambda i: (i, 0, 0))
    return pl.pallas_call(
        _attn_sample_kernel,
        grid=(ns // sb,),
        in_specs=[pl.BlockSpec((HEAD_ROWS, 1), lambda i: (0, 0)),
                  blk(HEAD_ROWS), new, new, blk(N_META), blk(N_META), win, win],
        out_specs=[blk(HEAD_ROWS), win_out, win_out],
        out_shape=[jax.ShapeDtypeStruct((ns, HEAD_ROWS, KV_WIDTH), F32),
                   jax.ShapeDtypeStruct((ns, KV_WIDTH, nwin), F32),
                   jax.ShapeDtypeStruct((ns, KV_WIDTH, nwin), F32)],
        compiler_params=_params("parallel"),
        name="attn_sample",
    )(sink_col, q_bd, k_small, v_small, meta_k, meta_v, win_k_t, win_v_t)


def _seg_mean_sq(o, mseg):
    sq = o * o
    hi = sq.astype(BF16)
    lo = (sq - hi.astype(F32)).astype(BF16)
    return _dot(hi, mseg) + _dot(lo, mseg)


def _gla_kernel(q_ref, k_ref, g_ref, v_ref, r_ref, s0_ref, gg_ref, ob_ref, st_ref, st_scr, *, n_chunks):
    cs = GLA_CHUNK
    nseq = q_ref.shape[0]

    @pl.when(pl.program_id(1) == 0)
    def _():
        for b in range(nseq):
            st_scr[b] = s0_ref[0]

    iota = lambda shape, axis: lax.broadcasted_iota(jnp.int32, shape, axis)
    lg_cs, lg_dk, lg_dv = cs.bit_length() - 1, B_DK.bit_length() - 1, B_DV.bit_length() - 1
    kd_mask = (iota((B_HEADS * cs, GK_WIDTH), 0) >> lg_cs) == (iota((B_HEADS * cs, GK_WIDTH), 1) >> lg_dk)
    vd_mask = (iota((B_HEADS * cs, B_WIDTH), 0) >> lg_cs) == (iota((B_HEADS * cs, B_WIDTH), 1) >> lg_dv)
    st_mask = (iota((B_WIDTH, GK_WIDTH), 0) >> lg_dv) == (iota((B_WIDTH, GK_WIDTH), 1) >> lg_dk)
    t_idx = iota((cs, B_HEADS * cs), 0)
    s_idx = iota((cs, B_HEADS * cs), 1) & (cs - 1)
    diff = t_idx ^ s_idx
    level = jnp.full((cs, B_HEADS * cs), lg_cs, jnp.int32)
    for bit in range(lg_cs):
        level = jnp.where(diff >= (1 << bit), bit, level)
    level = jnp.where(s_idx > t_idx, -1, level)
    odd_row = (iota((cs, GK_WIDTH), 0) & 1) == 1
    mseg =jnp.where((iota((B_WIDTH, B_WIDTH), 0) >> lg_dv) == (iota((B_WIDTH, B_WIDTH), 1) >> lg_dv),
                     1.0 / B_DV, 0.0).astype(BF16)
    gg = gg_ref[...]

    def scores(qt, kt):
        kbd = jnp.where(kd_mask, jnp.concatenate([kt.astype(BF16)] * B_HEADS, axis=0), 0.0)
        return _dot_nt(qt.astype(BF16), kbd)

    def level_ref(g, bit):
        if bit == 0:
            return jnp.where(odd_row, pltpu.roll(g, 1, 0), g)
        half = 1 << bit
        return jnp.concatenate([jnp.broadcast_to(g[p + half - 1:p + half, :], (2 * half, GK_WIDTH))
                                for p in range(0, cs, 2 * half)], axis=0)

    def body(c, carry):
        rows = pl.ds(pl.multiple_of(c * cs, cs), cs)
        seqs = range(nseq)
        q = [q_ref[b, rows, :] for b in seqs]
        k = [k_ref[b, rows, :] for b in seqs]
        g = [g_ref[b, rows, :] for b in seqs]
        att = [jnp.where(level == lg_cs, scores(q[b], k[b]), 0.0) for b in seqs]
        for bit in range(lg_cs):
            for b in seqs:
                decay = jnp.exp(-jnp.abs(g[b] - level_ref(g[b], bit)))
                att[b] = jnp.where(level == bit, scores(q[b] * decay, k[b] * decay), att[b])
        for b in seqs:
            g_end = g[b][cs - 1:cs, :]
            vb = v_ref[b, rows, :].astype(BF16)
            vbd = jnp.where(vd_mask, jnp.concatenate([vb] * B_HEADS, axis=0), 0.0)
            st = st_scr[b]
            o = _dot(att[b].astype(BF16), vbd) + _dot_nt((q[b] * jnp.exp(g[b])).astype(BF16), st.astype(BF16))
            kh = (k[b] * jnp.exp(g_end - g[b])).astype(BF16)
            upd = _dot_tn(vb, kh)
            st_scr[b] = st * jnp.exp(g_end) + jnp.where(st_mask, upd, 0.0)
            on = o * lax.rsqrt(_seg_mean_sq(o, mseg) + LN_EPS)
            ob_ref[b, rows, :] = (on * gg * _silu(r_ref[b, rows, :])).astype(BF16)
        return carry

    lax.fori_loop(0, n_chunks, body, 0)

    @pl.when(pl.program_id(1) == pl.num_programs(1) - 1)
    def _():
        for b in range(nseq):
            st_ref[b] = st_scr[b]


GLA_SEQS = 8
GLA_ROWS = 512


def _gla_prompt(q, k, g, v, r, s0, gg, nbatch):
    t = q.shape[0] // nbatch
    nb = min(GLA_SEQS, nbatch)
    tr = min(GLA_ROWS, t)
    seq = lambda a: a.reshape(nbatch, t, a.shape[-1])
    blk = lambda width: pl.BlockSpec((nb, tr, width), lambda b, j: (b, j, 0))
    state = pl.BlockSpec((nb, B_WIDTH, GK_WIDTH), lambda b, j: (b, 0, 0))
    return pl.pallas_call(
        functools.partial(_gla_kernel, n_chunks=tr // GLA_CHUNK),
        grid=(nbatch // nb, t // tr),
        in_specs=[blk(GK_WIDTH), blk(GK_WIDTH), blk(GK_WIDTH), blk(B_WIDTH), blk(B_WIDTH),
                  pl.BlockSpec((1, B_WIDTH, GK_WIDTH), lambda b, j: (0, 0, 0)),
                  pl.BlockSpec((1, B_WIDTH), lambda b, j: (0, 0))],
        out_specs=[blk(B_WIDTH), state],
        out_shape=[jax.ShapeDtypeStruct((nbatch, t, B_WIDTH), BF16),
                   jax.ShapeDtypeStruct((nbatch, B_WIDTH, GK_WIDTH), F32)],
        scratch_shapes=[pltpu.VMEM((nb, B_WIDTH, GK_WIDTH), F32)],
        compiler_params=_params("parallel", "arbitrary"),
        name="gla_prompt",
    )(seq(q), seq(k), seq(g), seq(v), seq(r), s0, gg)


def _gla_sample_kernel(q_ref, k_ref, la_ref, v_ref, r_ref, s_ref, gg_ref, ob_ref, sn_ref):
    qt = q_ref[...].T
    kt = k_ref[...].T
    at = jnp.exp(la_ref[...]).T
    vt = v_ref[...].T
    rt = r_ref[...].T
    gg = gg_ref[...]
    outs = []
    for h in range(B_HEADS):
        vh = vt[h * B_DV:(h + 1) * B_DV, :]
        o = jnp.zeros_like(vh)
        for d in range(B_DK):
            i = h * B_DK + d
            s_new = at[i:i + 1, :] * s_ref[i] + kt[i:i + 1, :] * vh
            sn_ref[i] = s_new
            o = o + qt[i:i + 1, :] * s_new
        on = o * lax.rsqrt(jnp.mean(o * o, axis=0, keepdims=True) + LN_EPS)
        outs.append(on * gg * _silu(rt[h * B_DV:(h + 1) * B_DV, :]))
    ob_ref[...] = jnp.concatenate(outs, axis=0).T.astype(BF16)


def _gla_sample(q, k, la, v, r, state_t, gg_col, layer):
    ns = state_t.shape[3]
    row = lambda width: pl.BlockSpec((ns, width), lambda i: (BLOCK // ns, 0))
    return pl.pallas_call(
        _gla_sample_kernel,
        grid=(1,),
        in_specs=[row(GK_WIDTH), row(GK_WIDTH), row(GK_WIDTH), row(B_WIDTH), row(B_WIDTH),
                  pl.BlockSpec((None, GK_WIDTH, B_DV, ns), lambda i: (layer, 0, 0, 0)),
                  pl.BlockSpec((B_DV, 1), lambda i: (0, 0))],
        out_specs=[pl.BlockSpec((ns, B_WIDTH), lambda i: (0, 0)),
                   pl.BlockSpec((GK_WIDTH, B_DV, ns), lambda i: (0, 0, 0))],
        out_shape=[jax.ShapeDtypeStruct((ns, B_WIDTH), BF16),
                   jax.ShapeDtypeStruct((GK_WIDTH, B_DV, ns), F32)],
        compiler_params=_params("arbitrary"),
        name="gla_sample",
    )(q, k, la, v, r, state_t, gg_col)


CONV_HIST = 32
CONV_CHUNK = 64
CONV_GROUP = 2


def _conv_post(y, dwb, lng, lnb, pww, pwb):
    y = _silu(_ln_rows(y + dwb, lng, lnb))
    return _dot(y.astype(BF16), pww) + pwb


def _conv_kernel(u_ref, hist_ref, dww_ref, dwb_ref, lng_ref, lnb_ref, pww_ref, pwb_ref,
                 oc_ref, tail_ref, ubuf, *, n_chunks):
    t = u_ref.shape[1]
    ubuf[0, 0:CONV_HIST, :] = hist_ref[...]
    ubuf[0, CONV_HIST:CONV_HIST + t, :] = u_ref[0]
    off = CONV_HIST - (CONV_W - 1)
    n_copy = t + CONV_HIST - SUBLANES
    for s in range(1, SUBLANES):
        ubuf[s, 0:n_copy, :] = ubuf[0, s:s + n_copy, :]

    group = CONV_GROUP if n_chunks % CONV_GROUP == 0 else 1

    def body(c, carry):
        starts = [pl.multiple_of((c * group + i) * CONV_CHUNK, CONV_CHUNK) for i in range(group)]
        accs = [jnp.zeros((CONV_CHUNK, C_WIDTH), F32) for _ in starts]
        for j in range(CONV_W):
            a, s = divmod(j + off, SUBLANES)
            w_j = dww_ref[j:j + 1, :]
            for i, r0 in enumerate(starts):
                rows = pl.ds(pl.multiple_of(r0 + a * SUBLANES, SUBLANES), CONV_CHUNK)
                accs[i] = accs[i] + ubuf[s, rows, :] * w_j
        for r0, acc in zip(starts, accs):
            oc = _conv_post(acc, dwb_ref[...], lng_ref[...], lnb_ref[...], pww_ref[...], pwb_ref[...])
            oc_ref[0, pl.ds(r0, CONV_CHUNK), :] = oc.astype(BF16)
        return carry

    lax.fori_loop(0, n_chunks // group, body, 0)
    tail_ref[0] = ubuf[0, t:t + CONV_HIST, :]


def _conv_prompt(u, hist, dww, dwb, lng, lnb, pww, pwb, nbatch):
    t = u.shape[0] // nbatch
    const = lambda shape: pl.BlockSpec(shape, lambda b: (0,) * len(shape))
    return pl.pallas_call(
        functools.partial(_conv_kernel, n_chunks=t // CONV_CHUNK),
        grid=(nbatch,),
        in_specs=[pl.BlockSpec((1, t, C_WIDTH), lambda b: (b, 0, 0)), const((CONV_HIST, C_WIDTH)),
                  const((CONV_HIST, C_WIDTH)), const((1, C_WIDTH)), const((1, C_WIDTH)), const((1, C_WIDTH)),
                  const((C_WIDTH, C_WIDTH)), const((1, C_WIDTH))],
        out_specs=[pl.BlockSpec((1, t, C_WIDTH), lambda b: (b, 0, 0)),
                   pl.BlockSpec((1, CONV_HIST, C_WIDTH), lambda b: (b, 0, 0))],
        out_shape=[jax.ShapeDtypeStruct((nbatch, t, C_WIDTH), BF16),
                   jax.ShapeDtypeStruct((nbatch, CONV_HIST, C_WIDTH), F32)],
        scratch_shapes=[pltpu.VMEM((SUBLANES, CONV_HIST + t, C_WIDTH), F32)],
        compiler_params=_params("parallel"),
        name="conv_prompt",
    )(u.reshape(nbatch, t, C_WIDTH), hist, dww, dwb, lng, lnb, pww, pwb)


def _conv_sample_kernel(hist_ref, u_ref, dww_ref, dwb_ref, lng_ref, lnb_ref, pww_ref, pwb_ref, oc_ref, tail_ref):
    nh = CONV_W - 1
    u = u_ref[...]
    y = u * dww_ref[nh:nh + 1, :]
    for j in range(nh):
        y = y + hist_ref[j] * dww_ref[j:j + 1, :]
    oc = _conv_post(y, dwb_ref[...], lng_ref[...], lnb_ref[...], pww_ref[...], pwb_ref[...])
    oc_ref[...] = oc.astype(BF16)
    for j in range(nh - 1):
        tail_ref[j] = hist_ref[j + 1]
    tail_ref[nh - 1] = u


def _conv_sample(hist_t, u, dww, dwb, lng, lnb, pww, pwb, layer):
    ns = hist_t.shape[2]
    nh = CONV_W - 1
    const = lambda shape: pl.BlockSpec(shape, lambda i: (0,) * len(shape))
    return pl.pallas_call(
        _conv_sample_kernel,
        grid=(1,),
        in_specs=[pl.BlockSpec((None, nh, ns, C_WIDTH), lambda i: (layer, 0, 0, 0)),
                  pl.BlockSpec((ns, C_WIDTH), lambda i: (BLOCK // ns, 0)),
                  const((CONV_HIST, C_WIDTH)), const((1, C_WIDTH)), const((1, C_WIDTH)), const((1, C_WIDTH)),
                  const((C_WIDTH, C_WIDTH)), const((1, C_WIDTH))],
        out_specs=[const((ns, C_WIDTH)), const((nh, ns, C_WIDTH))],
        out_shape=[jax.ShapeDtypeStruct((ns, C_WIDTH), BF16),
                   jax.ShapeDtypeStruct((nh, ns, C_WIDTH), F32)],
        compiler_params=_params("arbitrary"),
        name="conv_sample",
    )(hist_t, u, dww, dwb, lng, lnb, pww, pwb)


ROUTE_LANE0 = N_GROUPS


def _route(x, wr2, br):
    x_hi = x.astype(BF16)
    x_lo = (x - x_hi.astype(F32)).astype(BF16)
    l_hi = _dot(x_hi, wr2)
    logits = l_hi[:, :LANES] + l_hi[:, LANES:] + _dot(x_lo, wr2)[:, :LANES] + br
    lane = lax.broadcasted_iota(jnp.int32, logits.shape, 1).astype(F32)
    far = 1e3
    glm = jnp.where(lane < N_GROUPS, logits, NEG)
    gmax = jnp.max(glm, axis=1, keepdims=True)
    gi = jnp.min(jnp.where(glm == gmax, lane, far), axis=1, keepdims=True)
    p_grp = 1.0 / jnp.sum(jnp.exp(glm - gmax), axis=1, keepdims=True)
    lo = ROUTE_LANE0 + EXP_PER_GROUP * gi
    in_sel = (lane >= lo) & (lane < lo + EXP_PER_GROUP)
    elm = jnp.where(in_sel, logits, NEG)
    v1 = jnp.max(elm, axis=1, keepdims=True)
    i1 = jnp.min(jnp.where(elm == v1, lane, far), axis=1, keepdims=True)
    elm2 = jnp.where(lane == i1, NEG, elm)
    v2 = jnp.max(elm2, axis=1, keepdims=True)
    i2 = jnp.min(jnp.where((elm2 == v2) & in_sel & (lane != i1), lane, far), axis=1, keepdims=True)
    t = jnp.exp(v2 - v1)
    w1 = p_grp / (1.0 + t)
    w2 = w1 * t
    return jnp.where(lane == i1, w1, 0.0) + jnp.where(lane == i2, w2, 0.0), gi


PROJ_ROWS = 1024
FFN_ROWS = 512
FFN_CHUNK = 128
MOE_ROWS = 128
MOE_SINGLE = (128, 144, 176)
MOE_ALIGN = 16
GROUP_WIDTH = EXP_PER_GROUP * D_EXPERT


def _moe_sorted_rows(tm):
    need = tm + N_GROUPS * MOE_ALIGN + max(MOE_SINGLE + (MOE_ROWS,))
    return -(-need // LANES) * LANES


def _ffn_kernel(*refs, pre_ln):
    if pre_ln:
        lg_ref, lb_ref, *refs = refs
    (oa_ref, ob_ref, oc_ref, h_ref, wo_ref, g1_ref, b1_ref, wr_ref, br_ref, wg_ref, wu_ref, wd_ref,
     g_ref, b_ref, o_ref, x_ref, xs_ref, ys_ref, ws_ref) = refs
    tm = h_ref.shape[0]
    ns = xs_ref.shape[0]
    iota = lambda shape, axis: lax.broadcasted_iota(jnp.int32, shape, axis)
    ch = FFN_CHUNK if tm % FFN_CHUNK == 0 else tm
    chunks = [slice(r, r + ch) for r in range(0, tm, ch)]
    mixes = [(_dot(oa_ref[r, :], wo_ref[0:A_WIDTH, :])
              + _dot(ob_ref[r, :], wo_ref[A_WIDTH:A_WIDTH + B_WIDTH, :])
              + _dot(oc_ref[r, :], wo_ref[A_WIDTH + B_WIDTH:D_MODEL, :])) for r in chunks]
    xs_rows = []
    for r, mix in zip(chunks, mixes):
        h = h_ref[r, :]
        if pre_ln:
            h = _ln_rows(h, lg_ref[...], lb_ref[...])
        x_r = _ln_rows(ALPHA * h + mix, g1_ref[...], b1_ref[...])
        x_ref[r, :] = x_r
        xs_rows.append(x_r)
    routed = [_route(x_r, wr_ref[...], br_ref[...]) for x_r in xs_rows]
    x = jnp.concatenate(xs_rows, axis=0)
    dw = jnp.concatenate([d for d, _ in routed], axis=0)
    gi = jnp.concatenate([i for _, i in routed], axis=0)
    lane_f = iota((tm, LANES), 1).astype(F32)
    onehot = jnp.where(lane_f == gi, 1.0, 0.0)
    tri = jnp.where(iota((ch, ch), 1) <= iota((ch, ch), 0), 1.0, 0.0).astype(BF16)
    onehot_b = onehot.astype(BF16)
    carry = jnp.zeros((1, LANES), F32)
    cums = []
    for r in chunks:
        cums.append(_dot(tri, onehot_b[r]) + carry)
        carry = cums[-1][ch - 1:ch, :]
    cum = jnp.concatenate(cums, axis=0)
    rank = jnp.sum(onehot * (cum - 1.0), axis=1, keepdims=True)
    counts = cum[tm - 1:tm, :]

    starts, tiles, counts_i = [], [], []
    start = jnp.int32(0)
    for grp in range(N_GROUPS):
        n_g = counts[0, grp].astype(jnp.int32)
        counts_i.append(n_g)
        starts.append(start)
        tiles.append(sum((n_g > k * MOE_ROWS).astype(jnp.int32) for k in range(-(-tm // MOE_ROWS))))
        start = start + ((n_g + (MOE_ALIGN - 1)) & -MOE_ALIGN)
    lane1 = iota((1, LANES), 1)
    start_v = jnp.zeros((1, LANES), F32)
    for grp in range(N_GROUPS):
        start_v = jnp.where(lane1 == grp, starts[grp].astype(F32), start_v)
    pos = jnp.sum(onehot * start_v, axis=1, keepdims=True) + rank

    pos_i = pos.astype(jnp.int32)
    digits = jnp.where(lane_f == 0.0, (pos_i >> 5).astype(F32),
                       jnp.where(lane_f == 1.0, (pos_i & 31).astype(F32), 0.0)).astype(BF16)
    lane8 = iota((8, LANES), 1)
    radix = jnp.where(lane8 == 0, 32.0, jnp.where(lane8 == 1, 1.0, 0.0)).astype(BF16)
    pos_row = _dot_nt(radix, digits)[0:1, :]

    perm = jnp.where(iota((ns, tm), 0).astype(F32) == pos_row, 1.0, 0.0).astype(BF16)
    dw_hi = dw.astype(BF16)
    dw_lo = (dw - dw_hi.astype(F32)).astype(BF16)
    srt = _dot(perm, jnp.concatenate([x.astype(BF16), dw_hi, dw_lo], axis=1))
    xs_ref[...] = srt[:, :D_MODEL].astype(BF16)
    ws_ref[...] = srt[:, D_MODEL:D_MODEL + LANES] + srt[:, D_MODEL + LANES:]
    ys_ref[...] = jnp.zeros_like(ys_ref)

    def expert_tile(grp, r0, n_rows):
        rows = pl.ds(pl.multiple_of(r0, MOE_ALIGN), n_rows)
        xt = xs_ref[rows, :]
        w = ws_ref[rows, :]
        hes = []
        for e in range(EXP_PER_GROUP):
            ex = grp * EXP_PER_GROUP + e
            w_e = w[:, ROUTE_LANE0 + ex:ROUTE_LANE0 + ex + 1]
            hes.append((_silu(_dot(xt, wg_ref[ex])) * _dot(xt, wu_ref[ex]) * w_e).astype(BF16))
        ys_ref[rows, :] = _dot(jnp.concatenate(hes, axis=1), wd_ref[grp]).astype(BF16)

    for grp in range(N_GROUPS):
        n_g, lo = counts_i[grp], 0
        for size in MOE_SINGLE:
            pl.when((n_g > lo) & (n_g <= size))(functools.partial(expert_tile, grp, starts[grp], size))
            lo = size

        @pl.when(n_g > lo)
        def _(grp=grp):
            lax.fori_loop(0, tiles[grp],
                          lambda k, c: (expert_tile(grp, starts[grp] + k * MOE_ROWS, MOE_ROWS), c)[1], 0)

    lane_ns = iota((ch, ns), 1).astype(F32)
    ys = ys_ref[...]
    y_rows = [_dot(jnp.where(lane_ns == pos[r], 1.0, 0.0).astype(BF16), ys) for r in chunks]
    for r, y in zip(chunks, y_rows):
        o_ref[r, :] = _ln_rows(ALPHA * x_ref[r, :] + y, g_ref[...], b_ref[...])


def _ffn(oa, ob, oc, h, wo, g1, b1, wr2, br, wg, wu, wd, g2, b2, tm, layer, ln=None):
    n = h.shape[0]
    ns = _moe_sorted_rows(tm)
    assert ns <= 32 * 32, "sorted positions are transposed as two base-32 digits"
    row = lambda width: pl.BlockSpec((tm, width), lambda i: (i, 0))
    const = lambda shape: pl.BlockSpec(shape, lambda i: (0,) * len(shape))
    resident = lambda shape: pl.BlockSpec((None,) + shape, lambda i: (layer,) + (0,) * len(shape),
                                          pipeline_mode=pl.Buffered(1))
    ln_specs = [const((1, D_MODEL)), const((1, D_MODEL))] if ln else []
    return pl.pallas_call(
        functools.partial(_ffn_kernel, pre_ln=bool(ln)),
        grid=(n // tm,),
        in_specs=ln_specs + [row(A_WIDTH), row(B_WIDTH), row(C_WIDTH), row(D_MODEL),
                             resident((D_MODEL, D_MODEL)), const((1, D_MODEL)), const((1, D_MODEL)),
                             resident((D_MODEL, 2 * LANES)), const((1, LANES)),
                             resident((N_EXPERTS, D_MODEL, D_EXPERT)), resident((N_EXPERTS, D_MODEL, D_EXPERT)),
                             resident((N_GROUPS, GROUP_WIDTH, D_MODEL)),
                             const((1, D_MODEL)), const((1, D_MODEL))],
        out_specs=row(D_MODEL),
        out_shape=jax.ShapeDtypeStruct((n, D_MODEL), F32),
        scratch_shapes=[pltpu.VMEM((tm, D_MODEL), F32), pltpu.VMEM((ns, D_MODEL), BF16),
                        pltpu.VMEM((ns, D_MODEL), BF16), pltpu.VMEM((ns, LANES), F32)],
        compiler_params=_params("parallel"),
        name="ffn",
    )(*(ln or ()), oa, ob, oc, h, wo, g1, b1, wr2, br, wg, wu, wd, g2, b2)


def _rope_tables(pos):
    half = HEAD_DIM // 2
    inv = ROPE_THETA ** (-jnp.arange(half, dtype=F32) / half)
    ang = pos.astype(F32)[:, None] * inv[None, :]
    cos, sin = jnp.cos(ang), jnp.sin(ang)
    cos_t = jnp.concatenate([cos, cos] * (LANES // HEAD_DIM), axis=1)
    sin_t = jnp.concatenate([-sin, sin] * (LANES // HEAD_DIM), axis=1)
    return cos_t, sin_t


def _row(v):
    return v.reshape(1, -1)


def kernel(x_prompt, x_sample, cache_meta_k, cache_meta_v, cache_win_k, cache_win_v, state_gla, state_conv,
           meta_tokens, ln_in_g, ln_in_b, w_in, attn_sink, w_alpha, b_alpha, gla_norm_g,
           conv_dw_w, conv_dw_b, conv_ln_g, conv_ln_b, conv_pw_w, conv_pw_b, w_out, ln1_g, ln1_b,
           w_router_group, b_router_group, w_router_expert, b_router_expert,
           w_exp_gate, w_exp_up, w_exp_down, ln2_g, ln2_b):
    nb, seq, d = x_prompt.shape
    ns = x_sample.shape[0]
    nwin = cache_win_k.shape[2]
    n_big = nb * seq
    n_small = BLOCK + ns
    tm_big = min(FFN_ROWS, seq)
    tm_proj = min(PROJ_ROWS, seq)

    small_in = jnp.concatenate([jnp.zeros((META_PAD, d), F32), meta_tokens.astype(F32),
                                x_sample.reshape(ns, d)], axis=0)
    hb = x_prompt.reshape(n_big, d)
    hs = small_in
    ln_in = (_row(ln_in_g), _row(ln_in_b))

    cos_b, sin_b = _rope_tables(N_META + jnp.arange(seq))
    pos_small = jnp.concatenate([jnp.maximum(jnp.arange(BLOCK) - META_PAD, 0),
                                 jnp.full((ns,), PAST_LEN, jnp.int32)])
    cos_s, sin_s = _rope_tables(pos_small)

    outs = [[] for _ in range(12)]
    zeros_hist = jnp.zeros((CONV_HIST, C_WIDTH), F32)
    zeros_state = jnp.zeros((1, B_WIDTH, GK_WIDTH), F32)

    nl = w_in.shape[0]
    w_pad = jnp.concatenate([w_in[:, :, :C_AB + B_RANK], jnp.zeros((nl, d, LANES - B_RANK), F32),
                             w_in[:, :, C_AB + B_RANK:]], axis=2).astype(BF16)
    wa_all = jnp.concatenate([w_alpha, jnp.zeros((nl, LANES - B_RANK, GK_WIDTH), F32)], axis=1).astype(BF16)
    ba_all = b_alpha[:, None, :]
    wo_all = w_out.astype(BF16)
    wr = jnp.concatenate([w_router_group, w_router_expert,
                          jnp.zeros((nl, d, LANES - N_GROUPS - N_EXPERTS), F32)], axis=2)
    wr_hi = wr.astype(BF16)
    wr2_all = jnp.concatenate([wr_hi, (wr - wr_hi.astype(F32)).astype(BF16)], axis=2)
    wg_all, wu_all = w_exp_gate.astype(BF16), w_exp_up.astype(BF16)
    wd_all = w_exp_down.astype(BF16).reshape(nl, N_GROUPS, GROUP_WIDTH, d)
    win_k_t = cache_win_k.transpose(0, 1, 3, 4, 2).reshape(nl, ns, KV_WIDTH, nwin)
    win_v_t = cache_win_v.transpose(0, 1, 3, 4, 2).reshape(nl, ns, KV_WIDTH, nwin)
    gla_t = state_gla.transpose(0, 2, 3, 4, 1).reshape(nl, GK_WIDTH, B_DV, ns)
    conv_t = state_conv.transpose(0, 2, 1, 3)

    for l in range(DEPTH):
        sink = attn_sink[l].astype(F32)
        sink_col = jnp.concatenate([sink, jnp.zeros((HEAD_ROWS - A_HEADS,), F32)])[:, None]
        gg_t = _row(jnp.tile(gla_norm_g[l], B_HEADS))
        gg_col = gla_norm_g[l][:, None]
        dww = jnp.concatenate([conv_dw_w[l], jnp.zeros((CONV_HIST - CONV_W, C_WIDTH), F32)], axis=0)
        dwb, clg, clb = _row(conv_dw_b[l]), _row(conv_ln_g[l]), _row(conv_ln_b[l])
        pww, pwb = conv_pw_w[l].astype(BF16), _row(conv_pw_b[l])
        br = _row(jnp.concatenate([b_router_group[l], b_router_expert[l],
                                   jnp.zeros((LANES - N_GROUPS - N_EXPERTS,), F32)]))

        ln = ln_in if l == 0 else None
        qa_b, ka_b, va_b, qg_b, kg_b, la_b, vg_b, rg_b, u_b = _proj_in(
            hb, w_pad, wa_all, ba_all, cos_b, sin_b, layer=l, tm=tm_proj, n_pad=0, n_seq=tm_proj, ln=ln)
        qa_s, ka_s, va_s, qg_s, kg_s, la_s, vg_s, rg_s, u_s = _proj_in(
            hs, w_pad, wa_all, ba_all, cos_s, sin_s, layer=l, tm=n_small, n_pad=META_PAD, n_seq=BLOCK, ln=ln)

        oa_b = _attn_prompt(sink, qa_b, ka_b, va_b, ka_s, va_s, nb)
        oa_m = _attn_meta(sink, qa_s, ka_s, va_s)
        q_smp = qa_s[BLOCK:].reshape(ns, A_KV_HEADS, A_GROUP, 1, HEAD_DIM)
        eye = jnp.eye(A_KV_HEADS, dtype=BF16)[None, :, None, :, None]
        q_bd = (q_smp * eye).reshape(ns, A_HEADS, KV_WIDTH)
        q_bd = jnp.pad(q_bd, ((0, 0), (0, HEAD_ROWS - A_HEADS), (0, 0)))
        o_bd, nwk_t, nwv_t = _attn_sample(
            sink_col, q_bd, ka_s, va_s,
            cache_meta_k[l].reshape(ns, N_META, KV_WIDTH), cache_meta_v[l].reshape(ns, N_META, KV_WIDTH),
            win_k_t, win_v_t, l)
        o_bd = o_bd[:, :A_HEADS].reshape(ns, A_KV_HEADS, A_GROUP, A_KV_HEADS, HEAD_DIM)
        oa_smp = jnp.stack([o_bd[:, c, :, c, :] for c in range(A_KV_HEADS)], axis=1).reshape(ns, A_WIDTH)
        oa_s = jnp.concatenate([oa_m, oa_smp.astype(BF16)], axis=0)

        ob_m, st_m = _gla_prompt(qg_s[:BLOCK], kg_s[:BLOCK], la_s[:BLOCK], vg_s[:BLOCK], rg_s[:BLOCK],
                                 zeros_state, gg_t, 1)
        ob_b, st_b = _gla_prompt(qg_b, kg_b, la_b, vg_b, rg_b, st_m, gg_t, nb)
        ob_smp, s_new_t = _gla_sample(qg_s, kg_s, la_s, vg_s, rg_s, gla_t, gg_col, l)
        ob_s = jnp.concatenate([ob_m.reshape(BLOCK, B_WIDTH), ob_smp], axis=0)
        st5 = st_b.reshape(nb, B_HEADS, B_DV, B_HEADS, B_DK)
        gla_p = jnp.stack([st5[:, h, :, h, :] for h in range(B_HEADS)], axis=1).transpose(0, 1, 3, 2)

        oc_m, _ = _conv_prompt(u_s[:BLOCK], zeros_hist, dww, dwb, clg, clb, pww, pwb, 1)
        oc_b, tail_b = _conv_prompt(u_b, u_s[BLOCK - CONV_HIST:BLOCK], dww, dwb, clg, clb, pww, pwb, nb)
        oc_smp, tail_s_t = _conv_sample(conv_t, u_s, dww, dwb, clg, clb, pww, pwb, l)
        oc_s = jnp.concatenate([oc_m.reshape(BLOCK, C_WIDTH), oc_smp], axis=0)

        l1g, l1b, l2g, l2b = _row(ln1_g[l]), _row(ln1_b[l]), _row(ln2_g[l]), _row(ln2_b[l])
        hb = _ffn(oa_b, ob_b.reshape(n_big, B_WIDTH), oc_b.reshape(n_big, C_WIDTH), hb, wo_all, l1g, l1b,
                  wr2_all, br, wg_all, wu_all, wd_all, l2g, l2b, tm_big, l, ln=ln)
        hs = _ffn(oa_s, ob_s, oc_s, hs, wo_all, l1g, l1b, wr2_all, br, wg_all, wu_all, wd_all, l2g, l2b,
                  n_small, l, ln=ln)

        kv4 = lambda a: a.reshape(a.shape[0], a.shape[1], A_KV_HEADS, HEAD_DIM)
        win_t = lambda a: a.reshape(ns, A_KV_HEADS, HEAD_DIM, nwin).transpose(0, 3, 1, 2)
        meta_k = jnp.broadcast_to(ka_s[META_PAD:BLOCK][None], (nb, N_META, KV_WIDTH))
        meta_v = jnp.broadcast_to(va_s[META_PAD:BLOCK][None], (nb, N_META, KV_WIDTH))
        win_k = ka_b.reshape(nb, seq, KV_WIDTH)[:, seq - nwin:]
        win_v = va_b.reshape(nb, seq, KV_WIDTH)[:, seq - nwin:]
        layer_out = (None, None, kv4(meta_k), kv4(meta_v), kv4(win_k), kv4(win_v), win_t(nwk_t), win_t(nwv_t),
                     gla_p, s_new_t.reshape(B_HEADS, B_DK, B_DV, ns).transpose(3, 0, 1, 2),
                     tail_b[:, CONV_HIST - (CONV_W - 1):], tail_s_t.transpose(1, 0, 2))
        for i in range(2, 12):
            outs[i].append(layer_out[i])

    y_prompt = hb.reshape(nb, seq, d)
    y_sample = hs[BLOCK:].reshape(ns, 1, d)
    return (y_prompt, y_sample) + tuple(jnp.stack(o) for o in outs[2:])
```

```python
import functools

import jax
import jax.numpy as jnp
from jax import lax
from jax.experimental import pallas as pl
from jax.experimental.pallas import tpu as pltpu

F32 = jnp.float32
BF16 = jnp.bfloat16

D_MODEL = 1024
DEPTH = 2
PAST_LEN = 16384
N_META = 16
HEAD_DIM = 64
A_WIDTH = 512
A_HEADS = 8
A_KV_HEADS = 2
A_GROUP = 4
WINDOW = 128
BLOCK = 128
ROPE_THETA = 10000.0
B_WIDTH = 256
B_HEADS = 4
B_DV = 64
B_DK = 32
B_RANK = 16
GATE_TAU = 16.0
GLA_CHUNK = 64
C_WIDTH = 256
CONV_W = 31
N_GROUPS = 4
EXP_PER_GROUP = 4
N_EXPERTS = 16
D_EXPERT = 256
ALPHA = (2 * DEPTH) ** 0.25
LN_EPS = 1e-5

LANES = 128
SUBLANES = 8
META_PAD = BLOCK - N_META
KV_WIDTH = A_KV_HEADS * HEAD_DIM
GK_WIDTH = B_HEADS * B_DK
C_QA = 0
C_KA = C_QA + A_WIDTH
C_VA = C_KA + KV_WIDTH
C_QB = C_VA + KV_WIDTH
C_KB = C_QB + GK_WIDTH
C_VB = C_KB + GK_WIDTH
C_RB = C_VB + B_WIDTH
C_AB = C_RB + B_WIDTH
C_CG = C_AB + LANES
PROJ_PAD_WIDTH = C_CG + 2 * C_WIDTH
NEG = -1e30
VMEM_LIMIT = 56 * 1024 * 1024


def _dot(a, b):
    return jnp.dot(a, b, preferred_element_type=F32)


def _dot_nt(a, b):
    return lax.dot_general(a, b, (((1,), (1,)), ((), ())), preferred_element_type=F32)


def _dot_tn(a, b):
    return lax.dot_general(a, b, (((0,), (0,)), ((), ())), preferred_element_type=F32)


def _ln_rows(x, g, b):
    xc = x - jnp.mean(x, -1, keepdims=True)
    var = jnp.mean(xc * xc, -1, keepdims=True)
    return xc * lax.rsqrt(var + LN_EPS) * g + b


def _silu(x):
    return x * jax.nn.sigmoid(x)


def _split3(x):
    hi = x.astype(BF16)
    r1 = x - hi.astype(F32)
    mid = r1.astype(BF16)
    lo = (r1 - mid.astype(F32)).astype(BF16)
    return hi, mid, lo


def _params(*sem):
    return pltpu.CompilerParams(dimension_semantics=sem, vmem_limit_bytes=VMEM_LIMIT)


def _proj_in_kernel(*refs, n_pad, n_seq, pre_ln):
    if pre_ln:
        lg_ref, lb_ref, *refs = refs
    (x_ref, w_ref, wa_ref, ba_ref, cos_ref, sin_ref,
     qa_ref, ka_ref, va_ref, qg_ref, kg_ref, la_ref, vg_ref, rg_ref, u_ref) = refs
    x = x_ref[...]
    if pre_ln:
        x = _ln_rows(x, lg_ref[...], lb_ref[...])
    xb = x.astype(BF16)
    tm = xb.shape[0]
    za = _dot(xb, w_ref[:, C_QA:C_QB])
    cos = cos_ref[...]
    sin = sin_ref[...]
    lane = lax.broadcasted_iota(jnp.int32, (tm, LANES), 1)
    first_half = (lane & (HEAD_DIM // 2)) == 0

    def rope(z):
        rot = jnp.where(first_half, pltpu.roll(z, LANES - HEAD_DIM // 2, 1), pltpu.roll(z, HEAD_DIM // 2, 1))
        return z * cos + rot * sin

    if n_pad:
        valid = (lax.broadcasted_iota(jnp.int32, (tm, 1), 0) >= n_pad).astype(F32)
    else:
        valid = None

    zb = _dot(xb, w_ref[:, C_QB:C_CG])
    zc = _dot(xb, w_ref[:, C_CG:PROJ_PAD_WIDTH])
    for c in range(A_WIDTH // LANES):
        zq = za[:, c * LANES:(c + 1) * LANES]
        qa_ref[:, c * LANES:(c + 1) * LANES] = (rope(zq) * (HEAD_DIM ** -0.5)).astype(BF16)
    ka_ref[...] = rope(za[:, C_KA:C_VA])
    va_ref[...] = za[:, C_VA:C_QB]

    o = C_QB
    qg_ref[...] = zb[:, C_QB - o:C_KB - o] * (B_DK ** -0.5)
    kg = zb[:, C_KB - o:C_VB - o]
    vg_ref[...] = zb[:, C_VB - o:C_RB - o]
    rg_ref[...] = zb[:, C_RB - o:C_AB - o]
    ab = zb[:, C_AB - o:C_CG - o].astype(BF16)
    xa = _dot(ab, wa_ref[...]) + ba_ref[...]
    la = (jnp.minimum(xa, 0.0) - jnp.log(1.0 + jnp.exp(-jnp.abs(xa)))) * (1.0 / GATE_TAU)

    u = zc[:, :C_WIDTH] * jax.nn.sigmoid(zc[:, C_WIDTH:])
    if valid is not None:
        kg = kg * valid
        la = la * valid
        u = u * valid
    kg_ref[...] = kg
    u_ref[...] = u
    cs = GLA_CHUNK
    tri = jnp.where(lax.broadcasted_iota(jnp.int32, (cs, cs), 1) <= lax.broadcasted_iota(jnp.int32, (cs, cs), 0),
                    1.0, 0.0).astype(BF16)
    la_h, la_m, la_l = _split3(la)
    for c in range(n_seq // cs):
        rows = slice(c * cs, (c + 1) * cs)
        la_ref[rows, :] = _dot(tri, la_h[rows]) + _dot(tri, la_m[rows]) + _dot(tri, la_l[rows])
    if n_seq < tm:
        la_ref[n_seq:tm, :] = la[n_seq:tm]


def _proj_in(h, w, wa, ba, cos, sin, *, layer, tm, n_pad, n_seq, ln=None):
    n = h.shape[0]
    tb = cos.shape[0] // tm
    row = lambda width: pl.BlockSpec((tm, width), lambda i: (i, 0))
    const = lambda shape: pl.BlockSpec(shape, lambda i: (0, 0))
    per_layer = lambda shape: pl.BlockSpec((None,) + shape, lambda i: (layer, 0, 0))
    tab = pl.BlockSpec((tm, LANES), lambda i: (i % tb, 0))
    widths = (A_WIDTH, KV_WIDTH, KV_WIDTH, GK_WIDTH, GK_WIDTH, GK_WIDTH, B_WIDTH, B_WIDTH, C_WIDTH)
    dtypes = (BF16,) + (F32,) * 8
    ln_specs = [const((1, D_MODEL)), const((1, D_MODEL))] if ln else []
    return pl.pallas_call(
        functools.partial(_proj_in_kernel, n_pad=n_pad, n_seq=n_seq, pre_ln=bool(ln)),
        grid=(n // tm,),
        in_specs=ln_specs + [row(D_MODEL), per_layer((D_MODEL, PROJ_PAD_WIDTH)), per_layer((LANES, GK_WIDTH)),
                             per_layer((1, GK_WIDTH)), tab, tab],
        out_specs=[row(wd) for wd in widths],
        out_shape=[jax.ShapeDtypeStruct((n, wd), dt) for wd, dt in zip(widths, dtypes)],
        compiler_params=_params("parallel"),
        name="proj_in",
    )(*(ln or ()), h, w, wa, ba, cos, sin)


def _attn_kernel(sink_ref, q_ref, km_ref, vm_ref, *rest, meta_mode):
    if meta_mode:
        (o_ref,) = rest
        blocks = [(0, q_ref[...], km_ref[...], vm_ref[...], None)]
    else:
        kp_ref, vp_ref, kc_ref, vc_ref, o_ref = rest
        km, vm = km_ref[...], vm_ref[...]
        blocks = []
        for sub in range(q_ref.shape[0] // BLOCK):
            cur = slice(sub * BLOCK, (sub + 1) * BLOCK)
            if sub == 0:
                kp, vp = kp_ref[...], vp_ref[...]
                has_prev = pl.program_id(1) >= 1
            else:
                prev = slice((sub - 1) * BLOCK, sub * BLOCK)
                kp, vp = kc_ref[prev, :], vc_ref[prev, :]
                has_prev = True
            blocks.append((sub * BLOCK, q_ref[cur, :], jnp.concatenate([kp, kc_ref[cur, :], km], axis=0),
                           jnp.concatenate([vp, vc_ref[cur, :], vm], axis=0), has_prev))

    units = []
    for row0, q, k_all, v_all, has_prev in blocks:
        nk = k_all.shape[0]
        ki = lax.broadcasted_iota(jnp.int32, (nk, BLOCK), 0)
        qi = lax.broadcasted_iota(jnp.int32, (nk, BLOCK), 1)
        if has_prev is None:
            ok = ki <= qi - META_PAD
        else:
            prev_lo = qi if has_prev is True else qi + jnp.where(has_prev, 0, BLOCK)
            ok = ((ki >= prev_lo) & (ki < BLOCK)) | ((ki >= BLOCK) & (ki <= qi + BLOCK)) | (ki >= 2 * BLOCK)
        bias = jnp.where(ok, 0.0, NEG)
        kb, vb = k_all.astype(BF16), v_all.astype(BF16)
        for kvh in range(A_KV_HEADS):
            lanes = slice(kvh * HEAD_DIM, (kvh + 1) * HEAD_DIM)
            qs = jnp.concatenate([q[:, (kvh * A_GROUP + g) * HEAD_DIM:(kvh * A_GROUP + g + 1) * HEAD_DIM]
                                  for g in range(A_GROUP)], axis=0)
            units.append((row0, kvh, bias, _dot_nt(kb[:, lanes], qs), vb[:, lanes]))

    outs = []
    for row0, kvh, bias, st, vv in units:
        ps, dens = [], []
        for g in range(A_GROUP):
            s = st[:, g * BLOCK:(g + 1) * BLOCK] + bias
            sink = sink_ref[kvh * A_GROUP + g]
            m = jnp.maximum(jnp.max(s, axis=0, keepdims=True), sink)
            p = jnp.exp(s - m)
            dens.append(jnp.sum(p, axis=0, keepdims=True) + jnp.exp(sink - m))
            ps.append(p.astype(BF16))
        outs.append((row0, kvh, _dot_tn(vv, jnp.concatenate(ps, axis=1)), jnp.concatenate(dens, axis=1)))

    for row0, kvh, ot, den in outs:
        ot = ot / den
        for pair in range(A_GROUP // 2):
            two = jnp.concatenate([ot[:, (2 * pair) * BLOCK:(2 * pair + 1) * BLOCK],
                                   ot[:, (2 * pair + 1) * BLOCK:(2 * pair + 2) * BLOCK]], axis=0)
            h0 = kvh * A_GROUP + 2 * pair
            o_ref[row0:row0 + BLOCK, h0 * HEAD_DIM:(h0 + 2) * HEAD_DIM] = two.T.astype(BF16)


ATTN_SUB = 8


def _attn_prompt(sink, q, k, v, k_small, v_small, nbatch):
    n = q.shape[0]
    nblk = n // BLOCK // nbatch
    sub = ATTN_SUB if nblk % ATTN_SUB == 0 else 1
    nstep = nblk // sub
    kvs = pl.BlockSpec((sub * BLOCK, KV_WIDTH), lambda b, j: (b * nstep + j, 0))
    kvp = pl.BlockSpec((BLOCK, KV_WIDTH), lambda b, j: (b * nblk + jnp.maximum(j * sub - 1, 0), 0))
    kvm = pl.BlockSpec((N_META, KV_WIDTH), lambda b, j: (META_PAD // N_META, 0))
    return pl.pallas_call(
        functools.partial(_attn_kernel, meta_mode=False),
        grid=(nbatch, nstep),
        in_specs=[pl.BlockSpec(memory_space=pltpu.SMEM),
                  pl.BlockSpec((sub * BLOCK, A_WIDTH), lambda b, j: (b * nstep + j, 0)),
                  kvm, kvm, kvp, kvp, kvs, kvs],
        out_specs=pl.BlockSpec((sub * BLOCK, A_WIDTH), lambda b, j: (b * nstep + j, 0)),
        out_shape=jax.ShapeDtypeStruct((n, A_WIDTH), BF16),
        compiler_params=_params("parallel", "parallel"),
        name="attn_prompt",
    )(sink, q, k_small, v_small, k, v, k, v)


def _attn_meta(sink, q_small, k_small, v_small):
    kvm = pl.BlockSpec((N_META, KV_WIDTH), lambda i: (META_PAD // N_META, 0))
    return pl.pallas_call(
        functools.partial(_attn_kernel, meta_mode=True),
        grid=(1,),
        in_specs=[pl.BlockSpec(memory_space=pltpu.SMEM),
                  pl.BlockSpec((BLOCK, A_WIDTH), lambda i: (0, 0)), kvm, kvm],
        out_specs=pl.BlockSpec((BLOCK, A_WIDTH), lambda i: (0, 0)),
        out_shape=jax.ShapeDtypeStruct((BLOCK, A_WIDTH), BF16),
        compiler_params=_params("arbitrary"),
        name="attn_meta",
    )(sink, q_small, k_small, v_small)


SAMPLE_BLOCK = 32
HEAD_ROWS = 16


def _attn_sample_kernel(sink_ref, q_ref, kn_ref, vn_ref, mk_ref, mv_ref, wk_ref, wv_ref,
                        o_ref, nwk_ref, nwv_ref):
    sb = q_ref.shape[0]
    q = q_ref[...]
    kn = kn_ref[...]
    vn = vn_ref[...]
    s_win = jnp.einsum('bhl,blj->bhj', q, wk_ref[...].astype(BF16), preferred_element_type=F32)
    s_meta = jnp.einsum('bhl,bml->bhm', q, mk_ref[...].astype(BF16), preferred_element_type=F32)
    s_self = jnp.sum(q.astype(F32) * kn[:, None, :], axis=2, keepdims=True)
    sink = sink_ref[...][None]
    m = jnp.maximum(jnp.maximum(jnp.max(s_win, axis=2, keepdims=True), jnp.max(s_meta, axis=2, keepdims=True)),
                    jnp.maximum(s_self, sink))
    p_win = jnp.exp(s_win - m)
    p_meta = jnp.exp(s_meta - m)
    p_self = jnp.exp(s_self - m)
    denom = (jnp.sum(p_win, axis=2, keepdims=True) + jnp.sum(p_meta, axis=2, keepdims=True) + p_self
             + jnp.exp(sink - m))
    o = (jnp.einsum('bhj,blj->bhl', p_win.astype(BF16), wv_ref[...].astype(BF16), preferred_element_type=F32)
         + jnp.einsum('bhm,bml->bhl', p_meta.astype(BF16), mv_ref[...].astype(BF16), preferred_element_type=F32)
         + p_self * vn[:, None, :])
    o_ref[...] = o / denom
    nwin = wk_ref.shape[2]
    last = lax.broadcasted_iota(jnp.int32, (KV_WIDTH, nwin), 1) == nwin - 1
    knt = kn.T
    vnt = vn.T
    for j in range(sb):
        nwk_ref[j] = jnp.where(last, knt[:, j:j + 1], pltpu.roll(wk_ref[j], nwin - 1, 1))
        nwv_ref[j] = jnp.where(last, vnt[:, j:j + 1], pltpu.roll(wv_ref[j], nwin - 1, 1))


def _attn_sample(sink_col, q_bd, k_small, v_small, meta_k, meta_v, win_k_t, win_v_t, layer):
    ns = q_bd.shape[0]
    sb = min(SAMPLE_BLOCK, ns)
    nwin = win_k_t.shape[3]
    blk = lambda rows: pl.BlockSpec((sb, rows, KV_WIDTH), lambda i: (i, 0, 0))
    new = pl.BlockSpec((sb, KV_WIDTH), lambda i: (BLOCK // sb + i, 0))
    win = pl.BlockSpec((None, sb, KV_WIDTH, nwin), lambda i: (layer, i, 0, 0))
    win_out = pl.BlockSpec((sb, KV_WIDTH, nwin), lambda i: (i, 0, 0))
    return pl.pallas_call(
        _attn_sample_kernel,
        grid=(ns // sb,),
        in_specs=[pl.BlockSpec((HEAD_ROWS, 1), lambda i: (0, 0)),
                  blk(HEAD_ROWS), new, new, blk(N_META), blk(N_META), win, win],
        out_specs=[blk(HEAD_ROWS), win_out, win_out],
        out_shape=[jax.ShapeDtypeStruct((ns, HEAD_ROWS, KV_WIDTH), F32),
                   jax.ShapeDtypeStruct((ns, KV_WIDTH, nwin), F32),
                   jax.ShapeDtypeStruct((ns, KV_WIDTH, nwin), F32)],
        compiler_params=_params("parallel"),
        name="attn_sample",
    )(sink_col, q_bd, k_small, v_small, meta_k, meta_v, win_k_t, win_v_t)


def _seg_mean_sq(o, mseg):
    sq = o * o
    hi = sq.astype(BF16)
    lo = (sq - hi.astype(F32)).astype(BF16)
    return _dot(hi, mseg) + _dot(lo, mseg)


def _gla_kernel(q_ref, k_ref, g_ref, v_ref, r_ref, s0_ref, gg_ref, ob_ref, st_ref, st_scr, *, n_chunks):
    cs = GLA_CHUNK
    nseq = q_ref.shape[0]

    @pl.when(pl.program_id(1) == 0)
    def _():
        for b in range(nseq):
            st_scr[b] = s0_ref[0]

    iota = lambda shape, axis: lax.broadcasted_iota(jnp.int32, shape, axis)
    lg_cs, lg_dk, lg_dv = cs.bit_length() - 1, B_DK.bit_length() - 1, B_DV.bit_length() - 1
    kd_mask = (iota((B_HEADS * cs, GK_WIDTH), 0) >> lg_cs) == (iota((B_HEADS * cs, GK_WIDTH), 1) >> lg_dk)
    vd_mask = (iota((B_HEADS * cs, B_WIDTH), 0) >> lg_cs) == (iota((B_HEADS * cs, B_WIDTH), 1) >> lg_dv)
    st_mask = (iota((B_WIDTH, GK_WIDTH), 0) >> lg_dv) == (iota((B_WIDTH, GK_WIDTH), 1) >> lg_dk)
    t_idx = iota((cs, B_HEADS * cs), 0)
    s_idx = iota((cs, B_HEADS * cs), 1) & (cs - 1)
    diff = t_idx ^ s_idx
    level = jnp.full((cs, B_HEADS * cs), lg_cs, jnp.int32)
    for bit in range(lg_cs):
        level = jnp.where(diff >= (1 << bit), bit, level)
    level = jnp.where(s_idx > t_idx, -1, level)
    odd_row = (iota((cs, GK_WIDTH), 0) & 1) == 1
    mseg =jnp.where((iota((B_WIDTH, B_WIDTH), 0) >> lg_dv) == (iota((B_WIDTH, B_WIDTH), 1) >> lg_dv),
                     1.0 / B_DV, 0.0).astype(BF16)
    gg = gg_ref[...]

    def scores(qt, kt):
        kbd = jnp.where(kd_mask, jnp.concatenate([kt.astype(BF16)] * B_HEADS, axis=0), 0.0)
        return _dot_nt(qt.astype(BF16), kbd)

    def level_ref(g, bit):
        if bit == 0:
            return jnp.where(odd_row, pltpu.roll(g, 1, 0), g)
        half = 1 << bit
        return jnp.concatenate([jnp.broadcast_to(g[p + half - 1:p + half, :], (2 * half, GK_WIDTH))
                                for p in range(0, cs, 2 * half)], axis=0)

    def body(c, carry):
        rows = pl.ds(pl.multiple_of(c * cs, cs), cs)
        seqs = range(nseq)
        q = [q_ref[b, rows, :] for b in seqs]
        k = [k_ref[b, rows, :] for b in seqs]
        g = [g_ref[b, rows, :] for b in seqs]
        att = [jnp.where(level == lg_cs, scores(q[b], k[b]), 0.0) for b in seqs]
        for bit in range(lg_cs):
            for b in seqs:
                decay = jnp.exp(-jnp.abs(g[b] - level_ref(g[b], bit)))
                att[b] = jnp.where(level == bit, scores(q[b] * decay, k[b] * decay), att[b])
        for b in seqs:
            g_end = g[b][cs - 1:cs, :]
            vb = v_ref[b, rows, :].astype(BF16)
            vbd = jnp.where(vd_mask, jnp.concatenate([vb] * B_HEADS, axis=0), 0.0)
            st = st_scr[b]
            o = _dot(att[b].astype(BF16), vbd) + _dot_nt((q[b] * jnp.exp(g[b])).astype(BF16), st.astype(BF16))
            kh = (k[b] * jnp.exp(g_end - g[b])).astype(BF16)
            upd = _dot_tn(vb, kh)
            st_scr[b] = st * jnp.exp(g_end) + jnp.where(st_mask, upd, 0.0)
            on = o * lax.rsqrt(_seg_mean_sq(o, mseg) + LN_EPS)
            ob_ref[b, rows, :] = (on * gg * _silu(r_ref[b, rows, :])).astype(BF16)
        return carry

    lax.fori_loop(0, n_chunks, body, 0)

    @pl.when(pl.program_id(1) == pl.num_programs(1) - 1)
    def _():
        for b in range(nseq):
            st_ref[b] = st_scr[b]


GLA_SEQS = 8
GLA_ROWS = 512


def _gla_prompt(q, k, g, v, r, s0, gg, nbatch):
    t = q.shape[0] // nbatch
    nb = min(GLA_SEQS, nbatch)
    tr = min(GLA_ROWS, t)
    seq = lambda a: a.reshape(nbatch, t, a.shape[-1])
    blk = lambda width: pl.BlockSpec((nb, tr, width), lambda b, j: (b, j, 0))
    state = pl.BlockSpec((nb, B_WIDTH, GK_WIDTH), lambda b, j: (b, 0, 0))
    return pl.pallas_call(
        functools.partial(_gla_kernel, n_chunks=tr // GLA_CHUNK),
        grid=(nbatch // nb, t // tr),
        in_specs=[blk(GK_WIDTH), blk(GK_WIDTH), blk(GK_WIDTH), blk(B_WIDTH), blk(B_WIDTH),
                  pl.BlockSpec((1, B_WIDTH, GK_WIDTH), lambda b, j: (0, 0, 0)),
                  pl.BlockSpec((1, B_WIDTH), lambda b, j: (0, 0))],
        out_specs=[blk(B_WIDTH), state],
        out_shape=[jax.ShapeDtypeStruct((nbatch, t, B_WIDTH), BF16),
                   jax.ShapeDtypeStruct((nbatch, B_WIDTH, GK_WIDTH), F32)],
        scratch_shapes=[pltpu.VMEM((nb, B_WIDTH, GK_WIDTH), F32)],
        compiler_params=_params("parallel", "arbitrary"),
        name="gla_prompt",
    )(seq(q), seq(k), seq(g), seq(v), seq(r), s0, gg)


def _gla_sample_kernel(q_ref, k_ref, la_ref, v_ref, r_ref, s_ref, gg_ref, ob_ref, sn_ref):
    qt = q_ref[...].T
    kt = k_ref[...].T
    at = jnp.exp(la_ref[...]).T
    vt = v_ref[...].T
    rt = r_ref[...].T
    gg = gg_ref[...]
    outs = []
    for h in range(B_HEADS):
        vh = vt[h * B_DV:(h + 1) * B_DV, :]
        o = jnp.zeros_like(vh)
        for d in range(B_DK):
            i = h * B_DK + d
            s_new = at[i:i + 1, :] * s_ref[i] + kt[i:i + 1, :] * vh
            sn_ref[i] = s_new
            o = o + qt[i:i + 1, :] * s_new
        on = o * lax.rsqrt(jnp.mean(o * o, axis=0, keepdims=True) + LN_EPS)
        outs.append(on * gg * _silu(rt[h * B_DV:(h + 1) * B_DV, :]))
    ob_ref[...] = jnp.concatenate(outs, axis=0).T.astype(BF16)


def _gla_sample(q, k, la, v, r, state_t, gg_col, layer):
    ns = state_t.shape[3]
    row = lambda width: pl.BlockSpec((ns, width), lambda i: (BLOCK // ns, 0))
    return pl.pallas_call(
        _gla_sample_kernel,
        grid=(1,),
        in_specs=[row(GK_WIDTH), row(GK_WIDTH), row(GK_WIDTH), row(B_WIDTH), row(B_WIDTH),
                  pl.BlockSpec((None, GK_WIDTH, B_DV, ns), lambda i: (layer, 0, 0, 0)),
                  pl.BlockSpec((B_DV, 1), lambda i: (0, 0))],
        out_specs=[pl.BlockSpec((ns, B_WIDTH), lambda i: (0, 0)),
                   pl.BlockSpec((GK_WIDTH, B_DV, ns), lambda i: (0, 0, 0))],
        out_shape=[jax.ShapeDtypeStruct((ns, B_WIDTH), BF16),
                   jax.ShapeDtypeStruct((GK_WIDTH, B_DV, ns), F32)],
        compiler_params=_params("arbitrary"),
        name="gla_sample",
    )(q, k, la, v, r, state_t, gg_col)


CONV_HIST = 32
CONV_CHUNK = 64
CONV_GROUP = 2


def _conv_post(y, dwb, lng, lnb, pww, pwb):
    y = _silu(_ln_rows(y + dwb, lng, lnb))
    return _dot(y.astype(BF16), pww) + pwb


def _conv_kernel(u_ref, hist_ref, dww_ref, dwb_ref, lng_ref, lnb_ref, pww_ref, pwb_ref,
                 oc_ref, tail_ref, ubuf, *, n_chunks):
    t = u_ref.shape[1]
    ubuf[0, 0:CONV_HIST, :] = hist_ref[...]
    ubuf[0, CONV_HIST:CONV_HIST + t, :] = u_ref[0]
    off = CONV_HIST - (CONV_W - 1)
    n_copy = t + CONV_HIST - SUBLANES
    for s in range(1, SUBLANES):
        ubuf[s, 0:n_copy, :] = ubuf[0, s:s + n_copy, :]

    group = CONV_GROUP if n_chunks % CONV_GROUP == 0 else 1

    def body(c, carry):
        starts = [pl.multiple_of((c * group + i) * CONV_CHUNK, CONV_CHUNK) for i in range(group)]
        accs = [jnp.zeros((CONV_CHUNK, C_WIDTH), F32) for _ in starts]
        for j in range(CONV_W):
            a, s = divmod(j + off, SUBLANES)
            w_j = dww_ref[j:j + 1, :]
            for i, r0 in enumerate(starts):
                rows = pl.ds(pl.multiple_of(r0 + a * SUBLANES, SUBLANES), CONV_CHUNK)
                accs[i] = accs[i] + ubuf[s, rows, :] * w_j
        for r0, acc in zip(starts, accs):
            oc = _conv_post(acc, dwb_ref[...], lng_ref[...], lnb_ref[...], pww_ref[...], pwb_ref[...])
            oc_ref[0, pl.ds(r0, CONV_CHUNK), :] = oc.astype(BF16)
        return carry

    lax.fori_loop(0, n_chunks // group, body, 0)
    tail_ref[0] = ubuf[0, t:t + CONV_HIST, :]


def _conv_prompt(u, hist, dww, dwb, lng, lnb, pww, pwb, nbatch):
    t = u.shape[0] // nbatch
    const = lambda shape: pl.BlockSpec(shape, lambda b: (0,) * len(shape))
    return pl.pallas_call(
        functools.partial(_conv_kernel, n_chunks=t // CONV_CHUNK),
        grid=(nbatch,),
        in_specs=[pl.BlockSpec((1, t, C_WIDTH), lambda b: (b, 0, 0)), const((CONV_HIST, C_WIDTH)),
                  const((CONV_HIST, C_WIDTH)), const((1, C_WIDTH)), const((1, C_WIDTH)), const((1, C_WIDTH)),
                  const((C_WIDTH, C_WIDTH)), const((1, C_WIDTH))],
        out_specs=[pl.BlockSpec((1, t, C_WIDTH), lambda b: (b, 0, 0)),
                   pl.BlockSpec((1, CONV_HIST, C_WIDTH), lambda b: (b, 0, 0))],
        out_shape=[jax.ShapeDtypeStruct((nbatch, t, C_WIDTH), BF16),
                   jax.ShapeDtypeStruct((nbatch, CONV_HIST, C_WIDTH), F32)],
        scratch_shapes=[pltpu.VMEM((SUBLANES, CONV_HIST + t, C_WIDTH), F32)],
        compiler_params=_params("parallel"),
        name="conv_prompt",
    )(u.reshape(nbatch, t, C_WIDTH), hist, dww, dwb, lng, lnb, pww, pwb)


def _conv_sample_kernel(hist_ref, u_ref, dww_ref, dwb_ref, lng_ref, lnb_ref, pww_ref, pwb_ref, oc_ref, tail_ref):
    nh = CONV_W - 1
    u = u_ref[...]
    y = u * dww_ref[nh:nh + 1, :]
    for j in range(nh):
        y = y + hist_ref[j] * dww_ref[j:j + 1, :]
    oc = _conv_post(y, dwb_ref[...], lng_ref[...], lnb_ref[...], pww_ref[...], pwb_ref[...])
    oc_ref[...] = oc.astype(BF16)
    for j in range(nh - 1):
        tail_ref[j] = hist_ref[j + 1]
    tail_ref[nh - 1] = u


def _conv_sample(hist_t, u, dww, dwb, lng, lnb, pww, pwb, layer):
    ns = hist_t.shape[2]
    nh = CONV_W - 1
    const = lambda shape: pl.BlockSpec(shape, lambda i: (0,) * len(shape))
    return pl.pallas_call(
        _conv_sample_kernel,
        grid=(1,),
        in_specs=[pl.BlockSpec((None, nh, ns, C_WIDTH), lambda i: (layer, 0, 0, 0)),
                  pl.BlockSpec((ns, C_WIDTH), lambda i: (BLOCK // ns, 0)),
                  const((CONV_HIST, C_WIDTH)), const((1, C_WIDTH)), const((1, C_WIDTH)), const((1, C_WIDTH)),
                  const((C_WIDTH, C_WIDTH)), const((1, C_WIDTH))],
        out_specs=[const((ns, C_WIDTH)), const((nh, ns, C_WIDTH))],
        out_shape=[jax.ShapeDtypeStruct((ns, C_WIDTH), BF16),
                   jax.ShapeDtypeStruct((nh, ns, C_WIDTH), F32)],
        compiler_params=_params("arbitrary"),
        name="conv_sample",
    )(hist_t, u, dww, dwb, lng, lnb, pww, pwb)


ROUTE_LANE0 = N_GROUPS


def _route(x, wr2, br):
    x_hi = x.astype(BF16)
    x_lo = (x - x_hi.astype(F32)).astype(BF16)
    l_hi = _dot(x_hi, wr2)
    logits = l_hi[:, :LANES] + l_hi[:, LANES:] + _dot(x_lo, wr2)[:, :LANES] + br
    lane = lax.broadcasted_iota(jnp.int32, logits.shape, 1).astype(F32)
    far = 1e3
    glm = jnp.where(lane < N_GROUPS, logits, NEG)
    gmax = jnp.max(glm, axis=1, keepdims=True)
    gi = jnp.min(jnp.where(glm == gmax, lane, far), axis=1, keepdims=True)
    p_grp = 1.0 / jnp.sum(jnp.exp(glm - gmax), axis=1, keepdims=True)
    lo = ROUTE_LANE0 + EXP_PER_GROUP * gi
    in_sel = (lane >= lo) & (lane < lo + EXP_PER_GROUP)
    elm = jnp.where(in_sel, logits, NEG)
    v1 = jnp.max(elm, axis=1, keepdims=True)
    i1 = jnp.min(jnp.where(elm == v1, lane, far), axis=1, keepdims=True)
    elm2 = jnp.where(lane == i1, NEG, elm)
    v2 = jnp.max(elm2, axis=1, keepdims=True)
    i2 = jnp.min(jnp.where((elm2 == v2) & in_sel & (lane != i1), lane, far), axis=1, keepdims=True)
    t = jnp.exp(v2 - v1)
    w1 = p_grp / (1.0 + t)
    w2 = w1 * t
    return jnp.where(lane == i1, w1, 0.0) + jnp.where(lane == i2, w2, 0.0), gi


PROJ_ROWS = 1024
FFN_ROWS = 512
FFN_CHUNK = 256
MOE_ROWS = 128
MOE_SINGLE = (128, 144, 176)
MOE_ALIGN = 16
GROUP_WIDTH = EXP_PER_GROUP * D_EXPERT


def _moe_sorted_rows(tm):
    need = tm + N_GROUPS * MOE_ALIGN + max(MOE_SINGLE + (MOE_ROWS,))
    return -(-need // LANES) * LANES


def _ffn_kernel(*refs, pre_ln):
    if pre_ln:
        lg_ref, lb_ref, *refs = refs
    (oa_ref, ob_ref, oc_ref, h_ref, wo_ref, g1_ref, b1_ref, wr_ref, br_ref, wg_ref, wu_ref, wd_ref,
     g_ref, b_ref, o_ref, x_ref, xs_ref, ys_ref, ws_ref) = refs
    tm = h_ref.shape[0]
    ns = xs_ref.shape[0]
    iota = lambda shape, axis: lax.broadcasted_iota(jnp.int32, shape, axis)
    ch = FFN_CHUNK if tm % FFN_CHUNK == 0 else tm
    chunks = [slice(r, r + ch) for r in range(0, tm, ch)]
    mixes = [(_dot(oa_ref[r, :], wo_ref[0:A_WIDTH, :])
              + _dot(ob_ref[r, :], wo_ref[A_WIDTH:A_WIDTH + B_WIDTH, :])
              + _dot(oc_ref[r, :], wo_ref[A_WIDTH + B_WIDTH:D_MODEL, :])) for r in chunks]
    xs_rows = []
    for r, mix in zip(chunks, mixes):
        h = h_ref[r, :]
        if pre_ln:
            h = _ln_rows(h, lg_ref[...], lb_ref[...])
        x_r = _ln_rows(ALPHA * h + mix, g1_ref[...], b1_ref[...])
        x_ref[r, :] = x_r
        xs_rows.append(x_r)
    routed = [_route(x_r, wr_ref[...], br_ref[...]) for x_r in xs_rows]
    x = jnp.concatenate(xs_rows, axis=0)
    dw = jnp.concatenate([d for d, _ in routed], axis=0)
    gi = jnp.concatenate([i for _, i in routed], axis=0)
    lane_f = iota((tm, LANES), 1).astype(F32)
    onehot = jnp.where(lane_f == gi, 1.0, 0.0)
    tri = jnp.where(iota((ch, ch), 1) <= iota((ch, ch), 0), 1.0, 0.0).astype(BF16)
    onehot_b = onehot.astype(BF16)
    carry = jnp.zeros((1, LANES), F32)
    cums = []
    for r in chunks:
        cums.append(_dot(tri, onehot_b[r]) + carry)
        carry = cums[-1][ch - 1:ch, :]
    cum = jnp.concatenate(cums, axis=0)
    rank = jnp.sum(onehot * (cum - 1.0), axis=1, keepdims=True)
    counts = cum[tm - 1:tm, :]

    starts, tiles, counts_i = [], [], []
    start = jnp.int32(0)
    for grp in range(N_GROUPS):
        n_g = counts[0, grp].astype(jnp.int32)
        counts_i.append(n_g)
        starts.append(start)
        tiles.append(sum((n_g > k * MOE_ROWS).astype(jnp.int32) for k in range(-(-tm // MOE_ROWS))))
        start = start + ((n_g + (MOE_ALIGN - 1)) & -MOE_ALIGN)
    lane1 = iota((1, LANES), 1)
    start_v = jnp.zeros((1, LANES), F32)
    for grp in range(N_GROUPS):
        start_v = jnp.where(lane1 == grp, starts[grp].astype(F32), start_v)
    pos = jnp.sum(onehot * start_v, axis=1, keepdims=True) + rank

    pos_i = pos.astype(jnp.int32)
    digits = jnp.where(lane_f == 0.0, (pos_i >> 5).astype(F32),
                       jnp.where(lane_f == 1.0, (pos_i & 31).astype(F32), 0.0)).astype(BF16)
    lane8 = iota((8, LANES), 1)
    radix = jnp.where(lane8 == 0, 32.0, jnp.where(lane8 == 1, 1.0, 0.0)).astype(BF16)
    pos_row = _dot_nt(radix, digits)[0:1, :]

    perm = jnp.where(iota((ns, tm), 0).astype(F32) == pos_row, 1.0, 0.0).astype(BF16)
    dw_hi = dw.astype(BF16)
    dw_lo = (dw - dw_hi.astype(F32)).astype(BF16)
    srt = _dot(perm, jnp.concatenate([x.astype(BF16), dw_hi, dw_lo], axis=1))
    xs_ref[...] = srt[:, :D_MODEL].astype(BF16)
    ws_ref[...] = srt[:, D_MODEL:D_MODEL + LANES] + srt[:, D_MODEL + LANES:]
    ys_ref[...] = jnp.zeros_like(ys_ref)

    def expert_tile(grp, r0, n_rows):
        rows = pl.ds(pl.multiple_of(r0, MOE_ALIGN), n_rows)
        xt = xs_ref[rows, :]
        w = ws_ref[rows, :]
        hes = []
        for e in range(EXP_PER_GROUP):
            ex = grp * EXP_PER_GROUP + e
            w_e = w[:, ROUTE_LANE0 + ex:ROUTE_LANE0 + ex + 1]
            hes.append((_silu(_dot(xt, wg_ref[ex])) * _dot(xt, wu_ref[ex]) * w_e).astype(BF16))
        ys_ref[rows, :] = _dot(jnp.concatenate(hes, axis=1), wd_ref[grp]).astype(BF16)

    for grp in range(N_GROUPS):
        n_g, lo = counts_i[grp], 0
        for size in MOE_SINGLE:
            pl.when((n_g > lo) & (n_g <= size))(functools.partial(expert_tile, grp, starts[grp], size))
            lo = size

        @pl.when(n_g > lo)
        def _(grp=grp):
            lax.fori_loop(0, tiles[grp],
                          lambda k, c: (expert_tile(grp, starts[grp] + k * MOE_ROWS, MOE_ROWS), c)[1], 0)

    lane_ns = iota((ch, ns), 1).astype(F32)
    ys = ys_ref[...]
    y_rows = [_dot(jnp.where(lane_ns == pos[r], 1.0, 0.0).astype(BF16), ys) for r in chunks]
    for r, y in zip(chunks, y_rows):
        o_ref[r, :] = _ln_rows(ALPHA * x_ref[r, :] + y, g_ref[...], b_ref[...])


def _ffn(oa, ob, oc, h, wo, g1, b1, wr2, br, wg, wu, wd, g2, b2, tm, layer, ln=None):
    n = h.shape[0]
    ns = _moe_sorted_rows(tm)
    assert ns <= 32 * 32, "sorted positions are transposed as two base-32 digits"
    row = lambda width: pl.BlockSpec((tm, width), lambda i: (i, 0))
    const = lambda shape: pl.BlockSpec(shape, lambda i: (0,) * len(shape))
    resident = lambda shape: pl.BlockSpec((None,) + shape, lambda i: (layer,) + (0,) * len(shape),
                                          pipeline_mode=pl.Buffered(1))
    ln_specs = [const((1, D_MODEL)), const((1, D_MODEL))] if ln else []
    return pl.pallas_call(
        functools.partial(_ffn_kernel, pre_ln=bool(ln)),
        grid=(n // tm,),
        in_specs=ln_specs + [row(A_WIDTH), row(B_WIDTH), row(C_WIDTH), row(D_MODEL),
                             resident((D_MODEL, D_MODEL)), const((1, D_MODEL)), const((1, D_MODEL)),
                             resident((D_MODEL, 2 * LANES)), const((1, LANES)),
                             resident((N_EXPERTS, D_MODEL, D_EXPERT)), resident((N_EXPERTS, D_MODEL, D_EXPERT)),
                             resident((N_GROUPS, GROUP_WIDTH, D_MODEL)),
                             const((1, D_MODEL)), const((1, D_MODEL))],
        out_specs=row(D_MODEL),
        out_shape=jax.ShapeDtypeStruct((n, D_MODEL), F32),
        scratch_shapes=[pltpu.VMEM((tm, D_MODEL), F32), pltpu.VMEM((ns, D_MODEL), BF16),
                        pltpu.VMEM((ns, D_MODEL), BF16), pltpu.VMEM((ns, LANES), F32)],
        compiler_params=_params("parallel"),
        name="ffn",
    )(*(ln or ()), oa, ob, oc, h, wo, g1, b1, wr2, br, wg, wu, wd, g2, b2)


def _rope_tables(pos):
    half = HEAD_DIM // 2
    inv = ROPE_THETA ** (-jnp.arange(half, dtype=F32) / half)
    ang = pos.astype(F32)[:, None] * inv[None, :]
    cos, sin = jnp.cos(ang), jnp.sin(ang)
    cos_t = jnp.concatenate([cos, cos] * (LANES // HEAD_DIM), axis=1)
    sin_t = jnp.concatenate([-sin, sin] * (LANES // HEAD_DIM), axis=1)
    return cos_t, sin_t


def _row(v):
    return v.reshape(1, -1)


def kernel(x_prompt, x_sample, cache_meta_k, cache_meta_v, cache_win_k, cache_win_v, state_gla, state_conv,
           meta_tokens, ln_in_g, ln_in_b, w_in, attn_sink, w_alpha, b_alpha, gla_norm_g,
           conv_dw_w, conv_dw_b, conv_ln_g, conv_ln_b, conv_pw_w, conv_pw_b, w_out, ln1_g, ln1_b,
           w_router_group, b_router_group, w_router_expert, b_router_expert,
           w_exp_gate, w_exp_up, w_exp_down, ln2_g, ln2_b):
    nb, seq, d = x_prompt.shape
    ns = x_sample.shape[0]
    nwin = cache_win_k.shape[2]
    n_big = nb * seq
    n_small = BLOCK + ns
    tm_big = min(FFN_ROWS, seq)
    tm_proj = min(PROJ_ROWS, seq)

    small_in = jnp.concatenate([jnp.zeros((META_PAD, d), F32), meta_tokens.astype(F32),
                                x_sample.reshape(ns, d)], axis=0)
    hb = x_prompt.reshape(n_big, d)
    hs = small_in
    ln_in = (_row(ln_in_g), _row(ln_in_b))

    cos_b, sin_b = _rope_tables(N_META + jnp.arange(seq))
    pos_small = jnp.concatenate([jnp.maximum(jnp.arange(BLOCK) - META_PAD, 0),
                                 jnp.full((ns,), PAST_LEN, jnp.int32)])
    cos_s, sin_s = _rope_tables(pos_small)

    outs = [[] for _ in range(12)]
    zeros_hist = jnp.zeros((CONV_HIST, C_WIDTH), F32)
    zeros_state = jnp.zeros((1, B_WIDTH, GK_WIDTH), F32)

    nl = w_in.shape[0]
    w_pad = jnp.concatenate([w_in[:, :, :C_AB + B_RANK], jnp.zeros((nl, d, LANES - B_RANK), F32),
                             w_in[:, :, C_AB + B_RANK:]], axis=2).astype(BF16)
    wa_all = jnp.concatenate([w_alpha, jnp.zeros((nl, LANES - B_RANK, GK_WIDTH), F32)], axis=1).astype(BF16)
    ba_all = b_alpha[:, None, :]
    wo_all = w_out.astype(BF16)
    wr = jnp.concatenate([w_router_group, w_router_expert,
                          jnp.zeros((nl, d, LANES - N_GROUPS - N_EXPERTS), F32)], axis=2)
    wr_hi = wr.astype(BF16)
    wr2_all = jnp.concatenate([wr_hi, (wr - wr_hi.astype(F32)).astype(BF16)], axis=2)
    wg_all, wu_all = w_exp_gate.astype(BF16), w_exp_up.astype(BF16)
    wd_all = w_exp_down.astype(BF16).reshape(nl, N_GROUPS, GROUP_WIDTH, d)
    win_k_t = cache_win_k.transpose(0, 1, 3, 4, 2).reshape(nl, ns, KV_WIDTH, nwin)
    win_v_t = cache_win_v.transpose(0, 1, 3, 4, 2).reshape(nl, ns, KV_WIDTH, nwin)
    gla_t = state_gla.transpose(0, 2, 3, 4, 1).reshape(nl, GK_WIDTH, B_DV, ns)
    conv_t = state_conv.transpose(0, 2, 1, 3)

    for l in range(DEPTH):
        sink = attn_sink[l].astype(F32)
        sink_col = jnp.concatenate([sink, jnp.zeros((HEAD_ROWS - A_HEADS,), F32)])[:, None]
        gg_t = _row(jnp.tile(gla_norm_g[l], B_HEADS))
        gg_col = gla_norm_g[l][:, None]
        dww = jnp.concatenate([conv_dw_w[l], jnp.zeros((CONV_HIST - CONV_W, C_WIDTH), F32)], axis=0)
        dwb, clg, clb = _row(conv_dw_b[l]), _row(conv_ln_g[l]), _row(conv_ln_b[l])
        pww, pwb = conv_pw_w[l].astype(BF16), _row(conv_pw_b[l])
        br = _row(jnp.concatenate([b_router_group[l], b_router_expert[l],
                                   jnp.zeros((LANES - N_GROUPS - N_EXPERTS,), F32)]))

        ln = ln_in if l == 0 else None
        qa_b, ka_b, va_b, qg_b, kg_b, la_b, vg_b, rg_b, u_b = _proj_in(
            hb, w_pad, wa_all, ba_all, cos_b, sin_b, layer=l, tm=tm_proj, n_pad=0, n_seq=tm_proj, ln=ln)
        qa_s, ka_s, va_s, qg_s, kg_s, la_s, vg_s, rg_s, u_s = _proj_in(
            hs, w_pad, wa_all, ba_all, cos_s, sin_s, layer=l, tm=n_small, n_pad=META_PAD, n_seq=BLOCK, ln=ln)

        oa_b = _attn_prompt(sink, qa_b, ka_b, va_b, ka_s, va_s, nb)
        oa_m = _attn_meta(sink, qa_s, ka_s, va_s)
        q_smp = qa_s[BLOCK:].reshape(ns, A_KV_HEADS, A_GROUP, 1, HEAD_DIM)
        eye = jnp.eye(A_KV_HEADS, dtype=BF16)[None, :, None, :, None]
        q_bd = (q_smp * eye).reshape(ns, A_HEADS, KV_WIDTH)
        q_bd = jnp.pad(q_bd, ((0, 0), (0, HEAD_ROWS - A_HEADS), (0, 0)))
        o_bd, nwk_t, nwv_t = _attn_sample(
            sink_col, q_bd, ka_s, va_s,
            cache_meta_k[l].reshape(ns, N_META, KV_WIDTH), cache_meta_v[l].reshape(ns, N_META, KV_WIDTH),
            win_k_t, win_v_t, l)
        o_bd = o_bd[:, :A_HEADS].reshape(ns, A_KV_HEADS, A_GROUP, A_KV_HEADS, HEAD_DIM)
        oa_smp = jnp.stack([o_bd[:, c, :, c, :] for c in range(A_KV_HEADS)], axis=1).reshape(ns, A_WIDTH)
        oa_s = jnp.concatenate([oa_m, oa_smp.astype(BF16)], axis=0)

        ob_m, st_m = _gla_prompt(qg_s[:BLOCK], kg_s[:BLOCK], la_s[:BLOCK], vg_s[:BLOCK], rg_s[:BLOCK],
                                 zeros_state, gg_t, 1)
        ob_b, st_b = _gla_prompt(qg_b, kg_b, la_b, vg_b, rg_b, st_m, gg_t, nb)
        ob_smp, s_new_t = _gla_sample(qg_s, kg_s, la_s, vg_s, rg_s, gla_t, gg_col, l)
        ob_s = jnp.concatenate([ob_m.reshape(BLOCK, B_WIDTH), ob_smp], axis=0)
        st5 = st_b.reshape(nb, B_HEADS, B_DV, B_HEADS, B_DK)
        gla_p = jnp.stack([st5[:, h, :, h, :] for h in range(B_HEADS)], axis=1).transpose(0, 1, 3, 2)

        oc_m, _ = _conv_prompt(u_s[:BLOCK], zeros_hist, dww, dwb, clg, clb, pww, pwb, 1)
        oc_b, tail_b = _conv_prompt(u_b, u_s[BLOCK - CONV_HIST:BLOCK], dww, dwb, clg, clb, pww, pwb, nb)
        oc_smp, tail_s_t = _conv_sample(conv_t, u_s, dww, dwb, clg, clb, pww, pwb, l)
        oc_s = jnp.concatenate([oc_m.reshape(BLOCK, C_WIDTH), oc_smp], axis=0)

        l1g, l1b, l2g, l2b = _row(ln1_g[l]), _row(ln1_b[l]), _row(ln2_g[l]), _row(ln2_b[l])
        hb = _ffn(oa_b, ob_b.reshape(n_big, B_WIDTH), oc_b.reshape(n_big, C_WIDTH), hb, wo_all, l1g, l1b,
                  wr2_all, br, wg_all, wu_all, wd_all, l2g, l2b, tm_big, l, ln=ln)
        hs = _ffn(oa_s, ob_s, oc_s, hs, wo_all, l1g, l1b, wr2_all, br, wg_all, wu_all, wd_all, l2g, l2b,
                  n_small, l, ln=ln)

        kv4 = lambda a: a.reshape(a.shape[0], a.shape[1], A_KV_HEADS, HEAD_DIM)
        win_t = lambda a: a.reshape(ns, A_KV_HEADS, HEAD_DIM, nwin).transpose(0, 3, 1, 2)
        meta_k = jnp.broadcast_to(ka_s[META_PAD:BLOCK][None], (nb, N_META, KV_WIDTH))
        meta_v = jnp.broadcast_to(va_s[META_PAD:BLOCK][None], (nb, N_META, KV_WIDTH))
        win_k = ka_b.reshape(nb, seq, KV_WIDTH)[:, seq - nwin:]
        win_v = va_b.reshape(nb, seq, KV_WIDTH)[:, seq - nwin:]
        layer_out = (None, None, kv4(meta_k), kv4(meta_v), kv4(win_k), kv4(win_v), win_t(nwk_t), win_t(nwv_t),
                     gla_p, s_new_t.reshape(B_HEADS, B_DK, B_DV, ns).transpose(3, 0, 1, 2),
                     tail_b[:, CONV_HIST - (CONV_W - 1):], tail_s_t.transpose(1, 0, 2))
        for i in range(2, 12):
            outs[i].append(layer_out[i])

    y_prompt = hb.reshape(nb, seq, d)
    y_sample = hs[BLOCK:].reshape(ns, 1, d)
    return (y_prompt, y_sample) + tuple(jnp.stack(o) for o in outs[2:])
```
